```python
import jax, jax.numpy as jnp
from jax import lax
import numpy as np

D_MODEL = 1024
BATCH = 2
SEQ = 8192
DEPTH = 2

N_EVEN = (DEPTH + 1) // 2
N_ODD = DEPTH // 2

GLA_HEADS = 4
GLA_DK = 64
GLA_DV = 128
GLA_GATE_RANK = 16
GLA_GATE_TAU = 16.0
GLA_CHUNK = 64

DSA_HEADS = 4
DSA_HEAD_DIM = 128
IDX_HEADS = 8
IDX_DIM = 64
DSA_TOPK = 256

POOL_WINDOWS = (2, 4, 8, 16)
POOL_GROUPS = 4
POOL_CH = 128

SB_HEADS = 4
SB_HEAD_DIM = 128

BLOCK_Q = 128
ROPE_THETA = 10000.0
NORM_EPS = 1e-6
D_FF = ((8 * D_MODEL + 3 * 256 - 1) // (3 * 256)) * 256

AB_SPLITS = (GLA_HEADS * GLA_DK, GLA_HEADS * GLA_DK, GLA_HEADS * GLA_DV, GLA_GATE_RANK,
             GLA_HEADS * GLA_DV,
             DSA_HEADS * DSA_HEAD_DIM, DSA_HEADS * DSA_HEAD_DIM, DSA_HEADS * DSA_HEAD_DIM,
             IDX_HEADS * IDX_DIM, IDX_DIM, IDX_HEADS)
AB_IN = sum(AB_SPLITS)
D_MIX_AB = GLA_HEADS * GLA_DV + DSA_HEADS * DSA_HEAD_DIM
CD_SPLITS = (POOL_GROUPS * POOL_CH, SB_HEADS * SB_HEAD_DIM, SB_HEADS * SB_HEAD_DIM,
             SB_HEADS * SB_HEAD_DIM)
CD_IN = sum(CD_SPLITS)
D_MIX_CD = POOL_GROUPS * POOL_CH + SB_HEADS * SB_HEAD_DIM

kernel_name = "hybrid_gla_dsa_pool_stickbreak_adaln"


def split_cols(t, sizes):
    offsets = np.cumsum(sizes)[:-1].tolist()
    return jnp.split(t, offsets, axis=-1)


def rms_norm(t, w):
    tf = t.astype(jnp.float32)
    y = tf * lax.rsqrt(jnp.mean(tf * tf, axis=-1, keepdims=True) + NORM_EPS)
    return (y * w.astype(jnp.float32)).astype(t.dtype)


def modulate(h, shift, scale):
    return h * (1.0 + scale[:, None, :]) + shift[:, None, :]


def rope(t, positions):
    half = t.shape[-1] // 2
    inv_freq = ROPE_THETA ** (-jnp.arange(half, dtype=jnp.float32) / half)
    ang = positions.astype(jnp.float32)[:, :, None, None] * inv_freq
    cos, sin = jnp.cos(ang), jnp.sin(ang)
    tf = t.astype(jnp.float32)
    t1, t2 = tf[..., :half], tf[..., half:]
    return jnp.concatenate([t1 * cos - t2 * sin, t2 * cos + t1 * sin], axis=-1).astype(t.dtype)


def swiglu(h, w1, w2):
    gate, up = jnp.split(h @ w1, 2, axis=-1)
    return (jax.nn.silu(gate) * up) @ w2


def gla_chunked(q, k, v, log_a):
    B, S, H, DK = q.shape
    DV = v.shape[-1]
    C = GLA_CHUNK
    N = S // C

    def chunks(t):
        return t.astype(jnp.float32).reshape(B, N, C, H, t.shape[-1]).transpose(0, 3, 1, 2, 4)

    q, k, v, g = chunks(q), chunks(k), chunks(v), chunks(log_a)
    q = q * (DK ** -0.5)
    b = jnp.cumsum(g, axis=3)
    b_mid = b[:, :, :, C // 2 - 1:C // 2, :]
    att = jnp.einsum('bhnid,bhnjd->bhnij', q * jnp.exp(b - b_mid), k * jnp.exp(b_mid - b))
    causal = jnp.tril(jnp.ones((C, C), dtype=bool))
    att = jnp.where(causal, att, 0.0)
    o_intra = jnp.einsum('bhnij,bhnjv->bhniv', att, v)
    b_last = b[:, :, :, -1:, :]
    k_dec = k * jnp.exp(b_last - b)
    q_dec = q * jnp.exp(b)
    decay = jnp.exp(b_last[:, :, :, 0, :])
    kv = jnp.einsum('bhncd,bhncv->bhndv', k_dec, v)

    def step(state, inp):
        dec, kv_n = inp
        return dec[..., None] * state + kv_n, state

    init = jnp.zeros((B, H, DK, DV), jnp.float32)
    _, prev = lax.scan(step, init, (jnp.moveaxis(decay, 2, 0), jnp.moveaxis(kv, 2, 0)))
    prev = jnp.moveaxis(prev, 0, 2)
    o_inter = jnp.einsum('bhncd,bhndv->bhncv', q_dec, prev)
    o = o_intra + o_inter
    return o.transpose(0, 2, 3, 1, 4).reshape(B, S, H, DV)


def dsa_attention(q, k, v, iq, ik, iw, n_sel):
    B, S, H, Dh = q.shape
    bidx = jnp.arange(B)[:, None, None]
    key_pos = jnp.arange(S)
    idx_scale = (IDX_HEADS ** -0.5) * (IDX_DIM ** -0.5)
    att_scale = Dh ** -0.5

    def block(i):
        t0 = i * BLOCK_Q
        tpos = t0 + jnp.arange(BLOCK_Q)
        qb = lax.dynamic_slice_in_dim(q, t0, BLOCK_Q, axis=1)
        iqb = lax.dynamic_slice_in_dim(iq, t0, BLOCK_Q, axis=1)
        iwb = lax.dynamic_slice_in_dim(iw, t0, BLOCK_Q, axis=1)
        rel = jax.nn.relu(jnp.einsum('bqhd,bsd->bqhs', iqb, ik).astype(jnp.float32))
        score = jnp.einsum('bqhs,bqh->bqs', rel, iwb.astype(jnp.float32)) * idx_scale
        causal = key_pos[None, :] <= tpos[:, None]
        score = jnp.where(causal[None], score, -jnp.inf)
        _, sel = lax.top_k(score, n_sel)
        kg = k[bidx, sel]
        vg = v[bidx, sel]
        logits = jnp.einsum('bqhd,bqkhd->bhqk', qb, kg).astype(jnp.float32) * att_scale
        valid = (sel <= tpos[None, :, None])[:, None]
        logits = jnp.where(valid, logits, -jnp.inf)
        p = jax.nn.softmax(logits, axis=-1).astype(v.dtype)
        return jnp.einsum('bhqk,bqkhd->bqhd', p, vg)

    out = lax.map(block, jnp.arange(S // BLOCK_Q))
    return jnp.moveaxis(out, 0, 1).reshape(B, S, H, Dh)


def stick_breaking_attention(q, k, v):
    B, S, H, Dh = q.shape
    key_pos = jnp.arange(S)
    scale = Dh ** -0.5

    def block(i):
        t0 = i * BLOCK_Q
        tpos = t0 + jnp.arange(BLOCK_Q)
        qb = lax.dynamic_slice_in_dim(q, t0, BLOCK_Q, axis=1)
        z = jnp.einsum('bqhd,bshd->bhqs', qb, k).astype(jnp.float32) * scale
        strict = (key_pos[None, :] < tpos[:, None])[None, None]
        log_beta = jax.nn.log_sigmoid(z)
        log_1m = jnp.where(strict, jax.nn.log_sigmoid(-z), 0.0)
        after = lax.cumsum(log_1m, axis=3, reverse=True) - log_1m
        w = jnp.where(strict, jnp.exp(log_beta + after), 0.0)
        return jnp.einsum('bhqs,bshd->bqhd', w.astype(v.dtype), v)

    out = lax.map(block, jnp.arange(S // BLOCK_Q))
    return jnp.moveaxis(out, 0, 1).reshape(B, S, H, Dh)


def multiscale_pool(u, pool_w, pool_scale):
    B, S, _ = u.shape
    ug = u.astype(jnp.float32).reshape(B, S, POOL_GROUPS, POOL_CH)
    cs = jnp.cumsum(ug, axis=1)
    cs0 = jnp.concatenate([jnp.zeros((B, 1, POOL_GROUPS, POOL_CH), jnp.float32), cs], axis=1)
    t = jnp.arange(S, dtype=jnp.float32)
    pooled = []
    for g, w in enumerate(POOL_WINDOWS):
        lo = jnp.pad(cs0[:, :S + 1 - w, g], ((0, 0), (w - 1, 0), (0, 0)))
        count = jnp.minimum(t + 1.0, float(w))[None, :, None]
        pooled.append((cs[:, :, g] - lo) / count - ug[:, :, g])
    pooled = jnp.stack(pooled, axis=2)
    y = jnp.einsum('bsgc,gcd->bsgd', pooled, pool_w.astype(jnp.float32))
    y = y * pool_scale.astype(jnp.float32).reshape(POOL_GROUPS, POOL_CH)
    return y.reshape(B, S, POOL_GROUPS * POOL_CH).astype(u.dtype)


def ab_mixer(h, positions, w_in, gate_up, gate_b, out_norm, q_norm, k_norm, w_out):
    B, S, _ = h.shape
    (gq, gk, gv, glow, gr, dq, dk, dv, iq, ik, iw) = split_cols(h @ w_in, AB_SPLITS)
    log_a = jax.nn.log_sigmoid((glow @ gate_up + gate_b).astype(jnp.float32)) / GLA_GATE_TAU
    o_gla = gla_chunked(gq.reshape(B, S, GLA_HEADS, GLA_DK), gk.reshape(B, S, GLA_HEADS, GLA_DK),
                        gv.reshape(B, S, GLA_HEADS, GLA_DV), log_a.reshape(B, S, GLA_HEADS, GLA_DK))
    o_gla = rms_norm(o_gla.astype(h.dtype), out_norm) * jax.nn.silu(gr.reshape(B, S, GLA_HEADS, GLA_DV))
    q = rope(rms_norm(dq.reshape(B, S, DSA_HEADS, DSA_HEAD_DIM), q_norm), positions)
    k = rope(rms_norm(dk.reshape(B, S, DSA_HEADS, DSA_HEAD_DIM), k_norm), positions)
    v = dv.reshape(B, S, DSA_HEADS, DSA_HEAD_DIM)
    iq = rope(iq.reshape(B, S, IDX_HEADS, IDX_DIM), positions)
    ik = rope(ik[:, :, None, :], positions)[:, :, 0, :]
    n_sel = min(DSA_TOPK, S // 4)
    o_dsa = dsa_attention(q, k, v, iq, ik, iw, n_sel)
    mix = jnp.concatenate([o_gla.reshape(B, S, -1), o_dsa.reshape(B, S, -1)], axis=-1)
    return mix @ w_out


def cd_mixer(h, w_in, pool_w, pool_scale, q_norm, k_norm, w_out):
    B, S, _ = h.shape
    u, sq, sk, sv = split_cols(h @ w_in, CD_SPLITS)
    o_pool = multiscale_pool(u, pool_w, pool_scale)
    q = rms_norm(sq.reshape(B, S, SB_HEADS, SB_HEAD_DIM), q_norm)
    k = rms_norm(sk.reshape(B, S, SB_HEADS, SB_HEAD_DIM), k_norm)
    v = sv.reshape(B, S, SB_HEADS, SB_HEAD_DIM)
    o_sb = stick_breaking_attention(q, k, v)
    mix = jnp.concatenate([o_pool, o_sb.reshape(B, S, -1)], axis=-1)
    return mix @ w_out


def setup_inputs(seed: int = 0) -> dict:
    key = jax.random.key(seed)
    ks = jax.random.split(key, 24)
    f32 = jnp.float32

    def nrm(k, shape, scale):
        return jax.random.normal(k, shape, f32) * scale

    def gain(k, shape):
        return 1.0 + 0.1 * jax.random.normal(k, shape, f32)

    x = nrm(ks[0], (BATCH, SEQ, D_MODEL), 1.0)
    c = nrm(ks[1], (BATCH, D_MODEL), 1.0)
    positions = (jnp.arange(SEQ, dtype=jnp.int32)[None, :]
                 + jax.random.randint(ks[2], (BATCH, 1), 0, 4096, dtype=jnp.int32))
    return {
        "x": x,
        "c": c,
        "positions": positions,
        "ada_w": nrm(ks[3], (DEPTH, D_MODEL, 6 * D_MODEL), 0.5 * D_MODEL ** -0.5),
        "ada_b": nrm(ks[4], (DEPTH, 6 * D_MODEL), 0.01),
        "mix_norm": gain(ks[5], (DEPTH, D_MODEL)),
        "ffn_norm": gain(ks[6], (DEPTH, D_MODEL)),
        "ffn_w1": nrm(ks[7], (DEPTH, D_MODEL, 2 * D_FF), D_MODEL ** -0.5),
        "ffn_w2": nrm(ks[8], (DEPTH, D_FF, D_MODEL), D_FF ** -0.5),
        "ab_w_in": nrm(ks[9], (N_EVEN, D_MODEL, AB_IN), D_MODEL ** -0.5),
        "gla_gate_up": nrm(ks[10], (N_EVEN, GLA_GATE_RANK, GLA_HEADS * GLA_DK), GLA_GATE_RANK ** -0.5),
        "gla_gate_b": nrm(ks[11], (N_EVEN, GLA_HEADS * GLA_DK), 0.01),
        "gla_out_norm": gain(ks[12], (N_EVEN, GLA_DV)),
        "dsa_q_norm": gain(ks[13], (N_EVEN, DSA_HEAD_DIM)),
        "dsa_k_norm": gain(ks[14], (N_EVEN, DSA_HEAD_DIM)),
        "ab_w_out": nrm(ks[15], (N_EVEN, D_MIX_AB, D_MODEL), D_MIX_AB ** -0.5),
        "cd_w_in": nrm(ks[16], (N_ODD, D_MODEL, CD_IN), D_MODEL ** -0.5),
        "pool_w": nrm(ks[17], (N_ODD, POOL_GROUPS, POOL_CH, POOL_CH), POOL_CH ** -0.5),
        "pool_scale": gain(ks[18], (N_ODD, POOL_GROUPS * POOL_CH)),
        "sb_q_norm": gain(ks[19], (N_ODD, SB_HEAD_DIM)),
        "sb_k_norm": gain(ks[20], (N_ODD, SB_HEAD_DIM)),
        "cd_w_out": nrm(ks[21], (N_ODD, D_MIX_CD, D_MODEL), D_MIX_CD ** -0.5),
    }


def reference(x, c, positions, ada_w, ada_b, mix_norm, ffn_norm, ffn_w1, ffn_w2,
              ab_w_in, gla_gate_up, gla_gate_b, gla_out_norm, dsa_q_norm, dsa_k_norm, ab_w_out,
              cd_w_in, pool_w, pool_scale, sb_q_norm, sb_k_norm, cd_w_out):
    cond = jax.nn.silu(c)
    for layer in range(DEPTH):
        mod = cond @ ada_w[layer] + ada_b[layer]
        sh1, sc1, g1, sh2, sc2, g2 = jnp.split(mod, 6, axis=-1)
        h = modulate(rms_norm(x, mix_norm[layer]), sh1, sc1)
        if layer % 2 == 0:
            i = layer // 2
            y = ab_mixer(h, positions, ab_w_in[i], gla_gate_up[i], gla_gate_b[i], gla_out_norm[i],
                         dsa_q_norm[i], dsa_k_norm[i], ab_w_out[i])
        else:
            i = layer // 2
            y = cd_mixer(h, cd_w_in[i], pool_w[i], pool_scale[i], sb_q_norm[i], sb_k_norm[i],
                         cd_w_out[i])
        x = x + g1[:, None, :] * y
        h = modulate(rms_norm(x, ffn_norm[layer]), sh2, sc2)
        x = x + g2[:, None, :] * swiglu(h, ffn_w1[layer], ffn_w2[layer])
    return x
```

```python
import functools

import jax
import jax.numpy as jnp
from jax import lax
from jax.experimental import pallas as pl
from jax.experimental.pallas import tpu as pltpu

F32 = jnp.float32
BF16 = jnp.bfloat16
I32 = jnp.int32

D_MODEL = 1024
GLA_HEADS, GLA_DK, GLA_DV = 4, 64, 128
GLA_GATE_RANK = 16
GLA_GATE_TAU = 16.0
GLA_CHUNK = 64
DSA_HEADS, DSA_HEAD_DIM = 4, 128
IDX_HEADS, IDX_DIM = 8, 64
DSA_TOPK = 256
POOL_WINDOWS = (2, 4, 8, 16)
POOL_GROUPS, POOL_CH = 4, 128
SB_HEADS, SB_HEAD_DIM = 4, 128
ROPE_THETA = 10000.0
NORM_EPS = 1e-6
D_FF = 2816

LANES = 128
VMEM_LIMIT = 56 * 1024 * 1024

AB_GQ, AB_GK, AB_GV, AB_GR = 0, 256, 512, 1024
AB_DQ, AB_DK, AB_DV, AB_IQ = 1536, 2048, 2560, 3072
AB_SMALL = 3584
SM_IK, SM_GLOW, SM_IW = 0, 64, 80
AB_NPAD = 3840

INT_MIN = -2 ** 31
MASK_BIAS = -1e30
SB_DEAD_LOG = -110.0


def _dot(a, b):
    return jnp.dot(a, b, preferred_element_type=F32)


def _dot_nt(a, b):
    return lax.dot_general(a, b, (((1,), (1,)), ((), ())), preferred_element_type=F32)


def _dot_tn(a, b):
    return lax.dot_general(a, b, (((0,), (0,)), ((), ())), preferred_element_type=F32)


def _split3(a):
    hi = a.astype(BF16)
    r1 = a - hi.astype(F32)
    mid = r1.astype(BF16)
    lo = (r1 - mid.astype(F32)).astype(BF16)
    return hi, mid, lo


def _silu(x):
    return x * jax.nn.sigmoid(x)


def _rms(x, w):
    var = jnp.mean(x * x, axis=-1, keepdims=True)
    return x * lax.rsqrt(var + NORM_EPS) * w


def _cparams(sem):
    return pltpu.CompilerParams(dimension_semantics=sem, vmem_limit_bytes=VMEM_LIMIT)


def _resident(block_shape, index_map):
    return pl.BlockSpec(block_shape, index_map, pipeline_mode=pl.Buffered(1))


def _ada_kernel(c_ref, w_ref, b_ref, o_ref):
    cond = _silu(c_ref[...])
    c_hi, c_mid, _ = _split3(cond)
    w_hi, w_mid, _ = _split3(w_ref[...])
    acc = _dot(c_hi, w_hi) + _dot(c_hi, w_mid) + _dot(c_mid, w_hi)
    o_ref[...] = acc + b_ref[...]


def _ada_mod(c, ada_w, ada_b):
    depth, d, n = ada_w.shape
    bsz = c.shape[0]
    rows = 16
    cp =jnp.zeros((rows, d), F32).at[:bsz].set(c)
    tn = 1536
    out = pl.pallas_call(
        _ada_kernel,
        grid=(depth, n // tn),
        in_specs=[
            pl.BlockSpec((rows, d), lambda l, j: (0, 0)),
            pl.BlockSpec((None, d, tn), lambda l, j: (l, 0, j)),
            pl.BlockSpec((None, 1, tn), lambda l, j: (l, 0, j)),
        ],
        out_specs=pl.BlockSpec((None, rows, tn), lambda l, j: (l, 0, j)),
        out_shape=jax.ShapeDtypeStruct((depth, rows, n), F32),
        compiler_params=_cparams(("arbitrary", "arbitrary")),
        name="ada_mod",
    )(cp, ada_w, ada_b.reshape(depth, 1, n))
    return out[:, :bsz]


def _norm_proj_kernel(x_ref, nw_ref, sh_ref, sc_ref, w_ref, o_ref, *, tn):
    h = _rms(x_ref[...], nw_ref[...]) * (1.0 + sc_ref[...]) + sh_ref[...]
    hb = h.astype(BF16)
    n = w_ref.shape[1]
    for j in range(n // tn):
        o_ref[:, j * tn:(j + 1) * tn] = _dot(hb, w_ref[:, j * tn:(j + 1) * tn])


def _norm_proj(x, nw, shift, scale, w_bf16, *, tm, tn):
    bsz, s, d = x.shape
    n = w_bf16.shape[1]
    return pl.pallas_call(
        functools.partial(_norm_proj_kernel, tn=tn),
        grid=(bsz, s // tm),
        in_specs=[
            pl.BlockSpec((None, tm, d), lambda b, i: (b, i, 0)),
            pl.BlockSpec((1, d), lambda b, i: (0, 0)),
            pl.BlockSpec((None, 1, d), lambda b, i: (b, 0, 0)),
            pl.BlockSpec((None, 1, d), lambda b, i: (b, 0, 0)),
            _resident((d, n), lambda b, i: (0, 0)),
        ],
        out_specs=pl.BlockSpec((None, tm, n), lambda b, i: (b, i, 0)),
        out_shape=jax.ShapeDtypeStruct((bsz, s, n), F32),
        compiler_params=_cparams(("arbitrary", "arbitrary")),
        name="norm_proj",
    )(x, nw.reshape(1, d), shift, scale, w_bf16)


def _gla_kernel(q_ref, k_ref, v_ref, gr_ref, sm_ref, gup_ref, gb_ref, onw_ref, o_ref, st_ref, *, ts):
    c = GLA_CHUNK

    @pl.when(pl.program_id(1) == 0)
    def _():
        st_ref[...] = jnp.zeros_like(st_ref)

    row = lax.broadcasted_iota(I32, (c, c), 0)
    col = lax.broadcasted_iota(I32, (c, c), 1)
    causal = col <= row
    tri = jnp.where(causal, 1.0, 0.0).astype(BF16)
    gup = gup_ref[...].astype(BF16)
    gbias = gb_ref[...]
    onw = onw_ref[...]

    def chunk(ci, carry):
        r0 = pl.multiple_of(ci * c, c)
        q = q_ref[pl.ds(r0, c), :]
        k = k_ref[pl.ds(r0, c), :]
        v = v_ref[pl.ds(r0, c), :]
        gr = gr_ref[pl.ds(r0, c), :]
        glow = sm_ref[pl.ds(r0, c), SM_GLOW:SM_GLOW + GLA_GATE_RANK]
        a = _dot(glow.astype(BF16), gup) + gbias
        g = (jnp.minimum(a, 0.0) - jnp.log1p(jnp.exp(-jnp.abs(a)))) / GLA_GATE_TAU
        g_hi, g_mid, g_lo = _split3(g)
        b = _dot(tri, g_hi) + _dot(tri, g_mid) + _dot(tri, g_lo)
        b_mid = b[c // 2 - 1:c // 2, :]
        b_last = b[c - 1:c, :]
        qs = q * (GLA_DK ** -0.5)
        qe = (qs * jnp.exp(b - b_mid)).astype(BF16)
        ke = (k * jnp.exp(b_mid - b)).astype(BF16)
        kd = (k * jnp.exp(b_last - b)).astype(BF16)
        qd = (qs * jnp.exp(b)).astype(BF16)
        dec = jnp.exp(b_last)
        for h in range(GLA_HEADS):
            ks = slice(h * GLA_DK, (h + 1) * GLA_DK)
            vs = slice(h * GLA_DV, (h + 1) * GLA_DV)
            att = jnp.where(causal, _dot_nt(qe[:, ks], ke[:, ks]), 0.0)
            vh = v[:, vs].astype(BF16)
            st = st_ref[h]
            o = _dot(att.astype(BF16), vh) + _dot_nt(qd[:, ks], st.astype(BF16))
            st_ref[h] = st * dec[:, ks] + _dot_tn(vh, kd[:, ks])
            grh = gr[:, vs]
            o_ref[pl.ds(r0, c), vs] = (_rms(o, onw) * _silu(grh)).astype(o_ref.dtype)
        return carry

    lax.fori_loop(0, ts // c, chunk, 0)


def _gla(proj, gate_up, gate_b, out_norm, *, ts):
    bsz, s, _ = proj.shape
    hk, hv = GLA_HEADS * GLA_DK, GLA_HEADS * GLA_DV
    return pl.pallas_call(
        functools.partial(_gla_kernel, ts=ts),
        grid=(bsz, s // ts),
        in_specs=[
            pl.BlockSpec((None, ts, hk), lambda b, i: (b, i, AB_GQ // hk)),
            pl.BlockSpec((None, ts, hk), lambda b, i: (b, i, AB_GK // hk)),
            pl.BlockSpec((None, ts, hv), lambda b, i: (b, i, AB_GV // hv)),
            pl.BlockSpec((None, ts, hv), lambda b, i: (b, i, AB_GR // hv)),
            pl.BlockSpec((None, ts, LANES), lambda b, i: (b, i, AB_SMALL // LANES)),
            pl.BlockSpec((GLA_GATE_RANK, hk), lambda b, i: (0, 0)),
            pl.BlockSpec((1, hk), lambda b, i: (0, 0)),
            pl.BlockSpec((1, GLA_DV), lambda b, i: (0, 0)),
        ],
        out_specs=pl.BlockSpec((None, ts, hv), lambda b, i: (b, i, 0)),
        out_shape=jax.ShapeDtypeStruct((bsz, s, hv), BF16),
        scratch_shapes=[pltpu.VMEM((GLA_HEADS, GLA_DV, GLA_DK), F32)],
        compiler_params=_cparams(("arbitrary", "arbitrary")),
        name="gla",
    )(proj, proj, proj, proj, proj, gate_up, gate_b.reshape(1, hk), out_norm.reshape(1, GLA_DV))


def _dsa_prep_kernel(dq_ref, dk_ref, dv_ref, iq_ref, sm_ref, pos_ref, qn_ref, kn_ref, fa_ref, fi_ref,
                     q_o, k_o, v_o, iq_o, ik_o, iw_o):
    pos = pos_ref[...].astype(F32)
    lane = lax.broadcasted_iota(I32, (1, LANES), 1)

    ang = pos * fa_ref[...]
    cos_a, sin_a = jnp.cos(ang), jnp.sin(ang)
    sin_a = jnp.where(lane < DSA_HEAD_DIM // 2, -sin_a, sin_a)

    def rope_attn(t):
        return t * cos_a + pltpu.roll(t, DSA_HEAD_DIM // 2, 1) * sin_a

    ang = pos * fi_ref[...]
    cos_i, sin_i = jnp.cos(ang), jnp.sin(ang)
    first = (lane % IDX_DIM) < IDX_DIM // 2
    sin_i = jnp.where(first, -sin_i, sin_i)

    def rope_idx(t):
        rot = jnp.where(first, pltpu.roll(t, LANES - IDX_DIM // 2, 1), pltpu.roll(t, IDX_DIM // 2, 1))
        return t * cos_i + rot * sin_i

    qn, kn = qn_ref[...], kn_ref[...]
    for h in range(DSA_HEADS):
        hs = slice(h * DSA_HEAD_DIM, (h + 1) * DSA_HEAD_DIM)
        q_o[:, hs] = rope_attn(_rms(dq_ref[:, hs], qn)).astype(q_o.dtype)
        k_o[:, hs] = rope_attn(_rms(dk_ref[:, hs], kn)).astype(k_o.dtype)
    v_o[...] = dv_ref[...].astype(v_o.dtype)
    for j in range(IDX_HEADS * IDX_DIM // LANES):
        r = rope_idx(iq_ref[:, j * LANES:(j + 1) * LANES])
        iq_o[2 * j] = r[:, :IDX_DIM].astype(iq_o.dtype)
        iq_o[2 * j + 1] = r[:, IDX_DIM:].astype(iq_o.dtype)
    sm = sm_ref[...]
    ik_o[...] = rope_idx(sm)[:, SM_IK:SM_IK + IDX_DIM].astype(ik_o.dtype)
    idx_scale = (IDX_HEADS ** -0.5) * (IDX_DIM ** -0.5)
    iw_o[...] = sm[:, SM_IW:SM_IW + IDX_HEADS] * idx_scale


def _dsa_prep(proj, positions, q_norm, k_norm, *, ts):
    bsz, s, _ = proj.shape
    hd = DSA_HEADS * DSA_HEAD_DIM
    half_a, half_i = DSA_HEAD_DIM // 2, IDX_DIM // 2
    inv_a = ROPE_THETA ** (-jnp.arange(half_a, dtype=F32) / half_a)
    inv_i = ROPE_THETA ** (-jnp.arange(half_i, dtype=F32) / half_i)
    fa = jnp.tile(inv_a, LANES // half_a).reshape(1, LANES)
    fi = jnp.tile(inv_i, LANES // half_i).reshape(1, LANES)
    wide = lambda j: pl.BlockSpec((None, ts, hd), lambda b, i: (b, i, j))
    vec = lambda n: pl.BlockSpec((1, n), lambda b, i: (0, 0))
    return pl.pallas_call(
        _dsa_prep_kernel,
        grid=(bsz, s // ts),
        in_specs=[
            wide(AB_DQ // hd), wide(AB_DK // hd), wide(AB_DV // hd), wide(AB_IQ // hd),
            pl.BlockSpec((None, ts, LANES), lambda b, i: (b, i, AB_SMALL // LANES)),
            pl.BlockSpec((None, ts, 1), lambda b, i: (b, i, 0)),
            vec(DSA_HEAD_DIM), vec(DSA_HEAD_DIM), vec(LANES), vec(LANES),
        ],
        out_specs=[
            wide(0), wide(0), wide(0),
            pl.BlockSpec((None, IDX_HEADS, ts, IDX_DIM), lambda b, i: (b, 0, i, 0)),
            pl.BlockSpec((None, ts, IDX_DIM), lambda b, i: (b, i, 0)),
            pl.BlockSpec((None, ts, IDX_HEADS), lambda b, i: (b, i, 0)),
        ],
        out_shape=[
            jax.ShapeDtypeStruct((bsz, s, hd), BF16),
            jax.ShapeDtypeStruct((bsz, s, hd), BF16),
            jax.ShapeDtypeStruct((bsz, s, hd), BF16),
            jax.ShapeDtypeStruct((bsz, IDX_HEADS, s, IDX_DIM), BF16),
            jax.ShapeDtypeStruct((bsz, s, IDX_DIM), BF16),
            jax.ShapeDtypeStruct((bsz, s, IDX_HEADS), F32),
        ],
        compiler_params=_cparams(("arbitrary", "arbitrary")),
        name="dsa_prep",
    )(proj, proj, proj, proj, proj, positions.reshape(bsz, s, 1),
      q_norm.reshape(1, -1), k_norm.reshape(1, -1), fa, fi)


def _dsa_kernel(q_ref, iq_ref, iw_ref, k_ref, v_ref, ik_ref, o_ref,
                keys_ref, m_ref, l_ref, acc_ref, *, tq, n_sel, s_len):
    i = pl.program_id(1)
    nkb = i + 1
    row = i * tq + lax.broadcasted_iota(I32, (tq, tq), 0)
    col0 = lax.broadcasted_iota(I32, (tq, tq), 1)
    iw = iw_ref[...]

    def scores(kb, carry):
        k0 = pl.multiple_of(kb * tq, tq)
        ikb = ik_ref[pl.ds(k0, tq), :]
        sc = jnp.zeros((tq, tq), F32)
        for h in range(IDX_HEADS):
            sc = sc + jnp.maximum(_dot_nt(iq_ref[h], ikb), 0.0) * iw[:, h:h + 1]
        bits = pltpu.bitcast(sc, I32)
        bits = jnp.where(bits == INT_MIN, 0, bits)
        key = bits ^ ((bits >> 31) & 0x7FFFFFFF)
        keys_ref[kb] = jnp.where(kb * tq + col0 <= row, key, INT_MIN)
        return carry

    lax.fori_loop(0, nkb, scores, 0)

    def count(pred):
        def body(kb, acc):
            m = jnp.where(pred(keys_ref[kb], kb * tq + col0), 1, 0)
            for j in range(tq // LANES):
                acc = acc + m[:, j * LANES:(j + 1) * LANES]
            return acc
        acc = lax.fori_loop(0, nkb, body, jnp.zeros((tq, LANES), I32))
        return jnp.sum(acc, axis=1, keepdims=True)

    def bit_step(it, u):
        cand_u = u | lax.shift_left(jnp.int32(1), jnp.int32(31) - it)
        cand = cand_u ^ INT_MIN
        cnt = count(lambda key, c: key >= cand)
        return jnp.where(cnt >= n_sel, cand_u, u)

    u = lax.fori_loop(0, 32, bit_step, jnp.zeros((tq, 1), I32))
    thr = jnp.maximum(u ^ INT_MIN, INT_MIN + 1)
    cnt_gt = count(lambda key, c: key > thr)
    cnt_ge = count(lambda key, c: key >= thr)
    need = n_sel - cnt_gt
    excess = cnt_ge > n_sel

    def tie_columns():
        def step(it, jv):
            cand = jv | lax.shift_left(jnp.int32(1), jnp.int32(s_len.bit_length() - 1) - it)
            cnt = count(lambda key, c: (key == thr) & (c < cand))
            return jnp.where(cnt < need, cand, jv)
        jv = lax.fori_loop(0, s_len.bit_length(), step, jnp.zeros((tq, 1), I32))
        return jnp.where(excess, jv, s_len)

    any_excess = jnp.max(jnp.where(excess, 1, 0)) > 0
    jstar = lax.cond(any_excess, tie_columns, lambda: jnp.full((tq, 1), s_len, I32))

    m_ref[...] = jnp.full_like(m_ref, -jnp.inf)
    l_ref[...] = jnp.zeros_like(l_ref)
    acc_ref[...] = jnp.zeros_like(acc_ref)
    att_scale = DSA_HEAD_DIM ** -0.5

    def attend(kb, carry):
        k0 = pl.multiple_of(kb * tq, tq)
        key = keys_ref[kb]
        sel = (key > thr) | ((key == thr) & (kb * tq + col0 <= jstar))
        bias = jnp.where(sel, 0.0, MASK_BIAS)
        for h in range(DSA_HEADS):
            hs = slice(h * DSA_HEAD_DIM, (h + 1) * DSA_HEAD_DIM)
            s = _dot_nt(q_ref[:, hs], k_ref[pl.ds(k0, tq), hs]) * att_scale + bias
            m_old = m_ref[h]
            m_new = jnp.maximum(m_old, jnp.max(s, axis=1, keepdims=True))
            alpha = jnp.exp(m_old - m_new)
            p = jnp.exp(s - m_new)
            l_ref[h] = alpha * l_ref[h] + jnp.sum(p, axis=1, keepdims=True)
            acc_ref[h] = alpha * acc_ref[h] + _dot(p.astype(BF16), v_ref[pl.ds(k0, tq), hs])
            m_ref[h] = m_new
        return carry

    lax.fori_loop(0, nkb, attend, 0)
    for h in range(DSA_HEADS):
        hs = slice(h * DSA_HEAD_DIM, (h + 1) * DSA_HEAD_DIM)
        o_ref[:, hs] = (acc_ref[h] / l_ref[h]).astype(o_ref.dtype)


def _dsa(q, k, v, iq, ik, iw, *, tq):
    bsz, s, hd = q.shape
    n_sel = min(DSA_TOPK, s // 4)
    resident = lambda n: _resident((None, s, n), lambda b, i: (b, 0, 0))
    return pl.pallas_call(
        functools.partial(_dsa_kernel, tq=tq, n_sel=n_sel, s_len=s),
        grid=(bsz, s // tq),
        in_specs=[
            pl.BlockSpec((None, tq, hd), lambda b, i: (b, i, 0)),
            pl.BlockSpec((None, IDX_HEADS, tq, IDX_DIM), lambda b, i: (b, 0, i, 0)),
            pl.BlockSpec((None, tq, IDX_HEADS), lambda b, i: (b, i, 0)),
            resident(hd), resident(hd), resident(IDX_DIM),
        ],
        out_specs=pl.BlockSpec((None, tq, hd), lambda b, i: (b, i, 0)),
        out_shape=jax.ShapeDtypeStruct((bsz, s, hd), BF16),
        scratch_shapes=[
            pltpu.VMEM((s // tq, tq, tq), I32),
            pltpu.VMEM((DSA_HEADS, tq, 1), F32),
            pltpu.VMEM((DSA_HEADS, tq, 1), F32),
            pltpu.VMEM((DSA_HEADS, tq, DSA_HEAD_DIM), F32),
        ],
        compiler_params=_cparams(("arbitrary", "arbitrary")),
        name="dsa",
    )(q, iq, iw, k, v, ik)


def _out_proj_kernel(x_ref, a_ref, b_ref, wa_ref, wb_ref, g_ref, o_ref):
    y = _dot(a_ref[...], wa_ref[...]) + _dot(b_ref[...], wb_ref[...])
    o_ref[...] = x_ref[...] + g_ref[...] * y


def _out_proj(x, mix_a, mix_b, w_out_bf16, gate, *, tm):
    bsz, s, d = x.shape
    na = mix_a.shape[-1]
    assert mix_b.shape[-1] == na and w_out_bf16.shape[0] == 2 * na
    return pl.pallas_call(
        _out_proj_kernel,
        grid=(bsz, s // tm),
        in_specs=[
            pl.BlockSpec((None, tm, d), lambda b, i: (b, i, 0)),
            pl.BlockSpec((None, tm, na), lambda b, i: (b, i, 0)),
            pl.BlockSpec((None, tm, na), lambda b, i: (b, i, 0)),
            _resident((na, d), lambda b, i: (0, 0)),
            _resident((na, d), lambda b, i: (1, 0)),
            pl.BlockSpec((None, 1, d), lambda b, i: (b, 0, 0)),
        ],
        out_specs=pl.BlockSpec((None, tm, d), lambda b, i: (b, i, 0)),
        out_shape=jax.ShapeDtypeStruct((bsz, s, d), F32),
        compiler_params=_cparams(("arbitrary", "arbitrary")),
        name="out_proj",
    )(x, mix_a, mix_b, w_out_bf16, w_out_bf16, gate)


def _ffn_kernel(x_ref, nw_ref, sh_ref, sc_ref, g_ref, w1_ref, w2_ref, o_ref, *, tf):
    x = x_ref[...]
    hb = (_rms(x, nw_ref[...]) * (1.0 + sc_ref[...]) + sh_ref[...]).astype(BF16)
    f = w2_ref.shape[0]
    acc = jnp.zeros(x.shape, F32)
    for j in range(f // tf):
        gate = _dot(hb, w1_ref[:, j * tf:(j + 1) * tf])
        up = _dot(hb, w1_ref[:, f + j * tf:f + (j + 1) * tf])
        acc = acc + _dot((_silu(gate) * up).astype(BF16), w2_ref[j * tf:(j + 1) * tf, :])
    o_ref[...] = x + g_ref[...] * acc


def _ffn(x, nw, shift, scale, gate, w1_bf16, w2_bf16, *, tm, tf):
    bsz, s, d = x.shape
    f = w2_bf16.shape[0]
    mod = pl.BlockSpec((None, 1, d), lambda b, i: (b, 0, 0))
    return pl.pallas_call(
        functools.partial(_ffn_kernel, tf=tf),
        grid=(bsz, s // tm),
        in_specs=[
            pl.BlockSpec((None, tm, d), lambda b, i: (b, i, 0)),
            pl.BlockSpec((1, d), lambda b, i: (0, 0)),
            mod, mod, mod,
            _resident((d, 2 * f), lambda b, i: (0, 0)),
            _resident((f, d), lambda b, i: (0, 0)),
        ],
        out_specs=pl.BlockSpec((None, tm, d), lambda b, i: (b, i, 0)),
        out_shape=jax.ShapeDtypeStruct((bsz, s, d), F32),
        compiler_params=_cparams(("arbitrary", "arbitrary")),
        name="ffn",
    )(x, nw.reshape(1, d), shift, scale, gate, w1_bf16, w2_bf16)


HALO = max(POOL_WINDOWS)


def _cd_prep_kernel(u_ref, halo_ref, sq_ref, sk_ref, sv_ref, pw_ref, ps_ref, qn_ref, kn_ref,
                    pool_o, q_o, k_o, v_o, ext_ref, *, ts):
    i = pl.program_id(1)
    halo = halo_ref[...]
    ext_ref[0:HALO, :] = jnp.where(i == 0, jnp.zeros_like(halo), halo)
    ext_ref[HALO:HALO + ts, :] = u_ref[...]
    t = i * ts + lax.broadcasted_iota(I32, (ts, 1), 0)
    for g, w in enumerate(POOL_WINDOWS):
        gs = slice(g * POOL_CH, (g + 1) * POOL_CH)
        tot = ext_ref[HALO:HALO + ts, gs]
        for j in range(1, w):
            tot = tot + ext_ref[HALO - j:HALO - j + ts, gs]
        cnt = jnp.minimum(t + 1, w).astype(F32)
        pooled = tot / cnt - u_ref[:, gs]
        y = _dot(pooled.astype(BF16), pw_ref[g].astype(BF16)) * ps_ref[:, gs]
        pool_o[:, gs] = y.astype(pool_o.dtype)
    qn, kn = qn_ref[...], kn_ref[...]
    for h in range(SB_HEADS):
        hs = slice(h * SB_HEAD_DIM, (h + 1) * SB_HEAD_DIM)
        q_o[:, hs] = _rms(sq_ref[:, hs], qn).astype(q_o.dtype)
        k_o[:, hs] = _rms(sk_ref[:, hs], kn).astype(k_o.dtype)
    v_o[...] = sv_ref[...].astype(v_o.dtype)


def _cd_prep(proj, pool_w, pool_scale, q_norm, k_norm, *, ts):
    bsz, s, _ = proj.shape
    n = POOL_GROUPS * POOL_CH
    wide = lambda j: pl.BlockSpec((None, ts, n), lambda b, i: (b, i, j))
    vec = lambda m: pl.BlockSpec((1, m), lambda b, i: (0, 0))
    per_blk = ts // HALO
    return pl.pallas_call(
        functools.partial(_cd_prep_kernel, ts=ts),
        grid=(bsz, s // ts),
        in_specs=[
            wide(0),
            pl.BlockSpec((None, HALO, n), lambda b, i: (b, jnp.maximum(i * per_blk - 1, 0), 0)),
            wide(1), wide(2), wide(3),
            pl.BlockSpec((POOL_GROUPS, POOL_CH, POOL_CH), lambda b, i: (0, 0, 0)),
            vec(n), vec(SB_HEAD_DIM), vec(SB_HEAD_DIM),
        ],
        out_specs=[wide(0)] * 4,
        out_shape=[jax.ShapeDtypeStruct((bsz, s, n), BF16)] * 4,
        scratch_shapes=[pltpu.VMEM((HALO + ts, n), F32)],
        compiler_params=_cparams(("arbitrary", "arbitrary")),
        name="cd_prep",
    )(proj, proj, proj, proj, proj, pool_w, pool_scale.reshape(1, n),
      q_norm.reshape(1, -1), k_norm.reshape(1, -1))


def _sb_kernel(q_ref, k_ref, v_ref, o_ref, acc_ref, run_ref, *, tq):
    i = pl.program_id(1)
    r = lax.broadcasted_iota(I32, (tq, tq), 0)
    c = lax.broadcasted_iota(I32, (tq, tq), 1)
    later = jnp.where(r > c, 1.0, 0.0).astype(BF16)
    scale = SB_HEAD_DIM ** -0.5
    acc_ref[...] = jnp.zeros_like(acc_ref)
    run_ref[...] = jnp.zeros_like(run_ref)

    def block(state):
        kb, _ = state
        k0 = pl.multiple_of(kb * tq, tq)
        strict = (kb * tq + c) < (i * tq + r)
        live = jnp.float32(-jnp.inf)
        for h in range(SB_HEADS):
            hs = slice(h * SB_HEAD_DIM, (h + 1) * SB_HEAD_DIM)
            z = _dot_nt(q_ref[:, hs], k_ref[pl.ds(k0, tq), hs]) * scale
            sp = jnp.log1p(jnp.exp(-jnp.abs(z)))
            log_beta = jnp.minimum(z, 0.0) - sp
            log_1m = jnp.where(strict, -jnp.maximum(z, 0.0) - sp, 0.0)
            hi = log_1m.astype(BF16)
            lo = (log_1m - hi.astype(F32)).astype(BF16)
            run = run_ref[h]
            after = _dot(hi, later) + _dot(lo, later) + run
            w = jnp.where(strict, jnp.exp(log_beta + after), 0.0)
            acc_ref[h] = acc_ref[h] + _dot(w.astype(BF16), v_ref[pl.ds(k0, tq), hs])
            run = run + jnp.sum(log_1m, axis=1, keepdims=True)
            run_ref[h] = run
            live = jnp.maximum(live, jnp.max(run))
        return kb - 1, live > SB_DEAD_LOG

    lax.while_loop(lambda st: (st[0] >= 0) & st[1], block, (i, jnp.bool_(True)))
    for h in range(SB_HEADS):
        hs = slice(h * SB_HEAD_DIM, (h + 1) * SB_HEAD_DIM)
        o_ref[:, hs] = acc_ref[h].astype(o_ref.dtype)


def _sb(q, k, v, *, tq):
    bsz, s, hd = q.shape
    resident = _resident((None, s, hd), lambda b, i: (b, 0, 0))
    return pl.pallas_call(
        functools.partial(_sb_kernel, tq=tq),
        grid=(bsz, s // tq),
        in_specs=[pl.BlockSpec((None, tq, hd), lambda b, i: (b, i, 0)), resident, resident],
        out_specs=pl.BlockSpec((None, tq, hd), lambda b, i: (b, i, 0)),
        out_shape=jax.ShapeDtypeStruct((bsz, s, hd), BF16),
        scratch_shapes=[
            pltpu.VMEM((SB_HEADS, tq, SB_HEAD_DIM), F32),
            pltpu.VMEM((SB_HEADS, tq, 1), F32),
        ],
        compiler_params=_cparams(("arbitrary", "arbitrary")),
        name="stick_breaking",
    )(q, k, v)


def _pack_ab_weight(w):
    d = w.shape[0]
    gq, gk, gv, glow, gr, dq, dk, dv, iq, ik, iw = jnp.split(
        w, [256, 512, 1024, 1040, 1552, 2064, 2576, 3088, 3600, 3664], axis=1)
    pad = jnp.zeros((d, AB_NPAD - AB_SMALL - (IDX_DIM + GLA_GATE_RANK + IDX_HEADS)), w.dtype)
    return jnp.concatenate([gq, gk, gv, gr, dq, dk, dv, iq, ik, glow, iw, pad], axis=1)


def kernel(x, c, positions, ada_w, ada_b, mix_norm, ffn_norm, ffn_w1, ffn_w2, ab_w_in, gla_gate_up,
           gla_gate_b, gla_out_norm, dsa_q_norm, dsa_k_norm, ab_w_out, cd_w_in, pool_w, pool_scale,
           sb_q_norm, sb_k_norm, cd_w_out):
    bsz, s, d = x.shape
    depth = ada_w.shape[0]
    mod = _ada_mod(c, ada_w, ada_b).reshape(depth, bsz, 6, 1, d)
    tm = min(512, s)
    tq = min(256, s)
    for layer in range(depth):
        sh1, sc1, g1, sh2, sc2, g2 = (mod[layer, :, j] for j in range(6))
        i = layer // 2
        if layer % 2 == 0:
            proj = _norm_proj(x, mix_norm[layer], sh1, sc1, _pack_ab_weight(ab_w_in[i]).astype(BF16),
                              tm=min(256, s), tn=768)
            mix_a = _gla(proj, gla_gate_up[i], gla_gate_b[i], gla_out_norm[i], ts=tm)
            q, k, v, iq, ik, iw = _dsa_prep(proj, positions, dsa_q_norm[i], dsa_k_norm[i], ts=tm)
            mix_b = _dsa(q, k, v, iq, ik, iw, tq=tq)
            w_out = ab_w_out[i]
        else:
            proj = _norm_proj(x, mix_norm[layer], sh1, sc1, cd_w_in[i].astype(BF16), tm=tm, tn=512)
            mix_a, q, k, v = _cd_prep(proj, pool_w[i], pool_scale[i], sb_q_norm[i], sb_k_norm[i], ts=tm)
            mix_b = _sb(q, k, v, tq=tq)
            w_out = cd_w_out[i]
        x = _out_proj(x, mix_a, mix_b, w_out.astype(BF16), g1, tm=tm)
        x = _ffn(x, ffn_norm[layer], sh2, sc2, g2, ffn_w1[layer].astype(BF16),
                 ffn_w2[layer].astype(BF16), tm=tm, tf=256)
    return x
```

```python
import functools

import jax
import jax.numpy as jnp
from jax import lax
from jax.experimental import pallas as pl
from jax.experimental.pallas import tpu as pltpu

F32 = jnp.float32
BF16 = jnp.bfloat16
I32 = jnp.int32

D_MODEL = 1024
GLA_HEADS, GLA_DK, GLA_DV = 4, 64, 128
GLA_GATE_RANK = 16
GLA_GATE_TAU = 16.0
GLA_CHUNK = 64
DSA_HEADS, DSA_HEAD_DIM = 4, 128
IDX_HEADS, IDX_DIM = 8, 64
DSA_TOPK = 256
POOL_WINDOWS = (2, 4, 8, 16)
POOL_GROUPS, POOL_CH = 4, 128
SB_HEADS, SB_HEAD_DIM = 4, 128
ROPE_THETA = 10000.0
NORM_EPS = 1e-6
D_FF = 2816

LANES = 128
VMEM_LIMIT = 56 * 1024 * 1024

AB_GQ, AB_GK, AB_GV, AB_GR = 0, 256, 512, 1024
AB_DQ, AB_DK, AB_DV, AB_IQ = 1536, 2048, 2560, 3072
AB_SMALL = 3584
SM_IK, SM_GLOW, SM_IW = 0, 64, 80
AB_NPAD = 3840

INT_MIN = -2 ** 31
MASK_BIAS = -1e30
SB_DEAD_LOG = -110.0


def _dot(a, b):
    return jnp.dot(a, b, preferred_element_type=F32)


def _dot_nt(a, b):
    return lax.dot_general(a, b, (((1,), (1,)), ((), ())), preferred_element_type=F32)


def _dot_tn(a, b):
    return lax.dot_general(a, b, (((0,), (0,)), ((), ())), preferred_element_type=F32)


def _split3(a):
    hi = a.astype(BF16)
    r1 = a - hi.astype(F32)
    mid = r1.astype(BF16)
    lo = (r1 - mid.astype(F32)).astype(BF16)
    return hi, mid, lo


def _silu(x):
    return x * jax.nn.sigmoid(x)


def _rms(x, w):
    var = jnp.mean(x * x, axis=-1, keepdims=True)
    return x * lax.rsqrt(var + NORM_EPS) * w


def _cparams(sem):
    return pltpu.CompilerParams(dimension_semantics=sem, vmem_limit_bytes=VMEM_LIMIT)


def _resident(block_shape, index_map):
    return pl.BlockSpec(block_shape, index_map, pipeline_mode=pl.Buffered(1))


def _ada_kernel(c_ref, w_ref, b_ref, o_ref):
    cond = _silu(c_ref[...])
    c_hi, c_mid, _ = _split3(cond)
    w_hi, w_mid, _ = _split3(w_ref[...])
    acc = _dot(c_hi, w_hi) + _dot(c_hi, w_mid) + _dot(c_mid, w_hi)
    o_ref[...] = acc + b_ref[...]


def _ada_mod(c, ada_w, ada_b):
    depth, d, n = ada_w.shape
    bsz = c.shape[0]
    rows = 16
    cp =jnp.zeros((rows, d), F32).at[:bsz].set(c)
    tn = 1536
    out = pl.pallas_call(
        _ada_kernel,
        grid=(depth, n // tn),
        in_specs=[
            pl.BlockSpec((rows, d), lambda l, j: (0, 0)),
            pl.BlockSpec((None, d, tn), lambda l, j: (l, 0, j)),
            pl.BlockSpec((None, 1, tn), lambda l, j: (l, 0, j)),
        ],
        out_specs=pl.BlockSpec((None, rows, tn), lambda l, j: (l, 0, j)),
        out_shape=jax.ShapeDtypeStruct((depth, rows, n), F32),
        compiler_params=_cparams(("arbitrary", "arbitrary")),
        name="ada_mod",
    )(cp, ada_w, ada_b.reshape(depth, 1, n))
    return out[:, :bsz]


def _norm_proj_kernel(x_ref, nw_ref, sh_ref, sc_ref, w_ref, o_ref, *, tn):
    h = _rms(x_ref[...], nw_ref[...]) * (1.0 + sc_ref[...]) + sh_ref[...]
    hb = h.astype(BF16)
    n = w_ref.shape[1]
    for j in range(n // tn):
        o_ref[:, j * tn:(j + 1) * tn] = _dot(hb, w_ref[:, j * tn:(j + 1) * tn])


def _norm_proj(x, nw, shift, scale, w_bf16, *, tm, tn):
    bsz, s, d = x.shape
    n = w_bf16.shape[1]
    return pl.pallas_call(
        functools.partial(_norm_proj_kernel, tn=tn),
        grid=(bsz, s // tm),
        in_specs=[
            pl.BlockSpec((None, tm, d), lambda b, i: (b, i, 0)),
            pl.BlockSpec((1, d), lambda b, i: (0, 0)),
            pl.BlockSpec((None, 1, d), lambda b, i: (b, 0, 0)),
            pl.BlockSpec((None, 1, d), lambda b, i: (b, 0, 0)),
            _resident((d, n), lambda b, i: (0, 0)),
        ],
        out_specs=pl.BlockSpec((None, tm, n), lambda b, i: (b, i, 0)),
        out_shape=jax.ShapeDtypeStruct((bsz, s, n), F32),
        compiler_params=_cparams(("arbitrary", "arbitrary")),
        name="norm_proj",
    )(x, nw.reshape(1, d), shift, scale, w_bf16)


def _gla_kernel(q_ref, k_ref, v_ref, gr_ref, sm_ref, gup_ref, gb_ref, onw_ref, o_ref, st_ref, *, ts):
    c = GLA_CHUNK

    @pl.when(pl.program_id(1) == 0)
    def _():
        st_ref[...] = jnp.zeros_like(st_ref)

    row = lax.broadcasted_iota(I32, (c, c), 0)
    col = lax.broadcasted_iota(I32, (c, c), 1)
    causal = col <= row
    tri = jnp.where(causal, 1.0, 0.0).astype(BF16)
    gup = gup_ref[...].astype(BF16)
    gbias = gb_ref[...]
    onw = onw_ref[...]

    def chunk(ci, carry):
        r0 = pl.multiple_of(ci * c, c)
        q = q_ref[pl.ds(r0, c), :]
        k = k_ref[pl.ds(r0, c), :]
        v = v_ref[pl.ds(r0, c), :]
        gr = gr_ref[pl.ds(r0, c), :]
        glow = sm_ref[pl.ds(r0, c), SM_GLOW:SM_GLOW + GLA_GATE_RANK]
        a = _dot(glow.astype(BF16), gup) + gbias
        g = (jnp.minimum(a, 0.0) - jnp.log1p(jnp.exp(-jnp.abs(a)))) / GLA_GATE_TAU
        g_hi, g_mid, g_lo = _split3(g)
        b = _dot(tri, g_hi) + _dot(tri, g_mid) + _dot(tri, g_lo)
        b_mid = b[c // 2 - 1:c // 2, :]
        b_last = b[c - 1:c, :]
        qs = q * (GLA_DK ** -0.5)
        qe = (qs * jnp.exp(b - b_mid)).astype(BF16)
        ke = (k * jnp.exp(b_mid - b)).astype(BF16)
        kd = (k * jnp.exp(b_last - b)).astype(BF16)
        qd = (qs * jnp.exp(b)).astype(BF16)
        dec = jnp.exp(b_last)
        for h in range(GLA_HEADS):
            ks = slice(h * GLA_DK, (h + 1) * GLA_DK)
            vs = slice(h * GLA_DV, (h + 1) * GLA_DV)
            att = jnp.where(causal, _dot_nt(qe[:, ks], ke[:, ks]), 0.0)
            vh = v[:, vs].astype(BF16)
            st = st_ref[h]
            o = _dot(att.astype(BF16), vh) + _dot_nt(qd[:, ks], st.astype(BF16))
            st_ref[h] = st * dec[:, ks] + _dot_tn(vh, kd[:, ks])
            grh = gr[:, vs]
            o_ref[pl.ds(r0, c), vs] = (_rms(o, onw) * _silu(grh)).astype(o_ref.dtype)
        return carry

    lax.fori_loop(0, ts // c, chunk, 0)


def _gla(proj, gate_up, gate_b, out_norm, *, ts):
    bsz, s, _ = proj.shape
    hk, hv = GLA_HEADS * GLA_DK, GLA_HEADS * GLA_DV
    return pl.pallas_call(
        functools.partial(_gla_kernel, ts=ts),
        grid=(bsz, s // ts),
        in_specs=[
            pl.BlockSpec((None, ts, hk), lambda b, i: (b, i, AB_GQ // hk)),
            pl.BlockSpec((None, ts, hk), lambda b, i: (b, i, AB_GK // hk)),
            pl.BlockSpec((None, ts, hv), lambda b, i: (b, i, AB_GV // hv)),
            pl.BlockSpec((None, ts, hv), lambda b, i: (b, i, AB_GR // hv)),
            pl.BlockSpec((None, ts, LANES), lambda b, i: (b, i, AB_SMALL // LANES)),
            pl.BlockSpec((GLA_GATE_RANK, hk), lambda b, i: (0, 0)),
            pl.BlockSpec((1, hk), lambda b, i: (0, 0)),
            pl.BlockSpec((1, GLA_DV), lambda b, i: (0, 0)),
        ],
        out_specs=pl.BlockSpec((None, ts, hv), lambda b, i: (b, i, 0)),
        out_shape=jax.ShapeDtypeStruct((bsz, s, hv), BF16),
        scratch_shapes=[pltpu.VMEM((GLA_HEADS, GLA_DV, GLA_DK), F32)],
        compiler_params=_cparams(("arbitrary", "arbitrary")),
        name="gla",
    )(proj, proj, proj, proj, proj, gate_up, gate_b.reshape(1, hk), out_norm.reshape(1, GLA_DV))


def _dsa_prep_kernel(dq_ref, dk_ref, dv_ref, iq_ref, sm_ref, pos_ref, qn_ref, kn_ref, fa_ref, fi_ref,
                     q_o, k_o, v_o, iq_o, ik_o, iw_o):
    pos = pos_ref[...].astype(F32)
    lane = lax.broadcasted_iota(I32, (1, LANES), 1)

    ang = pos * fa_ref[...]
    cos_a, sin_a = jnp.cos(ang), jnp.sin(ang)
    sin_a = jnp.where(lane < DSA_HEAD_DIM // 2, -sin_a, sin_a)

    def rope_attn(t):
        return t * cos_a + pltpu.roll(t, DSA_HEAD_DIM // 2, 1) * sin_a

    ang = pos * fi_ref[...]
    cos_i, sin_i = jnp.cos(ang), jnp.sin(ang)
    first = (lane % IDX_DIM) < IDX_DIM // 2
    sin_i = jnp.where(first, -sin_i, sin_i)

    def rope_idx(t):
        rot = jnp.where(first, pltpu.roll(t, LANES - IDX_DIM // 2, 1), pltpu.roll(t, IDX_DIM // 2, 1))
        return t * cos_i + rot * sin_i

    qn, kn = qn_ref[...], kn_ref[...]
    for h in range(DSA_HEADS):
        hs = slice(h * DSA_HEAD_DIM, (h + 1) * DSA_HEAD_DIM)
        q_o[:, hs] = rope_attn(_rms(dq_ref[:, hs], qn)).astype(q_o.dtype)
        k_o[:, hs] = rope_attn(_rms(dk_ref[:, hs], kn)).astype(k_o.dtype)
    tk = v_o.shape[-1]
    for j in range(v_o.shape[0]):
        v_o[j] = dv_ref[j * tk:(j + 1) * tk, :].T.astype(v_o.dtype)
    for j in range(IDX_HEADS * IDX_DIM // LANES):
        r = rope_idx(iq_ref[:, j * LANES:(j + 1) * LANES])
        iq_o[2 * j] = r[:, :IDX_DIM].astype(iq_o.dtype)
        iq_o[2 * j + 1] = r[:, IDX_DIM:].astype(iq_o.dtype)
    sm = sm_ref[...]
    ik_o[...] = rope_idx(sm)[:, SM_IK:SM_IK + IDX_DIM].astype(ik_o.dtype)
    idx_scale = (IDX_HEADS ** -0.5) * (IDX_DIM ** -0.5)
    iw_o[...] = sm.T[SM_IW:SM_IW + IDX_HEADS, :] * idx_scale


def _dsa_prep(proj, positions, q_norm, k_norm, *, ts, tk):
    bsz, s, _ = proj.shape
    hd = DSA_HEADS * DSA_HEAD_DIM
    half_a, half_i = DSA_HEAD_DIM // 2, IDX_DIM // 2
    inv_a = ROPE_THETA ** (-jnp.arange(half_a, dtype=F32) / half_a)
    inv_i = ROPE_THETA ** (-jnp.arange(half_i, dtype=F32) / half_i)
    fa = jnp.tile(inv_a, LANES // half_a).reshape(1, LANES)
    fi = jnp.tile(inv_i, LANES // half_i).reshape(1, LANES)
    wide = lambda j: pl.BlockSpec((None, ts, hd), lambda b, i: (b, i, j))
    vec = lambda n: pl.BlockSpec((1, n), lambda b, i: (0, 0))
    return pl.pallas_call(
        _dsa_prep_kernel,
        grid=(bsz, s // ts),
        in_specs=[
            wide(AB_DQ // hd), wide(AB_DK // hd), wide(AB_DV // hd), wide(AB_IQ // hd),
            pl.BlockSpec((None, ts, LANES), lambda b, i: (b, i, AB_SMALL // LANES)),
            pl.BlockSpec((None, ts, 1), lambda b, i: (b, i, 0)),
            vec(DSA_HEAD_DIM), vec(DSA_HEAD_DIM), vec(LANES), vec(LANES),
        ],
        out_specs=[
            wide(0), wide(0),
            pl.BlockSpec((None, ts // tk, hd, tk), lambda b, i: (b, i, 0, 0)),
            pl.BlockSpec((None, IDX_HEADS, ts, IDX_DIM), lambda b, i: (b, 0, i, 0)),
            pl.BlockSpec((None, ts, IDX_DIM), lambda b, i: (b, i, 0)),
            pl.BlockSpec((None, IDX_HEADS, ts), lambda b, i: (b, 0, i)),
        ],
        out_shape=[
            jax.ShapeDtypeStruct((bsz, s, hd), BF16),
            jax.ShapeDtypeStruct((bsz, s, hd), BF16),
            jax.ShapeDtypeStruct((bsz, s // tk, hd, tk), BF16),
            jax.ShapeDtypeStruct((bsz, IDX_HEADS, s, IDX_DIM), BF16),
            jax.ShapeDtypeStruct((bsz, s, IDX_DIM), BF16),
            jax.ShapeDtypeStruct((bsz, IDX_HEADS, s), F32),
        ],
        compiler_params=_cparams(("arbitrary", "arbitrary")),
        name="dsa_prep",
    )(proj, proj, proj, proj, proj, positions.reshape(bsz, s, 1),
      q_norm.reshape(1, -1), k_norm.reshape(1, -1), fa, fi)


def _dsa_kernel(q_ref, iq_ref, iw_ref, k_ref, vt_ref, ik_ref, o_ref, keys_ref, acc_ref, *, tq, n_sel, s_len):
    i = pl.program_id(1)
    nkb = i + 1
    kofs = lax.broadcasted_iota(I32, (tq, tq), 0)
    qidx = i * tq + lax.broadcasted_iota(I32, (tq, tq), 1)
    iw = iw_ref[...]

    def scores(kb, carry):
        k0 = pl.multiple_of(kb * tq, tq)
        ikb = ik_ref[pl.ds(k0, tq), :]
        sc = jnp.zeros((tq, tq), F32)
        for h in range(IDX_HEADS):
            sc = sc + jnp.maximum(_dot_nt(ikb, iq_ref[h]), 0.0) * iw[h:h + 1, :]
        bits = pltpu.bitcast(sc, I32)
        bits = jnp.where(bits == INT_MIN, 0, bits)
        key = bits ^ ((bits >> 31) & 0x7FFFFFFF)
        keys_ref[kb] = jnp.where(kb * tq + kofs <= qidx, key, INT_MIN)
        return carry

    lax.fori_loop(0, nkb, scores, 0)

    def count(pred):
        def body(kb, acc):
            m = jnp.where(pred(keys_ref[kb], kb * tq + kofs), 1, 0)
            return acc + jnp.sum(m.reshape(tq // 8, 8, tq), axis=0)
        acc = lax.fori_loop(0, nkb, body, jnp.zeros((8, tq), I32))
        return jnp.sum(acc, axis=0, keepdims=True)

    n_valid = qidx[0:1, :] + 1

    def bit_step(st):
        it, u, cnt_u = st
        cand_u = u | lax.shift_left(jnp.int32(1), jnp.int32(31) - it)
        cand = cand_u ^ INT_MIN
        cnt = count(lambda key, c: key >= cand)
        take = cnt >= n_sel
        return it + 1, jnp.where(take, cand_u, u), jnp.where(take, cnt, cnt_u)

    _, u, cnt_u = lax.while_loop(lambda st: (st[0] < 32) & (jnp.max(st[2]) > n_sel), bit_step,
                                 (jnp.int32(0), jnp.zeros((1, tq), I32), n_valid))
    thr = jnp.maximum(u ^ INT_MIN, INT_MIN + 1)
    excess = cnt_u > n_sel

    def tie_limit():
        need = n_sel - count(lambda key, c: key > thr)
        nbits = s_len.bit_length()

        def step(it, jv):
            cand = jv | lax.shift_left(jnp.int32(1), jnp.int32(nbits - 1) - it)
            cnt = count(lambda key, c: (key == thr) & (c < cand))
            return jnp.where(cnt < need, cand, jv)
        jv = lax.fori_loop(0, nbits, step, jnp.zeros((1, tq), I32))
        return jnp.where(excess, jv, s_len)

    jstar = lax.cond(jnp.max(cnt_u) > n_sel, tie_limit, lambda: jnp.full((1, tq), s_len, I32))

    acc_ref[...] = jnp.zeros_like(acc_ref)
    att_scale = DSA_HEAD_DIM ** -0.5

    def attend(kb, carry):
        ms, ls = carry
        k0 = pl.multiple_of(kb * tq, tq)
        key = keys_ref[kb]
        sel = (key > thr) | ((key == thr) & (kb * tq + kofs <= jstar))
        bias = jnp.where(sel, 0.0, MASK_BIAS)
        heads = [slice(h * DSA_HEAD_DIM, (h + 1) * DSA_HEAD_DIM) for h in range(DSA_HEADS)]
        st = [_dot_nt(k_ref[pl.ds(k0, tq), hs], q_ref[:, hs]) for hs in heads]
        st = [x * att_scale + bias for x in st]
        new_m = [jnp.maximum(m, jnp.max(x, axis=0, keepdims=True)) for m, x in zip(ms, st)]
        alpha = [jnp.exp(m - mn) for m, mn in zip(ms, new_m)]
        p = [jnp.exp(x - mn) for x, mn in zip(st, new_m)]
        new_l = [a * l + jnp.sum(x, axis=0, keepdims=True) for a, l, x in zip(alpha, ls, p)]
        pv = [_dot(vt_ref[kb, hs, :], x.astype(BF16)) for hs, x in zip(heads, p)]
        for h in range(DSA_HEADS):
            acc_ref[h] = alpha[h] * acc_ref[h] + pv[h]
        return tuple(new_m), tuple(new_l)

    init = (tuple(jnp.full((1, tq), -jnp.inf, F32) for _ in range(DSA_HEADS)),
            tuple(jnp.zeros((1, tq), F32) for _ in range(DSA_HEADS)))
    _, ls = lax.fori_loop(0, nkb, attend, init)
    for h in range(DSA_HEADS):
        hs = slice(h * DSA_HEAD_DIM, (h + 1) * DSA_HEAD_DIM)
        o_ref[:, hs] = (acc_ref[h] / ls[h]).T.astype(o_ref.dtype)


def _dsa(q, k, vt, iq, ik, iw, *, tq):
    bsz, s, hd = q.shape
    n_sel = min(DSA_TOPK, s // 4)
    resident = lambda n: _resident((None, s, n), lambda b, i: (b, 0, 0))
    return pl.pallas_call(
        functools.partial(_dsa_kernel, tq=tq, n_sel=n_sel, s_len=s),
        grid=(bsz, s // tq),
        in_specs=[
            pl.BlockSpec((None, tq, hd), lambda b, i: (b, i, 0)),
            pl.BlockSpec((None, IDX_HEADS, tq, IDX_DIM), lambda b, i: (b, 0, i, 0)),
            pl.BlockSpec((None, IDX_HEADS, tq), lambda b, i: (b, 0, i)),
            resident(hd),
            _resident((None, s // tq, hd, tq), lambda b, i: (b, 0, 0, 0)),
            resident(IDX_DIM),
        ],
        out_specs=pl.BlockSpec((None, tq, hd), lambda b, i: (b, i, 0)),
        out_shape=jax.ShapeDtypeStruct((bsz, s, hd), BF16),
        scratch_shapes=[
            pltpu.VMEM((s // tq, tq, tq), I32),
            pltpu.VMEM((DSA_HEADS, DSA_HEAD_DIM, tq), F32),
        ],
        compiler_params=_cparams(("arbitrary", "arbitrary")),
        name="dsa",
    )(q, iq, iw, k, vt, ik)


def _out_proj_kernel(x_ref, a_ref, b_ref, wa_ref, wb_ref, g_ref, o_ref):
    y = _dot(a_ref[...], wa_ref[...]) + _dot(b_ref[...], wb_ref[...])
    o_ref[...] = x_ref[...] + g_ref[...] * y


def _out_proj(x, mix_a, mix_b, w_out_bf16, gate, *, tm):
    bsz, s, d = x.shape
    na = mix_a.shape[-1]
    assert mix_b.shape[-1] == na and w_out_bf16.shape[0] == 2 * na
    return pl.pallas_call(
        _out_proj_kernel,
        grid=(bsz, s // tm),
        in_specs=[
            pl.BlockSpec((None, tm, d), lambda b, i: (b, i, 0)),
            pl.BlockSpec((None, tm, na), lambda b, i: (b, i, 0)),
            pl.BlockSpec((None, tm, na), lambda b, i: (b, i, 0)),
            _resident((na, d), lambda b, i: (0, 0)),
            _resident((na, d), lambda b, i: (1, 0)),
            pl.BlockSpec((None, 1, d), lambda b, i: (b, 0, 0)),
        ],
        out_specs=pl.BlockSpec((None, tm, d), lambda b, i: (b, i, 0)),
        out_shape=jax.ShapeDtypeStruct((bsz, s, d), F32),
        compiler_params=_cparams(("arbitrary", "arbitrary")),
        name="out_proj",
    )(x, mix_a, mix_b, w_out_bf16, w_out_bf16, gate)


def _ffn_kernel(x_ref, nw_ref, sh_ref, sc_ref, g_ref, w1_ref, w2_ref, o_ref, *, tf):
    x = x_ref[...]
    hb = (_rms(x, nw_ref[...]) * (1.0 + sc_ref[...]) + sh_ref[...]).astype(BF16)
    f = w2_ref.shape[0]
    acc = jnp.zeros(x.shape, F32)
    for j in range(f // tf):
        gate = _dot(hb, w1_ref[:, j * tf:(j + 1) * tf])
        up = _dot(hb, w1_ref[:, f + j * tf:f + (j + 1) * tf])
        acc = acc + _dot((_silu(gate) * up).astype(BF16), w2_ref[j * tf:(j + 1) * tf, :])
    o_ref[...] = x + g_ref[...] * acc


def _ffn(x, nw, shift, scale, gate, w1_bf16, w2_bf16, *, tm, tf):
    bsz, s, d = x.shape
    f = w2_bf16.shape[0]
    mod = pl.BlockSpec((None, 1, d), lambda b, i: (b, 0, 0))
    return pl.pallas_call(
        functools.partial(_ffn_kernel, tf=tf),
        grid=(bsz, s // tm),
        in_specs=[
            pl.BlockSpec((None, tm, d), lambda b, i: (b, i, 0)),
            pl.BlockSpec((1, d), lambda b, i: (0, 0)),
            mod, mod, mod,
            _resident((d, 2 * f), lambda b, i: (0, 0)),
            _resident((f, d), lambda b, i: (0, 0)),
        ],
        out_specs=pl.BlockSpec((None, tm, d), lambda b, i: (b, i, 0)),
        out_shape=jax.ShapeDtypeStruct((bsz, s, d), F32),
        compiler_params=_cparams(("arbitrary", "arbitrary")),
        name="ffn",
    )(x, nw.reshape(1, d), shift, scale, gate, w1_bf16, w2_bf16)


HALO = max(POOL_WINDOWS)


def _cd_prep_kernel(u_ref, halo_ref, sq_ref, sk_ref, sv_ref, pw_ref, ps_ref, qn_ref, kn_ref,
                    pool_o, q_o, k_o, v_o, ext_ref, *, ts):
    i = pl.program_id(1)
    halo = halo_ref[...]
    ext_ref[0:HALO, :] = jnp.where(i == 0, jnp.zeros_like(halo), halo)
    ext_ref[HALO:HALO + ts, :] = u_ref[...]
    t = i * ts + lax.broadcasted_iota(I32, (ts, 1), 0)
    for g, w in enumerate(POOL_WINDOWS):
        gs = slice(g * POOL_CH, (g + 1) * POOL_CH)
        tot = ext_ref[HALO:HALO + ts, gs]
        for j in range(1, w):
            tot = tot + ext_ref[HALO - j:HALO - j + ts, gs]
        cnt = jnp.minimum(t + 1, w).astype(F32)
        pooled = tot / cnt - u_ref[:, gs]
        y = _dot(pooled.astype(BF16), pw_ref[g].astype(BF16)) * ps_ref[:, gs]
        pool_o[:, gs] = y.astype(pool_o.dtype)
    qn, kn = qn_ref[...], kn_ref[...]
    for h in range(SB_HEADS):
        hs = slice(h * SB_HEAD_DIM, (h + 1) * SB_HEAD_DIM)
        q_o[:, hs] = _rms(sq_ref[:, hs], qn).astype(q_o.dtype)
        k_o[:, hs] = _rms(sk_ref[:, hs], kn).astype(k_o.dtype)
    v_o[...] = sv_ref[...].astype(v_o.dtype)


def _cd_prep(proj, pool_w, pool_scale, q_norm, k_norm, *, ts):
    bsz, s, _ = proj.shape
    n = POOL_GROUPS * POOL_CH
    wide = lambda j: pl.BlockSpec((None, ts, n), lambda b, i: (b, i, j))
    vec = lambda m: pl.BlockSpec((1, m), lambda b, i: (0, 0))
    per_blk = ts // HALO
    return pl.pallas_call(
        functools.partial(_cd_prep_kernel, ts=ts),
        grid=(bsz, s // ts),
        in_specs=[
            wide(0),
            pl.BlockSpec((None, HALO, n), lambda b, i: (b, jnp.maximum(i * per_blk - 1, 0), 0)),
            wide(1), wide(2), wide(3),
            pl.BlockSpec((POOL_GROUPS, POOL_CH, POOL_CH), lambda b, i: (0, 0, 0)),
            vec(n), vec(SB_HEAD_DIM), vec(SB_HEAD_DIM),
        ],
        out_specs=[wide(0)] * 4,
        out_shape=[jax.ShapeDtypeStruct((bsz, s, n), BF16)] * 4,
        scratch_shapes=[pltpu.VMEM((HALO + ts, n), F32)],
        compiler_params=_cparams(("arbitrary", "arbitrary")),
        name="cd_prep",
    )(proj, proj, proj, proj, proj, pool_w, pool_scale.reshape(1, n),
      q_norm.reshape(1, -1), k_norm.reshape(1, -1))


def _sb_kernel(q_ref, k_ref, v_ref, o_ref, acc_ref, run_ref, *, tq):
    i = pl.program_id(1)
    r = lax.broadcasted_iota(I32, (tq, tq), 0)
    c = lax.broadcasted_iota(I32, (tq, tq), 1)
    later = jnp.where(r > c, 1.0, 0.0).astype(BF16)
    scale = SB_HEAD_DIM ** -0.5
    acc_ref[...] = jnp.zeros_like(acc_ref)
    run_ref[...] = jnp.zeros_like(run_ref)

    def block(state):
        kb, _ = state
        k0 = pl.multiple_of(kb * tq, tq)
        strict = (kb * tq + c) < (i * tq + r)
        live = jnp.float32(-jnp.inf)
        for h in range(SB_HEADS):
            hs = slice(h * SB_HEAD_DIM, (h + 1) * SB_HEAD_DIM)
            z = _dot_nt(q_ref[:, hs], k_ref[pl.ds(k0, tq), hs]) * scale
            sp = jnp.log1p(jnp.exp(-jnp.abs(z)))
            log_beta = jnp.minimum(z, 0.0) - sp
            log_1m = jnp.where(strict, -jnp.maximum(z, 0.0) - sp, 0.0)
            hi = log_1m.astype(BF16)
            lo = (log_1m - hi.astype(F32)).astype(BF16)
            run = run_ref[h]
            after = _dot(hi, later) + _dot(lo, later) + run
            w = jnp.where(strict, jnp.exp(log_beta + after), 0.0)
            acc_ref[h] = acc_ref[h] + _dot(w.astype(BF16), v_ref[pl.ds(k0, tq), hs])
            run = run + jnp.sum(log_1m, axis=1, keepdims=True)
            run_ref[h] = run
            live = jnp.maximum(live, jnp.max(run))
        return kb - 1, live > SB_DEAD_LOG

    lax.while_loop(lambda st: (st[0] >= 0) & st[1], block, (i, jnp.bool_(True)))
    for h in range(SB_HEADS):
        hs = slice(h * SB_HEAD_DIM, (h + 1) * SB_HEAD_DIM)
        o_ref[:, hs] = acc_ref[h].astype(o_ref.dtype)


def _sb(q, k, v, *, tq):
    bsz, s, hd = q.shape
    resident = _resident((None, s, hd), lambda b, i: (b, 0, 0))
    return pl.pallas_call(
        functools.partial(_sb_kernel, tq=tq),
        grid=(bsz, s // tq),
        in_specs=[pl.BlockSpec((None, tq, hd), lambda b, i: (b, i, 0)), resident, resident],
        out_specs=pl.BlockSpec((None, tq, hd), lambda b, i: (b, i, 0)),
        out_shape=jax.ShapeDtypeStruct((bsz, s, hd), BF16),
        scratch_shapes=[
            pltpu.VMEM((SB_HEADS, tq, SB_HEAD_DIM), F32),
            pltpu.VMEM((SB_HEADS, tq, 1), F32),
        ],
        compiler_params=_cparams(("arbitrary", "arbitrary")),
        name="stick_breaking",
    )(q, k, v)


def _pack_ab_weight(w):
    d = w.shape[0]
    gq, gk, gv, glow, gr, dq, dk, dv, iq, ik, iw = jnp.split(
        w, [256, 512, 1024, 1040, 1552, 2064, 2576, 3088, 3600, 3664], axis=1)
    pad = jnp.zeros((d, AB_NPAD - AB_SMALL - (IDX_DIM + GLA_GATE_RANK + IDX_HEADS)), w.dtype)
    return jnp.concatenate([gq, gk, gv, gr, dq, dk, dv, iq, ik, glow, iw, pad], axis=1)


def kernel(x, c, positions, ada_w, ada_b, mix_norm, ffn_norm, ffn_w1, ffn_w2, ab_w_in, gla_gate_up,
           gla_gate_b, gla_out_norm, dsa_q_norm, dsa_k_norm, ab_w_out, cd_w_in, pool_w, pool_scale,
           sb_q_norm, sb_k_norm, cd_w_out):
    bsz, s, d = x.shape
    depth = ada_w.shape[0]
    mod = _ada_mod(c, ada_w, ada_b).reshape(depth, bsz, 6, 1, d)
    tm = min(512, s)
    tq = min(256, s)
    for layer in range(depth):
        sh1, sc1, g1, sh2, sc2, g2 = (mod[layer, :, j] for j in range(6))
        i = layer // 2
        if layer % 2 == 0:
            proj = _norm_proj(x, mix_norm[layer], sh1, sc1, _pack_ab_weight(ab_w_in[i]).astype(BF16),
                              tm=min(256, s), tn=768)
            mix_a = _gla(proj, gla_gate_up[i], gla_gate_b[i], gla_out_norm[i], ts=tm)
            q, k, vt, iq, ik, iw = _dsa_prep(proj, positions, dsa_q_norm[i], dsa_k_norm[i], ts=tm, tk=tq)
            mix_b = _dsa(q, k, vt, iq, ik, iw, tq=tq)
            w_out = ab_w_out[i]
        else:
            proj = _norm_proj(x, mix_norm[layer], sh1, sc1, cd_w_in[i].astype(BF16), tm=tm, tn=512)
            mix_a, q, k, v = _cd_prep(proj, pool_w[i], pool_scale[i], sb_q_norm[i], sb_k_norm[i], ts=tm)
            mix_b = _sb(q, k, v, tq=tq)
            w_out = cd_w_out[i]
        x = _out_proj(x, mix_a, mix_b, w_out.astype(BF16), g1, tm=tm)
        x = _ffn(x, ffn_norm[layer], sh2, sc2, g2, ffn_w1[layer].astype(BF16),
                 ffn_w2[layer].astype(BF16), tm=tm, tf=256)
    return x
```

```python
import functools

import jax
import jax.numpy as jnp
from jax import lax
from jax.experimental import pallas as pl
from jax.experimental.pallas import tpu as pltpu

F32 = jnp.float32
BF16 = jnp.bfloat16
I32 = jnp.int32

D_MODEL = 1024
GLA_HEADS, GLA_DK, GLA_DV = 4, 64, 128
GLA_GATE_RANK = 16
GLA_GATE_TAU = 16.0
GLA_CHUNK = 64
DSA_HEADS, DSA_HEAD_DIM = 4, 128
IDX_HEADS, IDX_DIM = 8, 64
DSA_TOPK = 256
POOL_WINDOWS = (2, 4, 8, 16)
POOL_GROUPS, POOL_CH = 4, 128
SB_HEADS, SB_HEAD_DIM = 4, 128
ROPE_THETA = 10000.0
NORM_EPS = 1e-6
D_FF = 2816

LANES = 128
VMEM_LIMIT = 56 * 1024 * 1024

AB_GQ, AB_GK, AB_GV, AB_GR = 0, 256, 512, 1024
AB_DQ, AB_DK, AB_DV, AB_IQ = 1536, 2048, 2560, 3072
AB_SMALL = 3584
SM_IK, SM_GLOW, SM_IW = 0, 64, 80
AB_NPAD = 3840

DSA_Q_SCALE = 1.4426950408889634 * DSA_HEAD_DIM ** -0.5
INT_MIN = -2 ** 31
MASK_BIAS = -1e30
SB_DEAD_LOG = -110.0


def _dot(a, b):
    return jnp.dot(a, b, preferred_element_type=F32)


def _dot_nt(a, b):
    return lax.dot_general(a, b, (((1,), (1,)), ((), ())), preferred_element_type=F32)


def _dot_tn(a, b):
    return lax.dot_general(a, b, (((0,), (0,)), ((), ())), preferred_element_type=F32)


def _split3(a):
    hi = a.astype(BF16)
    r1 = a - hi.astype(F32)
    mid = r1.astype(BF16)
    lo = (r1 - mid.astype(F32)).astype(BF16)
    return hi, mid, lo


def _silu(x):
    return x * jax.nn.sigmoid(x)


def _rms(x, w):
    var = jnp.mean(x * x, axis=-1, keepdims=True)
    return x * lax.rsqrt(var + NORM_EPS) * w


def _cparams(sem):
    return pltpu.CompilerParams(dimension_semantics=sem, vmem_limit_bytes=VMEM_LIMIT)


def _resident(block_shape, index_map):
    return pl.BlockSpec(block_shape, index_map, pipeline_mode=pl.Buffered(1))


def _ada_kernel(c_ref, w_ref, b_ref, o_ref):
    cond = _silu(c_ref[...])
    c_hi, c_mid, _ = _split3(cond)
    w_hi, w_mid, _ = _split3(w_ref[...])
    acc = _dot(c_hi, w_hi) + _dot(c_hi, w_mid) + _dot(c_mid, w_hi)
    o_ref[...] = acc + b_ref[...]


def _ada_mod(c, ada_w, ada_b):
    depth, d, n = ada_w.shape
    bsz = c.shape[0]
    rows = 16
    cp =jnp.zeros((rows, d), F32).at[:bsz].set(c)
    tn = 1536
    out = pl.pallas_call(
        _ada_kernel,
        grid=(depth, n // tn),
        in_specs=[
            pl.BlockSpec((rows, d), lambda l, j: (0, 0)),
            pl.BlockSpec((None, d, tn), lambda l, j: (l, 0, j)),
            pl.BlockSpec((None, 1, tn), lambda l, j: (l, 0, j)),
        ],
        out_specs=pl.BlockSpec((None, rows, tn), lambda l, j: (l, 0, j)),
        out_shape=jax.ShapeDtypeStruct((depth, rows, n), F32),
        compiler_params=_cparams(("arbitrary", "arbitrary")),
        name="ada_mod",
    )(cp, ada_w, ada_b.reshape(depth, 1, n))
    return out[:, :bsz]


def _norm_proj_kernel(x_ref, nw_ref, sh_ref, sc_ref, w_ref, o_ref, *, tn):
    h = _rms(x_ref[...], nw_ref[...]) * (1.0 + sc_ref[...]) + sh_ref[...]
    hb = h.astype(BF16)
    n = w_ref.shape[1]
    for j in range(n // tn):
        o_ref[:, j * tn:(j + 1) * tn] = _dot(hb, w_ref[:, j * tn:(j + 1) * tn])


def _norm_proj(x, nw, shift, scale, w_bf16, *, tm, tn):
    bsz, s, d = x.shape
    n = w_bf16.shape[1]
    return pl.pallas_call(
        functools.partial(_norm_proj_kernel, tn=tn),
        grid=(bsz, s // tm),
        in_specs=[
            pl.BlockSpec((None, tm, d), lambda b, i: (b, i, 0)),
            pl.BlockSpec((1, d), lambda b, i: (0, 0)),
            pl.BlockSpec((None, 1, d), lambda b, i: (b, 0, 0)),
            pl.BlockSpec((None, 1, d), lambda b, i: (b, 0, 0)),
            _resident((d, n), lambda b, i: (0, 0)),
        ],
        out_specs=pl.BlockSpec((None, tm, n), lambda b, i: (b, i, 0)),
        out_shape=jax.ShapeDtypeStruct((bsz, s, n), F32),
        compiler_params=_cparams(("arbitrary", "arbitrary")),
        name="norm_proj",
    )(x, nw.reshape(1, d), shift, scale, w_bf16)


def _gla_kernel(q_ref, k_ref, v_ref, gr_ref, sm_ref, gup_ref, gb_ref, onw_ref, o_ref, st_ref, *, ts):
    c = GLA_CHUNK

    @pl.when(pl.program_id(1) == 0)
    def _():
        st_ref[...] = jnp.zeros_like(st_ref)

    row = lax.broadcasted_iota(I32, (c, c), 0)
    col = lax.broadcasted_iota(I32, (c, c), 1)
    causal = col <= row
    tri = jnp.where(causal, 1.0, 0.0).astype(BF16)
    gup = gup_ref[...].astype(BF16)
    gbias = gb_ref[...]
    onw = onw_ref[...]

    def chunk(ci, carry):
        r0 = pl.multiple_of(ci * c, c)
        q = q_ref[pl.ds(r0, c), :]
        k = k_ref[pl.ds(r0, c), :]
        v = v_ref[pl.ds(r0, c), :]
        gr = gr_ref[pl.ds(r0, c), :]
        glow = sm_ref[pl.ds(r0, c), SM_GLOW:SM_GLOW + GLA_GATE_RANK]
        a = _dot(glow.astype(BF16), gup) + gbias
        g = (jnp.minimum(a, 0.0) - jnp.log1p(jnp.exp(-jnp.abs(a)))) / GLA_GATE_TAU
        g_hi, g_mid, g_lo = _split3(g)
        b = _dot(tri, g_hi) + _dot(tri, g_mid) + _dot(tri, g_lo)
        b_mid = b[c // 2 - 1:c // 2, :]
        b_last = b[c - 1:c, :]
        qs = q * (GLA_DK ** -0.5)
        qe = (qs * jnp.exp(b - b_mid)).astype(BF16)
        ke = (k * jnp.exp(b_mid - b)).astype(BF16)
        kd = (k * jnp.exp(b_last - b)).astype(BF16)
        qd = (qs * jnp.exp(b)).astype(BF16)
        dec = jnp.exp(b_last)
        for h in range(GLA_HEADS):
            ks = slice(h * GLA_DK, (h + 1) * GLA_DK)
            vs = slice(h * GLA_DV, (h + 1) * GLA_DV)
            att = jnp.where(causal, _dot_nt(qe[:, ks], ke[:, ks]), 0.0)
            vh = v[:, vs].astype(BF16)
            st = st_ref[h]
            o = _dot(att.astype(BF16), vh) + _dot_nt(qd[:, ks], st.astype(BF16))
            st_ref[h] = st * dec[:, ks] + _dot_tn(vh, kd[:, ks])
            grh = gr[:, vs]
            o_ref[pl.ds(r0, c), vs] = (_rms(o, onw) * _silu(grh)).astype(o_ref.dtype)
        return carry

    lax.fori_loop(0, ts // c, chunk, 0)


def _gla(proj, gate_up, gate_b, out_norm, *, ts):
    bsz, s, _ = proj.shape
    hk, hv = GLA_HEADS * GLA_DK, GLA_HEADS * GLA_DV
    return pl.pallas_call(
        functools.partial(_gla_kernel, ts=ts),
        grid=(bsz, s // ts),
        in_specs=[
            pl.BlockSpec((None, ts, hk), lambda b, i: (b, i, AB_GQ // hk)),
            pl.BlockSpec((None, ts, hk), lambda b, i: (b, i, AB_GK // hk)),
            pl.BlockSpec((None, ts, hv), lambda b, i: (b, i, AB_GV // hv)),
            pl.BlockSpec((None, ts, hv), lambda b, i: (b, i, AB_GR // hv)),
            pl.BlockSpec((None, ts, LANES), lambda b, i: (b, i, AB_SMALL // LANES)),
            pl.BlockSpec((GLA_GATE_RANK, hk), lambda b, i: (0, 0)),
            pl.BlockSpec((1, hk), lambda b, i: (0, 0)),
            pl.BlockSpec((1, GLA_DV), lambda b, i: (0, 0)),
        ],
        out_specs=pl.BlockSpec((None, ts, hv), lambda b, i: (b, i, 0)),
        out_shape=jax.ShapeDtypeStruct((bsz, s, hv), BF16),
        scratch_shapes=[pltpu.VMEM((GLA_HEADS, GLA_DV, GLA_DK), F32)],
        compiler_params=_cparams(("arbitrary", "arbitrary")),
        name="gla",
    )(proj, proj, proj, proj, proj, gate_up, gate_b.reshape(1, hk), out_norm.reshape(1, GLA_DV))


def _dsa_prep_kernel(dq_ref, dk_ref, dv_ref, iq_ref, sm_ref, pos_ref, qn_ref, kn_ref, fa_ref, fi_ref,
                     q_o, k_o, v_o, iq_o, ik_o, iw_o):
    pos = pos_ref[...].astype(F32)
    lane = lax.broadcasted_iota(I32, (1, LANES), 1)

    ang = pos * fa_ref[...]
    cos_a, sin_a = jnp.cos(ang), jnp.sin(ang)
    sin_a = jnp.where(lane < DSA_HEAD_DIM // 2, -sin_a, sin_a)

    def rope_attn(t):
        return t * cos_a + pltpu.roll(t, DSA_HEAD_DIM // 2, 1) * sin_a

    ang = pos * fi_ref[...]
    cos_i, sin_i = jnp.cos(ang), jnp.sin(ang)
    first = (lane % IDX_DIM) < IDX_DIM // 2
    sin_i = jnp.where(first, -sin_i, sin_i)

    def rope_idx(t):
        rot = jnp.where(first, pltpu.roll(t, LANES - IDX_DIM // 2, 1), pltpu.roll(t, IDX_DIM // 2, 1))
        return t * cos_i + rot * sin_i

    qn, kn = qn_ref[...], kn_ref[...]
    for h in range(DSA_HEADS):
        hs = slice(h * DSA_HEAD_DIM, (h + 1) * DSA_HEAD_DIM)
        q_o[:, hs] = (rope_attn(_rms(dq_ref[:, hs], qn)) * DSA_Q_SCALE).astype(q_o.dtype)
        k_o[:, hs] = rope_attn(_rms(dk_ref[:, hs], kn)).astype(k_o.dtype)
    tk = v_o.shape[-1]
    for j in range(v_o.shape[0]):
        v_o[j] = dv_ref[j * tk:(j + 1) * tk, :].T.astype(v_o.dtype)
    for j in range(IDX_HEADS * IDX_DIM // LANES):
        r = rope_idx(iq_ref[:, j * LANES:(j + 1) * LANES])
        iq_o[2 * j] = r[:, :IDX_DIM].astype(iq_o.dtype)
        iq_o[2 * j + 1] = r[:, IDX_DIM:].astype(iq_o.dtype)
    sm = sm_ref[...]
    ik_o[...] = rope_idx(sm)[:, SM_IK:SM_IK + IDX_DIM].astype(ik_o.dtype)
    iw_o[...] = sm.T[SM_IW:SM_IW + IDX_HEADS, :]


def _dsa_prep(proj, positions, q_norm, k_norm, *, ts, tk):
    bsz, s, _ = proj.shape
    hd = DSA_HEADS * DSA_HEAD_DIM
    half_a, half_i = DSA_HEAD_DIM // 2, IDX_DIM // 2
    inv_a = ROPE_THETA ** (-jnp.arange(half_a, dtype=F32) / half_a)
    inv_i = ROPE_THETA ** (-jnp.arange(half_i, dtype=F32) / half_i)
    fa = jnp.tile(inv_a, LANES // half_a).reshape(1, LANES)
    fi = jnp.tile(inv_i, LANES // half_i).reshape(1, LANES)
    wide = lambda j: pl.BlockSpec((None, ts, hd), lambda b, i: (b, i, j))
    vec = lambda n: pl.BlockSpec((1, n), lambda b, i: (0, 0))
    return pl.pallas_call(
        _dsa_prep_kernel,
        grid=(bsz, s // ts),
        in_specs=[
            wide(AB_DQ // hd), wide(AB_DK // hd), wide(AB_DV // hd), wide(AB_IQ // hd),
            pl.BlockSpec((None, ts, LANES), lambda b, i: (b, i, AB_SMALL // LANES)),
            pl.BlockSpec((None, ts, 1), lambda b, i: (b, i, 0)),
            vec(DSA_HEAD_DIM), vec(DSA_HEAD_DIM), vec(LANES), vec(LANES),
        ],
        out_specs=[
            wide(0), wide(0),
            pl.BlockSpec((None, ts // tk, hd, tk), lambda b, i: (b, i, 0, 0)),
            pl.BlockSpec((None, IDX_HEADS, ts, IDX_DIM), lambda b, i: (b, 0, i, 0)),
            pl.BlockSpec((None, ts, IDX_DIM), lambda b, i: (b, i, 0)),
            pl.BlockSpec((None, IDX_HEADS, ts), lambda b, i: (b, 0, i)),
        ],
        out_shape=[
            jax.ShapeDtypeStruct((bsz, s, hd), BF16),
            jax.ShapeDtypeStruct((bsz, s, hd), BF16),
            jax.ShapeDtypeStruct((bsz, s // tk, hd, tk), BF16),
            jax.ShapeDtypeStruct((bsz, IDX_HEADS, s, IDX_DIM), BF16),
            jax.ShapeDtypeStruct((bsz, s, IDX_DIM), BF16),
            jax.ShapeDtypeStruct((bsz, IDX_HEADS, s), F32),
        ],
        compiler_params=_cparams(("arbitrary", "arbitrary")),
        name="dsa_prep",
    )(proj, proj, proj, proj, proj, positions.reshape(bsz, s, 1),
      q_norm.reshape(1, -1), k_norm.reshape(1, -1), fa, fi)


def _dsa_kernel(q_ref, iq_ref, iw_ref, k_ref, vt_ref, ik_ref, o_ref, keys_ref, top_ref, acc_ref,
                *, tq, n_sel, s_len):
    i = pl.program_id(1)
    nkb = i + 1
    kofs = lax.broadcasted_iota(I32, (tq, tq), 0)
    qidx = i * tq + lax.broadcasted_iota(I32, (tq, tq), 1)
    iw = iw_ref[...]
    idx_scale = (IDX_HEADS ** -0.5) * (IDX_DIM ** -0.5)

    def scores(kb, diagonal):
        k0 = pl.multiple_of(kb * tq, tq)
        ikb = ik_ref[pl.ds(k0, tq), :]
        sc = jnp.zeros((tq, tq), F32)
        for h in range(IDX_HEADS):
            sc = sc + jnp.maximum(_dot_nt(ikb, iq_ref[h]), 0.0) * iw[h:h + 1, :]
        bits = pltpu.bitcast(sc * idx_scale, I32)
        bits = jnp.where((bits & 0x7F800000) == 0, 0, bits)
        key = bits ^ ((bits >> 31) & 0x7FFFFFFF)
        top = pltpu.bitcast(bits & jnp.int32(-65536), F32)
        if diagonal:
            valid = kofs <= qidx - i * tq
            key = jnp.where(valid, key, INT_MIN)
            top = jnp.where(valid, top, jnp.nan)
        keys_ref[kb] = key
        top_ref[kb] = top.astype(BF16)

    def below_diagonal(kb, carry):
        scores(kb, False)
        return carry

    lax.fori_loop(0, i, below_diagonal, 0)
    scores(i, True)

    def count(pred):
        def body(kb, acc):
            m = jnp.where(pred(keys_ref[kb], kb * tq + kofs), 1, 0)
            return acc + jnp.sum(m.reshape(tq // 8, 8, tq), axis=0)
        acc = lax.fori_loop(0, nkb, body, jnp.zeros((8, tq), I32))
        return jnp.sum(acc, axis=0, keepdims=True)

    n_valid = qidx[0:1, :] + 1

    def count_top(cand):
        k16 = cand >> 16
        pattern = k16 ^ ((k16 >> 15) & 0x7FFF)
        pattern = jnp.where((pattern > 0) & (pattern < 0x80), 0x80, pattern)
        cand_bf = pltpu.bitcast(pattern << 16, F32).astype(BF16)

        def body(kb, acc):
            m = jnp.where(top_ref[kb] >= cand_bf, jnp.ones((), BF16), jnp.zeros((), BF16))
            part = m[0:16]
            for j in range(1, tq // 16):
                part = part + m[16 * j:16 * (j + 1)]
            return acc + part.astype(F32)
        acc = lax.fori_loop(0, nkb, body, jnp.zeros((16, tq), F32))
        return jnp.sum(acc, axis=0, keepdims=True).astype(I32)

    def bit_step(counter, it, u, cnt_u):
        cand_u = u | lax.shift_left(jnp.int32(1), jnp.int32(31) - it)
        cnt = counter(cand_u ^ INT_MIN)
        take = cnt >= n_sel
        return jnp.where(take, cand_u, u), jnp.where(take, cnt, cnt_u)

    u, cnt_u = lax.fori_loop(0, 16, lambda it, st: bit_step(count_top, it, *st),
                             (jnp.zeros((1, tq), I32), n_valid))

    def unsettled(st):
        over = jnp.max(jnp.where(st[2] > n_sel, 1.0, 0.0))
        return (st[0] < 32) & (over > 0.0)

    def low_step(st):
        it, u, cnt_u = st
        return (it + 1,) + bit_step(lambda cand: count(lambda key, c: key >= cand), it, u, cnt_u)

    _, u, cnt_u = lax.while_loop(unsettled, low_step, (jnp.int32(16), u, cnt_u))
    thr = jnp.maximum(u ^ INT_MIN, INT_MIN + 1)
    excess = cnt_u > n_sel

    def tie_limit():
        need = n_sel - count(lambda key, c: key > thr)
        nbits = s_len.bit_length()

        def step(it, jv):
            cand = jv | lax.shift_left(jnp.int32(1), jnp.int32(nbits - 1) - it)
            cnt = count(lambda key, c: (key == thr) & (c < cand))
            return jnp.where(cnt < need, cand, jv)
        jv = lax.fori_loop(0, nbits, step, jnp.zeros((1, tq), I32))
        return jnp.where(excess, jv, s_len)

    tied = jnp.max(jnp.where(excess, 1.0, 0.0)) > 0.0

    acc_ref[...] = jnp.zeros_like(acc_ref)

    def attend(select):
        def block(kb, carry):
            ms, ls = carry
            k0 = pl.multiple_of(kb * tq, tq)
            sel = select(keys_ref[kb], kb)
            heads = [slice(h * DSA_HEAD_DIM, (h + 1) * DSA_HEAD_DIM) for h in range(DSA_HEADS)]
            st = [_dot_nt(k_ref[pl.ds(k0, tq), hs], q_ref[:, hs]) for hs in heads]
            st = [jnp.where(sel, x, MASK_BIAS) for x in st]
            new_m = [jnp.maximum(m, jnp.max(x, axis=0, keepdims=True)) for m, x in zip(ms, st)]
            alpha = [jnp.exp2(m - mn) for m, mn in zip(ms, new_m)]
            p = [jnp.exp2(x - mn) for x, mn in zip(st, new_m)]
            new_l = [a * l + jnp.sum(x, axis=0, keepdims=True) for a, l, x in zip(alpha, ls, p)]
            pv = [_dot(vt_ref[kb, hs, :], x.astype(BF16)) for hs, x in zip(heads, p)]
            for h in range(DSA_HEADS):
                acc_ref[h] = alpha[h] * acc_ref[h] + pv[h]
            return tuple(new_m), tuple(new_l)

        init = (tuple(jnp.full((1, tq), -jnp.inf, F32) for _ in range(DSA_HEADS)),
                tuple(jnp.zeros((1, tq), F32) for _ in range(DSA_HEADS)))
        return lax.fori_loop(0, nkb, block, init)[1]

    def attend_tied():
        jstar = tie_limit()
        return attend(lambda key, kb: (key > thr) | ((key == thr) & (kb * tq + kofs <= jstar)))

    ls = lax.cond(tied, attend_tied, lambda: attend(lambda key, kb: key >= thr))
    for h in range(DSA_HEADS):
        hs = slice(h * DSA_HEAD_DIM, (h + 1) * DSA_HEAD_DIM)
        o_ref[:, hs] = (acc_ref[h] / ls[h]).T.astype(o_ref.dtype)


def _dsa(q, k, vt, iq, ik, iw, *, tq):
    bsz, s, hd = q.shape
    n_sel = min(DSA_TOPK, s // 4)
    resident = lambda n: _resident((None, s, n), lambda b, i: (b, 0, 0))
    return pl.pallas_call(
        functools.partial(_dsa_kernel, tq=tq, n_sel=n_sel, s_len=s),
        grid=(bsz, s // tq),
        in_specs=[
            pl.BlockSpec((None, tq, hd), lambda b, i: (b, i, 0)),
            pl.BlockSpec((None, IDX_HEADS, tq, IDX_DIM), lambda b, i: (b, 0, i, 0)),
            pl.BlockSpec((None, IDX_HEADS, tq), lambda b, i: (b, 0, i)),
            resident(hd),
            _resident((None, s // tq, hd, tq), lambda b, i: (b, 0, 0, 0)),
            resident(IDX_DIM),
        ],
        out_specs=pl.BlockSpec((None, tq, hd), lambda b, i: (b, i, 0)),
        out_shape=jax.ShapeDtypeStruct((bsz, s, hd), BF16),
        scratch_shapes=[
            pltpu.VMEM((s // tq, tq, tq), I32),
            pltpu.VMEM((s // tq, tq, tq), BF16),
            pltpu.VMEM((DSA_HEADS, DSA_HEAD_DIM, tq), F32),
        ],
        compiler_params=_cparams(("arbitrary", "arbitrary")),
        name="dsa",
    )(q, iq, iw, k, vt, ik)


def _out_proj_kernel(x_ref, a_ref, b_ref, wa_ref, wb_ref, g_ref, o_ref):
    y = _dot(a_ref[...], wa_ref[...]) + _dot(b_ref[...], wb_ref[...])
    o_ref[...] = x_ref[...] + g_ref[...] * y


def _out_proj(x, mix_a, mix_b, w_out_bf16, gate, *, tm):
    bsz, s, d = x.shape
    na = mix_a.shape[-1]
    assert mix_b.shape[-1] == na and w_out_bf16.shape[0] == 2 * na
    return pl.pallas_call(
        _out_proj_kernel,
        grid=(bsz, s // tm),
        in_specs=[
            pl.BlockSpec((None, tm, d), lambda b, i: (b, i, 0)),
            pl.BlockSpec((None, tm, na), lambda b, i: (b, i, 0)),
            pl.BlockSpec((None, tm, na), lambda b, i: (b, i, 0)),
            _resident((na, d), lambda b, i: (0, 0)),
            _resident((na, d), lambda b, i: (1, 0)),
            pl.BlockSpec((None, 1, d), lambda b, i: (b, 0, 0)),
        ],
        out_specs=pl.BlockSpec((None, tm, d), lambda b, i: (b, i, 0)),
        out_shape=jax.ShapeDtypeStruct((bsz, s, d), F32),
        compiler_params=_cparams(("arbitrary", "arbitrary")),
        name="out_proj",
    )(x, mix_a, mix_b, w_out_bf16, w_out_bf16, gate)


def _ffn_kernel(x_ref, nw_ref, sh_ref, sc_ref, g_ref, w1_ref, w2_ref, o_ref, *, tf):
    x = x_ref[...]
    hb = (_rms(x, nw_ref[...]) * (1.0 + sc_ref[...]) + sh_ref[...]).astype(BF16)
    f = w2_ref.shape[0]
    acc = jnp.zeros(x.shape, F32)
    for j in range(f // tf):
        gate = _dot(hb, w1_ref[:, j * tf:(j + 1) * tf])
        up = _dot(hb, w1_ref[:, f + j * tf:f + (j + 1) * tf])
        acc = acc + _dot((_silu(gate) * up).astype(BF16), w2_ref[j * tf:(j + 1) * tf, :])
    o_ref[...] = x + g_ref[...] * acc


def _ffn(x, nw, shift, scale, gate, w1_bf16, w2_bf16, *, tm, tf):
    bsz, s, d = x.shape
    f = w2_bf16.shape[0]
    mod = pl.BlockSpec((None, 1, d), lambda b, i: (b, 0, 0))
    return pl.pallas_call(
        functools.partial(_ffn_kernel, tf=tf),
        grid=(bsz, s // tm),
        in_specs=[
            pl.BlockSpec((None, tm, d), lambda b, i: (b, i, 0)),
            pl.BlockSpec((1, d), lambda b, i: (0, 0)),
            mod, mod, mod,
            _resident((d, 2 * f), lambda b, i: (0, 0)),
            _resident((f, d), lambda b, i: (0, 0)),
        ],
        out_specs=pl.BlockSpec((None, tm, d), lambda b, i: (b, i, 0)),
        out_shape=jax.ShapeDtypeStruct((bsz, s, d), F32),
        compiler_params=_cparams(("arbitrary", "arbitrary")),
        name="ffn",
    )(x, nw.reshape(1, d), shift, scale, gate, w1_bf16, w2_bf16)


HALO = max(POOL_WINDOWS)


def _cd_prep_kernel(u_ref, halo_ref, sq_ref, sk_ref, sv_ref, pw_ref, ps_ref, qn_ref, kn_ref,
                    pool_o, q_o, k_o, v_o, ext_ref, *, ts):
    i = pl.program_id(1)
    halo = halo_ref[...]
    ext_ref[0:HALO, :] = jnp.where(i == 0, jnp.zeros_like(halo), halo)
    ext_ref[HALO:HALO + ts, :] = u_ref[...]
    t = i * ts + lax.broadcasted_iota(I32, (ts, 1), 0)
    for g, w in enumerate(POOL_WINDOWS):
        gs = slice(g * POOL_CH, (g + 1) * POOL_CH)
        tot = ext_ref[HALO:HALO + ts, gs]
        for j in range(1, w):
            tot = tot + ext_ref[HALO - j:HALO - j + ts, gs]
        cnt = jnp.minimum(t + 1, w).astype(F32)
        pooled = tot / cnt - u_ref[:, gs]
        y = _dot(pooled.astype(BF16), pw_ref[g].astype(BF16)) * ps_ref[:, gs]
        pool_o[:, gs] = y.astype(pool_o.dtype)
    qn, kn = qn_ref[...], kn_ref[...]
    for h in range(SB_HEADS):
        hs = slice(h * SB_HEAD_DIM, (h + 1) * SB_HEAD_DIM)
        q_o[:, hs] = _rms(sq_ref[:, hs], qn).astype(q_o.dtype)
        k_o[:, hs] = _rms(sk_ref[:, hs], kn).astype(k_o.dtype)
    v_o[...] = sv_ref[...].astype(v_o.dtype)


def _cd_prep(proj, pool_w, pool_scale, q_norm, k_norm, *, ts):
    bsz, s, _ = proj.shape
    n = POOL_GROUPS * POOL_CH
    wide = lambda j: pl.BlockSpec((None, ts, n), lambda b, i: (b, i, j))
    vec = lambda m: pl.BlockSpec((1, m), lambda b, i: (0, 0))
    per_blk = ts // HALO
    return pl.pallas_call(
        functools.partial(_cd_prep_kernel, ts=ts),
        grid=(bsz, s // ts),
        in_specs=[
            wide(0),
            pl.BlockSpec((None, HALO, n), lambda b, i: (b, jnp.maximum(i * per_blk - 1, 0), 0)),
            wide(1), wide(2), wide(3),
            pl.BlockSpec((POOL_GROUPS, POOL_CH, POOL_CH), lambda b, i: (0, 0, 0)),
            vec(n), vec(SB_HEAD_DIM), vec(SB_HEAD_DIM),
        ],
        out_specs=[wide(0)] * 4,
        out_shape=[jax.ShapeDtypeStruct((bsz, s, n), BF16)] * 4,
        scratch_shapes=[pltpu.VMEM((HALO + ts, n), F32)],
        compiler_params=_cparams(("arbitrary", "arbitrary")),
        name="cd_prep",
    )(proj, proj, proj, proj, proj, pool_w, pool_scale.reshape(1, n),
      q_norm.reshape(1, -1), k_norm.reshape(1, -1))


def _sb_kernel(q_ref, k_ref, v_ref, o_ref, acc_ref, run_ref, *, tq):
    i = pl.program_id(1)
    r = lax.broadcasted_iota(I32, (tq, tq), 0)
    c = lax.broadcasted_iota(I32, (tq, tq), 1)
    later = jnp.where(r > c, 1.0, 0.0).astype(BF16)
    scale = SB_HEAD_DIM ** -0.5
    acc_ref[...] = jnp.zeros_like(acc_ref)
    run_ref[...] = jnp.zeros_like(run_ref)

    def block(state):
        kb, _ = state
        k0 = pl.multiple_of(kb * tq, tq)
        strict = (kb * tq + c) < (i * tq + r)
        live = jnp.float32(-jnp.inf)
        for h in range(SB_HEADS):
            hs = slice(h * SB_HEAD_DIM, (h + 1) * SB_HEAD_DIM)
            z = _dot_nt(q_ref[:, hs], k_ref[pl.ds(k0, tq), hs]) * scale
            sp = jnp.log1p(jnp.exp(-jnp.abs(z)))
            log_beta = jnp.minimum(z, 0.0) - sp
            log_1m = jnp.where(strict, -jnp.maximum(z, 0.0) - sp, 0.0)
            hi = log_1m.astype(BF16)
            lo = (log_1m - hi.astype(F32)).astype(BF16)
            run = run_ref[h]
            after = _dot(hi, later) + _dot(lo, later) + run
            w = jnp.where(strict, jnp.exp(log_beta + after), 0.0)
            acc_ref[h] = acc_ref[h] + _dot(w.astype(BF16), v_ref[pl.ds(k0, tq), hs])
            run = run + jnp.sum(log_1m, axis=1, keepdims=True)
            run_ref[h] = run
            live = jnp.maximum(live, jnp.max(run))
        return kb - 1, live > SB_DEAD_LOG

    lax.while_loop(lambda st: (st[0] >= 0) & st[1], block, (i, jnp.bool_(True)))
    for h in range(SB_HEADS):
        hs = slice(h * SB_HEAD_DIM, (h + 1) * SB_HEAD_DIM)
        o_ref[:, hs] = acc_ref[h].astype(o_ref.dtype)


def _sb(q, k, v, *, tq):
    bsz, s, hd = q.shape
    resident = _resident((None, s, hd), lambda b, i: (b, 0, 0))
    return pl.pallas_call(
        functools.partial(_sb_kernel, tq=tq),
        grid=(bsz, s // tq),
        in_specs=[pl.BlockSpec((None, tq, hd), lambda b, i: (b, i, 0)), resident, resident],
        out_specs=pl.BlockSpec((None, tq, hd), lambda b, i: (b, i, 0)),
        out_shape=jax.ShapeDtypeStruct((bsz, s, hd), BF16),
        scratch_shapes=[
            pltpu.VMEM((SB_HEADS, tq, SB_HEAD_DIM), F32),
            pltpu.VMEM((SB_HEADS, tq, 1), F32),
        ],
        compiler_params=_cparams(("arbitrary", "arbitrary")),
        name="stick_breaking",
    )(q, k, v)


def _pack_ab_weight(w):
    d = w.shape[0]
    gq, gk, gv, glow, gr, dq, dk, dv, iq, ik, iw = jnp.split(
        w, [256, 512, 1024, 1040, 1552, 2064, 2576, 3088, 3600, 3664], axis=1)
    pad = jnp.zeros((d, AB_NPAD - AB_SMALL - (IDX_DIM + GLA_GATE_RANK + IDX_HEADS)), w.dtype)
    return jnp.concatenate([gq, gk, gv, gr, dq, dk, dv, iq, ik, glow, iw, pad], axis=1)


def kernel(x, c, positions, ada_w, ada_b, mix_norm, ffn_norm, ffn_w1, ffn_w2, ab_w_in, gla_gate_up,
           gla_gate_b, gla_out_norm, dsa_q_norm, dsa_k_norm, ab_w_out, cd_w_in, pool_w, pool_scale,
           sb_q_norm, sb_k_norm, cd_w_out):
    bsz, s, d = x.shape
    depth = ada_w.shape[0]
    mod = _ada_mod(c, ada_w, ada_b).reshape(depth, bsz, 6, 1, d)
    tm = min(512, s)
    tq = min(256, s)
    for layer in range(depth):
        sh1, sc1, g1, sh2, sc2, g2 = (mod[layer, :, j] for j in range(6))
        i = layer // 2
        if layer % 2 == 0:
            proj = _norm_proj(x, mix_norm[layer], sh1, sc1, _pack_ab_weight(ab_w_in[i]).astype(BF16),
                              tm=min(256, s), tn=768)
            mix_a = _gla(proj, gla_gate_up[i], gla_gate_b[i], gla_out_norm[i], ts=tm)
            q, k, vt, iq, ik, iw = _dsa_prep(proj, positions, dsa_q_norm[i], dsa_k_norm[i], ts=tm, tk=tq)
            mix_b = _dsa(q, k, vt, iq, ik, iw, tq=tq)
            w_out = ab_w_out[i]
        else:
            proj = _norm_proj(x, mix_norm[layer], sh1, sc1, cd_w_in[i].astype(BF16), tm=tm, tn=512)
            mix_a, q, k, v = _cd_prep(proj, pool_w[i], pool_scale[i], sb_q_norm[i], sb_k_norm[i], ts=tm)
            mix_b = _sb(q, k, v, tq=tq)
            w_out = cd_w_out[i]
        x = _out_proj(x, mix_a, mix_b, w_out.astype(BF16), g1, tm=tm)
        x = _ffn(x, ffn_norm[layer], sh2, sc2, g2, ffn_w1[layer].astype(BF16),
                 ffn_w2[layer].astype(BF16), tm=tm, tf=256)
    return x
```

```python
import functools

import jax
import jax.numpy as jnp
from jax import lax
from jax.experimental import pallas as pl
from jax.experimental.pallas import tpu as pltpu

F32 = jnp.float32
BF16 = jnp.bfloat16
I32 = jnp.int32

D_MODEL = 1024
GLA_HEADS, GLA_DK, GLA_DV = 4, 64, 128
GLA_GATE_RANK = 16
GLA_GATE_TAU = 16.0
GLA_CHUNK = 64
DSA_HEADS, DSA_HEAD_DIM = 4, 128
IDX_HEADS, IDX_DIM = 8, 64
DSA_TOPK = 256
POOL_WINDOWS = (2, 4, 8, 16)
POOL_GROUPS, POOL_CH = 4, 128
SB_HEADS, SB_HEAD_DIM = 4, 128
ROPE_THETA = 10000.0
NORM_EPS = 1e-6
D_FF = 2816

LANES = 128
VMEM_LIMIT = 56 * 1024 * 1024

AB_GQ, AB_GK, AB_GV, AB_GR = 0, 256, 512, 1024
AB_DQ, AB_DK, AB_DV, AB_IQ = 1536, 2048, 2560, 3072
AB_SMALL = 3584
SM_IK, SM_GLOW, SM_IW = 0, 64, 80
AB_NPAD = 3840

DSA_Q_SCALE = 1.4426950408889634 * DSA_HEAD_DIM ** -0.5
INT_MIN = -2 ** 31
MASK_BIAS = -1e30
SB_DEAD_LOG = -110.0


def _dot(a, b):
    return jnp.dot(a, b, preferred_element_type=F32)


def _dot_nt(a, b):
    return lax.dot_general(a, b, (((1,), (1,)), ((), ())), preferred_element_type=F32)


def _dot_tn(a, b):
    return lax.dot_general(a, b, (((0,), (0,)), ((), ())), preferred_element_type=F32)


def _split3(a):
    hi = a.astype(BF16)
    r1 = a - hi.astype(F32)
    mid = r1.astype(BF16)
    lo = (r1 - mid.astype(F32)).astype(BF16)
    return hi, mid, lo


def _silu(x):
    return x * jax.nn.sigmoid(x)


def _rms(x, w):
    var = jnp.mean(x * x, axis=-1, keepdims=True)
    return x * lax.rsqrt(var + NORM_EPS) * w


def _cparams(sem):
    return pltpu.CompilerParams(dimension_semantics=sem, vmem_limit_bytes=VMEM_LIMIT)


def _resident(block_shape, index_map):
    return pl.BlockSpec(block_shape, index_map, pipeline_mode=pl.Buffered(1))


def _ada_kernel(c_ref, w_ref, b_ref, o_ref):
    cond = _silu(c_ref[...])
    c_hi, c_mid, _ = _split3(cond)
    w_hi, w_mid, _ = _split3(w_ref[...])
    acc = _dot(c_hi, w_hi) + _dot(c_hi, w_mid) + _dot(c_mid, w_hi)
    o_ref[...] = acc + b_ref[...]


def _ada_mod(c, ada_w, ada_b):
    depth, d, n = ada_w.shape
    bsz = c.shape[0]
    rows = 16
    cp =jnp.zeros((rows, d), F32).at[:bsz].set(c)
    tn = 1536
    out = pl.pallas_call(
        _ada_kernel,
        grid=(depth, n // tn),
        in_specs=[
            pl.BlockSpec((rows, d), lambda l, j: (0, 0)),
            pl.BlockSpec((None, d, tn), lambda l, j: (l, 0, j)),
            pl.BlockSpec((None, 1, tn), lambda l, j: (l, 0, j)),
        ],
        out_specs=pl.BlockSpec((None, rows, tn), lambda l, j: (l, 0, j)),
        out_shape=jax.ShapeDtypeStruct((depth, rows, n), F32),
        compiler_params=_cparams(("arbitrary", "arbitrary")),
        name="ada_mod",
    )(cp, ada_w, ada_b.reshape(depth, 1, n))
    return out[:, :bsz]


def _norm_proj_kernel(x_ref, nw_ref, sh_ref, sc_ref, w_ref, o_ref, *, tn):
    h = _rms(x_ref[...], nw_ref[...]) * (1.0 + sc_ref[...]) + sh_ref[...]
    hb = h.astype(BF16)
    n = w_ref.shape[1]
    for j in range(n // tn):
        o_ref[:, j * tn:(j + 1) * tn] = _dot(hb, w_ref[:, j * tn:(j + 1) * tn])


def _norm_proj(x, nw, shift, scale, w_bf16, *, tm, tn):
    bsz, s, d = x.shape
    n = w_bf16.shape[1]
    return pl.pallas_call(
        functools.partial(_norm_proj_kernel, tn=tn),
        grid=(bsz, s // tm),
        in_specs=[
            pl.BlockSpec((None, tm, d), lambda b, i: (b, i, 0)),
            pl.BlockSpec((1, d), lambda b, i: (0, 0)),
            pl.BlockSpec((None, 1, d), lambda b, i: (b, 0, 0)),
            pl.BlockSpec((None, 1, d), lambda b, i: (b, 0, 0)),
            _resident((d, n), lambda b, i: (0, 0)),
        ],
        out_specs=pl.BlockSpec((None, tm, n), lambda b, i: (b, i, 0)),
        out_shape=jax.ShapeDtypeStruct((bsz, s, n), F32),
        compiler_params=_cparams(("arbitrary", "arbitrary")),
        name="norm_proj",
    )(x, nw.reshape(1, d), shift, scale, w_bf16)


def _gla_kernel(q_ref, k_ref, v_ref, gr_ref, sm_ref, gup_ref, gb_ref, onw_ref, o_ref, st_ref, *, ts):
    c = GLA_CHUNK

    @pl.when(pl.program_id(1) == 0)
    def _():
        st_ref[...] = jnp.zeros_like(st_ref)

    nc = ts // c
    row = lax.broadcasted_iota(I32, (ts, ts), 0)
    col = lax.broadcasted_iota(I32, (ts, ts), 1)
    tri = jnp.where((row // c == col // c) & (col <= row), 1.0, 0.0).astype(BF16)
    causal = lax.broadcasted_iota(I32, (c, c), 1) <= lax.broadcasted_iota(I32, (c, c), 0)
    onw = onw_ref[...]

    glow = sm_ref[:, SM_GLOW:SM_GLOW + GLA_GATE_RANK]
    a = _dot(glow.astype(BF16), gup_ref[...].astype(BF16)) + gb_ref[...]
    g = (jnp.minimum(a, 0.0) - jnp.log1p(jnp.exp(-jnp.abs(a)))) / GLA_GATE_TAU
    g_hi, g_mid, g_lo = _split3(g)
    b = _dot(tri, g_hi) + _dot(tri, g_mid) + _dot(tri, g_lo)
    qs = q_ref[...] * (GLA_DK ** -0.5)
    k = k_ref[...]

    qe, ke, kd, qd, dec = [], [], [], [], []
    for ci in range(nc):
        r = slice(ci * c, (ci + 1) * c)
        bc = b[r]
        b_mid = bc[c // 2 - 1:c // 2, :]
        b_last = bc[c - 1:c, :]
        qe.append((qs[r] * jnp.exp(bc - b_mid)).astype(BF16))
        ke.append((k[r] * jnp.exp(b_mid - bc)).astype(BF16))
        kd.append((k[r] * jnp.exp(b_last - bc)).astype(BF16))
        qd.append((qs[r] * jnp.exp(bc)).astype(BF16))
        dec.append(jnp.exp(b_last))

    pairs = [(ci, h) for ci in range(nc) for h in range(GLA_HEADS)]
    ksl = lambda h: slice(h * GLA_DK, (h + 1) * GLA_DK)
    vsl = lambda h: slice(h * GLA_DV, (h + 1) * GLA_DV)
    vh = {(ci, h): v_ref[ci * c:(ci + 1) * c, vsl(h)].astype(BF16) for ci, h in pairs}
    att = {p: _dot_nt(qe[p[0]][:, ksl(p[1])], ke[p[0]][:, ksl(p[1])]) for p in pairs}
    att = {p: jnp.where(causal, att[p], 0.0).astype(BF16) for p in pairs}
    o = {p: _dot(att[p], vh[p]) for p in pairs}
    kvt = {p: _dot_tn(vh[p], kd[p[0]][:, ksl(p[1])]) for p in pairs}
    st_in = {}
    for h in range(GLA_HEADS):
        st = st_ref[h]
        for ci in range(nc):
            st_in[(ci, h)] = st.astype(BF16)
            st = st * dec[ci][:, ksl(h)] + kvt[(ci, h)]
        st_ref[h] = st
    for ci, h in pairs:
        out = o[(ci, h)] + _dot_nt(qd[ci][:, ksl(h)], st_in[(ci, h)])
        grh = gr_ref[ci * c:(ci + 1) * c, vsl(h)]
        o_ref[ci * c:(ci + 1) * c, vsl(h)] = (_rms(out, onw) * _silu(grh)).astype(o_ref.dtype)


def _gla(proj, gate_up, gate_b, out_norm, *, ts):
    bsz, s, _ = proj.shape
    hk, hv = GLA_HEADS * GLA_DK, GLA_HEADS * GLA_DV
    return pl.pallas_call(
        functools.partial(_gla_kernel, ts=ts),
        grid=(bsz, s // ts),
        in_specs=[
            pl.BlockSpec((None, ts, hk), lambda b, i: (b, i, AB_GQ // hk)),
            pl.BlockSpec((None, ts, hk), lambda b, i: (b, i, AB_GK // hk)),
            pl.BlockSpec((None, ts, hv), lambda b, i: (b, i, AB_GV // hv)),
            pl.BlockSpec((None, ts, hv), lambda b, i: (b, i, AB_GR // hv)),
            pl.BlockSpec((None, ts, LANES), lambda b, i: (b, i, AB_SMALL // LANES)),
            pl.BlockSpec((GLA_GATE_RANK, hk), lambda b, i: (0, 0)),
            pl.BlockSpec((1, hk), lambda b, i: (0, 0)),
            pl.BlockSpec((1, GLA_DV), lambda b, i: (0, 0)),
        ],
        out_specs=pl.BlockSpec((None, ts, hv), lambda b, i: (b, i, 0)),
        out_shape=jax.ShapeDtypeStruct((bsz, s, hv), BF16),
        scratch_shapes=[pltpu.VMEM((GLA_HEADS, GLA_DV, GLA_DK), F32)],
        compiler_params=_cparams(("arbitrary", "arbitrary")),
        name="gla",
    )(proj, proj, proj, proj, proj, gate_up, gate_b.reshape(1, hk), out_norm.reshape(1, GLA_DV))


def _dsa_prep_kernel(dq_ref, dk_ref, dv_ref, iq_ref, sm_ref, pos_ref, qn_ref, kn_ref, freq_ref,
                     q_o, k_o, v_o, iq_o, ik_o, iw_o):
    pos = pos_ref[...].astype(F32)
    lane = lax.broadcasted_iota(I32, (1, LANES), 1)
    half_a, half_i = DSA_HEAD_DIM // 2, IDX_DIM // 2

    ang = pos * freq_ref[...]
    cos_t, sin_t = jnp.cos(ang), jnp.sin(ang)

    def attn_table(t):
        return jnp.where(lane < half_a, t, pltpu.roll(t, half_a, 1))

    def idx_table(t):
        return jnp.where(lane < half_i, pltpu.roll(t, half_a, 1),
                         jnp.where(lane < 2 * half_i, pltpu.roll(t, half_a + half_i, 1), t))

    cos_a = attn_table(cos_t)
    sin_a = jnp.where(lane < half_a, -1.0, 1.0) * attn_table(sin_t)

    def rope_attn(t):
        return t * cos_a + pltpu.roll(t, half_a, 1) * sin_a

    first = (lane % IDX_DIM) < half_i
    cos_i = idx_table(cos_t)
    sin_i = jnp.where(first, -1.0, 1.0) * idx_table(sin_t)

    def rope_idx(t):
        rot = jnp.where(first, pltpu.roll(t, LANES - IDX_DIM // 2, 1), pltpu.roll(t, IDX_DIM // 2, 1))
        return t * cos_i + rot * sin_i

    qn, kn = qn_ref[...], kn_ref[...]
    for h in range(DSA_HEADS):
        hs = slice(h * DSA_HEAD_DIM, (h + 1) * DSA_HEAD_DIM)
        q_o[:, hs] = (rope_attn(_rms(dq_ref[:, hs], qn)) * DSA_Q_SCALE).astype(q_o.dtype)
        k_o[:, hs] = rope_attn(_rms(dk_ref[:, hs], kn)).astype(k_o.dtype)
    tk = v_o.shape[-1]
    for j in range(v_o.shape[0]):
        v_o[j] = dv_ref[j * tk:(j + 1) * tk, :].T.astype(v_o.dtype)
    for j in range(IDX_HEADS * IDX_DIM // LANES):
        r = rope_idx(iq_ref[:, j * LANES:(j + 1) * LANES])
        iq_o[2 * j] = r[:, :IDX_DIM].astype(iq_o.dtype)
        iq_o[2 * j + 1] = r[:, IDX_DIM:].astype(iq_o.dtype)
    sm = sm_ref[...]
    ik_o[...] = rope_idx(sm)[:, SM_IK:SM_IK + IDX_DIM].astype(ik_o.dtype)
    iw_o[...] = sm.T[SM_IW:SM_IW + IDX_HEADS, :]


def _dsa_prep(proj, positions, q_norm, k_norm, *, ts, tk):
    bsz, s, _ = proj.shape
    hd = DSA_HEADS * DSA_HEAD_DIM
    half_a, half_i = DSA_HEAD_DIM // 2, IDX_DIM // 2
    inv_a = ROPE_THETA ** (-jnp.arange(half_a, dtype=F32) / half_a)
    inv_i = ROPE_THETA ** (-jnp.arange(half_i, dtype=F32) / half_i)
    freq = jnp.concatenate([inv_a, inv_i, inv_i]).reshape(1, LANES)
    wide = lambda j: pl.BlockSpec((None, ts, hd), lambda b, i: (b, i, j))
    vec = lambda n: pl.BlockSpec((1, n), lambda b, i: (0, 0))
    return pl.pallas_call(
        _dsa_prep_kernel,
        grid=(bsz, s // ts),
        in_specs=[
            wide(AB_DQ // hd), wide(AB_DK // hd), wide(AB_DV // hd), wide(AB_IQ // hd),
            pl.BlockSpec((None, ts, LANES), lambda b, i: (b, i, AB_SMALL // LANES)),
            pl.BlockSpec((None, ts, 1), lambda b, i: (b, i, 0)),
            vec(DSA_HEAD_DIM), vec(DSA_HEAD_DIM), vec(LANES),
        ],
        out_specs=[
            wide(0), wide(0),
            pl.BlockSpec((None, ts // tk, hd, tk), lambda b, i: (b, i, 0, 0)),
            pl.BlockSpec((None, IDX_HEADS, ts, IDX_DIM), lambda b, i: (b, 0, i, 0)),
            pl.BlockSpec((None, ts, IDX_DIM), lambda b, i: (b, i, 0)),
            pl.BlockSpec((None, IDX_HEADS, ts), lambda b, i: (b, 0, i)),
        ],
        out_shape=[
            jax.ShapeDtypeStruct((bsz, s, hd), BF16),
            jax.ShapeDtypeStruct((bsz, s, hd), BF16),
            jax.ShapeDtypeStruct((bsz, s // tk, hd, tk), BF16),
            jax.ShapeDtypeStruct((bsz, IDX_HEADS, s, IDX_DIM), BF16),
            jax.ShapeDtypeStruct((bsz, s, IDX_DIM), BF16),
            jax.ShapeDtypeStruct((bsz, IDX_HEADS, s), F32),
        ],
        compiler_params=_cparams(("arbitrary", "arbitrary")),
        name="dsa_prep",
    )(proj, proj, proj, proj, proj, positions.reshape(bsz, s, 1),
      q_norm.reshape(1, -1), k_norm.reshape(1, -1), freq)


def _dsa_kernel(q_ref, iq_ref, iw_ref, k_ref, vt_ref, ik_ref, o_ref, keys_ref, top_ref, acc_ref,
                *, tq, n_sel, s_len):
    i = pl.program_id(1)
    nkb = i + 1
    kofs = lax.broadcasted_iota(I32, (tq, tq), 0)
    qidx = i * tq + lax.broadcasted_iota(I32, (tq, tq), 1)
    iw = iw_ref[...]
    idx_scale = (IDX_HEADS ** -0.5) * (IDX_DIM ** -0.5)

    def scores(kb, diagonal):
        k0 = pl.multiple_of(kb * tq, tq)
        ikb = ik_ref[pl.ds(k0, tq), :]
        sc = jnp.zeros((tq, tq), F32)
        for h in range(IDX_HEADS):
            sc = sc + jnp.maximum(_dot_nt(ikb, iq_ref[h]), 0.0) * iw[h:h + 1, :]
        bits = pltpu.bitcast(sc * idx_scale, I32)
        bits = jnp.where((bits & 0x7F800000) == 0, 0, bits)
        key = bits ^ ((bits >> 31) & 0x7FFFFFFF)
        top = pltpu.bitcast(bits & jnp.int32(-65536), F32)
        if diagonal:
            valid = kofs <= qidx - i * tq
            key = jnp.where(valid, key, INT_MIN)
            top = jnp.where(valid, top, jnp.nan)
        keys_ref[kb] = key
        top_ref[kb] = top.astype(BF16)

    def below_diagonal(kb, carry):
        scores(kb, False)
        return carry

    lax.fori_loop(0, i, below_diagonal, 0)
    scores(i, True)

    def count(pred):
        def body(kb, acc):
            m = jnp.where(pred(keys_ref[kb], kb * tq + kofs), 1, 0)
            return acc + jnp.sum(m.reshape(tq // 8, 8, tq), axis=0)
        acc = lax.fori_loop(0, nkb, body, jnp.zeros((8, tq), I32))
        return jnp.sum(acc, axis=0, keepdims=True)

    n_valid = qidx[0:1, :] + 1

    def count_top(cand):
        k16 = cand >> 16
        pattern = k16 ^ ((k16 >> 15) & 0x7FFF)
        pattern = jnp.where((pattern > 0) & (pattern < 0x80), 0x80, pattern)
        cand_bf = pltpu.bitcast(pattern << 16, F32).astype(BF16)

        def body(kb, acc):
            m = jnp.where(top_ref[kb] >= cand_bf, jnp.ones((), BF16), jnp.zeros((), BF16))
            part = m[0:16]
            for j in range(1, tq // 16):
                part = part + m[16 * j:16 * (j + 1)]
            return acc + part.astype(F32)
        acc = lax.fori_loop(0, nkb, body, jnp.zeros((16, tq), F32))
        return jnp.sum(acc, axis=0, keepdims=True).astype(I32)

    def bit_step(counter, it, u, cnt_u):
        cand_u = u | lax.shift_left(jnp.int32(1), jnp.int32(31) - it)
        cnt = counter(cand_u ^ INT_MIN)
        take = cnt >= n_sel
        return jnp.where(take, cand_u, u), jnp.where(take, cnt, cnt_u)

    u, cnt_u = lax.fori_loop(0, 16, lambda it, st: bit_step(count_top, it, *st),
                             (jnp.zeros((1, tq), I32), n_valid))

    def unsettled(st):
        over = jnp.max(jnp.where(st[2] > n_sel, 1.0, 0.0))
        return (st[0] < 32) & (over > 0.0)

    def low_step(st):
        it, u, cnt_u = st
        return (it + 1,) + bit_step(lambda cand: count(lambda key, c: key >= cand), it, u, cnt_u)

    _, u, cnt_u = lax.while_loop(unsettled, low_step, (jnp.int32(16), u, cnt_u))
    thr = jnp.maximum(u ^ INT_MIN, INT_MIN + 1)
    excess = cnt_u > n_sel

    def tie_limit():
        need = n_sel - count(lambda key, c: key > thr)
        nbits = s_len.bit_length()

        def step(it, jv):
            cand = jv | lax.shift_left(jnp.int32(1), jnp.int32(nbits - 1) - it)
            cnt = count(lambda key, c: (key == thr) & (c < cand))
            return jnp.where(cnt < need, cand, jv)
        jv = lax.fori_loop(0, nbits, step, jnp.zeros((1, tq), I32))
        return jnp.where(excess, jv, s_len)

    tied = jnp.max(jnp.where(excess, 1.0, 0.0)) > 0.0

    acc_ref[...] = jnp.zeros_like(acc_ref)

    def attend(select):
        def block(kb, carry, width):
            ms, ls = carry
            k0 = pl.multiple_of(kb * tq, tq)
            rows = width * tq
            sel = select(keys_ref[pl.ds(kb, width)].reshape(rows, tq), kb, rows)
            heads = [slice(h * DSA_HEAD_DIM, (h + 1) * DSA_HEAD_DIM) for h in range(DSA_HEADS)]
            st = [_dot_nt(k_ref[pl.ds(k0, rows), hs], q_ref[:, hs]) for hs in heads]
            st = [jnp.where(sel, x, MASK_BIAS) for x in st]
            new_m = [jnp.maximum(m, jnp.max(x, axis=0, keepdims=True)) for m, x in zip(ms, st)]
            alpha = [jnp.exp2(m - mn) for m, mn in zip(ms, new_m)]
            p = [jnp.exp2(x - mn) for x, mn in zip(st, new_m)]
            new_l = [a * l + jnp.sum(x, axis=0, keepdims=True) for a, l, x in zip(alpha, ls, p)]
            p = [x.astype(BF16) for x in p]
            for h, hs in enumerate(heads):
                pv = _dot(vt_ref[kb, hs, :], p[h][0:tq])
                for j in range(1, width):
                    pv = pv + _dot(vt_ref[kb + j, hs, :], p[h][j * tq:(j + 1) * tq])
                acc_ref[h] = alpha[h] * acc_ref[h] + pv
            return tuple(new_m), tuple(new_l)

        carry = (tuple(jnp.full((1, tq), -jnp.inf, F32) for _ in range(DSA_HEADS)),
                 tuple(jnp.zeros((1, tq), F32) for _ in range(DSA_HEADS)))
        carry = lax.fori_loop(0, nkb // 2, lambda j, c: block(2 * j, c, 2), carry)
        carry = lax.cond(nkb % 2 == 1, lambda c: block(nkb - 1, c, 1), lambda c: c, carry)
        return carry[1]

    def attend_tied():
        jstar = tie_limit()

        def select(key, kb, rows):
            kidx = kb * tq + lax.broadcasted_iota(I32, (rows, tq), 0)
            return (key > thr) | ((key == thr) & (kidx <= jstar))
        return attend(select)

    ls = lax.cond(tied, attend_tied, lambda: attend(lambda key, kb, rows: key >= thr))
    for h in range(DSA_HEADS):
        hs = slice(h * DSA_HEAD_DIM, (h + 1) * DSA_HEAD_DIM)
        o_ref[:, hs] = (acc_ref[h] / ls[h]).T.astype(o_ref.dtype)


def _dsa(q, k, vt, iq, ik, iw, *, tq):
    bsz, s, hd = q.shape
    n_sel = min(DSA_TOPK, s // 4)
    resident = lambda n: _resident((None, s, n), lambda b, i: (b, 0, 0))
    return pl.pallas_call(
        functools.partial(_dsa_kernel, tq=tq, n_sel=n_sel, s_len=s),
        grid=(bsz, s // tq),
        in_specs=[
            pl.BlockSpec((None, tq, hd), lambda b, i: (b, i, 0)),
            pl.BlockSpec((None, IDX_HEADS, tq, IDX_DIM), lambda b, i: (b, 0, i, 0)),
            pl.BlockSpec((None, IDX_HEADS, tq), lambda b, i: (b, 0, i)),
            resident(hd),
            _resident((None, s // tq, hd, tq), lambda b, i: (b, 0, 0, 0)),
            resident(IDX_DIM),
        ],
        out_specs=pl.BlockSpec((None, tq, hd), lambda b, i: (b, i, 0)),
        out_shape=jax.ShapeDtypeStruct((bsz, s, hd), BF16),
        scratch_shapes=[
            pltpu.VMEM((s // tq, tq, tq), I32),
            pltpu.VMEM((s // tq, tq, tq), BF16),
            pltpu.VMEM((DSA_HEADS, DSA_HEAD_DIM, tq), F32),
        ],
        compiler_params=_cparams(("arbitrary", "arbitrary")),
        name="dsa",
    )(q, iq, iw, k, vt, ik)


def _out_proj_kernel(x_ref, a_ref, b_ref, wa_ref, wb_ref, g_ref, o_ref):
    y = _dot(a_ref[...], wa_ref[...]) + _dot(b_ref[...], wb_ref[...])
    o_ref[...] = x_ref[...] + g_ref[...] * y


def _out_proj(x, mix_a, mix_b, w_out_bf16, gate, *, tm):
    bsz, s, d = x.shape
    na = mix_a.shape[-1]
    assert mix_b.shape[-1] == na and w_out_bf16.shape[0] == 2 * na
    return pl.pallas_call(
        _out_proj_kernel,
        grid=(bsz, s // tm),
        in_specs=[
            pl.BlockSpec((None, tm, d), lambda b, i: (b, i, 0)),
            pl.BlockSpec((None, tm, na), lambda b, i: (b, i, 0)),
            pl.BlockSpec((None, tm, na), lambda b, i: (b, i, 0)),
            _resident((na, d), lambda b, i: (0, 0)),
            _resident((na, d), lambda b, i: (1, 0)),
            pl.BlockSpec((None, 1, d), lambda b, i: (b, 0, 0)),
        ],
        out_specs=pl.BlockSpec((None, tm, d), lambda b, i: (b, i, 0)),
        out_shape=jax.ShapeDtypeStruct((bsz, s, d), F32),
        compiler_params=_cparams(("arbitrary", "arbitrary")),
        name="out_proj",
    )(x, mix_a, mix_b, w_out_bf16, w_out_bf16, gate)


def _ffn_kernel(x_ref, nw_ref, sh_ref, sc_ref, g_ref, w1_ref, w2_ref, o_ref, *, tf):
    x = x_ref[...]
    hb = (_rms(x, nw_ref[...]) * (1.0 + sc_ref[...]) + sh_ref[...]).astype(BF16)
    f = w2_ref.shape[0]
    acc = jnp.zeros(x.shape, F32)
    for j in range(f // tf):
        gate = _dot(hb, w1_ref[:, j * tf:(j + 1) * tf])
        up = _dot(hb, w1_ref[:, f + j * tf:f + (j + 1) * tf])
        acc = acc + _dot((_silu(gate) * up).astype(BF16), w2_ref[j * tf:(j + 1) * tf, :])
    o_ref[...] = x + g_ref[...] * acc


def _ffn(x, nw, shift, scale, gate, w1_bf16, w2_bf16, *, tm, tf):
    bsz, s, d = x.shape
    f = w2_bf16.shape[0]
    mod = pl.BlockSpec((None, 1, d), lambda b, i: (b, 0, 0))
    return pl.pallas_call(
        functools.partial(_ffn_kernel, tf=tf),
        grid=(bsz, s // tm),
        in_specs=[
            pl.BlockSpec((None, tm, d), lambda b, i: (b, i, 0)),
            pl.BlockSpec((1, d), lambda b, i: (0, 0)),
            mod, mod, mod,
            _resident((d, 2 * f), lambda b, i: (0, 0)),
            _resident((f, d), lambda b, i: (0, 0)),
        ],
        out_specs=pl.BlockSpec((None, tm, d), lambda b, i: (b, i, 0)),
        out_shape=jax.ShapeDtypeStruct((bsz, s, d), F32),
        compiler_params=_cparams(("arbitrary", "arbitrary")),
        name="ffn",
    )(x, nw.reshape(1, d), shift, scale, gate, w1_bf16, w2_bf16)


HALO = max(POOL_WINDOWS)


def _cd_prep_kernel(u_ref, halo_ref, sq_ref, sk_ref, sv_ref, pw_ref, ps_ref, qn_ref, kn_ref,
                    pool_o, q_o, k_o, v_o, ext_ref, *, ts):
    i = pl.program_id(1)
    halo = halo_ref[...]
    ext_ref[0:HALO, :] = jnp.where(i == 0, jnp.zeros_like(halo), halo)
    ext_ref[HALO:HALO + ts, :] = u_ref[...]
    t = i * ts + lax.broadcasted_iota(I32, (ts, 1), 0)
    for g, w in enumerate(POOL_WINDOWS):
        gs = slice(g * POOL_CH, (g + 1) * POOL_CH)
        tot = ext_ref[HALO:HALO + ts, gs]
        for j in range(1, w):
            tot = tot + ext_ref[HALO - j:HALO - j + ts, gs]
        cnt = jnp.minimum(t + 1, w).astype(F32)
        pooled = tot / cnt - u_ref[:, gs]
        y = _dot(pooled.astype(BF16), pw_ref[g].astype(BF16)) * ps_ref[:, gs]
        pool_o[:, gs] = y.astype(pool_o.dtype)
    qn, kn = qn_ref[...], kn_ref[...]
    for h in range(SB_HEADS):
        hs = slice(h * SB_HEAD_DIM, (h + 1) * SB_HEAD_DIM)
        q_o[:, hs] = (_rms(sq_ref[:, hs], qn) * SB_HEAD_DIM ** -0.5).astype(q_o.dtype)
        k_o[:, hs] = _rms(sk_ref[:, hs], kn).astype(k_o.dtype)
    tk = v_o.shape[-1]
    for j in range(v_o.shape[0]):
        v_o[j] = sv_ref[j * tk:(j + 1) * tk, :].T.astype(v_o.dtype)


def _cd_prep(proj, pool_w, pool_scale, q_norm, k_norm, *, ts, tk):
    bsz, s, _ = proj.shape
    n = POOL_GROUPS * POOL_CH
    wide = lambda j: pl.BlockSpec((None, ts, n), lambda b, i: (b, i, j))
    vec = lambda m: pl.BlockSpec((1, m), lambda b, i: (0, 0))
    per_blk = ts // HALO
    return pl.pallas_call(
        functools.partial(_cd_prep_kernel, ts=ts),
        grid=(bsz, s // ts),
        in_specs=[
            wide(0),
            pl.BlockSpec((None, HALO, n), lambda b, i: (b, jnp.maximum(i * per_blk - 1, 0), 0)),
            wide(1), wide(2), wide(3),
            pl.BlockSpec((POOL_GROUPS, POOL_CH, POOL_CH), lambda b, i: (0, 0, 0)),
            vec(n), vec(SB_HEAD_DIM), vec(SB_HEAD_DIM),
        ],
        out_specs=[wide(0)] * 3 + [pl.BlockSpec((None, ts // tk, n, tk), lambda b, i: (b, i, 0, 0))],
        out_shape=[jax.ShapeDtypeStruct((bsz, s, n), BF16)] * 3
        + [jax.ShapeDtypeStruct((bsz, s // tk, n, tk), BF16)],
        scratch_shapes=[pltpu.VMEM((HALO + ts, n), F32)],
        compiler_params=_cparams(("arbitrary", "arbitrary")),
        name="cd_prep",
    )(proj, proj, proj, proj, proj, pool_w, pool_scale.reshape(1, n),
      q_norm.reshape(1, -1), k_norm.reshape(1, -1))


def _sb_kernel(q_ref, k_ref, vt_ref, o_ref, acc_ref, *, tq):
    i = pl.program_id(1)
    kofs = lax.broadcasted_iota(I32, (tq, tq), 0)
    qofs = lax.broadcasted_iota(I32, (tq, tq), 1)
    later = jnp.where(qofs > kofs, 1.0, 0.0).astype(BF16)
    heads = [slice(h * SB_HEAD_DIM, (h + 1) * SB_HEAD_DIM) for h in range(SB_HEADS)]
    acc_ref[...] = jnp.zeros_like(acc_ref)

    def block(kb, runs, diagonal):
        k0 = pl.multiple_of(kb * tq, tq)
        z = [_dot_nt(k_ref[pl.ds(k0, tq), hs], q_ref[:, hs]) for hs in heads]
        sp = [jnp.log(1.0 + jnp.exp(-jnp.abs(x))) for x in z]
        log_beta = [jnp.minimum(x, 0.0) - s for x, s in zip(z, sp)]
        log_1m = [-jnp.maximum(x, 0.0) - s for x, s in zip(z, sp)]
        if diagonal:
            strict = kofs < qofs
            log_1m = [jnp.where(strict, x, 0.0) for x in log_1m]
        hi = [x.astype(BF16) for x in log_1m]
        lo = [(x - h.astype(F32)).astype(BF16) for x, h in zip(log_1m, hi)]
        after = [_dot(later, h) + _dot(later, l) + r for h, l, r in zip(hi, lo, runs)]
        w = [jnp.exp(lb + a) for lb, a in zip(log_beta, after)]
        if diagonal:
            w = [jnp.where(strict, x, 0.0) for x in w]
        for h, hs in enumerate(heads):
            acc_ref[h] = acc_ref[h] + _dot(vt_ref[kb, hs, :], w[h].astype(BF16))
        return tuple(r + jnp.sum(x, axis=0, keepdims=True) for r, x in zip(runs, log_1m))

    def alive(runs):
        worst = functools.reduce(jnp.maximum, runs)
        return jnp.max(worst) > SB_DEAD_LOG

    runs = block(i, tuple(jnp.zeros((1, tq), F32) for _ in heads), True)

    def earlier(state):
        kb, runs, _ = state
        runs = block(kb, runs, False)
        return kb - 1, runs, alive(runs)

    lax.while_loop(lambda st: (st[0] >= 0) & st[2], earlier, (i - 1, runs, alive(runs)))
    for h, hs in enumerate(heads):
        o_ref[:, hs] = acc_ref[h].T.astype(o_ref.dtype)


def _sb(q, k, vt, *, tq):
    bsz, s, hd = q.shape
    return pl.pallas_call(
        functools.partial(_sb_kernel, tq=tq),
        grid=(bsz, s // tq),
        in_specs=[
            pl.BlockSpec((None, tq, hd), lambda b, i: (b, i, 0)),
            _resident((None, s, hd), lambda b, i: (b, 0, 0)),
            _resident((None, s // tq, hd, tq), lambda b, i: (b, 0, 0, 0)),
        ],
        out_specs=pl.BlockSpec((None, tq, hd), lambda b, i: (b, i, 0)),
        out_shape=jax.ShapeDtypeStruct((bsz, s, hd), BF16),
        scratch_shapes=[pltpu.VMEM((SB_HEADS, SB_HEAD_DIM, tq), F32)],
        compiler_params=_cparams(("arbitrary", "arbitrary")),
        name="stick_breaking",
    )(q, k, vt)


def _pack_ab_weight(w):
    d = w.shape[0]
    gq, gk, gv, glow, gr, dq, dk, dv, iq, ik, iw = jnp.split(
        w, [256, 512, 1024, 1040, 1552, 2064, 2576, 3088, 3600, 3664], axis=1)
    pad = jnp.zeros((d, AB_NPAD - AB_SMALL - (IDX_DIM + GLA_GATE_RANK + IDX_HEADS)), w.dtype)
    return jnp.concatenate([gq, gk, gv, gr, dq, dk, dv, iq, ik, glow, iw, pad], axis=1)


def kernel(x, c, positions, ada_w, ada_b, mix_norm, ffn_norm, ffn_w1, ffn_w2, ab_w_in, gla_gate_up,
           gla_gate_b, gla_out_norm, dsa_q_norm, dsa_k_norm, ab_w_out, cd_w_in, pool_w, pool_scale,
           sb_q_norm, sb_k_norm, cd_w_out):
    bsz, s, d = x.shape
    depth = ada_w.shape[0]
    mod = _ada_mod(c, ada_w, ada_b).reshape(depth, bsz, 6, 1, d)
    tm = min(512, s)
    tq = min(256, s)
    for layer in range(depth):
        sh1, sc1, g1, sh2, sc2, g2 = (mod[layer, :, j] for j in range(6))
        i = layer // 2
        if layer % 2 == 0:
            proj = _norm_proj(x, mix_norm[layer], sh1, sc1, _pack_ab_weight(ab_w_in[i]).astype(BF16),
                              tm=min(256, s), tn=768)
            mix_a = _gla(proj, gla_gate_up[i], gla_gate_b[i], gla_out_norm[i], ts=tm)
            q, k, vt, iq, ik, iw = _dsa_prep(proj, positions, dsa_q_norm[i], dsa_k_norm[i], ts=tm, tk=tq)
            mix_b = _dsa(q, k, vt, iq, ik, iw, tq=tq)
            w_out = ab_w_out[i]
        else:
            proj = _norm_proj(x, mix_norm[layer], sh1, sc1, cd_w_in[i].astype(BF16), tm=tm, tn=512)
            mix_a, q, k, vt = _cd_prep(proj, pool_w[i], pool_scale[i], sb_q_norm[i], sb_k_norm[i],
                                       ts=tm, tk=tq)
            mix_b = _sb(q, k, vt, tq=tq)
            w_out = cd_w_out[i]
        x = _out_proj(x, mix_a, mix_b, w_out.astype(BF16), g1, tm=tm)
        x = _ffn(x, ffn_norm[layer], sh2, sc2, g2, ffn_w1[layer].astype(BF16),
                 ffn_w2[layer].astype(BF16), tm=tm, tf=256)
    return x
```

```python
import functools

import jax
import jax.numpy as jnp
from jax import lax
from jax.experimental import pallas as pl
from jax.experimental.pallas import tpu as pltpu

F32 = jnp.float32
BF16 = jnp.bfloat16
I32 = jnp.int32

D_MODEL = 1024
GLA_HEADS, GLA_DK, GLA_DV = 4, 64, 128
GLA_GATE_RANK = 16
GLA_GATE_TAU = 16.0
GLA_CHUNK = 64
DSA_HEADS, DSA_HEAD_DIM = 4, 128
IDX_HEADS, IDX_DIM = 8, 64
DSA_TOPK = 256
POOL_WINDOWS = (2, 4, 8, 16)
POOL_GROUPS, POOL_CH = 4, 128
SB_HEADS, SB_HEAD_DIM = 4, 128
ROPE_THETA = 10000.0
NORM_EPS = 1e-6
D_FF = 2816

LANES = 128
VMEM_LIMIT = 56 * 1024 * 1024

AB_GQ, AB_GK, AB_GV, AB_GR = 0, 256, 512, 1024
AB_DQ, AB_DK, AB_DV, AB_IQ = 1536, 2048, 2560, 3072
AB_SMALL = 3584
SM_IK, SM_GLOW, SM_IW = 0, 64, 80
AB_NPAD = 3840

DSA_Q_SCALE = 1.4426950408889634 * DSA_HEAD_DIM ** -0.5
INT_MIN = -2 ** 31
INT_MAX = 2 ** 31 - 1
PEEL_AFTER_BITS = 4
MASK_BIAS = -1e30
SB_DEAD_LOG = -110.0


def _dot(a, b):
    return jnp.dot(a, b, preferred_element_type=F32)


def _dot_nt(a, b):
    return lax.dot_general(a, b, (((1,), (1,)), ((), ())), preferred_element_type=F32)


def _dot_tn(a, b):
    return lax.dot_general(a, b, (((0,), (0,)), ((), ())), preferred_element_type=F32)


def _split3(a):
    hi = a.astype(BF16)
    r1 = a - hi.astype(F32)
    mid = r1.astype(BF16)
    lo = (r1 - mid.astype(F32)).astype(BF16)
    return hi, mid, lo


def _silu(x):
    return x * jax.nn.sigmoid(x)


def _rms(x, w):
    var = jnp.mean(x * x, axis=-1, keepdims=True)
    return x * lax.rsqrt(var + NORM_EPS) * w


def _cparams(sem):
    return pltpu.CompilerParams(dimension_semantics=sem, vmem_limit_bytes=VMEM_LIMIT)


def _resident(block_shape, index_map):
    return pl.BlockSpec(block_shape, index_map, pipeline_mode=pl.Buffered(1))


def _ada_kernel(c_ref, w_ref, b_ref, o_ref):
    cond = _silu(c_ref[...])
    c_hi, c_mid, _ = _split3(cond)
    w_hi, w_mid, _ = _split3(w_ref[...])
    acc = _dot(c_hi, w_hi) + _dot(c_hi, w_mid) + _dot(c_mid, w_hi)
    o_ref[...] = acc + b_ref[...]


def _ada_mod(c, ada_w, ada_b):
    depth, d, n = ada_w.shape
    bsz = c.shape[0]
    rows = 16
    cp =jnp.zeros((rows, d), F32).at[:bsz].set(c)
    tn = 1536
    out = pl.pallas_call(
        _ada_kernel,
        grid=(depth, n // tn),
        in_specs=[
            pl.BlockSpec((rows, d), lambda l, j: (0, 0)),
            pl.BlockSpec((None, d, tn), lambda l, j: (l, 0, j)),
            pl.BlockSpec((None, 1, tn), lambda l, j: (l, 0, j)),
        ],
        out_specs=pl.BlockSpec((None, rows, tn), lambda l, j: (l, 0, j)),
        out_shape=jax.ShapeDtypeStruct((depth, rows, n), F32),
        compiler_params=_cparams(("arbitrary", "arbitrary")),
        name="ada_mod",
    )(cp, ada_w, ada_b.reshape(depth, 1, n))
    return out[:, :bsz]


def _norm_proj_kernel(x_ref, nw_ref, sh_ref, sc_ref, w_ref, o_ref, *, tn):
    h = _rms(x_ref[...], nw_ref[...]) * (1.0 + sc_ref[...]) + sh_ref[...]
    hb = h.astype(BF16)
    n = w_ref.shape[1]
    for j in range(n // tn):
        o_ref[:, j * tn:(j + 1) * tn] = _dot(hb, w_ref[:, j * tn:(j + 1) * tn])


def _norm_proj(x, nw, shift, scale, w_bf16, *, tm, tn):
    bsz, s, d = x.shape
    n = w_bf16.shape[1]
    return pl.pallas_call(
        functools.partial(_norm_proj_kernel, tn=tn),
        grid=(bsz, s // tm),
        in_specs=[
            pl.BlockSpec((None, tm, d), lambda b, i: (b, i, 0)),
            pl.BlockSpec((1, d), lambda b, i: (0, 0)),
            pl.BlockSpec((None, 1, d), lambda b, i: (b, 0, 0)),
            pl.BlockSpec((None, 1, d), lambda b, i: (b, 0, 0)),
            _resident((d, n), lambda b, i: (0, 0)),
        ],
        out_specs=pl.BlockSpec((None, tm, n), lambda b, i: (b, i, 0)),
        out_shape=jax.ShapeDtypeStruct((bsz, s, n), F32),
        compiler_params=_cparams(("arbitrary", "arbitrary")),
        name="norm_proj",
    )(x, nw.reshape(1, d), shift, scale, w_bf16)


def _gla_kernel(q_ref, k_ref, v_ref, gr_ref, sm_ref, gup_ref, gb_ref, onw_ref, o_ref, st_ref, *, ts):
    c = GLA_CHUNK

    @pl.when(pl.program_id(1) == 0)
    def _():
        st_ref[...] = jnp.zeros_like(st_ref)

    nc = ts // c
    row = lax.broadcasted_iota(I32, (ts, ts), 0)
    col = lax.broadcasted_iota(I32, (ts, ts), 1)
    tri = jnp.where((row // c == col // c) & (col <= row), 1.0, 0.0).astype(BF16)
    causal = lax.broadcasted_iota(I32, (c, c), 1) <= lax.broadcasted_iota(I32, (c, c), 0)
    onw = onw_ref[...]

    glow = sm_ref[:, SM_GLOW:SM_GLOW + GLA_GATE_RANK]
    a = _dot(glow.astype(BF16), gup_ref[...].astype(BF16)) + gb_ref[...]
    g = (jnp.minimum(a, 0.0) - jnp.log1p(jnp.exp(-jnp.abs(a)))) / GLA_GATE_TAU
    g_hi, g_mid, g_lo = _split3(g)
    b = _dot(tri, g_hi) + _dot(tri, g_mid) + _dot(tri, g_lo)
    qs = q_ref[...] * (GLA_DK ** -0.5)
    k = k_ref[...]

    qe, ke, kd, qd, dec = [], [], [], [], []
    for ci in range(nc):
        r = slice(ci * c, (ci + 1) * c)
        bc = b[r]
        b_mid = bc[c // 2 - 1:c // 2, :]
        b_last = bc[c - 1:c, :]
        qe.append((qs[r] * jnp.exp(bc - b_mid)).astype(BF16))
        ke.append((k[r] * jnp.exp(b_mid - bc)).astype(BF16))
        kd.append((k[r] * jnp.exp(b_last - bc)).astype(BF16))
        qd.append((qs[r] * jnp.exp(bc)).astype(BF16))
        dec.append(jnp.exp(b_last))

    pairs = [(ci, h) for ci in range(nc) for h in range(GLA_HEADS)]
    ksl = lambda h: slice(h * GLA_DK, (h + 1) * GLA_DK)
    vsl = lambda h: slice(h * GLA_DV, (h + 1) * GLA_DV)
    vh = {(ci, h): v_ref[ci * c:(ci + 1) * c, vsl(h)].astype(BF16) for ci, h in pairs}
    att = {p: _dot_nt(qe[p[0]][:, ksl(p[1])], ke[p[0]][:, ksl(p[1])]) for p in pairs}
    att = {p: jnp.where(causal, att[p], 0.0).astype(BF16) for p in pairs}
    o = {p: _dot(att[p], vh[p]) for p in pairs}
    kvt = {p: _dot_tn(vh[p], kd[p[0]][:, ksl(p[1])]) for p in pairs}
    st_in = {}
    for h in range(GLA_HEADS):
        st = st_ref[h]
        for ci in range(nc):
            st_in[(ci, h)] = st.astype(BF16)
            st = st * dec[ci][:, ksl(h)] + kvt[(ci, h)]
        st_ref[h] = st
    for ci, h in pairs:
        out = o[(ci, h)] + _dot_nt(qd[ci][:, ksl(h)], st_in[(ci, h)])
        grh = gr_ref[ci * c:(ci + 1) * c, vsl(h)]
        o_ref[ci * c:(ci + 1) * c, vsl(h)] = (_rms(out, onw) * _silu(grh)).astype(o_ref.dtype)


def _gla(proj, gate_up, gate_b, out_norm, *, ts):
    bsz, s, _ = proj.shape
    hk, hv = GLA_HEADS * GLA_DK, GLA_HEADS * GLA_DV
    return pl.pallas_call(
        functools.partial(_gla_kernel, ts=ts),
        grid=(bsz, s // ts),
        in_specs=[
            pl.BlockSpec((None, ts, hk), lambda b, i: (b, i, AB_GQ // hk)),
            pl.BlockSpec((None, ts, hk), lambda b, i: (b, i, AB_GK // hk)),
            pl.BlockSpec((None, ts, hv), lambda b, i: (b, i, AB_GV // hv)),
            pl.BlockSpec((None, ts, hv), lambda b, i: (b, i, AB_GR // hv)),
            pl.BlockSpec((None, ts, LANES), lambda b, i: (b, i, AB_SMALL // LANES)),
            pl.BlockSpec((GLA_GATE_RANK, hk), lambda b, i: (0, 0)),
            pl.BlockSpec((1, hk), lambda b, i: (0, 0)),
            pl.BlockSpec((1, GLA_DV), lambda b, i: (0, 0)),
        ],
        out_specs=pl.BlockSpec((None, ts, hv), lambda b, i: (b, i, 0)),
        out_shape=jax.ShapeDtypeStruct((bsz, s, hv), BF16),
        scratch_shapes=[pltpu.VMEM((GLA_HEADS, GLA_DV, GLA_DK), F32)],
        compiler_params=_cparams(("arbitrary", "arbitrary")),
        name="gla",
    )(proj, proj, proj, proj, proj, gate_up, gate_b.reshape(1, hk), out_norm.reshape(1, GLA_DV))


def _dsa_prep_kernel(dq_ref, dk_ref, dv_ref, iq_ref, sm_ref, pos_ref, qn_ref, kn_ref, freq_ref,
                     q_o, k_o, v_o, iq_o, ik_o, iw_o):
    pos = pos_ref[...].astype(F32)
    lane = lax.broadcasted_iota(I32, (1, LANES), 1)
    half_a, half_i = DSA_HEAD_DIM // 2, IDX_DIM // 2

    ang = pos * freq_ref[...]
    cos_t, sin_t = jnp.cos(ang), jnp.sin(ang)

    def attn_table(t):
        return jnp.where(lane < half_a, t, pltpu.roll(t, half_a, 1))

    def idx_table(t):
        return jnp.where(lane < half_i, pltpu.roll(t, half_a, 1),
                         jnp.where(lane < 2 * half_i, pltpu.roll(t, half_a + half_i, 1), t))

    cos_a = attn_table(cos_t)
    sin_a = jnp.where(lane < half_a, -1.0, 1.0) * attn_table(sin_t)

    def rope_attn(t):
        return t * cos_a + pltpu.roll(t, half_a, 1) * sin_a

    first = (lane % IDX_DIM) < half_i
    cos_i = idx_table(cos_t)
    sin_i = jnp.where(first, -1.0, 1.0) * idx_table(sin_t)

    def rope_idx(t):
        rot = jnp.where(first, pltpu.roll(t, LANES - IDX_DIM // 2, 1), pltpu.roll(t, IDX_DIM // 2, 1))
        return t * cos_i + rot * sin_i

    qn, kn = qn_ref[...], kn_ref[...]
    for h in range(DSA_HEADS):
        hs = slice(h * DSA_HEAD_DIM, (h + 1) * DSA_HEAD_DIM)
        q_o[:, hs] = (rope_attn(_rms(dq_ref[:, hs], qn)) * DSA_Q_SCALE).astype(q_o.dtype)
        k_o[:, hs] = rope_attn(_rms(dk_ref[:, hs], kn)).astype(k_o.dtype)
    tk = v_o.shape[-1]
    for j in range(v_o.shape[0]):
        v_o[j] = dv_ref[j * tk:(j + 1) * tk, :].T.astype(v_o.dtype)
    for j in range(IDX_HEADS * IDX_DIM // LANES):
        r = rope_idx(iq_ref[:, j * LANES:(j + 1) * LANES])
        iq_o[2 * j] = r[:, :IDX_DIM].astype(iq_o.dtype)
        iq_o[2 * j + 1] = r[:, IDX_DIM:].astype(iq_o.dtype)
    sm = sm_ref[...]
    ik_o[...] = rope_idx(sm)[:, SM_IK:SM_IK + IDX_DIM].astype(ik_o.dtype)
    iw_o[...] = sm.T[SM_IW:SM_IW + IDX_HEADS, :]


def _dsa_prep(proj, positions, q_norm, k_norm, *, ts, tk):
    bsz, s, _ = proj.shape
    hd = DSA_HEADS * DSA_HEAD_DIM
    half_a, half_i = DSA_HEAD_DIM // 2, IDX_DIM // 2
    inv_a = ROPE_THETA ** (-jnp.arange(half_a, dtype=F32) / half_a)
    inv_i = ROPE_THETA ** (-jnp.arange(half_i, dtype=F32) / half_i)
    freq = jnp.concatenate([inv_a, inv_i, inv_i]).reshape(1, LANES)
    wide = lambda j: pl.BlockSpec((None, ts, hd), lambda b, i: (b, i, j))
    vec = lambda n: pl.BlockSpec((1, n), lambda b, i: (0, 0))
    return pl.pallas_call(
        _dsa_prep_kernel,
        grid=(bsz, s // ts),
        in_specs=[
            wide(AB_DQ // hd), wide(AB_DK // hd), wide(AB_DV // hd), wide(AB_IQ // hd),
            pl.BlockSpec((None, ts, LANES), lambda b, i: (b, i, AB_SMALL // LANES)),
            pl.BlockSpec((None, ts, 1), lambda b, i: (b, i, 0)),
            vec(DSA_HEAD_DIM), vec(DSA_HEAD_DIM), vec(LANES),
        ],
        out_specs=[
            wide(0), wide(0),
            pl.BlockSpec((None, ts // tk, hd, tk), lambda b, i: (b, i, 0, 0)),
            pl.BlockSpec((None, IDX_HEADS, ts, IDX_DIM), lambda b, i: (b, 0, i, 0)),
            pl.BlockSpec((None, ts, IDX_DIM), lambda b, i: (b, i, 0)),
            pl.BlockSpec((None, IDX_HEADS, ts), lambda b, i: (b, 0, i)),
        ],
        out_shape=[
            jax.ShapeDtypeStruct((bsz, s, hd), BF16),
            jax.ShapeDtypeStruct((bsz, s, hd), BF16),
            jax.ShapeDtypeStruct((bsz, s // tk, hd, tk), BF16),
            jax.ShapeDtypeStruct((bsz, IDX_HEADS, s, IDX_DIM), BF16),
            jax.ShapeDtypeStruct((bsz, s, IDX_DIM), BF16),
            jax.ShapeDtypeStruct((bsz, IDX_HEADS, s), F32),
        ],
        compiler_params=_cparams(("arbitrary", "arbitrary")),
        name="dsa_prep",
    )(proj, proj, proj, proj, proj, positions.reshape(bsz, s, 1),
      q_norm.reshape(1, -1), k_norm.reshape(1, -1), freq)


def _dsa_kernel(q_ref, iq_ref, iw_ref, k_ref, vt_ref, ik_ref, o_ref, keys_ref, top_ref, acc_ref,
                *, tq, n_sel, s_len):
    i = pl.program_id(1)
    nkb = i + 1
    kofs = lax.broadcasted_iota(I32, (tq, tq), 0)
    qidx = i * tq + lax.broadcasted_iota(I32, (tq, tq), 1)
    iw = iw_ref[...]
    idx_scale = (IDX_HEADS ** -0.5) * (IDX_DIM ** -0.5)

    def scores(kb, diagonal):
        k0 = pl.multiple_of(kb * tq, tq)
        ikb = ik_ref[pl.ds(k0, tq), :]
        sc = jnp.zeros((tq, tq), F32)
        for h in range(IDX_HEADS):
            sc = sc + jnp.maximum(_dot_nt(ikb, iq_ref[h]), 0.0) * iw[h:h + 1, :]
        bits = pltpu.bitcast(sc * idx_scale, I32)
        bits = jnp.where((bits & 0x7F800000) == 0, 0, bits)
        key = bits ^ ((bits >> 31) & 0x7FFFFFFF)
        top = pltpu.bitcast(bits & jnp.int32(-65536), F32)
        if diagonal:
            valid = kofs <= qidx - i * tq
            key = jnp.where(valid, key, INT_MIN)
            top = jnp.where(valid, top, jnp.nan)
        keys_ref[kb] = key
        top_ref[kb] = top.astype(BF16)

    def below_diagonal(kb, carry):
        scores(kb, False)
        return carry

    lax.fori_loop(0, i, below_diagonal, 0)
    scores(i, True)

    def count(pred):
        def body(kb, acc):
            m = jnp.where(pred(keys_ref[kb], kb * tq + kofs), 1, 0)
            return acc + jnp.sum(m.reshape(tq // 8, 8, tq), axis=0)
        acc = lax.fori_loop(0, nkb, body, jnp.zeros((8, tq), I32))
        return jnp.sum(acc, axis=0, keepdims=True)

    n_valid = qidx[0:1, :] + 1

    def count_top(cand):
        k16 = cand >> 16
        pattern = k16 ^ ((k16 >> 15) & 0x7FFF)
        pattern = jnp.where((pattern > 0) & (pattern < 0x80), 0x80, pattern)
        cand_bf = pltpu.bitcast(pattern << 16, F32).astype(BF16)

        def body(kb, acc):
            m = jnp.where(top_ref[kb] >= cand_bf, jnp.ones((), BF16), jnp.zeros((), BF16))
            part = m[0:16]
            for j in range(1, tq // 16):
                part = part + m[16 * j:16 * (j + 1)]
            return acc + part.astype(F32)
        acc = lax.fori_loop(0, nkb, body, jnp.zeros((16, tq), F32))
        return jnp.sum(acc, axis=0, keepdims=True).astype(I32)

    def bit_step(counter, it, u, cnt_u, cnt_above, final):
        cand_u = u | lax.shift_left(jnp.int32(1), jnp.int32(31) - it)
        cnt = counter(cand_u ^ INT_MIN)
        take = (cnt >= n_sel) & (final == 0)
        drop = (cnt < n_sel) & (final == 0)
        return (jnp.where(take, cand_u, u), jnp.where(take, cnt, cnt_u),
                jnp.where(drop, cnt, cnt_above), final)

    def any_lane(flag):
        return jnp.max(jnp.where(flag, 1.0, 0.0)) > 0.0

    def open_queries(st):
        return (st[2] > n_sel) & (st[4] == 0)

    count_ge = lambda cand: count(lambda key, c: key >= cand)
    st = (jnp.zeros((1, tq), I32), n_valid, jnp.zeros((1, tq), I32), jnp.zeros((1, tq), I32))
    st = lax.fori_loop(0, 16, lambda it, s: bit_step(count_top, it, *s), st)

    def low_bits(last_bit, st):
        step = lambda s: (s[0] + 1,) + bit_step(count_ge, s[0], *s[1:])
        cond = lambda s: (s[0] < last_bit) & any_lane(open_queries(s))
        return lax.while_loop(cond, step, st)

    st = low_bits(16 + PEEL_AFTER_BITS, (jnp.int32(16),) + st)

    def peel(st):
        it, u, cnt_u, cnt_above, final = st
        lo = u ^ INT_MIN
        hi = lo + (1 << (16 - PEEL_AFTER_BITS))

        def body(kb, carry):
            top, bot = carry
            key = keys_ref[kb].reshape(tq // 8, 8, tq)
            top = jnp.maximum(top, jnp.max(jnp.where(key < hi, key, INT_MIN), axis=0))
            bot = jnp.minimum(bot, jnp.min(jnp.where(key >= lo, key, INT_MAX), axis=0))
            return top, bot
        top, bot = lax.fori_loop(0, nkb, body, (jnp.full((8, tq), INT_MIN, I32), jnp.full((8, tq), INT_MAX, I32)))
        top = jnp.max(top, axis=0, keepdims=True)
        bot = jnp.min(bot, axis=0, keepdims=True)
        from_top = (cnt_above == n_sel - 1) | (top == bot)
        from_bot = (cnt_u == n_sel + 1) & jnp.logical_not(from_top)
        guess = jnp.where(from_top, top, bot + 1)
        cnt = count_ge(guess)
        short = from_bot & (cnt < n_sel)
        new_thr = jnp.where(short, bot, guess)
        new_cnt = jnp.where(short, cnt_u, cnt)
        hit = open_queries(st) & (from_top | from_bot)
        return (it, jnp.where(hit, new_thr ^ INT_MIN, u), jnp.where(hit, new_cnt, cnt_u), cnt_above,
                jnp.where(hit, 1, final))

    st = lax.cond(any_lane(open_queries(st)), peel, lambda s: s, st)
    _, u, cnt_u, _, _ = low_bits(32, st)
    thr = jnp.maximum(u ^ INT_MIN, INT_MIN + 1)
    excess = cnt_u > n_sel

    def tie_limit():
        need = n_sel - count(lambda key, c: key > thr)
        nbits = s_len.bit_length()

        def step(it, jv):
            cand = jv | lax.shift_left(jnp.int32(1), jnp.int32(nbits - 1) - it)
            cnt = count(lambda key, c: (key == thr) & (c < cand))
            return jnp.where(cnt < need, cand, jv)
        jv = lax.fori_loop(0, nbits, step, jnp.zeros((1, tq), I32))
        return jnp.where(excess, jv, s_len)

    tied = jnp.max(jnp.where(excess, 1.0, 0.0)) > 0.0

    acc_ref[...] = jnp.zeros_like(acc_ref)

    def attend(select):
        def block(kb, carry, width):
            ms, ls = carry
            k0 = pl.multiple_of(kb * tq, tq)
            rows = width * tq
            sel = select(keys_ref[pl.ds(kb, width)].reshape(rows, tq), kb, rows)
            heads = [slice(h * DSA_HEAD_DIM, (h + 1) * DSA_HEAD_DIM) for h in range(DSA_HEADS)]
            st = [_dot_nt(k_ref[pl.ds(k0, rows), hs], q_ref[:, hs]) for hs in heads]
            st = [jnp.where(sel, x, MASK_BIAS) for x in st]
            new_m = [jnp.maximum(m, jnp.max(x, axis=0, keepdims=True)) for m, x in zip(ms, st)]
            alpha = [jnp.exp2(m - mn) for m, mn in zip(ms, new_m)]
            p = [jnp.exp2(x - mn) for x, mn in zip(st, new_m)]
            new_l = [a * l + jnp.sum(x, axis=0, keepdims=True) for a, l, x in zip(alpha, ls, p)]
            p = [x.astype(BF16) for x in p]
            for h, hs in enumerate(heads):
                pv = _dot(vt_ref[kb, hs, :], p[h][0:tq])
                for j in range(1, width):
                    pv = pv + _dot(vt_ref[kb + j, hs, :], p[h][j * tq:(j + 1) * tq])
                acc_ref[h] = alpha[h] * acc_ref[h] + pv
            return tuple(new_m), tuple(new_l)

        carry = (tuple(jnp.full((1, tq), -jnp.inf, F32) for _ in range(DSA_HEADS)),
                 tuple(jnp.zeros((1, tq), F32) for _ in range(DSA_HEADS)))
        carry = lax.fori_loop(0, nkb // 2, lambda j, c: block(2 * j, c, 2), carry)
        carry = lax.cond(nkb % 2 == 1, lambda c: block(nkb - 1, c, 1), lambda c: c, carry)
        return carry[1]

    def attend_tied():
        jstar = tie_limit()

        def select(key, kb, rows):
            kidx = kb * tq + lax.broadcasted_iota(I32, (rows, tq), 0)
            return (key > thr) | ((key == thr) & (kidx <= jstar))
        return attend(select)

    ls = lax.cond(tied, attend_tied, lambda: attend(lambda key, kb, rows: key >= thr))
    for h in range(DSA_HEADS):
        hs = slice(h * DSA_HEAD_DIM, (h + 1) * DSA_HEAD_DIM)
        o_ref[:, hs] = (acc_ref[h] / ls[h]).T.astype(o_ref.dtype)


def _dsa(q, k, vt, iq, ik, iw, *, tq):
    bsz, s, hd = q.shape
    n_sel = min(DSA_TOPK, s // 4)
    resident = lambda n: _resident((None, s, n), lambda b, i: (b, 0, 0))
    return pl.pallas_call(
        functools.partial(_dsa_kernel, tq=tq, n_sel=n_sel, s_len=s),
        grid=(bsz, s // tq),
        in_specs=[
            pl.BlockSpec((None, tq, hd), lambda b, i: (b, i, 0)),
            pl.BlockSpec((None, IDX_HEADS, tq, IDX_DIM), lambda b, i: (b, 0, i, 0)),
            pl.BlockSpec((None, IDX_HEADS, tq), lambda b, i: (b, 0, i)),
            resident(hd),
            _resident((None, s // tq, hd, tq), lambda b, i: (b, 0, 0, 0)),
            resident(IDX_DIM),
        ],
        out_specs=pl.BlockSpec((None, tq, hd), lambda b, i: (b, i, 0)),
        out_shape=jax.ShapeDtypeStruct((bsz, s, hd), BF16),
        scratch_shapes=[
            pltpu.VMEM((s // tq, tq, tq), I32),
            pltpu.VMEM((s // tq, tq, tq), BF16),
            pltpu.VMEM((DSA_HEADS, DSA_HEAD_DIM, tq), F32),
        ],
        compiler_params=_cparams(("arbitrary", "arbitrary")),
        name="dsa",
    )(q, iq, iw, k, vt, ik)


def _mix_ffn_kernel(x_ref, a_ref, b_ref, wa_ref, wb_ref, g1_ref, nw_ref, sh_ref, sc_ref, g2_ref,
                    w1_ref, w2_ref, o_ref, *, tf):
    x = x_ref[...] + g1_ref[...] * (_dot(a_ref[...], wa_ref[...]) + _dot(b_ref[...], wb_ref[...]))
    hb = (_rms(x, nw_ref[...]) * (1.0 + sc_ref[...]) + sh_ref[...]).astype(BF16)
    f = w2_ref.shape[0]
    acc = jnp.zeros(x.shape, F32)
    for j in range(f // tf):
        gate = _dot(hb, w1_ref[:, j * tf:(j + 1) * tf])
        up = _dot(hb, w1_ref[:, f + j * tf:f + (j + 1) * tf])
        acc = acc + _dot((_silu(gate) * up).astype(BF16), w2_ref[j * tf:(j + 1) * tf, :])
    o_ref[...] = x + g2_ref[...] * acc


def _mix_ffn(x, mix_a, mix_b, w_out_bf16, gate1, nw, shift, scale, gate2, w1_bf16, w2_bf16, *, tm, tf):
    bsz, s, d = x.shape
    f = w2_bf16.shape[0]
    na = mix_a.shape[-1]
    assert mix_b.shape[-1] == na and w_out_bf16.shape[0] == 2 * na
    rows = lambda n: pl.BlockSpec((None, tm, n), lambda b, i: (b, i, 0))
    mod = pl.BlockSpec((None, 1, d), lambda b, i: (b, 0, 0))
    return pl.pallas_call(
        functools.partial(_mix_ffn_kernel, tf=tf),
        grid=(bsz, s // tm),
        in_specs=[
            rows(d), rows(na), rows(na),
            _resident((na, d), lambda b, i: (0, 0)),
            _resident((na, d), lambda b, i: (1, 0)),
            mod,
            pl.BlockSpec((1, d), lambda b, i: (0, 0)),
            mod, mod, mod,
            _resident((d, 2 * f), lambda b, i: (0, 0)),
            _resident((f, d), lambda b, i: (0, 0)),
        ],
        out_specs=rows(d),
        out_shape=jax.ShapeDtypeStruct((bsz, s, d), F32),
        compiler_params=_cparams(("arbitrary", "arbitrary")),
        name="mix_ffn",
    )(x, mix_a, mix_b, w_out_bf16, w_out_bf16, gate1, nw.reshape(1, d), shift, scale, gate2,
      w1_bf16, w2_bf16)


HALO = max(POOL_WINDOWS)


def _cd_prep_kernel(u_ref, halo_ref, sq_ref, sk_ref, sv_ref, pw_ref, ps_ref, qn_ref, kn_ref,
                    pool_o, q_o, k_o, v_o, ext_ref, *, ts):
    i = pl.program_id(1)
    halo = halo_ref[...]
    ext_ref[0:HALO, :] = jnp.where(i == 0, jnp.zeros_like(halo), halo)
    ext_ref[HALO:HALO + ts, :] = u_ref[...]
    t = i * ts + lax.broadcasted_iota(I32, (ts, 1), 0)
    for g, w in enumerate(POOL_WINDOWS):
        gs = slice(g * POOL_CH, (g + 1) * POOL_CH)
        tot = ext_ref[HALO:HALO + ts, gs]
        for j in range(1, w):
            tot = tot + ext_ref[HALO - j:HALO - j + ts, gs]
        cnt = jnp.minimum(t + 1, w).astype(F32)
        pooled = tot / cnt - u_ref[:, gs]
        y = _dot(pooled.astype(BF16), pw_ref[g].astype(BF16)) * ps_ref[:, gs]
        pool_o[:, gs] = y.astype(pool_o.dtype)
    qn, kn = qn_ref[...], kn_ref[...]
    for h in range(SB_HEADS):
        hs = slice(h * SB_HEAD_DIM, (h + 1) * SB_HEAD_DIM)
        q_o[:, hs] = (_rms(sq_ref[:, hs], qn) * SB_HEAD_DIM ** -0.5).astype(q_o.dtype)
        k_o[:, hs] = _rms(sk_ref[:, hs], kn).astype(k_o.dtype)
    tk = v_o.shape[-1]
    for j in range(v_o.shape[0]):
        v_o[j] = sv_ref[j * tk:(j + 1) * tk, :].T.astype(v_o.dtype)


def _cd_prep(proj, pool_w, pool_scale, q_norm, k_norm, *, ts, tk):
    bsz, s, _ = proj.shape
    n = POOL_GROUPS * POOL_CH
    wide = lambda j: pl.BlockSpec((None, ts, n), lambda b, i: (b, i, j))
    vec = lambda m: pl.BlockSpec((1, m), lambda b, i: (0, 0))
    per_blk = ts // HALO
    return pl.pallas_call(
        functools.partial(_cd_prep_kernel, ts=ts),
        grid=(bsz, s // ts),
        in_specs=[
            wide(0),
            pl.BlockSpec((None, HALO, n), lambda b, i: (b, jnp.maximum(i * per_blk - 1, 0), 0)),
            wide(1), wide(2), wide(3),
            pl.BlockSpec((POOL_GROUPS, POOL_CH, POOL_CH), lambda b, i: (0, 0, 0)),
            vec(n), vec(SB_HEAD_DIM), vec(SB_HEAD_DIM),
        ],
        out_specs=[wide(0)] * 3 + [pl.BlockSpec((None, ts // tk, n, tk), lambda b, i: (b, i, 0, 0))],
        out_shape=[jax.ShapeDtypeStruct((bsz, s, n), BF16)] * 3
        + [jax.ShapeDtypeStruct((bsz, s // tk, n, tk), BF16)],
        scratch_shapes=[pltpu.VMEM((HALO + ts, n), F32)],
        compiler_params=_cparams(("arbitrary", "arbitrary")),
        name="cd_prep",
    )(proj, proj, proj, proj, proj, pool_w, pool_scale.reshape(1, n),
      q_norm.reshape(1, -1), k_norm.reshape(1, -1))


def _sb_kernel(q_ref, k_ref, vt_ref, o_ref, acc_ref, *, tq):
    i = pl.program_id(1)
    kofs = lax.broadcasted_iota(I32, (tq, tq), 0)
    qofs = lax.broadcasted_iota(I32, (tq, tq), 1)
    later = jnp.where(qofs > kofs, 1.0, 0.0).astype(BF16)
    heads = [slice(h * SB_HEAD_DIM, (h + 1) * SB_HEAD_DIM) for h in range(SB_HEADS)]
    acc_ref[...] = jnp.zeros_like(acc_ref)

    def block(kb, runs, diagonal):
        k0 = pl.multiple_of(kb * tq, tq)
        z = [_dot_nt(k_ref[pl.ds(k0, tq), hs], q_ref[:, hs]) for hs in heads]
        sp = [jnp.log(1.0 + jnp.exp(-jnp.abs(x))) for x in z]
        log_beta = [jnp.minimum(x, 0.0) - s for x, s in zip(z, sp)]
        log_1m = [-jnp.maximum(x, 0.0) - s for x, s in zip(z, sp)]
        if diagonal:
            strict = kofs < qofs
            log_1m = [jnp.where(strict, x, 0.0) for x in log_1m]
        hi = [x.astype(BF16) for x in log_1m]
        lo = [(x - h.astype(F32)).astype(BF16) for x, h in zip(log_1m, hi)]
        after = [_dot(later, h) + _dot(later, l) + r for h, l, r in zip(hi, lo, runs)]
        w = [jnp.exp(lb + a) for lb, a in zip(log_beta, after)]
        if diagonal:
            w = [jnp.where(strict, x, 0.0) for x in w]
        for h, hs in enumerate(heads):
            acc_ref[h] = acc_ref[h] + _dot(vt_ref[kb, hs, :], w[h].astype(BF16))
        return tuple(r + jnp.sum(x, axis=0, keepdims=True) for r, x in zip(runs, log_1m))

    def alive(runs):
        worst = functools.reduce(jnp.maximum, runs)
        return jnp.max(worst) > SB_DEAD_LOG

    runs = block(i, tuple(jnp.zeros((1, tq), F32) for _ in heads), True)

    def earlier(state):
        kb, runs, _ = state
        runs = block(kb, runs, False)
        return kb - 1, runs, alive(runs)

    lax.while_loop(lambda st: (st[0] >= 0) & st[2], earlier, (i - 1, runs, alive(runs)))
    for h, hs in enumerate(heads):
        o_ref[:, hs] = acc_ref[h].T.astype(o_ref.dtype)


def _sb(q, k, vt, *, tq):
    bsz, s, hd = q.shape
    return pl.pallas_call(
        functools.partial(_sb_kernel, tq=tq),
        grid=(bsz, s // tq),
        in_specs=[
            pl.BlockSpec((None, tq, hd), lambda b, i: (b, i, 0)),
            _resident((None, s, hd), lambda b, i: (b, 0, 0)),
            _resident((None, s // tq, hd, tq), lambda b, i: (b, 0, 0, 0)),
        ],
        out_specs=pl.BlockSpec((None, tq, hd), lambda b, i: (b, i, 0)),
        out_shape=jax.ShapeDtypeStruct((bsz, s, hd), BF16),
        scratch_shapes=[pltpu.VMEM((SB_HEADS, SB_HEAD_DIM, tq), F32)],
        compiler_params=_cparams(("arbitrary", "arbitrary")),
        name="stick_breaking",
    )(q, k, vt)


def _pack_ab_weight(w):
    d = w.shape[0]
    gq, gk, gv, glow, gr, dq, dk, dv, iq, ik, iw = jnp.split(
        w, [256, 512, 1024, 1040, 1552, 2064, 2576, 3088, 3600, 3664], axis=1)
    pad = jnp.zeros((d, AB_NPAD - AB_SMALL - (IDX_DIM + GLA_GATE_RANK + IDX_HEADS)), w.dtype)
    return jnp.concatenate([gq, gk, gv, gr, dq, dk, dv, iq, ik, glow, iw, pad], axis=1)


def kernel(x, c, positions, ada_w, ada_b, mix_norm, ffn_norm, ffn_w1, ffn_w2, ab_w_in, gla_gate_up,
           gla_gate_b, gla_out_norm, dsa_q_norm, dsa_k_norm, ab_w_out, cd_w_in, pool_w, pool_scale,
           sb_q_norm, sb_k_norm, cd_w_out):
    bsz, s, d = x.shape
    depth = ada_w.shape[0]
    mod = _ada_mod(c, ada_w, ada_b).reshape(depth, bsz, 6, 1, d)
    tm = min(512, s)
    tq = min(256, s)
    for layer in range(depth):
        sh1, sc1, g1, sh2, sc2, g2 = (mod[layer, :, j] for j in range(6))
        i = layer // 2
        if layer % 2 == 0:
            proj = _norm_proj(x, mix_norm[layer], sh1, sc1, _pack_ab_weight(ab_w_in[i]).astype(BF16),
                              tm=min(256, s), tn=768)
            mix_a = _gla(proj, gla_gate_up[i], gla_gate_b[i], gla_out_norm[i], ts=tm)
            q, k, vt, iq, ik, iw = _dsa_prep(proj, positions, dsa_q_norm[i], dsa_k_norm[i], ts=tm, tk=tq)
            mix_b = _dsa(q, k, vt, iq, ik, iw, tq=tq)
            w_out = ab_w_out[i]
        else:
            proj = _norm_proj(x, mix_norm[layer], sh1, sc1, cd_w_in[i].astype(BF16), tm=tm, tn=512)
            mix_a, q, k, vt = _cd_prep(proj, pool_w[i], pool_scale[i], sb_q_norm[i], sb_k_norm[i],
                                       ts=tm, tk=tq)
            mix_b = _sb(q, k, vt, tq=tq)
            w_out = cd_w_out[i]
        x = _mix_ffn(x, mix_a, mix_b, w_out.astype(BF16), g1, ffn_norm[layer], sh2, sc2, g2,
                     ffn_w1[layer].astype(BF16), ffn_w2[layer].astype(BF16), tm=tm, tf=256)
    return x
```

```python
import functools

import jax
import jax.numpy as jnp
from jax import lax
from jax.experimental import pallas as pl
from jax.experimental.pallas import tpu as pltpu

F32 = jnp.float32
BF16 = jnp.bfloat16
I32 = jnp.int32

D_MODEL = 1024
GLA_HEADS, GLA_DK, GLA_DV = 4, 64, 128
GLA_GATE_RANK = 16
GLA_GATE_TAU = 16.0
GLA_CHUNK = 64
DSA_HEADS, DSA_HEAD_DIM = 4, 128
IDX_HEADS, IDX_DIM = 8, 64
DSA_TOPK = 256
POOL_WINDOWS = (2, 4, 8, 16)
POOL_GROUPS, POOL_CH = 4, 128
SB_HEADS, SB_HEAD_DIM = 4, 128
ROPE_THETA = 10000.0
NORM_EPS = 1e-6
D_FF = 2816

LANES = 128
VMEM_LIMIT = 56 * 1024 * 1024

AB_GQ, AB_GK, AB_GV, AB_GR = 0, 256, 512, 1024
AB_SMALL = 1536
SM_IK, SM_GLOW, SM_IW = 0, 64, 80
AB_GLA_N = AB_SMALL + LANES
AB_DSA_N = 4 * 512

DSA_Q_SCALE = 1.4426950408889634 * DSA_HEAD_DIM ** -0.5
INT_MIN = -2 ** 31
INT_MAX = 2 ** 31 - 1
FLT_TINY = 2.0 ** -126
KEY_TINY = 0x00800000
KEY_LOWEST = INT_MIN + 0x00800000
KEY_LOWEST_BF16 = INT_MIN + 0x00810000
BF16_HALF_STEP = 0x8000
PEEL_AFTER_STEPS = 5
MASK_BIAS = -1e30
SB_DEAD_LOG = -110.0


def _dot(a, b):
    return jnp.dot(a, b, preferred_element_type=F32)


def _dot_nt(a, b):
    return lax.dot_general(a, b, (((1,), (1,)), ((), ())), preferred_element_type=F32)


def _dot_tn(a, b):
    return lax.dot_general(a, b, (((0,), (0,)), ((), ())), preferred_element_type=F32)


def _split3(a):
    hi = a.astype(BF16)
    r1 = a - hi.astype(F32)
    mid = r1.astype(BF16)
    lo = (r1 - mid.astype(F32)).astype(BF16)
    return hi, mid, lo


def _silu(x):
    return x * jax.nn.sigmoid(x)


def _rms(x, w):
    var = jnp.mean(x * x, axis=-1, keepdims=True)
    return x * lax.rsqrt(var + NORM_EPS) * w


def _cparams(sem):
    return pltpu.CompilerParams(dimension_semantics=sem, vmem_limit_bytes=VMEM_LIMIT)


def _resident(block_shape, index_map):
    return pl.BlockSpec(block_shape, index_map, pipeline_mode=pl.Buffered(1))


def _ada_kernel(c_ref, w_ref, b_ref, o_ref):
    cond = _silu(c_ref[...])
    c_hi, c_mid, _ = _split3(cond)
    w_hi, w_mid, _ = _split3(w_ref[...])
    acc = _dot(c_hi, w_hi) + _dot(c_hi, w_mid) + _dot(c_mid, w_hi)
    o_ref[...] = acc + b_ref[...]


def _ada_mod(c, ada_w, ada_b):
    depth, d, n = ada_w.shape
    bsz = c.shape[0]
    rows = 16
    cp =jnp.zeros((rows, d), F32).at[:bsz].set(c)
    tn = 1536
    out = pl.pallas_call(
        _ada_kernel,
        grid=(depth, n // tn),
        in_specs=[
            pl.BlockSpec((rows, d), lambda l, j: (0, 0)),
            pl.BlockSpec((None, d, tn), lambda l, j: (l, 0, j)),
            pl.BlockSpec((None, 1, tn), lambda l, j: (l, 0, j)),
        ],
        out_specs=pl.BlockSpec((None, rows, tn), lambda l, j: (l, 0, j)),
        out_shape=jax.ShapeDtypeStruct((depth, rows, n), F32),
        compiler_params=_cparams(("arbitrary", "arbitrary")),
        name="ada_mod",
    )(cp, ada_w, ada_b.reshape(depth, 1, n))
    return out[:, :bsz]


def _gla_kernel(q_ref, k_ref, v_ref, gr_ref, sm_ref, gup_ref, gb_ref, onw_ref, o_ref, st_ref, *, ts):
    c = GLA_CHUNK

    @pl.when(pl.program_id(1) == 0)
    def _():
        st_ref[...] = jnp.zeros_like(st_ref)

    nc = ts // c
    row = lax.broadcasted_iota(I32, (ts, ts), 0)
    col = lax.broadcasted_iota(I32, (ts, ts), 1)
    tri = jnp.where((row // c == col // c) & (col <= row), 1.0, 0.0).astype(BF16)
    causal = lax.broadcasted_iota(I32, (c, c), 1) <= lax.broadcasted_iota(I32, (c, c), 0)
    onw = onw_ref[...]

    glow = sm_ref[:, SM_GLOW:SM_GLOW + GLA_GATE_RANK]
    a = _dot(glow.astype(BF16), gup_ref[...].astype(BF16)) + gb_ref[...]
    g = (jnp.minimum(a, 0.0) - jnp.log1p(jnp.exp(-jnp.abs(a)))) / GLA_GATE_TAU
    g_hi, g_mid, g_lo = _split3(g)
    b = _dot(tri, g_hi) + _dot(tri, g_mid) + _dot(tri, g_lo)
    qs = q_ref[...] * (GLA_DK ** -0.5)
    k = k_ref[...]

    qe, ke, kd, qd, dec = [], [], [], [], []
    for ci in range(nc):
        r = slice(ci * c, (ci + 1) * c)
        bc = b[r]
        b_mid = bc[c // 2 - 1:c // 2, :]
        b_last = bc[c - 1:c, :]
        qe.append((qs[r] * jnp.exp(bc - b_mid)).astype(BF16))
        ke.append((k[r] * jnp.exp(b_mid - bc)).astype(BF16))
        kd.append((k[r] * jnp.exp(b_last - bc)).astype(BF16))
        qd.append((qs[r] * jnp.exp(bc)).astype(BF16))
        dec.append(jnp.exp(b_last))

    pairs = [(ci, h) for ci in range(nc) for h in range(GLA_HEADS)]
    ksl = lambda h: slice(h * GLA_DK, (h + 1) * GLA_DK)
    vsl = lambda h: slice(h * GLA_DV, (h + 1) * GLA_DV)
    vh = {(ci, h): v_ref[ci * c:(ci + 1) * c, vsl(h)].astype(BF16) for ci, h in pairs}
    att = {p: _dot_nt(qe[p[0]][:, ksl(p[1])], ke[p[0]][:, ksl(p[1])]) for p in pairs}
    att = {p: jnp.where(causal, att[p], 0.0).astype(BF16) for p in pairs}
    o = {p: _dot(att[p], vh[p]) for p in pairs}
    kvt = {p: _dot_tn(vh[p], kd[p[0]][:, ksl(p[1])]) for p in pairs}
    st_in = {}
    for h in range(GLA_HEADS):
        st = st_ref[h]
        for ci in range(nc):
            st_in[(ci, h)] = st.astype(BF16)
            st = st * dec[ci][:, ksl(h)] + kvt[(ci, h)]
        st_ref[h] = st
    for ci, h in pairs:
        out = o[(ci, h)] + _dot_nt(qd[ci][:, ksl(h)], st_in[(ci, h)])
        grh = gr_ref[ci * c:(ci + 1) * c, vsl(h)]
        o_ref[ci * c:(ci + 1) * c, vsl(h)] = (_rms(out, onw) * _silu(grh)).astype(o_ref.dtype)


def _gla(proj, gate_up, gate_b, out_norm, *, ts):
    bsz, s, _ = proj.shape
    hk, hv = GLA_HEADS * GLA_DK, GLA_HEADS * GLA_DV
    return pl.pallas_call(
        functools.partial(_gla_kernel, ts=ts),
        grid=(bsz, s // ts),
        in_specs=[
            pl.BlockSpec((None, ts, hk), lambda b, i: (b, i, AB_GQ // hk)),
            pl.BlockSpec((None, ts, hk), lambda b, i: (b, i, AB_GK // hk)),
            pl.BlockSpec((None, ts, hv), lambda b, i: (b, i, AB_GV // hv)),
            pl.BlockSpec((None, ts, hv), lambda b, i: (b, i, AB_GR // hv)),
            pl.BlockSpec((None, ts, LANES), lambda b, i: (b, i, AB_SMALL // LANES)),
            pl.BlockSpec((GLA_GATE_RANK, hk), lambda b, i: (0, 0)),
            pl.BlockSpec((1, hk), lambda b, i: (0, 0)),
            pl.BlockSpec((1, GLA_DV), lambda b, i: (0, 0)),
        ],
        out_specs=pl.BlockSpec((None, ts, hv), lambda b, i: (b, i, 0)),
        out_shape=jax.ShapeDtypeStruct((bsz, s, hv), BF16),
        scratch_shapes=[pltpu.VMEM((GLA_HEADS, GLA_DV, GLA_DK), F32)],
        compiler_params=_cparams(("arbitrary", "arbitrary")),
        name="gla",
    )(proj, proj, proj, proj, proj, gate_up, gate_b.reshape(1, hk), out_norm.reshape(1, GLA_DV))


def _dsa_prep_kernel(dq_ref, dk_ref, dv_ref, iq_ref, sm_ref, pos_ref, qn_ref, kn_ref, freq_ref,
                     q_o, k_o, v_o, iq_o, ik_o, iw_o):
    pos = pos_ref[...].astype(F32)
    lane = lax.broadcasted_iota(I32, (1, LANES), 1)
    half_a, half_i = DSA_HEAD_DIM // 2, IDX_DIM // 2

    ang = pos * freq_ref[...]
    cos_t, sin_t = jnp.cos(ang), jnp.sin(ang)

    def attn_table(t):
        return jnp.where(lane < half_a, t, pltpu.roll(t, half_a, 1))

    def idx_table(t):
        return jnp.where(lane < half_i, pltpu.roll(t, half_a, 1),
                         jnp.where(lane < 2 * half_i, pltpu.roll(t, half_a + half_i, 1), t))

    cos_a = attn_table(cos_t)
    sin_a = jnp.where(lane < half_a, -1.0, 1.0) * attn_table(sin_t)

    def rope_attn(t):
        return t * cos_a + pltpu.roll(t, half_a, 1) * sin_a

    first = (lane % IDX_DIM) < half_i
    cos_i = idx_table(cos_t)
    sin_i = jnp.where(first, -1.0, 1.0) * idx_table(sin_t)

    def rope_idx(t):
        rot = jnp.where(first, pltpu.roll(t, LANES - IDX_DIM // 2, 1), pltpu.roll(t, IDX_DIM // 2, 1))
        return t * cos_i + rot * sin_i

    qn, kn = qn_ref[...], kn_ref[...]
    for h in range(DSA_HEADS):
        hs = slice(h * DSA_HEAD_DIM, (h + 1) * DSA_HEAD_DIM)
        q_o[:, hs] = (rope_attn(_rms(dq_ref[:, hs], qn)) * DSA_Q_SCALE).astype(q_o.dtype)
        k_o[:, hs] = rope_attn(_rms(dk_ref[:, hs], kn)).astype(k_o.dtype)
    tk = v_o.shape[-1]
    for j in range(v_o.shape[0]):
        v_o[j] = dv_ref[j * tk:(j + 1) * tk, :].T.astype(v_o.dtype)
    for j in range(IDX_HEADS * IDX_DIM // LANES):
        r = rope_idx(iq_ref[:, j * LANES:(j + 1) * LANES])
        iq_o[2 * j] = r[:, :IDX_DIM].astype(iq_o.dtype)
        iq_o[2 * j + 1] = r[:, IDX_DIM:].astype(iq_o.dtype)
    sm = sm_ref[...]
    ik_o[...] = rope_idx(sm)[:, SM_IK:SM_IK + IDX_DIM].astype(ik_o.dtype)
    iw_o[...] = sm.T[SM_IW:SM_IW + IDX_HEADS, :]


def _proj_ab_kernel(x_ref, nw_ref, sh_ref, sc_ref, w_ref, pos_ref, qn_ref, kn_ref, freq_ref,
                    gla_o, q_o, k_o, v_o, iq_o, ik_o, iw_o, dsa_ref):
    hb = (_rms(x_ref[...], nw_ref[...]) * (1.0 + sc_ref[...]) + sh_ref[...]).astype(BF16)
    gla_o[...] = _dot(hb, w_ref[:, :AB_GLA_N])
    hd = DSA_HEADS * DSA_HEAD_DIM
    for j in range(AB_DSA_N // hd):
        dsa_ref[:, j * hd:(j + 1) * hd] = _dot(hb, w_ref[:, AB_GLA_N + j * hd:AB_GLA_N + (j + 1) * hd])
    part = lambda j: dsa_ref.at[:, j * hd:(j + 1) * hd]
    _dsa_prep_kernel(part(0), part(1), part(2), part(3), gla_o.at[:, AB_SMALL:AB_SMALL + LANES],
                     pos_ref, qn_ref, kn_ref, freq_ref, q_o, k_o, v_o, iq_o, ik_o, iw_o)


def _proj_ab(x, nw, shift, scale, w_bf16, positions, q_norm, k_norm, *, tm, tk):
    bsz, s, d = x.shape
    hd = DSA_HEADS * DSA_HEAD_DIM
    half_a, half_i = DSA_HEAD_DIM // 2, IDX_DIM // 2
    inv_a = ROPE_THETA ** (-jnp.arange(half_a, dtype=F32) / half_a)
    inv_i = ROPE_THETA ** (-jnp.arange(half_i, dtype=F32) / half_i)
    freq = jnp.concatenate([inv_a, inv_i, inv_i]).reshape(1, LANES)
    rows = lambda n: pl.BlockSpec((None, tm, n), lambda b, i: (b, i, 0))
    vec = lambda n: pl.BlockSpec((1, n), lambda b, i: (0, 0))
    mod = pl.BlockSpec((None, 1, d), lambda b, i: (b, 0, 0))
    return pl.pallas_call(
        _proj_ab_kernel,
        grid=(bsz, s // tm),
        in_specs=[
            rows(d), vec(d), mod, mod,
            _resident((d, AB_GLA_N + AB_DSA_N), lambda b, i: (0, 0)),
            rows(1), vec(DSA_HEAD_DIM), vec(DSA_HEAD_DIM), vec(LANES),
        ],
        out_specs=[
            rows(AB_GLA_N), rows(hd), rows(hd),
            pl.BlockSpec((None, tm // tk, hd, tk), lambda b, i: (b, i, 0, 0)),
            pl.BlockSpec((None, IDX_HEADS, tm, IDX_DIM), lambda b, i: (b, 0, i, 0)),
            rows(IDX_DIM),
            pl.BlockSpec((None, IDX_HEADS, tm), lambda b, i: (b, 0, i)),
        ],
        out_shape=[
            jax.ShapeDtypeStruct((bsz, s, AB_GLA_N), F32),
            jax.ShapeDtypeStruct((bsz, s, hd), BF16),
            jax.ShapeDtypeStruct((bsz, s, hd), BF16),
            jax.ShapeDtypeStruct((bsz, s // tk, hd, tk), BF16),
            jax.ShapeDtypeStruct((bsz, IDX_HEADS, s, IDX_DIM), BF16),
            jax.ShapeDtypeStruct((bsz, s, IDX_DIM), BF16),
            jax.ShapeDtypeStruct((bsz, IDX_HEADS, s), F32),
        ],
        scratch_shapes=[pltpu.VMEM((tm, AB_DSA_N), F32)],
        compiler_params=_cparams(("arbitrary", "arbitrary")),
        name="proj_ab",
    )(x, nw.reshape(1, d), shift, scale, w_bf16, positions.reshape(bsz, s, 1),
      q_norm.reshape(1, -1), k_norm.reshape(1, -1), freq)


def _dsa_kernel(q_ref, iq_ref, iw_ref, k_ref, vt_ref, ik_ref, o_ref, sc_ref, top_ref, acc_ref,
                *, tq, n_sel, s_len):
    i = pl.program_id(1)
    nkb = i + 1
    kofs = lax.broadcasted_iota(I32, (tq, tq), 0)
    qidx = i * tq + lax.broadcasted_iota(I32, (tq, tq), 1)
    iw = iw_ref[...]
    idx_scale = (IDX_HEADS ** -0.5) * (IDX_DIM ** -0.5)

    def over_tiles(step, init, n=nkb):
        carry = lax.fori_loop(0, n // 2, lambda j, c: step(2 * j, 2, c), init)
        return lax.cond(n % 2 == 1, lambda c: step(n - 1, 1, c), lambda c: c, carry)

    def key_index(kb, rows):
        return kb * tq + lax.broadcasted_iota(I32, (rows, tq), 0)

    def as_float(key):
        key = jnp.maximum(key, KEY_LOWEST)
        key = jnp.where((key > 0) & (key < KEY_TINY), KEY_TINY, key)
        return pltpu.bitcast(key ^ ((key >> 31) & 0x7FFFFFFF), F32)

    def scores(kb, width, diagonal):
        k0 = pl.multiple_of(kb * tq, tq)
        rows = width * tq
        ikb = ik_ref[pl.ds(k0, rows), :]
        sc = jnp.zeros((rows, tq), F32)
        for h in range(IDX_HEADS):
            sc = sc + jnp.maximum(_dot_nt(ikb, iq_ref[h]), 0.0) * iw[h:h + 1, :]
        sc = sc * idx_scale
        sc = jnp.where(jnp.abs(sc) < FLT_TINY, 0.0, sc)
        if diagonal:
            sc = jnp.where(kofs <= qidx - i * tq, sc, -jnp.inf)
        sc_ref[pl.ds(kb, width)] = sc.reshape(width, tq, tq)
        top_ref[pl.ds(kb, width)] = sc.astype(BF16).reshape(width, tq, tq)

    def below_diagonal(kb, width, carry):
        scores(kb, width, False)
        return carry

    over_tiles(below_diagonal, 0, n=i)
    scores(i, 1, True)

    def count(pred):
        def body(kb, width, acc):
            rows = width * tq
            m = jnp.where(pred(sc_ref[pl.ds(kb, width)].reshape(rows, tq), lambda: key_index(kb, rows)), 1, 0)
            return acc + jnp.sum(m.reshape(rows // 8, 8, tq), axis=0)
        return jnp.sum(over_tiles(body, jnp.zeros((8, tq), I32)), axis=0, keepdims=True)

    n_valid = qidx[0:1, :] + 1

    def count_top(cand):
        k16 = jnp.maximum(cand, KEY_LOWEST_BF16) >> 16
        pattern = k16 ^ ((k16 >> 15) & 0x7FFF)
        pattern = jnp.where((pattern > 0) & (pattern < 0x80), 0x80, pattern)
        cand_bf = pltpu.bitcast(pattern << 16, F32).astype(BF16)

        def body(kb, width, acc):
            rows = width * tq
            top = top_ref[pl.ds(kb, width)].reshape(rows, tq)
            m = jnp.where(top >= cand_bf, jnp.ones((), BF16), jnp.zeros((), BF16))
            part = m[0:16]
            for j in range(1, rows // 16):
                part = part + m[16 * j:16 * (j + 1)]
            return acc + part.astype(F32)
        acc = over_tiles(body, jnp.zeros((16, tq), F32))
        return jnp.sum(acc, axis=0, keepdims=True).astype(I32)

    def top_step(it, st):
        u, cnt_u = st
        cand_u = u | lax.shift_left(jnp.int32(1), jnp.int32(31) - it)
        cnt = count_top(cand_u ^ INT_MIN)
        take = cnt >= n_sel
        return jnp.where(take, cand_u, u), jnp.where(take, cnt, cnt_u)

    u, _ = lax.fori_loop(0, 16, top_step, (jnp.zeros((1, tq), I32), n_valid))

    def any_lane(flag):
        return jnp.max(jnp.where(flag, 1.0, 0.0)) > 0.0

    base = u ^ INT_MIN
    base = jnp.where(base < 0, base | 0xFFFF, base)
    base = jnp.clip(base, KEY_LOWEST + BF16_HALF_STEP + 1, INT_MAX - 2 * BF16_HALF_STEP)
    many = n_valid > n_sel
    lo = jnp.where(many, base - BF16_HALF_STEP - 1, KEY_LOWEST)
    hi = jnp.where(many, base + 2 * BF16_HALF_STEP, INT_MAX)
    def count_two(pred_a, pred_b):
        def body(kb, width, acc):
            x = sc_ref[pl.ds(kb, width)].reshape(width * tq // 8, 8, tq)
            return (acc[0] + jnp.sum(jnp.where(pred_a(x), 1, 0), axis=0),
                    acc[1] + jnp.sum(jnp.where(pred_b(x), 1, 0), axis=0))
        a, b = over_tiles(body, (jnp.zeros((8, tq), I32), jnp.zeros((8, tq), I32)))
        return jnp.sum(a, axis=0, keepdims=True), jnp.sum(b, axis=0, keepdims=True)

    lo_f, hi_f = as_float(lo), as_float(hi)
    cnt_lo, cnt_hi = count_two(lambda x: x >= lo_f, lambda x: x >= hi_f)

    def open_queries(lo, cnt_lo, hi, done):
        return (cnt_lo > n_sel) & (hi - lo > 1) & (done == 0)

    def bisect(last, st):
        def step(s):
            it, lo, cnt_lo, hi, cnt_hi, done = s
            mid = lo + ((hi - lo) >> 1)
            mid_f = as_float(mid)
            cnt = count(lambda x, kidx: x >= mid_f)
            live = open_queries(lo, cnt_lo, hi, done)
            up = live & (cnt >= n_sel)
            down = live & (cnt < n_sel)
            return (it + 1, jnp.where(up, mid, lo), jnp.where(up, cnt, cnt_lo),
                    jnp.where(down, mid, hi), jnp.where(down, cnt, cnt_hi), done)
        cond = lambda s: (s[0] < last) & any_lane(open_queries(s[1], s[2], s[3], s[5]))
        return lax.while_loop(cond, step, st)

    st = bisect(PEEL_AFTER_STEPS, (jnp.int32(0), lo, cnt_lo, hi, cnt_hi, jnp.zeros((1, tq), I32)))

    def peel(st):
        it, lo, cnt_lo, hi, cnt_hi, done = st
        lo_f, hi_f = as_float(lo), as_float(hi)

        def extremes(kb, width, carry):
            top, bot = carry
            x = sc_ref[pl.ds(kb, width)].reshape(width * tq // 8, 8, tq)
            top = jnp.maximum(top, jnp.max(jnp.where(x < hi_f, x, -jnp.inf), axis=0))
            bot = jnp.minimum(bot, jnp.min(jnp.where(x >= lo_f, x, jnp.inf), axis=0))
            return top, bot
        top, bot = over_tiles(extremes, (jnp.full((8, tq), -jnp.inf, F32), jnp.full((8, tq), jnp.inf, F32)))
        top = jnp.max(top, axis=0, keepdims=True)
        bot = jnp.min(bot, axis=0, keepdims=True)
        from_top = (cnt_hi == n_sel - 1) | (top == bot)
        from_bot = (cnt_lo == n_sel + 1) & jnp.logical_not(from_top)
        cnt_top, cnt_bot = count_two(lambda x: x >= top, lambda x: x > bot)
        cnt = jnp.where(from_top, cnt_top, cnt_bot)
        short = from_bot & (cnt < n_sel)
        hit = open_queries(lo, cnt_lo, hi, done) & (from_top | from_bot)
        thr = jnp.where(from_top, top, bot)
        incl = jnp.where(from_top | short, 1, 0)
        return hit, thr, incl, jnp.where(short, cnt_lo, cnt)

    def with_peel(st):
        hit, thr, incl, cnt = peel(st)
        st = bisect(64, st[:5] + (jnp.where(hit, 1, 0),))
        _, lo, cnt_lo, _, _, _ = st
        return jnp.where(hit, thr, as_float(lo)), jnp.where(hit, incl, 1), jnp.where(hit, cnt, cnt_lo)

    def without_peel(st):
        _, lo, cnt_lo, _, _, _ = st
        return as_float(lo), jnp.ones((1, tq), I32), cnt_lo

    thr, incl, cnt = lax.cond(any_lane(open_queries(*st[1:4], st[5])), with_peel, without_peel, st)
    admits_equal = incl != 0
    excess = cnt > n_sel

    def tie_limit():
        need = n_sel - count(lambda x, kidx: x > thr)
        nbits = s_len.bit_length()

        def step(it, jv):
            cand = jv | lax.shift_left(jnp.int32(1), jnp.int32(nbits - 1) - it)
            cnt = count(lambda x, kidx: (x == thr) & (kidx() < cand))
            return jnp.where(cnt < need, cand, jv)
        jv = lax.fori_loop(0, nbits, step, jnp.zeros((1, tq), I32))
        return jnp.where(excess, jv, jnp.where(admits_equal, s_len, -1))

    tied = jnp.max(jnp.where(excess, 1.0, 0.0)) > 0.0

    acc_ref[...] = jnp.zeros_like(acc_ref)

    def attend(select):
        def block(kb, width, carry):
            ms, ls = carry
            k0 = pl.multiple_of(kb * tq, tq)
            rows = width * tq
            sel = select(sc_ref[pl.ds(kb, width)].reshape(rows, tq), kb, rows)
            heads = [slice(h * DSA_HEAD_DIM, (h + 1) * DSA_HEAD_DIM) for h in range(DSA_HEADS)]
            st = [_dot_nt(k_ref[pl.ds(k0, rows), hs], q_ref[:, hs]) for hs in heads]
            st = [jnp.where(sel, x, MASK_BIAS) for x in st]
            new_m = [jnp.maximum(m, jnp.max(x, axis=0, keepdims=True)) for m, x in zip(ms, st)]
            alpha = [jnp.exp2(m - mn) for m, mn in zip(ms, new_m)]
            p = [jnp.exp2(x - mn) for x, mn in zip(st, new_m)]
            new_l = [a * l + jnp.sum(x, axis=0, keepdims=True) for a, l, x in zip(alpha, ls, p)]
            p = [x.astype(BF16) for x in p]
            for h, hs in enumerate(heads):
                pv = _dot(vt_ref[kb, hs, :], p[h][0:tq])
                for j in range(1, width):
                    pv = pv + _dot(vt_ref[kb + j, hs, :], p[h][j * tq:(j + 1) * tq])
                acc_ref[h] = alpha[h] * acc_ref[h] + pv
            return tuple(new_m), tuple(new_l)

        carry = (tuple(jnp.full((1, tq), -jnp.inf, F32) for _ in range(DSA_HEADS)),
                 tuple(jnp.zeros((1, tq), F32) for _ in range(DSA_HEADS)))
        return over_tiles(block, carry)[1]

    def attend_tied():
        jstar = tie_limit()
        return attend(lambda x, kb, rows: (x > thr) | ((x == thr) & (key_index(kb, rows) <= jstar)))

    at_least = jnp.where(admits_equal, thr, jnp.inf)
    above = jnp.where(admits_equal, jnp.inf, thr)
    ls = lax.cond(tied, attend_tied, lambda: attend(lambda x, kb, rows: (x >= at_least) | (x > above)))
    for h in range(DSA_HEADS):
        hs = slice(h * DSA_HEAD_DIM, (h + 1) * DSA_HEAD_DIM)
        o_ref[:, hs] = (acc_ref[h] / ls[h]).T.astype(o_ref.dtype)


def _dsa(q, k, vt, iq, ik, iw, *, tq):
    bsz, s, hd = q.shape
    n_sel = min(DSA_TOPK, s // 4)
    resident = lambda n: _resident((None, s, n), lambda b, i: (b, 0, 0))
    return pl.pallas_call(
        functools.partial(_dsa_kernel, tq=tq, n_sel=n_sel, s_len=s),
        grid=(bsz, s // tq),
        in_specs=[
            pl.BlockSpec((None, tq, hd), lambda b, i: (b, i, 0)),
            pl.BlockSpec((None, IDX_HEADS, tq, IDX_DIM), lambda b, i: (b, 0, i, 0)),
            pl.BlockSpec((None, IDX_HEADS, tq), lambda b, i: (b, 0, i)),
            resident(hd),
            _resident((None, s // tq, hd, tq), lambda b, i: (b, 0, 0, 0)),
            resident(IDX_DIM),
        ],
        out_specs=pl.BlockSpec((None, tq, hd), lambda b, i: (b, i, 0)),
        out_shape=jax.ShapeDtypeStruct((bsz, s, hd), BF16),
        scratch_shapes=[
            pltpu.VMEM((s // tq, tq, tq), F32),
            pltpu.VMEM((s // tq, tq, tq), BF16),
            pltpu.VMEM((DSA_HEADS, DSA_HEAD_DIM, tq), F32),
        ],
        compiler_params=_cparams(("arbitrary", "arbitrary")),
        name="dsa",
    )(q, iq, iw, k, vt, ik)


def _mix_ffn_kernel(x_ref, a_ref, b_ref, wa_ref, wb_ref, g1_ref, nw_ref, sh_ref, sc_ref, g2_ref,
                    w1_ref, w2_ref, o_ref, *, tf):
    x = x_ref[...] + g1_ref[...] * (_dot(a_ref[...], wa_ref[...]) + _dot(b_ref[...], wb_ref[...]))
    hb = (_rms(x, nw_ref[...]) * (1.0 + sc_ref[...]) + sh_ref[...]).astype(BF16)
    f = w2_ref.shape[0]
    acc = jnp.zeros(x.shape, F32)
    for j in range(f // tf):
        gate = _dot(hb, w1_ref[:, j * tf:(j + 1) * tf])
        up = _dot(hb, w1_ref[:, f + j * tf:f + (j + 1) * tf])
        acc = acc + _dot((_silu(gate) * up).astype(BF16), w2_ref[j * tf:(j + 1) * tf, :])
    o_ref[...] = x + g2_ref[...] * acc


def _mix_ffn(x, mix_a, mix_b, w_out_bf16, gate1, nw, shift, scale, gate2, w1_bf16, w2_bf16, *, tm, tf):
    bsz, s, d = x.shape
    f = w2_bf16.shape[0]
    na = mix_a.shape[-1]
    assert mix_b.shape[-1] == na and w_out_bf16.shape[0] == 2 * na
    rows = lambda n: pl.BlockSpec((None, tm, n), lambda b, i: (b, i, 0))
    mod = pl.BlockSpec((None, 1, d), lambda b, i: (b, 0, 0))
    return pl.pallas_call(
        functools.partial(_mix_ffn_kernel, tf=tf),
        grid=(bsz, s // tm),
        in_specs=[
            rows(d), rows(na), rows(na),
            _resident((na, d), lambda b, i: (0, 0)),
            _resident((na, d), lambda b, i: (1, 0)),
            mod,
            pl.BlockSpec((1, d), lambda b, i: (0, 0)),
            mod, mod, mod,
            _resident((d, 2 * f), lambda b, i: (0, 0)),
            _resident((f, d), lambda b, i: (0, 0)),
        ],
        out_specs=rows(d),
        out_shape=jax.ShapeDtypeStruct((bsz, s, d), F32),
        compiler_params=_cparams(("arbitrary", "arbitrary")),
        name="mix_ffn",
    )(x, mix_a, mix_b, w_out_bf16, w_out_bf16, gate1, nw.reshape(1, d), shift, scale, gate2,
      w1_bf16, w2_bf16)


HALO = max(POOL_WINDOWS)


def _proj_cd_kernel(x_ref, nw_ref, sh_ref, sc_ref, w_ref, pw_ref, ps_ref, qn_ref, kn_ref,
                    pool_o, q_o, k_o, v_o, ext_ref, sb_ref):
    i = pl.program_id(1)
    tm = x_ref.shape[0]
    n = POOL_GROUPS * POOL_CH
    hb = (_rms(x_ref[...], nw_ref[...]) * (1.0 + sc_ref[...]) + sh_ref[...]).astype(BF16)

    @pl.when(i == 0)
    def _():
        ext_ref[0:HALO, :] = jnp.zeros((HALO, n), F32)

    ext_ref[HALO:HALO + tm, :] = _dot(hb, w_ref[:, :n])
    for j in range(3):
        sb_ref[:, j * n:(j + 1) * n] = _dot(hb, w_ref[:, (j + 1) * n:(j + 2) * n])

    t = i * tm + lax.broadcasted_iota(I32, (tm, 1), 0)
    for g, w in enumerate(POOL_WINDOWS):
        gs = slice(g * POOL_CH, (g + 1) * POOL_CH)
        u = ext_ref[HALO:HALO + tm, gs]
        tot = u
        for j in range(1, w):
            tot = tot + ext_ref[HALO - j:HALO - j + tm, gs]
        cnt = jnp.minimum(t + 1, w).astype(F32)
        pooled = tot / cnt - u
        y = _dot(pooled.astype(BF16), pw_ref[g].astype(BF16)) * ps_ref[:, gs]
        pool_o[:, gs] = y.astype(pool_o.dtype)
    ext_ref[0:HALO, :] = ext_ref[tm:tm + HALO, :]

    qn, kn = qn_ref[...], kn_ref[...]
    for h in range(SB_HEADS):
        hs = slice(h * SB_HEAD_DIM, (h + 1) * SB_HEAD_DIM)
        q_o[:, hs] = (_rms(sb_ref[:, hs], qn) * SB_HEAD_DIM ** -0.5).astype(q_o.dtype)
        k_o[:, hs] = _rms(sb_ref[:, n + h * SB_HEAD_DIM:n + (h + 1) * SB_HEAD_DIM], kn).astype(k_o.dtype)
    tk = v_o.shape[-1]
    for j in range(v_o.shape[0]):
        v_o[j] = sb_ref[j * tk:(j + 1) * tk, 2 * n:3 * n].T.astype(v_o.dtype)


def _proj_cd(x, nw, shift, scale, w_bf16, pool_w, pool_scale, q_norm, k_norm, *, tm, tk):
    bsz, s, d = x.shape
    n = POOL_GROUPS * POOL_CH
    assert w_bf16.shape[1] == 4 * n
    rows = lambda m: pl.BlockSpec((None, tm, m), lambda b, i: (b, i, 0))
    vec = lambda m: pl.BlockSpec((1, m), lambda b, i: (0, 0))
    mod = pl.BlockSpec((None, 1, d), lambda b, i: (b, 0, 0))
    return pl.pallas_call(
        _proj_cd_kernel,
        grid=(bsz, s // tm),
        in_specs=[
            rows(d), vec(d), mod, mod,
            _resident((d, 4 * n), lambda b, i: (0, 0)),
            pl.BlockSpec((POOL_GROUPS, POOL_CH, POOL_CH), lambda b, i: (0, 0, 0)),
            vec(n), vec(SB_HEAD_DIM), vec(SB_HEAD_DIM),
        ],
        out_specs=[rows(n)] * 3 + [pl.BlockSpec((None, tm // tk, n, tk), lambda b, i: (b, i, 0, 0))],
        out_shape=[jax.ShapeDtypeStruct((bsz, s, n), BF16)] * 3
        + [jax.ShapeDtypeStruct((bsz, s // tk, n, tk), BF16)],
        scratch_shapes=[pltpu.VMEM((HALO + tm, n), F32), pltpu.VMEM((tm, 3 * n), F32)],
        compiler_params=_cparams(("arbitrary", "arbitrary")),
        name="proj_cd",
    )(x, nw.reshape(1, d), shift, scale, w_bf16, pool_w, pool_scale.reshape(1, n),
      q_norm.reshape(1, -1), k_norm.reshape(1, -1))


def _sb_kernel(q_ref, k_ref, vt_ref, o_ref, acc_ref, *, tq):
    i = pl.program_id(1)
    kofs = lax.broadcasted_iota(I32, (tq, tq), 0)
    qofs = lax.broadcasted_iota(I32, (tq, tq), 1)
    later = jnp.where(qofs > kofs, 1.0, 0.0).astype(BF16)
    heads = [slice(h * SB_HEAD_DIM, (h + 1) * SB_HEAD_DIM) for h in range(SB_HEADS)]
    acc_ref[...] = jnp.zeros_like(acc_ref)

    def block(kb, runs, diagonal):
        k0 = pl.multiple_of(kb * tq, tq)
        z = [_dot_nt(k_ref[pl.ds(k0, tq), hs], q_ref[:, hs]) for hs in heads]
        sp = [jnp.log(1.0 + jnp.exp(-jnp.abs(x))) for x in z]
        log_beta = [jnp.minimum(x, 0.0) - s for x, s in zip(z, sp)]
        log_1m = [-jnp.maximum(x, 0.0) - s for x, s in zip(z, sp)]
        if diagonal:
            strict = kofs < qofs
            log_1m = [jnp.where(strict, x, 0.0) for x in log_1m]
        hi = [x.astype(BF16) for x in log_1m]
        lo = [(x - h.astype(F32)).astype(BF16) for x, h in zip(log_1m, hi)]
        after = [_dot(later, h) + _dot(later, l) + r for h, l, r in zip(hi, lo, runs)]
        w = [jnp.exp(lb + a) for lb, a in zip(log_beta, after)]
        if diagonal:
            w = [jnp.where(strict, x, 0.0) for x in w]
        for h, hs in enumerate(heads):
            acc_ref[h] = acc_ref[h] + _dot(vt_ref[kb, hs, :], w[h].astype(BF16))
        return tuple(r + jnp.sum(x, axis=0, keepdims=True) for r, x in zip(runs, log_1m))

    def alive(runs):
        worst = functools.reduce(jnp.maximum, runs)
        return jnp.max(worst) > SB_DEAD_LOG

    runs = block(i, tuple(jnp.zeros((1, tq), F32) for _ in heads), True)

    def earlier(state):
        kb, runs, _ = state
        runs = block(kb, runs, False)
        return kb - 1, runs, alive(runs)

    lax.while_loop(lambda st: (st[0] >= 0) & st[2], earlier, (i - 1, runs, alive(runs)))
    for h, hs in enumerate(heads):
        o_ref[:, hs] = acc_ref[h].T.astype(o_ref.dtype)


def _sb(q, k, vt, *, tq):
    bsz, s, hd = q.shape
    return pl.pallas_call(
        functools.partial(_sb_kernel, tq=tq),
        grid=(bsz, s // tq),
        in_specs=[
            pl.BlockSpec((None, tq, hd), lambda b, i: (b, i, 0)),
            _resident((None, s, hd), lambda b, i: (b, 0, 0)),
            _resident((None, s // tq, hd, tq), lambda b, i: (b, 0, 0, 0)),
        ],
        out_specs=pl.BlockSpec((None, tq, hd), lambda b, i: (b, i, 0)),
        out_shape=jax.ShapeDtypeStruct((bsz, s, hd), BF16),
        scratch_shapes=[pltpu.VMEM((SB_HEADS, SB_HEAD_DIM, tq), F32)],
        compiler_params=_cparams(("arbitrary", "arbitrary")),
        name="stick_breaking",
    )(q, k, vt)


def _pack_ab_weight(w):
    d = w.shape[0]
    gq, gk, gv, glow, gr, dq, dk, dv, iq, ik, iw = jnp.split(
        w, [256, 512, 1024, 1040, 1552, 2064, 2576, 3088, 3600, 3664], axis=1)
    pad = jnp.zeros((d, LANES - (IDX_DIM + GLA_GATE_RANK + IDX_HEADS)), w.dtype)
    return jnp.concatenate([gq, gk, gv, gr, ik, glow, iw, pad, dq, dk, dv, iq], axis=1)


def kernel(x, c, positions, ada_w, ada_b, mix_norm, ffn_norm, ffn_w1, ffn_w2, ab_w_in, gla_gate_up,
           gla_gate_b, gla_out_norm, dsa_q_norm, dsa_k_norm, ab_w_out, cd_w_in, pool_w, pool_scale,
           sb_q_norm, sb_k_norm, cd_w_out):
    bsz, s, d = x.shape
    depth = ada_w.shape[0]
    mod = _ada_mod(c, ada_w, ada_b).reshape(depth, bsz, 6, 1, d)
    tm = min(512, s)
    tq = min(256, s)
    for layer in range(depth):
        sh1, sc1, g1, sh2, sc2, g2 = (mod[layer, :, j] for j in range(6))
        i = layer // 2
        if layer % 2 == 0:
            proj, q, k, vt, iq, ik, iw = _proj_ab(
                x, mix_norm[layer], sh1, sc1, _pack_ab_weight(ab_w_in[i]).astype(BF16), positions,
                dsa_q_norm[i], dsa_k_norm[i], tm=tq, tk=tq)
            mix_a = _gla(proj, gla_gate_up[i], gla_gate_b[i], gla_out_norm[i], ts=tm)
            mix_b = _dsa(q, k, vt, iq, ik, iw, tq=tq)
            w_out = ab_w_out[i]
        else:
            mix_a, q, k, vt = _proj_cd(x, mix_norm[layer], sh1, sc1, cd_w_in[i].astype(BF16), pool_w[i],
                                       pool_scale[i], sb_q_norm[i], sb_k_norm[i], tm=tm, tk=tq)
            mix_b = _sb(q, k, vt, tq=tq)
            w_out = cd_w_out[i]
        x = _mix_ffn(x, mix_a, mix_b, w_out.astype(BF16), g1, ffn_norm[layer], sh2, sc2, g2,
                     ffn_w1[layer].astype(BF16), ffn_w2[layer].astype(BF16), tm=tm, tf=256)
    return x
```

```python
import functools

import jax
import jax.numpy as jnp
from jax import lax
from jax.experimental import pallas as pl
from jax.experimental.pallas import tpu as pltpu

F32 = jnp.float32
BF16 = jnp.bfloat16
I32 = jnp.int32

D_MODEL = 1024
GLA_HEADS, GLA_DK, GLA_DV = 4, 64, 128
GLA_GATE_RANK = 16
GLA_GATE_TAU = 16.0
GLA_CHUNK = 64
DSA_HEADS, DSA_HEAD_DIM = 4, 128
IDX_HEADS, IDX_DIM = 8, 64
DSA_TOPK = 256
POOL_WINDOWS = (2, 4, 8, 16)
POOL_GROUPS, POOL_CH = 4, 128
SB_HEADS, SB_HEAD_DIM = 4, 128
ROPE_THETA = 10000.0
NORM_EPS = 1e-6
D_FF = 2816

LANES = 128
VMEM_LIMIT = 56 * 1024 * 1024

AB_GQ, AB_GK, AB_GV, AB_GR = 0, 256, 512, 1024
AB_SMALL = 1536
SM_IK, SM_GLOW, SM_IW = 0, 64, 80
AB_GLA_N = AB_SMALL + LANES
AB_DSA_N = 4 * 512

DSA_Q_SCALE = 1.4426950408889634 * DSA_HEAD_DIM ** -0.5
INT_MIN = -2 ** 31
INT_MAX = 2 ** 31 - 1
FLT_TINY = 2.0 ** -126
KEY_TINY = 0x00800000
KEY_LOWEST = INT_MIN + 0x00800000
KEY_LOWEST_BF16 = INT_MIN + 0x00810000
BF16_HALF_STEP = 0x8000
PEEL_AFTER_STEPS = 5
MASK_BIAS = -1e30
SB_DEAD_LOG = -110.0


def _dot(a, b):
    return jnp.dot(a, b, preferred_element_type=F32)


def _dot_nt(a, b):
    return lax.dot_general(a, b, (((1,), (1,)), ((), ())), preferred_element_type=F32)


def _dot_tn(a, b):
    return lax.dot_general(a, b, (((0,), (0,)), ((), ())), preferred_element_type=F32)


def _split3(a):
    hi = a.astype(BF16)
    r1 = a - hi.astype(F32)
    mid = r1.astype(BF16)
    lo = (r1 - mid.astype(F32)).astype(BF16)
    return hi, mid, lo


def _silu(x):
    return x * jax.nn.sigmoid(x)


def _rms(x, w):
    var = jnp.mean(x * x, axis=-1, keepdims=True)
    return x * lax.rsqrt(var + NORM_EPS) * w


def _cparams(sem):
    return pltpu.CompilerParams(dimension_semantics=sem, vmem_limit_bytes=VMEM_LIMIT)


def _resident(block_shape, index_map):
    return pl.BlockSpec(block_shape, index_map, pipeline_mode=pl.Buffered(1))


def _ada_kernel(c_ref, w_ref, b_ref, o_ref):
    cond = _silu(c_ref[...])
    c_hi, c_mid, _ = _split3(cond)
    w_hi, w_mid, _ = _split3(w_ref[...])
    acc = _dot(c_hi, w_hi) + _dot(c_hi, w_mid) + _dot(c_mid, w_hi)
    o_ref[...] = acc + b_ref[...]


def _ada_mod(c, ada_w, ada_b):
    depth, d, n = ada_w.shape
    bsz = c.shape[0]
    rows = 16
    cp =jnp.zeros((rows, d), F32).at[:bsz].set(c)
    tn = 1536
    out = pl.pallas_call(
        _ada_kernel,
        grid=(depth, n // tn),
        in_specs=[
            pl.BlockSpec((rows, d), lambda l, j: (0, 0)),
            pl.BlockSpec((None, d, tn), lambda l, j: (l, 0, j)),
            pl.BlockSpec((None, 1, tn), lambda l, j: (l, 0, j)),
        ],
        out_specs=pl.BlockSpec((None, rows, tn), lambda l, j: (l, 0, j)),
        out_shape=jax.ShapeDtypeStruct((depth, rows, n), F32),
        compiler_params=_cparams(("arbitrary", "arbitrary")),
        name="ada_mod",
    )(cp, ada_w, ada_b.reshape(depth, 1, n))
    return out[:, :bsz]


def _gla_kernel(q_ref, k_ref, v_ref, gr_ref, sm_ref, gup_ref, gb_ref, onw_ref, o_ref, st_ref, *, ts):
    c = GLA_CHUNK

    @pl.when(pl.program_id(1) == 0)
    def _():
        st_ref[...] = jnp.zeros_like(st_ref)

    nc = ts // c
    row = lax.broadcasted_iota(I32, (ts, ts), 0)
    col = lax.broadcasted_iota(I32, (ts, ts), 1)
    tri = jnp.where((row // c == col // c) & (col <= row), 1.0, 0.0).astype(BF16)
    causal = lax.broadcasted_iota(I32, (c, c), 1) <= lax.broadcasted_iota(I32, (c, c), 0)
    onw = onw_ref[...]

    glow = sm_ref[:, SM_GLOW:SM_GLOW + GLA_GATE_RANK]
    a = _dot(glow.astype(BF16), gup_ref[...].astype(BF16)) + gb_ref[...]
    g = (jnp.minimum(a, 0.0) - jnp.log1p(jnp.exp(-jnp.abs(a)))) / GLA_GATE_TAU
    g_hi, g_mid, g_lo = _split3(g)
    b = _dot(tri, g_hi) + _dot(tri, g_mid) + _dot(tri, g_lo)
    qs = q_ref[...] * (GLA_DK ** -0.5)
    k = k_ref[...]

    qe, ke, kd, qd, dec = [], [], [], [], []
    for ci in range(nc):
        r = slice(ci * c, (ci + 1) * c)
        bc = b[r]
        b_mid = bc[c // 2 - 1:c // 2, :]
        b_last = bc[c - 1:c, :]
        qe.append((qs[r] * jnp.exp(bc - b_mid)).astype(BF16))
        ke.append((k[r] * jnp.exp(b_mid - bc)).astype(BF16))
        kd.append((k[r] * jnp.exp(b_last - bc)).astype(BF16))
        qd.append((qs[r] * jnp.exp(bc)).astype(BF16))
        dec.append(jnp.exp(b_last))

    pairs = [(ci, h) for ci in range(nc) for h in range(GLA_HEADS)]
    ksl = lambda h: slice(h * GLA_DK, (h + 1) * GLA_DK)
    vsl = lambda h: slice(h * GLA_DV, (h + 1) * GLA_DV)
    vh = {(ci, h): v_ref[ci * c:(ci + 1) * c, vsl(h)].astype(BF16) for ci, h in pairs}
    att = {p: _dot_nt(qe[p[0]][:, ksl(p[1])], ke[p[0]][:, ksl(p[1])]) for p in pairs}
    att = {p: jnp.where(causal, att[p], 0.0).astype(BF16) for p in pairs}
    o = {p: _dot(att[p], vh[p]) for p in pairs}
    kvt = {p: _dot_tn(vh[p], kd[p[0]][:, ksl(p[1])]) for p in pairs}
    st_in = {}
    for h in range(GLA_HEADS):
        st = st_ref[h]
        for ci in range(nc):
            st_in[(ci, h)] = st.astype(BF16)
            st = st * dec[ci][:, ksl(h)] + kvt[(ci, h)]
        st_ref[h] = st
    for ci, h in pairs:
        out = o[(ci, h)] + _dot_nt(qd[ci][:, ksl(h)], st_in[(ci, h)])
        grh = gr_ref[ci * c:(ci + 1) * c, vsl(h)]
        o_ref[ci * c:(ci + 1) * c, vsl(h)] = (_rms(out, onw) * _silu(grh)).astype(o_ref.dtype)


def _gla(proj, gate_up, gate_b, out_norm, *, ts):
    bsz, s, _ = proj.shape
    hk, hv = GLA_HEADS * GLA_DK, GLA_HEADS * GLA_DV
    return pl.pallas_call(
        functools.partial(_gla_kernel, ts=ts),
        grid=(bsz, s // ts),
        in_specs=[
            pl.BlockSpec((None, ts, hk), lambda b, i: (b, i, AB_GQ // hk)),
            pl.BlockSpec((None, ts, hk), lambda b, i: (b, i, AB_GK // hk)),
            pl.BlockSpec((None, ts, hv), lambda b, i: (b, i, AB_GV // hv)),
            pl.BlockSpec((None, ts, hv), lambda b, i: (b, i, AB_GR // hv)),
            pl.BlockSpec((None, ts, LANES), lambda b, i: (b, i, AB_SMALL // LANES)),
            pl.BlockSpec((GLA_GATE_RANK, hk), lambda b, i: (0, 0)),
            pl.BlockSpec((1, hk), lambda b, i: (0, 0)),
            pl.BlockSpec((1, GLA_DV), lambda b, i: (0, 0)),
        ],
        out_specs=pl.BlockSpec((None, ts, hv), lambda b, i: (b, i, 0)),
        out_shape=jax.ShapeDtypeStruct((bsz, s, hv), BF16),
        scratch_shapes=[pltpu.VMEM((GLA_HEADS, GLA_DV, GLA_DK), F32)],
        compiler_params=_cparams(("arbitrary", "arbitrary")),
        name="gla",
    )(proj, proj, proj, proj, proj, gate_up, gate_b.reshape(1, hk), out_norm.reshape(1, GLA_DV))


def _dsa_prep_kernel(dq_ref, dk_ref, dv_ref, iq_ref, sm_ref, pos_ref, qn_ref, kn_ref, freq_ref,
                     q_o, k_o, v_o, iq_o, ik_o, iw_o):
    pos = pos_ref[...].astype(F32)
    lane = lax.broadcasted_iota(I32, (1, LANES), 1)
    half_a, half_i = DSA_HEAD_DIM // 2, IDX_DIM // 2

    ang = pos * freq_ref[...]
    cos_t, sin_t = jnp.cos(ang), jnp.sin(ang)

    def attn_table(t):
        return jnp.where(lane < half_a, t, pltpu.roll(t, half_a, 1))

    def idx_table(t):
        return jnp.where(lane < half_i, pltpu.roll(t, half_a, 1),
                         jnp.where(lane < 2 * half_i, pltpu.roll(t, half_a + half_i, 1), t))

    cos_a = attn_table(cos_t)
    sin_a = jnp.where(lane < half_a, -1.0, 1.0) * attn_table(sin_t)

    def rope_attn(t):
        return t * cos_a + pltpu.roll(t, half_a, 1) * sin_a

    first = (lane % IDX_DIM) < half_i
    cos_i = idx_table(cos_t)
    sin_i = jnp.where(first, -1.0, 1.0) * idx_table(sin_t)

    def rope_idx(t):
        rot = jnp.where(first, pltpu.roll(t, LANES - IDX_DIM // 2, 1), pltpu.roll(t, IDX_DIM // 2, 1))
        return t * cos_i + rot * sin_i

    qn, kn = qn_ref[...], kn_ref[...]
    for h in range(DSA_HEADS):
        hs = slice(h * DSA_HEAD_DIM, (h + 1) * DSA_HEAD_DIM)
        q_o[:, hs] = (rope_attn(_rms(dq_ref[:, hs], qn)) * DSA_Q_SCALE).astype(q_o.dtype)
        k_o[:, hs] = rope_attn(_rms(dk_ref[:, hs], kn)).astype(k_o.dtype)
    tk = v_o.shape[-1]
    for j in range(v_o.shape[0]):
        v_o[j] = dv_ref[j * tk:(j + 1) * tk, :].T.astype(v_o.dtype)
    for j in range(IDX_HEADS * IDX_DIM // LANES):
        r = rope_idx(iq_ref[:, j * LANES:(j + 1) * LANES])
        iq_o[2 * j] = r[:, :IDX_DIM].astype(iq_o.dtype)
        iq_o[2 * j + 1] = r[:, IDX_DIM:].astype(iq_o.dtype)
    sm = sm_ref[...]
    ik_o[...] = rope_idx(sm)[:, SM_IK:SM_IK + IDX_DIM].astype(ik_o.dtype)
    iw_o[...] = sm.T[SM_IW:SM_IW + IDX_HEADS, :]


def _proj_ab_kernel(x_ref, nw_ref, sh_ref, sc_ref, w_ref, pos_ref, qn_ref, kn_ref, freq_ref,
                    gla_o, q_o, k_o, v_o, iq_o, ik_o, iw_o, dsa_ref):
    hb = (_rms(x_ref[...], nw_ref[...]) * (1.0 + sc_ref[...]) + sh_ref[...]).astype(BF16)
    hd = DSA_HEADS * DSA_HEAD_DIM
    gla_o[:, AB_SMALL:] = _dot(hb, w_ref[:, AB_SMALL:AB_GLA_N])
    for j in range(AB_DSA_N // hd):
        dsa_ref[:, j * hd:(j + 1) * hd] = _dot(hb, w_ref[:, AB_GLA_N + j * hd:AB_GLA_N + (j + 1) * hd])
    part = lambda j: dsa_ref.at[:, j * hd:(j + 1) * hd]
    _dsa_prep_kernel(part(0), part(1), part(2), part(3), gla_o.at[:, AB_SMALL:AB_SMALL + LANES],
                     pos_ref, qn_ref, kn_ref, freq_ref, q_o, k_o, v_o, iq_o, ik_o, iw_o)
    gla_o[:, :AB_SMALL] = _dot(hb, w_ref[:, :AB_SMALL])


def _proj_ab(x, nw, shift, scale, w_bf16, positions, q_norm, k_norm, *, tm, tk):
    bsz, s, d = x.shape
    hd = DSA_HEADS * DSA_HEAD_DIM
    half_a, half_i = DSA_HEAD_DIM // 2, IDX_DIM // 2
    inv_a = ROPE_THETA ** (-jnp.arange(half_a, dtype=F32) / half_a)
    inv_i = ROPE_THETA ** (-jnp.arange(half_i, dtype=F32) / half_i)
    freq = jnp.concatenate([inv_a, inv_i, inv_i]).reshape(1, LANES)
    rows = lambda n: pl.BlockSpec((None, tm, n), lambda b, i: (b, i, 0))
    vec = lambda n: pl.BlockSpec((1, n), lambda b, i: (0, 0))
    mod = pl.BlockSpec((None, 1, d), lambda b, i: (b, 0, 0))
    return pl.pallas_call(
        _proj_ab_kernel,
        grid=(bsz, s // tm),
        in_specs=[
            rows(d), vec(d), mod, mod,
            _resident((d, AB_GLA_N + AB_DSA_N), lambda b, i: (0, 0)),
            rows(1), vec(DSA_HEAD_DIM), vec(DSA_HEAD_DIM), vec(LANES),
        ],
        out_specs=[
            rows(AB_GLA_N), rows(hd), rows(hd),
            pl.BlockSpec((None, tm // tk, hd, tk), lambda b, i: (b, i, 0, 0)),
            pl.BlockSpec((None, IDX_HEADS, tm, IDX_DIM), lambda b, i: (b, 0, i, 0)),
            rows(IDX_DIM),
            pl.BlockSpec((None, IDX_HEADS, tm), lambda b, i: (b, 0, i)),
        ],
        out_shape=[
            jax.ShapeDtypeStruct((bsz, s, AB_GLA_N), F32),
            jax.ShapeDtypeStruct((bsz, s, hd), BF16),
            jax.ShapeDtypeStruct((bsz, s, hd), BF16),
            jax.ShapeDtypeStruct((bsz, s // tk, hd, tk), BF16),
            jax.ShapeDtypeStruct((bsz, IDX_HEADS, s, IDX_DIM), BF16),
            jax.ShapeDtypeStruct((bsz, s, IDX_DIM), BF16),
            jax.ShapeDtypeStruct((bsz, IDX_HEADS, s), F32),
        ],
        scratch_shapes=[pltpu.VMEM((tm, AB_DSA_N), F32)],
        compiler_params=_cparams(("arbitrary", "arbitrary")),
        name="proj_ab",
    )(x, nw.reshape(1, d), shift, scale, w_bf16, positions.reshape(bsz, s, 1),
      q_norm.reshape(1, -1), k_norm.reshape(1, -1), freq)


def _dsa_kernel(q_ref, iq_ref, iw_ref, k_ref, vt_ref, ik_ref, o_ref, sc_ref, top_ref, acc_ref,
                *, tq, n_sel, s_len):
    i = pl.program_id(1)
    nkb = i + 1
    kofs = lax.broadcasted_iota(I32, (tq, tq), 0)
    qidx = i * tq + lax.broadcasted_iota(I32, (tq, tq), 1)
    iw = iw_ref[...]
    idx_scale = (IDX_HEADS ** -0.5) * (IDX_DIM ** -0.5)

    def over_tiles(step, init, n=nkb, widest=4):
        carry = lax.fori_loop(0, n // widest, lambda j, c: step(widest * j, widest, c), init)
        rest = n % widest
        start = n - rest
        width = widest // 2
        while width >= 1:
            carry = lax.cond((rest & width) != 0, functools.partial(step, start, width), lambda c: c, carry)
            start = start + (rest & width)
            width //= 2
        return carry

    def key_index(kb, rows):
        return kb * tq + lax.broadcasted_iota(I32, (rows, tq), 0)

    def as_float(key):
        key = jnp.maximum(key, KEY_LOWEST)
        key = jnp.where((key > 0) & (key < KEY_TINY), KEY_TINY, key)
        return pltpu.bitcast(key ^ ((key >> 31) & 0x7FFFFFFF), F32)

    def scores(kb, width, diagonal):
        k0 = pl.multiple_of(kb * tq, tq)
        rows = width * tq
        ikb = ik_ref[pl.ds(k0, rows), :]
        sc = jnp.zeros((rows, tq), F32)
        for h in range(IDX_HEADS):
            sc = sc + jnp.maximum(_dot_nt(ikb, iq_ref[h]), 0.0) * iw[h:h + 1, :]
        sc = sc * idx_scale
        sc = jnp.where(jnp.abs(sc) < FLT_TINY, 0.0, sc)
        if diagonal:
            sc = jnp.where(kofs <= qidx - i * tq, sc, -jnp.inf)
        sc_ref[pl.ds(kb, width)] = sc.reshape(width, tq, tq)
        top_ref[pl.ds(kb, width)] = sc.astype(BF16).reshape(width, tq, tq)

    def below_diagonal(kb, width, carry):
        scores(kb, width, False)
        return carry

    over_tiles(below_diagonal, 0, n=i)
    scores(i, 1, True)

    def count(pred):
        def body(kb, width, acc):
            rows = width * tq
            m = jnp.where(pred(sc_ref[pl.ds(kb, width)].reshape(rows, tq), lambda: key_index(kb, rows)), 1, 0)
            return acc + jnp.sum(m.reshape(rows // 8, 8, tq), axis=0)
        return jnp.sum(over_tiles(body, jnp.zeros((8, tq), I32)), axis=0, keepdims=True)

    n_valid = qidx[0:1, :] + 1

    def count_top(cand):
        k16 = jnp.maximum(cand, KEY_LOWEST_BF16) >> 16
        pattern = k16 ^ ((k16 >> 15) & 0x7FFF)
        pattern = jnp.where((pattern > 0) & (pattern < 0x80), 0x80, pattern)
        cand_bf = pltpu.bitcast(pattern << 16, F32).astype(BF16)

        def body(kb, width, acc):
            rows = width * tq
            top = top_ref[pl.ds(kb, width)].reshape(rows, tq)
            m = jnp.where(top >= cand_bf, jnp.ones((), BF16), jnp.zeros((), BF16))
            part = m[0:16]
            for j in range(1, rows // 16):
                part = part + m[16 * j:16 * (j + 1)]
            return acc + part.astype(F32)
        acc = over_tiles(body, jnp.zeros((16, tq), F32))
        return jnp.sum(acc, axis=0, keepdims=True).astype(I32)

    def top_step(it, st):
        u, cnt_u = st
        cand_u = u | lax.shift_left(jnp.int32(1), jnp.int32(31) - it)
        cnt = count_top(cand_u ^ INT_MIN)
        take = cnt >= n_sel
        return jnp.where(take, cand_u, u), jnp.where(take, cnt, cnt_u)

    u, _ = lax.fori_loop(0, 16, top_step, (jnp.zeros((1, tq), I32), n_valid))

    def any_lane(flag):
        return jnp.max(jnp.where(flag, 1.0, 0.0)) > 0.0

    base = u ^ INT_MIN
    base = jnp.where(base < 0, base | 0xFFFF, base)
    base = jnp.clip(base, KEY_LOWEST + BF16_HALF_STEP + 1, INT_MAX - 2 * BF16_HALF_STEP)
    many = n_valid > n_sel
    lo = jnp.where(many, base - BF16_HALF_STEP - 1, KEY_LOWEST)
    hi = jnp.where(many, base + 2 * BF16_HALF_STEP, INT_MAX)
    def count_two(pred_a, pred_b):
        def body(kb, width, acc):
            x = sc_ref[pl.ds(kb, width)].reshape(width * tq // 8, 8, tq)
            return (acc[0] + jnp.sum(jnp.where(pred_a(x), 1, 0), axis=0),
                    acc[1] + jnp.sum(jnp.where(pred_b(x), 1, 0), axis=0))
        a, b = over_tiles(body, (jnp.zeros((8, tq), I32), jnp.zeros((8, tq), I32)))
        return jnp.sum(a, axis=0, keepdims=True), jnp.sum(b, axis=0, keepdims=True)

    lo_f, hi_f = as_float(lo), as_float(hi)
    cnt_lo, cnt_hi = count_two(lambda x: x >= lo_f, lambda x: x >= hi_f)

    def open_queries(lo, cnt_lo, hi, done):
        return (cnt_lo > n_sel) & (hi - lo > 1) & (done == 0)

    def bisect(last, st):
        def step(s):
            it, lo, cnt_lo, hi, cnt_hi, done = s
            mid = lo + ((hi - lo) >> 1)
            mid_f = as_float(mid)
            cnt = count(lambda x, kidx: x >= mid_f)
            live = open_queries(lo, cnt_lo, hi, done)
            up = live & (cnt >= n_sel)
            down = live & (cnt < n_sel)
            return (it + 1, jnp.where(up, mid, lo), jnp.where(up, cnt, cnt_lo),
                    jnp.where(down, mid, hi), jnp.where(down, cnt, cnt_hi), done)
        cond = lambda s: (s[0] < last) & any_lane(open_queries(s[1], s[2], s[3], s[5]))
        return lax.while_loop(cond, step, st)

    st = bisect(PEEL_AFTER_STEPS, (jnp.int32(0), lo, cnt_lo, hi, cnt_hi, jnp.zeros((1, tq), I32)))

    def peel(st):
        it, lo, cnt_lo, hi, cnt_hi, done = st
        lo_f, hi_f = as_float(lo), as_float(hi)

        def extremes(kb, width, carry):
            top, bot = carry
            x = sc_ref[pl.ds(kb, width)].reshape(width * tq // 8, 8, tq)
            top = jnp.maximum(top, jnp.max(jnp.where(x < hi_f, x, -jnp.inf), axis=0))
            bot = jnp.minimum(bot, jnp.min(jnp.where(x >= lo_f, x, jnp.inf), axis=0))
            return top, bot
        top, bot = over_tiles(extremes, (jnp.full((8, tq), -jnp.inf, F32), jnp.full((8, tq), jnp.inf, F32)))
        top = jnp.max(top, axis=0, keepdims=True)
        bot = jnp.min(bot, axis=0, keepdims=True)
        from_top = (cnt_hi == n_sel - 1) | (top == bot)
        from_bot = (cnt_lo == n_sel + 1) & jnp.logical_not(from_top)
        cnt_top, cnt_bot = count_two(lambda x: x >= top, lambda x: x > bot)
        cnt = jnp.where(from_top, cnt_top, cnt_bot)
        short = from_bot & (cnt < n_sel)
        hit = open_queries(lo, cnt_lo, hi, done) & (from_top | from_bot)
        thr = jnp.where(from_top, top, bot)
        incl = jnp.where(from_top | short, 1, 0)
        return hit, thr, incl, jnp.where(short, cnt_lo, cnt)

    def with_peel(st):
        hit, thr, incl, cnt = peel(st)
        st = bisect(64, st[:5] + (jnp.where(hit, 1, 0),))
        _, lo, cnt_lo, _, _, _ = st
        return jnp.where(hit, thr, as_float(lo)), jnp.where(hit, incl, 1), jnp.where(hit, cnt, cnt_lo)

    def without_peel(st):
        _, lo, cnt_lo, _, _, _ = st
        return as_float(lo), jnp.ones((1, tq), I32), cnt_lo

    thr, incl, cnt = lax.cond(any_lane(open_queries(*st[1:4], st[5])), with_peel, without_peel, st)
    admits_equal = incl != 0
    excess = cnt > n_sel

    def tie_limit():
        need = n_sel - count(lambda x, kidx: x > thr)
        nbits = s_len.bit_length()

        def step(it, jv):
            cand = jv | lax.shift_left(jnp.int32(1), jnp.int32(nbits - 1) - it)
            cnt = count(lambda x, kidx: (x == thr) & (kidx() < cand))
            return jnp.where(cnt < need, cand, jv)
        jv = lax.fori_loop(0, nbits, step, jnp.zeros((1, tq), I32))
        return jnp.where(excess, jv, jnp.where(admits_equal, s_len, -1))

    tied = jnp.max(jnp.where(excess, 1.0, 0.0)) > 0.0

    acc_ref[...] = jnp.zeros_like(acc_ref)

    def attend(select):
        def block(kb, width, carry):
            ms, ls = carry
            k0 = pl.multiple_of(kb * tq, tq)
            rows = width * tq
            sel = select(sc_ref[pl.ds(kb, width)].reshape(rows, tq), kb, rows)
            heads = [slice(h * DSA_HEAD_DIM, (h + 1) * DSA_HEAD_DIM) for h in range(DSA_HEADS)]
            st = [_dot_nt(k_ref[pl.ds(k0, rows), hs], q_ref[:, hs]) for hs in heads]
            st = [jnp.where(sel, x, MASK_BIAS) for x in st]
            new_m = [jnp.maximum(m, jnp.max(x, axis=0, keepdims=True)) for m, x in zip(ms, st)]
            alpha = [jnp.exp2(m - mn) for m, mn in zip(ms, new_m)]
            p = [jnp.exp2(x - mn) for x, mn in zip(st, new_m)]
            new_l = [a * l + jnp.sum(x, axis=0, keepdims=True) for a, l, x in zip(alpha, ls, p)]
            p = [x.astype(BF16) for x in p]
            for h, hs in enumerate(heads):
                pv = _dot(vt_ref[kb, hs, :], p[h][0:tq])
                for j in range(1, width):
                    pv = pv + _dot(vt_ref[kb + j, hs, :], p[h][j * tq:(j + 1) * tq])
                acc_ref[h] = alpha[h] * acc_ref[h] + pv
            return tuple(new_m), tuple(new_l)

        carry = (tuple(jnp.full((1, tq), -jnp.inf, F32) for _ in range(DSA_HEADS)),
                 tuple(jnp.zeros((1, tq), F32) for _ in range(DSA_HEADS)))
        return over_tiles(block, carry)[1]

    def attend_tied():
        jstar = tie_limit()
        return attend(lambda x, kb, rows: (x > thr) | ((x == thr) & (key_index(kb, rows) <= jstar)))

    at_least = jnp.where(admits_equal, thr, jnp.inf)
    above = jnp.where(admits_equal, jnp.inf, thr)
    ls = lax.cond(tied, attend_tied, lambda: attend(lambda x, kb, rows: (x >= at_least) | (x > above)))
    for h in range(DSA_HEADS):
        hs = slice(h * DSA_HEAD_DIM, (h + 1) * DSA_HEAD_DIM)
        o_ref[:, hs] = (acc_ref[h] / ls[h]).T.astype(o_ref.dtype)


def _dsa(q, k, vt, iq, ik, iw, *, tq):
    bsz, s, hd = q.shape
    n_sel = min(DSA_TOPK, s // 4)
    resident = lambda n: _resident((None, s, n), lambda b, i: (b, 0, 0))
    return pl.pallas_call(
        functools.partial(_dsa_kernel, tq=tq, n_sel=n_sel, s_len=s),
        grid=(bsz, s // tq),
        in_specs=[
            pl.BlockSpec((None, tq, hd), lambda b, i: (b, i, 0)),
            pl.BlockSpec((None, IDX_HEADS, tq, IDX_DIM), lambda b, i: (b, 0, i, 0)),
            pl.BlockSpec((None, IDX_HEADS, tq), lambda b, i: (b, 0, i)),
            resident(hd),
            _resident((None, s // tq, hd, tq), lambda b, i: (b, 0, 0, 0)),
            resident(IDX_DIM),
        ],
        out_specs=pl.BlockSpec((None, tq, hd), lambda b, i: (b, i, 0)),
        out_shape=jax.ShapeDtypeStruct((bsz, s, hd), BF16),
        scratch_shapes=[
            pltpu.VMEM((s // tq, tq, tq), F32),
            pltpu.VMEM((s // tq, tq, tq), BF16),
            pltpu.VMEM((DSA_HEADS, DSA_HEAD_DIM, tq), F32),
        ],
        compiler_params=_cparams(("arbitrary", "arbitrary")),
        name="dsa",
    )(q, iq, iw, k, vt, ik)


def _mix_ffn_kernel(x_ref, a_ref, b_ref, wa_ref, wb_ref, g1_ref, nw_ref, sh_ref, sc_ref, g2_ref,
                    w1_ref, w2_ref, o_ref, *, tf):
    x = x_ref[...] + g1_ref[...] * (_dot(a_ref[...], wa_ref[...]) + _dot(b_ref[...], wb_ref[...]))
    hb = (_rms(x, nw_ref[...]) * (1.0 + sc_ref[...]) + sh_ref[...]).astype(BF16)
    f = w2_ref.shape[0]
    acc = jnp.zeros(x.shape, F32)
    for j in range(f // tf):
        gate = _dot(hb, w1_ref[:, j * tf:(j + 1) * tf])
        up = _dot(hb, w1_ref[:, f + j * tf:f + (j + 1) * tf])
        acc = acc + _dot((_silu(gate) * up).astype(BF16), w2_ref[j * tf:(j + 1) * tf, :])
    o_ref[...] = x + g2_ref[...] * acc


def _mix_ffn(x, mix_a, mix_b, w_out_bf16, gate1, nw, shift, scale, gate2, w1_bf16, w2_bf16, *, tm, tf):
    bsz, s, d = x.shape
    f = w2_bf16.shape[0]
    na = mix_a.shape[-1]
    assert mix_b.shape[-1] == na and w_out_bf16.shape[0] == 2 * na
    rows = lambda n: pl.BlockSpec((None, tm, n), lambda b, i: (b, i, 0))
    mod = pl.BlockSpec((None, 1, d), lambda b, i: (b, 0, 0))
    return pl.pallas_call(
        functools.partial(_mix_ffn_kernel, tf=tf),
        grid=(bsz, s // tm),
        in_specs=[
            rows(d), rows(na), rows(na),
            _resident((na, d), lambda b, i: (0, 0)),
            _resident((na, d), lambda b, i: (1, 0)),
            mod,
            pl.BlockSpec((1, d), lambda b, i: (0, 0)),
            mod, mod, mod,
            _resident((d, 2 * f), lambda b, i: (0, 0)),
            _resident((f, d), lambda b, i: (0, 0)),
        ],
        out_specs=rows(d),
        out_shape=jax.ShapeDtypeStruct((bsz, s, d), F32),
        compiler_params=_cparams(("arbitrary", "arbitrary")),
        name="mix_ffn",
    )(x, mix_a, mix_b, w_out_bf16, w_out_bf16, gate1, nw.reshape(1, d), shift, scale, gate2,
      w1_bf16, w2_bf16)


HALO = max(POOL_WINDOWS)


def _proj_cd_kernel(x_ref, nw_ref, sh_ref, sc_ref, w_ref, pw_ref, ps_ref, qn_ref, kn_ref,
                    pool_o, q_o, k_o, v_o, ext_ref, sb_ref):
    i = pl.program_id(1)
    tm = x_ref.shape[0]
    n = POOL_GROUPS * POOL_CH
    hb = (_rms(x_ref[...], nw_ref[...]) * (1.0 + sc_ref[...]) + sh_ref[...]).astype(BF16)

    @pl.when(i == 0)
    def _():
        ext_ref[0:HALO, :] = jnp.zeros((HALO, n), F32)

    ext_ref[HALO:HALO + tm, :] = _dot(hb, w_ref[:, :n])
    for j in range(3):
        sb_ref[:, j * n:(j + 1) * n] = _dot(hb, w_ref[:, (j + 1) * n:(j + 2) * n])

    t = i * tm + lax.broadcasted_iota(I32, (tm, 1), 0)
    for g, w in enumerate(POOL_WINDOWS):
        gs = slice(g * POOL_CH, (g + 1) * POOL_CH)
        u = ext_ref[HALO:HALO + tm, gs]
        tot = u
        for j in range(1, w):
            tot = tot + ext_ref[HALO - j:HALO - j + tm, gs]
        cnt = jnp.minimum(t + 1, w).astype(F32)
        pooled = tot / cnt - u
        y = _dot(pooled.astype(BF16), pw_ref[g].astype(BF16)) * ps_ref[:, gs]
        pool_o[:, gs] = y.astype(pool_o.dtype)
    ext_ref[0:HALO, :] = ext_ref[tm:tm + HALO, :]

    qn, kn = qn_ref[...], kn_ref[...]
    for h in range(SB_HEADS):
        hs = slice(h * SB_HEAD_DIM, (h + 1) * SB_HEAD_DIM)
        q_o[:, hs] = (_rms(sb_ref[:, hs], qn) * SB_HEAD_DIM ** -0.5).astype(q_o.dtype)
        k_o[:, hs] = _rms(sb_ref[:, n + h * SB_HEAD_DIM:n + (h + 1) * SB_HEAD_DIM], kn).astype(k_o.dtype)
    tk = v_o.shape[-1]
    for j in range(v_o.shape[0]):
        v_o[j] = sb_ref[j * tk:(j + 1) * tk, 2 * n:3 * n].T.astype(v_o.dtype)


def _proj_cd(x, nw, shift, scale, w_bf16, pool_w, pool_scale, q_norm, k_norm, *, tm, tk):
    bsz, s, d = x.shape
    n = POOL_GROUPS * POOL_CH
    assert w_bf16.shape[1] == 4 * n
    rows = lambda m: pl.BlockSpec((None, tm, m), lambda b, i: (b, i, 0))
    vec = lambda m: pl.BlockSpec((1, m), lambda b, i: (0, 0))
    mod = pl.BlockSpec((None, 1, d), lambda b, i: (b, 0, 0))
    return pl.pallas_call(
        _proj_cd_kernel,
        grid=(bsz, s // tm),
        in_specs=[
            rows(d), vec(d), mod, mod,
            _resident((d, 4 * n), lambda b, i: (0, 0)),
            pl.BlockSpec((POOL_GROUPS, POOL_CH, POOL_CH), lambda b, i: (0, 0, 0)),
            vec(n), vec(SB_HEAD_DIM), vec(SB_HEAD_DIM),
        ],
        out_specs=[rows(n)] * 3 + [pl.BlockSpec((None, tm // tk, n, tk), lambda b, i: (b, i, 0, 0))],
        out_shape=[jax.ShapeDtypeStruct((bsz, s, n), BF16)] * 3
        + [jax.ShapeDtypeStruct((bsz, s // tk, n, tk), BF16)],
        scratch_shapes=[pltpu.VMEM((HALO + tm, n), F32), pltpu.VMEM((tm, 3 * n), F32)],
        compiler_params=_cparams(("arbitrary", "arbitrary")),
        name="proj_cd",
    )(x, nw.reshape(1, d), shift, scale, w_bf16, pool_w, pool_scale.reshape(1, n),
      q_norm.reshape(1, -1), k_norm.reshape(1, -1))


def _sb_kernel(q_ref, k_ref, vt_ref, o_ref, acc_ref, *, tq):
    i = pl.program_id(1)
    kofs = lax.broadcasted_iota(I32, (tq, tq), 0)
    qofs = lax.broadcasted_iota(I32, (tq, tq), 1)
    later = jnp.where(qofs > kofs, 1.0, 0.0).astype(BF16)
    heads = [slice(h * SB_HEAD_DIM, (h + 1) * SB_HEAD_DIM) for h in range(SB_HEADS)]
    acc_ref[...] = jnp.zeros_like(acc_ref)

    def block(kb, runs, diagonal):
        k0 = pl.multiple_of(kb * tq, tq)
        z = [_dot_nt(k_ref[pl.ds(k0, tq), hs], q_ref[:, hs]) for hs in heads]
        sp = [jnp.log(1.0 + jnp.exp(-jnp.abs(x))) for x in z]
        log_beta = [jnp.minimum(x, 0.0) - s for x, s in zip(z, sp)]
        log_1m = [-jnp.maximum(x, 0.0) - s for x, s in zip(z, sp)]
        if diagonal:
            strict = kofs < qofs
            log_1m = [jnp.where(strict, x, 0.0) for x in log_1m]
        hi = [x.astype(BF16) for x in log_1m]
        lo = [(x - h.astype(F32)).astype(BF16) for x, h in zip(log_1m, hi)]
        after = [_dot(later, h) + _dot(later, l) + r for h, l, r in zip(hi, lo, runs)]
        w = [jnp.exp(lb + a) for lb, a in zip(log_beta, after)]
        if diagonal:
            w = [jnp.where(strict, x, 0.0) for x in w]
        for h, hs in enumerate(heads):
            acc_ref[h] = acc_ref[h] + _dot(vt_ref[kb, hs, :], w[h].astype(BF16))
        return tuple(r + jnp.sum(x, axis=0, keepdims=True) for r, x in zip(runs, log_1m))

    def alive(runs):
        worst = functools.reduce(jnp.maximum, runs)
        return jnp.max(worst) > SB_DEAD_LOG

    runs = block(i, tuple(jnp.zeros((1, tq), F32) for _ in heads), True)

    def earlier(state):
        kb, runs, _ = state
        runs = block(kb, runs, False)
        return kb - 1, runs, alive(runs)

    lax.while_loop(lambda st: (st[0] >= 0) & st[2], earlier, (i - 1, runs, alive(runs)))
    for h, hs in enumerate(heads):
        o_ref[:, hs] = acc_ref[h].T.astype(o_ref.dtype)


def _sb(q, k, vt, *, tq):
    bsz, s, hd = q.shape
    return pl.pallas_call(
        functools.partial(_sb_kernel, tq=tq),
        grid=(bsz, s // tq),
        in_specs=[
            pl.BlockSpec((None, tq, hd), lambda b, i: (b, i, 0)),
            _resident((None, s, hd), lambda b, i: (b, 0, 0)),
            _resident((None, s // tq, hd, tq), lambda b, i: (b, 0, 0, 0)),
        ],
        out_specs=pl.BlockSpec((None, tq, hd), lambda b, i: (b, i, 0)),
        out_shape=jax.ShapeDtypeStruct((bsz, s, hd), BF16),
        scratch_shapes=[pltpu.VMEM((SB_HEADS, SB_HEAD_DIM, tq), F32)],
        compiler_params=_cparams(("arbitrary", "arbitrary")),
        name="stick_breaking",
    )(q, k, vt)


def _pack_ab_weight(w):
    d = w.shape[0]
    gq, gk, gv, glow, gr, dq, dk, dv, iq, ik, iw = jnp.split(
        w, [256, 512, 1024, 1040, 1552, 2064, 2576, 3088, 3600, 3664], axis=1)
    pad = jnp.zeros((d, LANES - (IDX_DIM + GLA_GATE_RANK + IDX_HEADS)), w.dtype)
    return jnp.concatenate([gq, gk, gv, gr, ik, glow, iw, pad, dq, dk, dv, iq], axis=1)


def kernel(x, c, positions, ada_w, ada_b, mix_norm, ffn_norm, ffn_w1, ffn_w2, ab_w_in, gla_gate_up,
           gla_gate_b, gla_out_norm, dsa_q_norm, dsa_k_norm, ab_w_out, cd_w_in, pool_w, pool_scale,
           sb_q_norm, sb_k_norm, cd_w_out):
    bsz, s, d = x.shape
    depth = ada_w.shape[0]
    mod = _ada_mod(c, ada_w, ada_b).reshape(depth, bsz, 6, 1, d)
    tm = min(512, s)
    tq = min(256, s)
    for layer in range(depth):
        sh1, sc1, g1, sh2, sc2, g2 = (mod[layer, :, j] for j in range(6))
        i = layer // 2
        if layer % 2 == 0:
            proj, q, k, vt, iq, ik, iw = _proj_ab(
                x, mix_norm[layer], sh1, sc1, _pack_ab_weight(ab_w_in[i]).astype(BF16), positions,
                dsa_q_norm[i], dsa_k_norm[i], tm=tq, tk=tq)
            mix_a = _gla(proj, gla_gate_up[i], gla_gate_b[i], gla_out_norm[i], ts=tm)
            mix_b = _dsa(q, k, vt, iq, ik, iw, tq=tq)
            w_out = ab_w_out[i]
        else:
            mix_a, q, k, vt = _proj_cd(x, mix_norm[layer], sh1, sc1, cd_w_in[i].astype(BF16), pool_w[i],
                                       pool_scale[i], sb_q_norm[i], sb_k_norm[i], tm=tm, tk=tq)
            mix_b = _sb(q, k, vt, tq=tq)
            w_out = cd_w_out[i]
        x = _mix_ffn(x, mix_a, mix_b, w_out.astype(BF16), g1, ffn_norm[layer], sh2, sc2, g2,
                     ffn_w1[layer].astype(BF16), ffn_w2[layer].astype(BF16), tm=tm, tf=256)
    return x
```

```python
import functools

import jax
import jax.numpy as jnp
from jax import lax
from jax.experimental import pallas as pl
from jax.experimental.pallas import tpu as pltpu

F32 = jnp.float32
BF16 = jnp.bfloat16
I32 = jnp.int32

D_MODEL = 1024
GLA_HEADS, GLA_DK, GLA_DV = 4, 64, 128
GLA_GATE_RANK = 16
GLA_GATE_TAU = 16.0
GLA_CHUNK = 64
DSA_HEADS, DSA_HEAD_DIM = 4, 128
IDX_HEADS, IDX_DIM = 8, 64
DSA_TOPK = 256
POOL_WINDOWS = (2, 4, 8, 16)
POOL_GROUPS, POOL_CH = 4, 128
SB_HEADS, SB_HEAD_DIM = 4, 128
ROPE_THETA = 10000.0
NORM_EPS = 1e-6
D_FF = 2816

LANES = 128
VMEM_LIMIT = 56 * 1024 * 1024

AB_GQ, AB_GK, AB_GV, AB_GR = 0, 256, 512, 1024
AB_SMALL = 1536
SM_IK, SM_GLOW, SM_IW = 0, 64, 80
AB_GLA_N = AB_SMALL + LANES
AB_DSA_N = 4 * 512

DSA_Q_SCALE = 1.4426950408889634 * DSA_HEAD_DIM ** -0.5
INT_MIN = -2 ** 31
INT_MAX = 2 ** 31 - 1
FLT_TINY = 2.0 ** -126
KEY_TINY = 0x00800000
KEY_LOWEST = INT_MIN + 0x00800000
KEY_LOWEST_BF16 = INT_MIN + 0x00810000
BF16_HALF_STEP = 0x8000
PEEL_AFTER_STEPS = 5
MASK_BIAS = -1e30
SB_DEAD_LOG = -110.0


def _dot(a, b):
    return jnp.dot(a, b, preferred_element_type=F32)


def _dot_nt(a, b):
    return lax.dot_general(a, b, (((1,), (1,)), ((), ())), preferred_element_type=F32)


def _dot_tn(a, b):
    return lax.dot_general(a, b, (((0,), (0,)), ((), ())), preferred_element_type=F32)


def _split3(a):
    hi = a.astype(BF16)
    r1 = a - hi.astype(F32)
    mid = r1.astype(BF16)
    lo = (r1 - mid.astype(F32)).astype(BF16)
    return hi, mid, lo


def _silu(x):
    return x * jax.nn.sigmoid(x)


def _rms(x, w):
    var = jnp.mean(x * x, axis=-1, keepdims=True)
    return x * lax.rsqrt(var + NORM_EPS) * w


def _cparams(sem):
    return pltpu.CompilerParams(dimension_semantics=sem, vmem_limit_bytes=VMEM_LIMIT)


def _resident(block_shape, index_map):
    return pl.BlockSpec(block_shape, index_map, pipeline_mode=pl.Buffered(1))


def _ada_kernel(c_ref, w_ref, b_ref, o_ref):
    cond = _silu(c_ref[...])
    c_hi, c_mid, _ = _split3(cond)
    w_hi, w_mid, _ = _split3(w_ref[...])
    acc = _dot(c_hi, w_hi) + _dot(c_hi, w_mid) + _dot(c_mid, w_hi)
    o_ref[...] = acc + b_ref[...]


def _ada_mod(c, ada_w, ada_b):
    depth, d, n = ada_w.shape
    bsz = c.shape[0]
    rows = 16
    cp =jnp.zeros((rows, d), F32).at[:bsz].set(c)
    tn = 1536
    out = pl.pallas_call(
        _ada_kernel,
        grid=(depth, n // tn),
        in_specs=[
            pl.BlockSpec((rows, d), lambda l, j: (0, 0)),
            pl.BlockSpec((None, d, tn), lambda l, j: (l, 0, j)),
            pl.BlockSpec((None, 1, tn), lambda l, j: (l, 0, j)),
        ],
        out_specs=pl.BlockSpec((None, rows, tn), lambda l, j: (l, 0, j)),
        out_shape=jax.ShapeDtypeStruct((depth, rows, n), F32),
        compiler_params=_cparams(("arbitrary", "arbitrary")),
        name="ada_mod",
    )(cp, ada_w, ada_b.reshape(depth, 1, n))
    return out[:, :bsz]


def _gla_kernel(q_ref, k_ref, v_ref, gr_ref, sm_ref, gup_ref, gb_ref, onw_ref, o_ref, st_ref, tri_ref, *, ts):
    c = GLA_CHUNK

    @pl.when(pl.program_id(1) == 0)
    def _():
        st_ref[...] = jnp.zeros_like(st_ref)
        row = lax.broadcasted_iota(I32, (ts, ts), 0)
        col = lax.broadcasted_iota(I32, (ts, ts), 1)
        tri_ref[...] = jnp.where((row // c == col // c) & (col <= row), 1.0, 0.0).astype(BF16)

    nc = ts // c
    tri = tri_ref[...]
    causal = lax.broadcasted_iota(I32, (c, c), 1) <= lax.broadcasted_iota(I32, (c, c), 0)
    onw = onw_ref[...]

    glow = sm_ref[:, SM_GLOW:SM_GLOW + GLA_GATE_RANK]
    a = _dot(glow.astype(BF16), gup_ref[...].astype(BF16)) + gb_ref[...]
    g = (jnp.minimum(a, 0.0) - jnp.log1p(jnp.exp(-jnp.abs(a)))) / GLA_GATE_TAU
    g_hi, g_mid, g_lo = _split3(g)
    b = _dot(tri, g_hi) + _dot(tri, g_mid) + _dot(tri, g_lo)
    qs = q_ref[...] * (GLA_DK ** -0.5)
    k = k_ref[...]

    qe, ke, kd, qd, dec = [], [], [], [], []
    for ci in range(nc):
        r = slice(ci * c, (ci + 1) * c)
        bc = b[r]
        b_mid = bc[c // 2 - 1:c // 2, :]
        b_last = bc[c - 1:c, :]
        qe.append((qs[r] * jnp.exp(bc - b_mid)).astype(BF16))
        ke.append((k[r] * jnp.exp(b_mid - bc)).astype(BF16))
        kd.append((k[r] * jnp.exp(b_last - bc)).astype(BF16))
        qd.append((qs[r] * jnp.exp(bc)).astype(BF16))
        dec.append(jnp.exp(b_last))

    pairs = [(ci, h) for ci in range(nc) for h in range(GLA_HEADS)]
    ksl = lambda h: slice(h * GLA_DK, (h + 1) * GLA_DK)
    vsl = lambda h: slice(h * GLA_DV, (h + 1) * GLA_DV)
    vh = {(ci, h): v_ref[ci * c:(ci + 1) * c, vsl(h)].astype(BF16) for ci, h in pairs}
    att = {p: _dot_nt(qe[p[0]][:, ksl(p[1])], ke[p[0]][:, ksl(p[1])]) for p in pairs}
    att = {p: jnp.where(causal, att[p], 0.0).astype(BF16) for p in pairs}
    o = {p: _dot(att[p], vh[p]) for p in pairs}
    kvt = {p: _dot_tn(vh[p], kd[p[0]][:, ksl(p[1])]) for p in pairs}
    st_in = {}
    for h in range(GLA_HEADS):
        st = st_ref[h]
        for ci in range(nc):
            st_in[(ci, h)] = st.astype(BF16)
            st = st * dec[ci][:, ksl(h)] + kvt[(ci, h)]
        st_ref[h] = st
    for ci, h in pairs:
        out = o[(ci, h)] + _dot_nt(qd[ci][:, ksl(h)], st_in[(ci, h)])
        grh = gr_ref[ci * c:(ci + 1) * c, vsl(h)]
        o_ref[ci * c:(ci + 1) * c, vsl(h)] = (_rms(out, onw) * _silu(grh)).astype(o_ref.dtype)


def _gla(proj, gate_up, gate_b, out_norm, *, ts):
    bsz, s, _ = proj.shape
    hk, hv = GLA_HEADS * GLA_DK, GLA_HEADS * GLA_DV
    return pl.pallas_call(
        functools.partial(_gla_kernel, ts=ts),
        grid=(bsz, s // ts),
        in_specs=[
            pl.BlockSpec((None, ts, hk), lambda b, i: (b, i, AB_GQ // hk)),
            pl.BlockSpec((None, ts, hk), lambda b, i: (b, i, AB_GK // hk)),
            pl.BlockSpec((None, ts, hv), lambda b, i: (b, i, AB_GV // hv)),
            pl.BlockSpec((None, ts, hv), lambda b, i: (b, i, AB_GR // hv)),
            pl.BlockSpec((None, ts, LANES), lambda b, i: (b, i, AB_SMALL // LANES)),
            pl.BlockSpec((GLA_GATE_RANK, hk), lambda b, i: (0, 0)),
            pl.BlockSpec((1, hk), lambda b, i: (0, 0)),
            pl.BlockSpec((1, GLA_DV), lambda b, i: (0, 0)),
        ],
        out_specs=pl.BlockSpec((None, ts, hv), lambda b, i: (b, i, 0)),
        out_shape=jax.ShapeDtypeStruct((bsz, s, hv), BF16),
        scratch_shapes=[pltpu.VMEM((GLA_HEADS, GLA_DV, GLA_DK), F32), pltpu.VMEM((ts, ts), BF16)],
        compiler_params=_cparams(("arbitrary", "arbitrary")),
        name="gla",
    )(proj, proj, proj, proj, proj, gate_up, gate_b.reshape(1, hk), out_norm.reshape(1, GLA_DV))


def _dsa_prep_kernel(dq_ref, dk_ref, dv_ref, iq_ref, sm_ref, pos_ref, qn_ref, kn_ref, freq_ref,
                     q_o, k_o, v_o, iq_o, ik_o, iw_o):
    pos = pos_ref[...].astype(F32)
    lane = lax.broadcasted_iota(I32, (1, LANES), 1)
    half_a, half_i = DSA_HEAD_DIM // 2, IDX_DIM // 2

    ang = pos * freq_ref[...]
    cos_t, sin_t = jnp.cos(ang), jnp.sin(ang)

    def attn_table(t):
        return jnp.where(lane < half_a, t, pltpu.roll(t, half_a, 1))

    def idx_table(t):
        return jnp.where(lane < half_i, pltpu.roll(t, half_a, 1),
                         jnp.where(lane < 2 * half_i, pltpu.roll(t, half_a + half_i, 1), t))

    cos_a = attn_table(cos_t)
    sin_a = jnp.where(lane < half_a, -1.0, 1.0) * attn_table(sin_t)

    def rope_attn(t):
        return t * cos_a + pltpu.roll(t, half_a, 1) * sin_a

    first = (lane % IDX_DIM) < half_i
    cos_i = idx_table(cos_t)
    sin_i = jnp.where(first, -1.0, 1.0) * idx_table(sin_t)

    def rope_idx(t):
        rot = jnp.where(first, pltpu.roll(t, LANES - IDX_DIM // 2, 1), pltpu.roll(t, IDX_DIM // 2, 1))
        return t * cos_i + rot * sin_i

    qn, kn = qn_ref[...], kn_ref[...]
    for h in range(DSA_HEADS):
        hs = slice(h * DSA_HEAD_DIM, (h + 1) * DSA_HEAD_DIM)
        q_o[:, hs] = (rope_attn(_rms(dq_ref[:, hs], qn)) * DSA_Q_SCALE).astype(q_o.dtype)
        k_o[:, hs] = rope_attn(_rms(dk_ref[:, hs], kn)).astype(k_o.dtype)
    tk = v_o.shape[-1]
    for j in range(v_o.shape[0]):
        v_o[j] = dv_ref[j * tk:(j + 1) * tk, :].T.astype(v_o.dtype)
    for j in range(IDX_HEADS * IDX_DIM // LANES):
        r = rope_idx(iq_ref[:, j * LANES:(j + 1) * LANES])
        iq_o[2 * j] = r[:, :IDX_DIM].astype(iq_o.dtype)
        iq_o[2 * j + 1] = r[:, IDX_DIM:].astype(iq_o.dtype)
    sm = sm_ref[...]
    ik_o[...] = rope_idx(sm)[:, SM_IK:SM_IK + IDX_DIM].astype(ik_o.dtype)
    iw_o[...] = sm.T[SM_IW:SM_IW + IDX_HEADS, :]


def _proj_ab_kernel(x_ref, nw_ref, sh_ref, sc_ref, w_ref, pos_ref, qn_ref, kn_ref, freq_ref,
                    gla_o, q_o, k_o, v_o, iq_o, ik_o, iw_o, dsa_ref):
    hb = (_rms(x_ref[...], nw_ref[...]) * (1.0 + sc_ref[...]) + sh_ref[...]).astype(BF16)
    hd = DSA_HEADS * DSA_HEAD_DIM
    gla_o[:, AB_SMALL:] = _dot(hb, w_ref[:, AB_SMALL:AB_GLA_N])
    for j in range(AB_DSA_N // hd):
        dsa_ref[:, j * hd:(j + 1) * hd] = _dot(hb, w_ref[:, AB_GLA_N + j * hd:AB_GLA_N + (j + 1) * hd])
    part = lambda j: dsa_ref.at[:, j * hd:(j + 1) * hd]
    _dsa_prep_kernel(part(0), part(1), part(2), part(3), gla_o.at[:, AB_SMALL:AB_SMALL + LANES],
                     pos_ref, qn_ref, kn_ref, freq_ref, q_o, k_o, v_o, iq_o, ik_o, iw_o)
    gla_o[:, :AB_SMALL] = _dot(hb, w_ref[:, :AB_SMALL])


def _proj_ab(x, nw, shift, scale, w_bf16, positions, q_norm, k_norm, *, tm, tk):
    bsz, s, d = x.shape
    hd = DSA_HEADS * DSA_HEAD_DIM
    half_a, half_i = DSA_HEAD_DIM // 2, IDX_DIM // 2
    inv_a = ROPE_THETA ** (-jnp.arange(half_a, dtype=F32) / half_a)
    inv_i = ROPE_THETA ** (-jnp.arange(half_i, dtype=F32) / half_i)
    freq = jnp.concatenate([inv_a, inv_i, inv_i]).reshape(1, LANES)
    rows = lambda n: pl.BlockSpec((None, tm, n), lambda b, i: (b, i, 0))
    vec = lambda n: pl.BlockSpec((1, n), lambda b, i: (0, 0))
    mod = pl.BlockSpec((None, 1, d), lambda b, i: (b, 0, 0))
    return pl.pallas_call(
        _proj_ab_kernel,
        grid=(bsz, s // tm),
        in_specs=[
            rows(d), vec(d), mod, mod,
            _resident((d, AB_GLA_N + AB_DSA_N), lambda b, i: (0, 0)),
            rows(1), vec(DSA_HEAD_DIM), vec(DSA_HEAD_DIM), vec(LANES),
        ],
        out_specs=[
            rows(AB_GLA_N), rows(hd), rows(hd),
            pl.BlockSpec((None, tm // tk, hd, tk), lambda b, i: (b, i, 0, 0)),
            pl.BlockSpec((None, IDX_HEADS, tm, IDX_DIM), lambda b, i: (b, 0, i, 0)),
            rows(IDX_DIM),
            pl.BlockSpec((None, IDX_HEADS, tm), lambda b, i: (b, 0, i)),
        ],
        out_shape=[
            jax.ShapeDtypeStruct((bsz, s, AB_GLA_N), F32),
            jax.ShapeDtypeStruct((bsz, s, hd), BF16),
            jax.ShapeDtypeStruct((bsz, s, hd), BF16),
            jax.ShapeDtypeStruct((bsz, s // tk, hd, tk), BF16),
            jax.ShapeDtypeStruct((bsz, IDX_HEADS, s, IDX_DIM), BF16),
            jax.ShapeDtypeStruct((bsz, s, IDX_DIM), BF16),
            jax.ShapeDtypeStruct((bsz, IDX_HEADS, s), F32),
        ],
        scratch_shapes=[pltpu.VMEM((tm, AB_DSA_N), F32)],
        compiler_params=_cparams(("arbitrary", "arbitrary")),
        name="proj_ab",
    )(x, nw.reshape(1, d), shift, scale, w_bf16, positions.reshape(bsz, s, 1),
      q_norm.reshape(1, -1), k_norm.reshape(1, -1), freq)


def _dsa_kernel(q_ref, iq_ref, iw_ref, k_ref, vt_ref, ik_ref, o_ref, sc_ref, top_ref, acc_ref,
                *, tq, n_sel, s_len):
    i = pl.program_id(1)
    nkb = i + 1
    kofs = lax.broadcasted_iota(I32, (tq, tq), 0)
    qidx = i * tq + lax.broadcasted_iota(I32, (tq, tq), 1)
    iw = iw_ref[...]
    idx_scale = (IDX_HEADS ** -0.5) * (IDX_DIM ** -0.5)

    def over_tiles(step, init, n=nkb, widest=4):
        carry = lax.fori_loop(0, n // widest, lambda j, c: step(widest * j, widest, c), init)
        rest = n % widest
        start = n - rest
        width = widest // 2
        while width >= 1:
            carry = lax.cond((rest & width) != 0, functools.partial(step, start, width), lambda c: c, carry)
            start = start + (rest & width)
            width //= 2
        return carry

    def key_index(kb, rows):
        return kb * tq + lax.broadcasted_iota(I32, (rows, tq), 0)

    def as_float(key):
        key = jnp.maximum(key, KEY_LOWEST)
        key = jnp.where((key > 0) & (key < KEY_TINY), KEY_TINY, key)
        return pltpu.bitcast(key ^ ((key >> 31) & 0x7FFFFFFF), F32)

    def scores(kb, width, diagonal):
        k0 = pl.multiple_of(kb * tq, tq)
        rows = width * tq
        ikb = ik_ref[pl.ds(k0, rows), :]
        sc = jnp.zeros((rows, tq), F32)
        for h in range(IDX_HEADS):
            sc = sc + jnp.maximum(_dot_nt(ikb, iq_ref[h]), 0.0) * iw[h:h + 1, :]
        sc = sc * idx_scale
        sc = jnp.where(jnp.abs(sc) < FLT_TINY, 0.0, sc)
        if diagonal:
            sc = jnp.where(kofs <= qidx - i * tq, sc, -jnp.inf)
        sc_ref[pl.ds(kb, width)] = sc.reshape(width, tq, tq)
        top_ref[pl.ds(kb, width)] = sc.astype(BF16).reshape(width, tq, tq)

    def below_diagonal(kb, width, carry):
        scores(kb, width, False)
        return carry

    over_tiles(below_diagonal, 0, n=i)
    scores(i, 1, True)

    def count(pred):
        def body(kb, width, acc):
            rows = width * tq
            m = jnp.where(pred(sc_ref[pl.ds(kb, width)].reshape(rows, tq), lambda: key_index(kb, rows)), 1, 0)
            return acc + jnp.sum(m.reshape(rows // 8, 8, tq), axis=0)
        return jnp.sum(over_tiles(body, jnp.zeros((8, tq), I32)), axis=0, keepdims=True)

    n_valid = qidx[0:1, :] + 1

    def count_top(cand):
        k16 = jnp.maximum(cand, KEY_LOWEST_BF16) >> 16
        pattern = k16 ^ ((k16 >> 15) & 0x7FFF)
        pattern = jnp.where((pattern > 0) & (pattern < 0x80), 0x80, pattern)
        cand_bf = pltpu.bitcast(pattern << 16, F32).astype(BF16)

        def body(kb, width, acc):
            rows = width * tq
            top = top_ref[pl.ds(kb, width)].reshape(rows, tq)
            m = jnp.where(top >= cand_bf, jnp.ones((), BF16), jnp.zeros((), BF16))
            part = m[0:16]
            for j in range(1, rows // 16):
                part = part + m[16 * j:16 * (j + 1)]
            return acc + part.astype(F32)
        acc = over_tiles(body, jnp.zeros((16, tq), F32))
        return jnp.sum(acc, axis=0, keepdims=True).astype(I32)

    def top_step(it, st):
        u, cnt_u = st
        cand_u = u | lax.shift_left(jnp.int32(1), jnp.int32(31) - it)
        cnt = count_top(cand_u ^ INT_MIN)
        take = cnt >= n_sel
        return jnp.where(take, cand_u, u), jnp.where(take, cnt, cnt_u)

    u, _ = lax.fori_loop(0, 16, top_step, (jnp.zeros((1, tq), I32), n_valid))

    def any_lane(flag):
        return jnp.max(jnp.where(flag, 1.0, 0.0)) > 0.0

    base = u ^ INT_MIN
    base = jnp.where(base < 0, base | 0xFFFF, base)
    base = jnp.clip(base, KEY_LOWEST + BF16_HALF_STEP + 1, INT_MAX - 2 * BF16_HALF_STEP)
    many = n_valid > n_sel
    lo = jnp.where(many, base - BF16_HALF_STEP - 1, KEY_LOWEST)
    hi = jnp.where(many, base + 2 * BF16_HALF_STEP, INT_MAX)
    def count_two(pred_a, pred_b):
        def body(kb, width, acc):
            x = sc_ref[pl.ds(kb, width)].reshape(width * tq // 8, 8, tq)
            return (acc[0] + jnp.sum(jnp.where(pred_a(x), 1, 0), axis=0),
                    acc[1] + jnp.sum(jnp.where(pred_b(x), 1, 0), axis=0))
        a, b = over_tiles(body, (jnp.zeros((8, tq), I32), jnp.zeros((8, tq), I32)))
        return jnp.sum(a, axis=0, keepdims=True), jnp.sum(b, axis=0, keepdims=True)

    lo_f, hi_f = as_float(lo), as_float(hi)
    cnt_lo, cnt_hi = count_two(lambda x: x >= lo_f, lambda x: x >= hi_f)

    def open_queries(lo, cnt_lo, hi, done):
        return (cnt_lo > n_sel) & (hi - lo > 1) & (done == 0)

    def halve(s):
        lo, cnt_lo, hi, cnt_hi, done = s
        mid = lo + ((hi - lo) >> 1)
        mid_f = as_float(mid)
        cnt = count(lambda x, kidx: x >= mid_f)
        live = open_queries(lo, cnt_lo, hi, done)
        up = live & (cnt >= n_sel)
        down = live & (cnt < n_sel)
        return (jnp.where(up, mid, lo), jnp.where(up, cnt, cnt_lo),
                jnp.where(down, mid, hi), jnp.where(down, cnt, cnt_hi), done)

    st = lax.fori_loop(0, PEEL_AFTER_STEPS, lambda _, s: halve(s),
                       (lo, cnt_lo, hi, cnt_hi, jnp.zeros((1, tq), I32)))

    def peel(st):
        lo, cnt_lo, hi, cnt_hi, done = st
        lo_f, hi_f = as_float(lo), as_float(hi)

        def extremes(kb, width, carry):
            top, bot = carry
            x = sc_ref[pl.ds(kb, width)].reshape(width * tq // 8, 8, tq)
            top = jnp.maximum(top, jnp.max(jnp.where(x < hi_f, x, -jnp.inf), axis=0))
            bot = jnp.minimum(bot, jnp.min(jnp.where(x >= lo_f, x, jnp.inf), axis=0))
            return top, bot
        top, bot = over_tiles(extremes, (jnp.full((8, tq), -jnp.inf, F32), jnp.full((8, tq), jnp.inf, F32)))
        top = jnp.max(top, axis=0, keepdims=True)
        bot = jnp.min(bot, axis=0, keepdims=True)
        from_top = (cnt_hi == n_sel - 1) | (top == bot)
        from_bot = (cnt_lo == n_sel + 1) & jnp.logical_not(from_top)
        cnt_top, cnt_bot = count_two(lambda x: x >= top, lambda x: x > bot)
        cnt = jnp.where(from_top, cnt_top, cnt_bot)
        short = from_bot & (cnt < n_sel)
        hit = open_queries(lo, cnt_lo, hi, done) & (from_top | from_bot)
        thr = jnp.where(from_top, top, bot)
        incl = jnp.where(from_top | short, 1, 0)
        return hit, thr, incl, jnp.where(short, cnt_lo, cnt)

    def with_peel(st):
        hit, thr, incl, cnt = peel(st)
        st = st[:4] + (jnp.where(hit, 1, 0),)
        lo, cnt_lo, _, _, _ = lax.while_loop(lambda s: any_lane(open_queries(s[0], s[1], s[2], s[4])), halve, st)
        return jnp.where(hit, thr, as_float(lo)), jnp.where(hit, incl, 1), jnp.where(hit, cnt, cnt_lo)

    def without_peel(st):
        lo, cnt_lo, _, _, _ = st
        return as_float(lo), jnp.ones((1, tq), I32), cnt_lo

    thr, incl, cnt = lax.cond(any_lane(open_queries(st[0], st[1], st[2], st[4])), with_peel, without_peel, st)
    admits_equal = incl != 0
    excess = cnt > n_sel

    def tie_limit():
        need = n_sel - count(lambda x, kidx: x > thr)
        nbits = s_len.bit_length()

        def step(it, jv):
            cand = jv | lax.shift_left(jnp.int32(1), jnp.int32(nbits - 1) - it)
            cnt = count(lambda x, kidx: (x == thr) & (kidx() < cand))
            return jnp.where(cnt < need, cand, jv)
        jv = lax.fori_loop(0, nbits, step, jnp.zeros((1, tq), I32))
        return jnp.where(excess, jv, jnp.where(admits_equal, s_len, -1))

    tied = jnp.max(jnp.where(excess, 1.0, 0.0)) > 0.0

    acc_ref[...] = jnp.zeros_like(acc_ref)

    def attend(select):
        def block(kb, width, carry):
            ms, ls = carry
            k0 = pl.multiple_of(kb * tq, tq)
            rows = width * tq
            sel = select(sc_ref[pl.ds(kb, width)].reshape(rows, tq), kb, rows)
            heads = [slice(h * DSA_HEAD_DIM, (h + 1) * DSA_HEAD_DIM) for h in range(DSA_HEADS)]
            st = [_dot_nt(k_ref[pl.ds(k0, rows), hs], q_ref[:, hs]) for hs in heads]
            st = [jnp.where(sel, x, MASK_BIAS) for x in st]
            new_m = [jnp.maximum(m, jnp.max(x, axis=0, keepdims=True)) for m, x in zip(ms, st)]
            alpha = [jnp.exp2(m - mn) for m, mn in zip(ms, new_m)]
            p = [jnp.exp2(x - mn) for x, mn in zip(st, new_m)]
            new_l = [a * l + jnp.sum(x, axis=0, keepdims=True) for a, l, x in zip(alpha, ls, p)]
            p = [x.astype(BF16) for x in p]
            for h, hs in enumerate(heads):
                pv = _dot(vt_ref[kb, hs, :], p[h][0:tq])
                for j in range(1, width):
                    pv = pv + _dot(vt_ref[kb + j, hs, :], p[h][j * tq:(j + 1) * tq])
                acc_ref[h] = alpha[h] * acc_ref[h] + pv
            return tuple(new_m), tuple(new_l)

        carry = (tuple(jnp.full((1, tq), -jnp.inf, F32) for _ in range(DSA_HEADS)),
                 tuple(jnp.zeros((1, tq), F32) for _ in range(DSA_HEADS)))
        return over_tiles(block, carry)[1]

    def attend_tied():
        jstar = tie_limit()
        return attend(lambda x, kb, rows: (x > thr) | ((x == thr) & (key_index(kb, rows) <= jstar)))

    at_least = jnp.where(admits_equal, thr, jnp.inf)
    above = jnp.where(admits_equal, jnp.inf, thr)
    ls = lax.cond(tied, attend_tied, lambda: attend(lambda x, kb, rows: (x >= at_least) | (x > above)))
    for h in range(DSA_HEADS):
        hs = slice(h * DSA_HEAD_DIM, (h + 1) * DSA_HEAD_DIM)
        o_ref[:, hs] = (acc_ref[h] / ls[h]).T.astype(o_ref.dtype)


def _dsa(q, k, vt, iq, ik, iw, *, tq):
    bsz, s, hd = q.shape
    n_sel = min(DSA_TOPK, s // 4)
    resident = lambda n: _resident((None, s, n), lambda b, i: (b, 0, 0))
    return pl.pallas_call(
        functools.partial(_dsa_kernel, tq=tq, n_sel=n_sel, s_len=s),
        grid=(bsz, s // tq),
        in_specs=[
            pl.BlockSpec((None, tq, hd), lambda b, i: (b, i, 0)),
            pl.BlockSpec((None, IDX_HEADS, tq, IDX_DIM), lambda b, i: (b, 0, i, 0)),
            pl.BlockSpec((None, IDX_HEADS, tq), lambda b, i: (b, 0, i)),
            resident(hd),
            _resident((None, s // tq, hd, tq), lambda b, i: (b, 0, 0, 0)),
            resident(IDX_DIM),
        ],
        out_specs=pl.BlockSpec((None, tq, hd), lambda b, i: (b, i, 0)),
        out_shape=jax.ShapeDtypeStruct((bsz, s, hd), BF16),
        scratch_shapes=[
            pltpu.VMEM((s // tq, tq, tq), F32),
            pltpu.VMEM((s // tq, tq, tq), BF16),
            pltpu.VMEM((DSA_HEADS, DSA_HEAD_DIM, tq), F32),
        ],
        compiler_params=_cparams(("arbitrary", "arbitrary")),
        name="dsa",
    )(q, iq, iw, k, vt, ik)


def _mix_ffn_kernel(x_ref, a_ref, b_ref, wa_ref, wb_ref, g1_ref, nw_ref, sh_ref, sc_ref, g2_ref,
                    w1_ref, w2_ref, o_ref, *, tf):
    x = x_ref[...] + g1_ref[...] * (_dot(a_ref[...], wa_ref[...]) + _dot(b_ref[...], wb_ref[...]))
    hb = (_rms(x, nw_ref[...]) * (1.0 + sc_ref[...]) + sh_ref[...]).astype(BF16)
    f = w2_ref.shape[0]
    acc = jnp.zeros(x.shape, F32)
    for j in range(f // tf):
        gate = _dot(hb, w1_ref[:, j * tf:(j + 1) * tf])
        up = _dot(hb, w1_ref[:, f + j * tf:f + (j + 1) * tf])
        acc = acc + _dot((_silu(gate) * up).astype(BF16), w2_ref[j * tf:(j + 1) * tf, :])
    o_ref[...] = x + g2_ref[...] * acc


def _mix_ffn(x, mix_a, mix_b, w_out_bf16, gate1, nw, shift, scale, gate2, w1_bf16, w2_bf16, *, tm, tf):
    bsz, s, d = x.shape
    f = w2_bf16.shape[0]
    na = mix_a.shape[-1]
    assert mix_b.shape[-1] == na and w_out_bf16.shape[0] == 2 * na
    rows = lambda n: pl.BlockSpec((None, tm, n), lambda b, i: (b, i, 0))
    mod = pl.BlockSpec((None, 1, d), lambda b, i: (b, 0, 0))
    return pl.pallas_call(
        functools.partial(_mix_ffn_kernel, tf=tf),
        grid=(bsz, s // tm),
        in_specs=[
            rows(d), rows(na), rows(na),
            _resident((na, d), lambda b, i: (0, 0)),
            _resident((na, d), lambda b, i: (1, 0)),
            mod,
            pl.BlockSpec((1, d), lambda b, i: (0, 0)),
            mod, mod, mod,
            _resident((d, 2 * f), lambda b, i: (0, 0)),
            _resident((f, d), lambda b, i: (0, 0)),
        ],
        out_specs=rows(d),
        out_shape=jax.ShapeDtypeStruct((bsz, s, d), F32),
        compiler_params=_cparams(("arbitrary", "arbitrary")),
        name="mix_ffn",
    )(x, mix_a, mix_b, w_out_bf16, w_out_bf16, gate1, nw.reshape(1, d), shift, scale, gate2,
      w1_bf16, w2_bf16)


HALO = max(POOL_WINDOWS)


def _proj_cd_kernel(x_ref, nw_ref, sh_ref, sc_ref, w_ref, pw_ref, ps_ref, qn_ref, kn_ref,
                    pool_o, q_o, k_o, v_o, ext_ref, sb_ref):
    i = pl.program_id(1)
    tm = x_ref.shape[0]
    n = POOL_GROUPS * POOL_CH
    hb = (_rms(x_ref[...], nw_ref[...]) * (1.0 + sc_ref[...]) + sh_ref[...]).astype(BF16)

    @pl.when(i == 0)
    def _():
        ext_ref[0:HALO, :] = jnp.zeros((HALO, n), F32)

    ext_ref[HALO:HALO + tm, :] = _dot(hb, w_ref[:, :n])
    for j in range(3):
        sb_ref[:, j * n:(j + 1) * n] = _dot(hb, w_ref[:, (j + 1) * n:(j + 2) * n])

    t = i * tm + lax.broadcasted_iota(I32, (tm, 1), 0)
    for g, w in enumerate(POOL_WINDOWS):
        gs = slice(g * POOL_CH, (g + 1) * POOL_CH)
        u = ext_ref[HALO:HALO + tm, gs]
        tot = u
        for j in range(1, w):
            tot = tot + ext_ref[HALO - j:HALO - j + tm, gs]
        cnt = jnp.minimum(t + 1, w).astype(F32)
        pooled = tot / cnt - u
        y = _dot(pooled.astype(BF16), pw_ref[g].astype(BF16)) * ps_ref[:, gs]
        pool_o[:, gs] = y.astype(pool_o.dtype)
    ext_ref[0:HALO, :] = ext_ref[tm:tm + HALO, :]

    qn, kn = qn_ref[...], kn_ref[...]
    for h in range(SB_HEADS):
        hs = slice(h * SB_HEAD_DIM, (h + 1) * SB_HEAD_DIM)
        q_o[:, hs] = (_rms(sb_ref[:, hs], qn) * SB_HEAD_DIM ** -0.5).astype(q_o.dtype)
        k_o[:, hs] = _rms(sb_ref[:, n + h * SB_HEAD_DIM:n + (h + 1) * SB_HEAD_DIM], kn).astype(k_o.dtype)
    tk = v_o.shape[-1]
    for j in range(v_o.shape[0]):
        v_o[j] = sb_ref[j * tk:(j + 1) * tk, 2 * n:3 * n].T.astype(v_o.dtype)


def _proj_cd(x, nw, shift, scale, w_bf16, pool_w, pool_scale, q_norm, k_norm, *, tm, tk):
    bsz, s, d = x.shape
    n = POOL_GROUPS * POOL_CH
    assert w_bf16.shape[1] == 4 * n
    rows = lambda m: pl.BlockSpec((None, tm, m), lambda b, i: (b, i, 0))
    vec = lambda m: pl.BlockSpec((1, m), lambda b, i: (0, 0))
    mod = pl.BlockSpec((None, 1, d), lambda b, i: (b, 0, 0))
    return pl.pallas_call(
        _proj_cd_kernel,
        grid=(bsz, s // tm),
        in_specs=[
            rows(d), vec(d), mod, mod,
            _resident((d, 4 * n), lambda b, i: (0, 0)),
            pl.BlockSpec((POOL_GROUPS, POOL_CH, POOL_CH), lambda b, i: (0, 0, 0)),
            vec(n), vec(SB_HEAD_DIM), vec(SB_HEAD_DIM),
        ],
        out_specs=[rows(n)] * 3 + [pl.BlockSpec((None, tm // tk, n, tk), lambda b, i: (b, i, 0, 0))],
        out_shape=[jax.ShapeDtypeStruct((bsz, s, n), BF16)] * 3
        + [jax.ShapeDtypeStruct((bsz, s // tk, n, tk), BF16)],
        scratch_shapes=[pltpu.VMEM((HALO + tm, n), F32), pltpu.VMEM((tm, 3 * n), F32)],
        compiler_params=_cparams(("arbitrary", "arbitrary")),
        name="proj_cd",
    )(x, nw.reshape(1, d), shift, scale, w_bf16, pool_w, pool_scale.reshape(1, n),
      q_norm.reshape(1, -1), k_norm.reshape(1, -1))


def _sb_kernel(q_ref, k_ref, vt_ref, o_ref, acc_ref, *, tq):
    i = pl.program_id(1)
    kofs = lax.broadcasted_iota(I32, (tq, tq), 0)
    qofs = lax.broadcasted_iota(I32, (tq, tq), 1)
    later = jnp.where(qofs > kofs, 1.0, 0.0).astype(BF16)
    heads = [slice(h * SB_HEAD_DIM, (h + 1) * SB_HEAD_DIM) for h in range(SB_HEADS)]
    acc_ref[...] = jnp.zeros_like(acc_ref)

    def block(kb, runs, diagonal):
        k0 = pl.multiple_of(kb * tq, tq)
        z = [_dot_nt(k_ref[pl.ds(k0, tq), hs], q_ref[:, hs]) for hs in heads]
        sp = [jnp.log(1.0 + jnp.exp(-jnp.abs(x))) for x in z]
        log_beta = [jnp.minimum(x, 0.0) - s for x, s in zip(z, sp)]
        log_1m = [-jnp.maximum(x, 0.0) - s for x, s in zip(z, sp)]
        if diagonal:
            strict = kofs < qofs
            log_1m = [jnp.where(strict, x, 0.0) for x in log_1m]
        hi = [x.astype(BF16) for x in log_1m]
        lo = [(x - h.astype(F32)).astype(BF16) for x, h in zip(log_1m, hi)]
        after = [_dot(later, h) + _dot(later, l) + r for h, l, r in zip(hi, lo, runs)]
        w = [jnp.exp(lb + a) for lb, a in zip(log_beta, after)]
        if diagonal:
            w = [jnp.where(strict, x, 0.0) for x in w]
        for h, hs in enumerate(heads):
            acc_ref[h] = acc_ref[h] + _dot(vt_ref[kb, hs, :], w[h].astype(BF16))
        return tuple(r + jnp.sum(x, axis=0, keepdims=True) for r, x in zip(runs, log_1m))

    def alive(runs):
        worst = functools.reduce(jnp.maximum, runs)
        return jnp.max(worst) > SB_DEAD_LOG

    runs = block(i, tuple(jnp.zeros((1, tq), F32) for _ in heads), True)

    def earlier(state):
        kb, runs, _ = state
        runs = block(kb, runs, False)
        return kb - 1, runs, alive(runs)

    lax.while_loop(lambda st: (st[0] >= 0) & st[2], earlier, (i - 1, runs, alive(runs)))
    for h, hs in enumerate(heads):
        o_ref[:, hs] = acc_ref[h].T.astype(o_ref.dtype)


def _sb(q, k, vt, *, tq):
    bsz, s, hd = q.shape
    return pl.pallas_call(
        functools.partial(_sb_kernel, tq=tq),
        grid=(bsz, s // tq),
        in_specs=[
            pl.BlockSpec((None, tq, hd), lambda b, i: (b, i, 0)),
            _resident((None, s, hd), lambda b, i: (b, 0, 0)),
            _resident((None, s // tq, hd, tq), lambda b, i: (b, 0, 0, 0)),
        ],
        out_specs=pl.BlockSpec((None, tq, hd), lambda b, i: (b, i, 0)),
        out_shape=jax.ShapeDtypeStruct((bsz, s, hd), BF16),
        scratch_shapes=[pltpu.VMEM((SB_HEADS, SB_HEAD_DIM, tq), F32)],
        compiler_params=_cparams(("arbitrary", "arbitrary")),
        name="stick_breaking",
    )(q, k, vt)


def _pack_ab_weight(w):
    d = w.shape[0]
    gq, gk, gv, glow, gr, dq, dk, dv, iq, ik, iw = jnp.split(
        w, [256, 512, 1024, 1040, 1552, 2064, 2576, 3088, 3600, 3664], axis=1)
    pad = jnp.zeros((d, LANES - (IDX_DIM + GLA_GATE_RANK + IDX_HEADS)), w.dtype)
    return jnp.concatenate([gq, gk, gv, gr, ik, glow, iw, pad, dq, dk, dv, iq], axis=1)


def kernel(x, c, positions, ada_w, ada_b, mix_norm, ffn_norm, ffn_w1, ffn_w2, ab_w_in, gla_gate_up,
           gla_gate_b, gla_out_norm, dsa_q_norm, dsa_k_norm, ab_w_out, cd_w_in, pool_w, pool_scale,
           sb_q_norm, sb_k_norm, cd_w_out):
    bsz, s, d = x.shape
    depth = ada_w.shape[0]
    mod = _ada_mod(c, ada_w, ada_b).reshape(depth, bsz, 6, 1, d)
    tm = min(512, s)
    tq = min(256, s)
    for layer in range(depth):
        sh1, sc1, g1, sh2, sc2, g2 = (mod[layer, :, j] for j in range(6))
        i = layer // 2
        if layer % 2 == 0:
            proj, q, k, vt, iq, ik, iw = _proj_ab(
                x, mix_norm[layer], sh1, sc1, _pack_ab_weight(ab_w_in[i]).astype(BF16), positions,
                dsa_q_norm[i], dsa_k_norm[i], tm=tq, tk=tq)
            mix_a = _gla(proj, gla_gate_up[i], gla_gate_b[i], gla_out_norm[i], ts=tm)
            mix_b = _dsa(q, k, vt, iq, ik, iw, tq=tq)
            w_out = ab_w_out[i]
        else:
            mix_a, q, k, vt = _proj_cd(x, mix_norm[layer], sh1, sc1, cd_w_in[i].astype(BF16), pool_w[i],
                                       pool_scale[i], sb_q_norm[i], sb_k_norm[i], tm=tm, tk=tq)
            mix_b = _sb(q, k, vt, tq=tq)
            w_out = cd_w_out[i]
        x = _mix_ffn(x, mix_a, mix_b, w_out.astype(BF16), g1, ffn_norm[layer], sh2, sc2, g2,
                     ffn_w1[layer].astype(BF16), ffn_w2[layer].astype(BF16), tm=tm, tf=256)
    return x
```

```python
import functools

import jax
import jax.numpy as jnp
from jax import lax
from jax.experimental import pallas as pl
from jax.experimental.pallas import tpu as pltpu

F32 = jnp.float32
BF16 = jnp.bfloat16
I32 = jnp.int32

D_MODEL = 1024
GLA_HEADS, GLA_DK, GLA_DV = 4, 64, 128
GLA_GATE_RANK = 16
GLA_GATE_TAU = 16.0
GLA_CHUNK = 64
DSA_HEADS, DSA_HEAD_DIM = 4, 128
IDX_HEADS, IDX_DIM = 8, 64
DSA_TOPK = 256
POOL_WINDOWS = (2, 4, 8, 16)
POOL_GROUPS, POOL_CH = 4, 128
SB_HEADS, SB_HEAD_DIM = 4, 128
ROPE_THETA = 10000.0
NORM_EPS = 1e-6
D_FF = 2816

LANES = 128
VMEM_LIMIT = 56 * 1024 * 1024
DSA_VMEM_LIMIT = 61 * 1024 * 1024

AB_GQ, AB_GK, AB_GV, AB_GR = 0, 256, 512, 1024
AB_SMALL = 1536
SM_IK, SM_GLOW, SM_IW = 0, 64, 80
AB_GLA_N = AB_SMALL + LANES
AB_DSA_N = 4 * 512

DSA_Q_SCALE = 1.4426950408889634 * DSA_HEAD_DIM ** -0.5
INT_MIN = -2 ** 31
INT_MAX = 2 ** 31 - 1
FLT_TINY = 2.0 ** -126
KEY_TINY = 0x00800000
KEY_LOWEST = INT_MIN + 0x00800000
KEY_LOWEST_BF16 = INT_MIN + 0x00810000
BF16_HALF_STEP = 0x8000
PEEL_AFTER_STEPS = 5
MASK_BIAS = -1e30
SB_DEAD_LOG = -110.0


def _dot(a, b):
    return jnp.dot(a, b, preferred_element_type=F32)


def _dot_nt(a, b):
    return lax.dot_general(a, b, (((1,), (1,)), ((), ())), preferred_element_type=F32)


def _dot_tn(a, b):
    return lax.dot_general(a, b, (((0,), (0,)), ((), ())), preferred_element_type=F32)


def _split3(a):
    hi = a.astype(BF16)
    r1 = a - hi.astype(F32)
    mid = r1.astype(BF16)
    lo = (r1 - mid.astype(F32)).astype(BF16)
    return hi, mid, lo


def _silu(x):
    return x * jax.nn.sigmoid(x)


def _rms(x, w):
    var = jnp.mean(x * x, axis=-1, keepdims=True)
    return x * lax.rsqrt(var + NORM_EPS) * w


def _cparams(sem, vmem_limit=VMEM_LIMIT):
    return pltpu.CompilerParams(dimension_semantics=sem, vmem_limit_bytes=vmem_limit)


def _resident(block_shape, index_map):
    return pl.BlockSpec(block_shape, index_map, pipeline_mode=pl.Buffered(1))


def _ada_kernel(c_ref, w_ref, b_ref, o_ref):
    cond = _silu(c_ref[...])
    c_hi, c_mid, _ = _split3(cond)
    w_hi, w_mid, _ = _split3(w_ref[...])
    acc = _dot(c_hi, w_hi) + _dot(c_hi, w_mid) + _dot(c_mid, w_hi)
    o_ref[...] = acc + b_ref[...]


def _ada_mod(c, ada_w, ada_b):
    depth, d, n = ada_w.shape
    bsz = c.shape[0]
    rows = 16
    cp =jnp.zeros((rows, d), F32).at[:bsz].set(c)
    tn = 1536
    out = pl.pallas_call(
        _ada_kernel,
        grid=(depth, n // tn),
        in_specs=[
            pl.BlockSpec((rows, d), lambda l, j: (0, 0)),
            pl.BlockSpec((None, d, tn), lambda l, j: (l, 0, j)),
            pl.BlockSpec((None, 1, tn), lambda l, j: (l, 0, j)),
        ],
        out_specs=pl.BlockSpec((None, rows, tn), lambda l, j: (l, 0, j)),
        out_shape=jax.ShapeDtypeStruct((depth, rows, n), F32),
        compiler_params=_cparams(("arbitrary", "arbitrary")),
        name="ada_mod",
    )(cp, ada_w, ada_b.reshape(depth, 1, n))
    return out[:, :bsz]


def _gla_kernel(q_ref, k_ref, v_ref, gr_ref, sm_ref, gup_ref, gb_ref, onw_ref, o_ref, st_ref, tri_ref, *, ts):
    c = GLA_CHUNK

    @pl.when(pl.program_id(1) == 0)
    def _():
        st_ref[...] = jnp.zeros_like(st_ref)
        row = lax.broadcasted_iota(I32, (ts, ts), 0)
        col = lax.broadcasted_iota(I32, (ts, ts), 1)
        tri_ref[...] = jnp.where((row // c == col // c) & (col <= row), 1.0, 0.0).astype(BF16)

    nc = ts // c
    tri = tri_ref[...]
    causal = lax.broadcasted_iota(I32, (c, c), 1) <= lax.broadcasted_iota(I32, (c, c), 0)
    onw = onw_ref[...]

    glow = sm_ref[:, SM_GLOW:SM_GLOW + GLA_GATE_RANK]
    a = _dot(glow.astype(BF16), gup_ref[...].astype(BF16)) + gb_ref[...]
    g = (jnp.minimum(a, 0.0) - jnp.log1p(jnp.exp(-jnp.abs(a)))) / GLA_GATE_TAU
    g_hi, g_mid, g_lo = _split3(g)
    b = _dot(tri, g_hi) + _dot(tri, g_mid) + _dot(tri, g_lo)
    qs = q_ref[...] * (GLA_DK ** -0.5)
    k = k_ref[...]

    qe, ke, kd, qd, dec = [], [], [], [], []
    for ci in range(nc):
        r = slice(ci * c, (ci + 1) * c)
        bc = b[r]
        b_mid = bc[c // 2 - 1:c // 2, :]
        b_last = bc[c - 1:c, :]
        qe.append((qs[r] * jnp.exp(bc - b_mid)).astype(BF16))
        ke.append((k[r] * jnp.exp(b_mid - bc)).astype(BF16))
        kd.append((k[r] * jnp.exp(b_last - bc)).astype(BF16))
        qd.append((qs[r] * jnp.exp(bc)).astype(BF16))
        dec.append(jnp.exp(b_last))

    pairs = [(ci, h) for ci in range(nc) for h in range(GLA_HEADS)]
    ksl = lambda h: slice(h * GLA_DK, (h + 1) * GLA_DK)
    vsl = lambda h: slice(h * GLA_DV, (h + 1) * GLA_DV)
    vh = {(ci, h): v_ref[ci * c:(ci + 1) * c, vsl(h)].astype(BF16) for ci, h in pairs}
    att = {p: _dot_nt(qe[p[0]][:, ksl(p[1])], ke[p[0]][:, ksl(p[1])]) for p in pairs}
    att = {p: jnp.where(causal, att[p], 0.0).astype(BF16) for p in pairs}
    o = {p: _dot(att[p], vh[p]) for p in pairs}
    kvt = {p: _dot_tn(vh[p], kd[p[0]][:, ksl(p[1])]) for p in pairs}
    st_in = {}
    for h in range(GLA_HEADS):
        st = st_ref[h]
        for ci in range(nc):
            st_in[(ci, h)] = st.astype(BF16)
            st = st * dec[ci][:, ksl(h)] + kvt[(ci, h)]
        st_ref[h] = st
    for ci, h in pairs:
        out = o[(ci, h)] + _dot_nt(qd[ci][:, ksl(h)], st_in[(ci, h)])
        grh = gr_ref[ci * c:(ci + 1) * c, vsl(h)]
        o_ref[ci * c:(ci + 1) * c, vsl(h)] = (_rms(out, onw) * _silu(grh)).astype(o_ref.dtype)


def _gla(proj, gate_up, gate_b, out_norm, *, ts):
    bsz, s, _ = proj.shape
    hk, hv = GLA_HEADS * GLA_DK, GLA_HEADS * GLA_DV
    return pl.pallas_call(
        functools.partial(_gla_kernel, ts=ts),
        grid=(bsz, s // ts),
        in_specs=[
            pl.BlockSpec((None, ts, hk), lambda b, i: (b, i, AB_GQ // hk)),
            pl.BlockSpec((None, ts, hk), lambda b, i: (b, i, AB_GK // hk)),
            pl.BlockSpec((None, ts, hv), lambda b, i: (b, i, AB_GV // hv)),
            pl.BlockSpec((None, ts, hv), lambda b, i: (b, i, AB_GR // hv)),
            pl.BlockSpec((None, ts, LANES), lambda b, i: (b, i, AB_SMALL // LANES)),
            pl.BlockSpec((GLA_GATE_RANK, hk), lambda b, i: (0, 0)),
            pl.BlockSpec((1, hk), lambda b, i: (0, 0)),
            pl.BlockSpec((1, GLA_DV), lambda b, i: (0, 0)),
        ],
        out_specs=pl.BlockSpec((None, ts, hv), lambda b, i: (b, i, 0)),
        out_shape=jax.ShapeDtypeStruct((bsz, s, hv), BF16),
        scratch_shapes=[pltpu.VMEM((GLA_HEADS, GLA_DV, GLA_DK), F32), pltpu.VMEM((ts, ts), BF16)],
        compiler_params=_cparams(("arbitrary", "arbitrary")),
        name="gla",
    )(proj, proj, proj, proj, proj, gate_up, gate_b.reshape(1, hk), out_norm.reshape(1, GLA_DV))


def _dsa_prep_kernel(dq_ref, dk_ref, dv_ref, iq_ref, sm_ref, pos_ref, qn_ref, kn_ref, freq_ref,
                     q_o, k_o, v_o, iq_o, ik_o, iw_o):
    pos = pos_ref[...].astype(F32)
    lane = lax.broadcasted_iota(I32, (1, LANES), 1)
    half_a, half_i = DSA_HEAD_DIM // 2, IDX_DIM // 2

    ang = pos * freq_ref[...]
    cos_t, sin_t = jnp.cos(ang), jnp.sin(ang)

    def attn_table(t):
        return jnp.where(lane < half_a, t, pltpu.roll(t, half_a, 1))

    def idx_table(t):
        return jnp.where(lane < half_i, pltpu.roll(t, half_a, 1),
                         jnp.where(lane < 2 * half_i, pltpu.roll(t, half_a + half_i, 1), t))

    cos_a = attn_table(cos_t)
    sin_a = jnp.where(lane < half_a, -1.0, 1.0) * attn_table(sin_t)

    def rope_attn(t):
        return t * cos_a + pltpu.roll(t, half_a, 1) * sin_a

    first = (lane % IDX_DIM) < half_i
    cos_i = idx_table(cos_t)
    sin_i = jnp.where(first, -1.0, 1.0) * idx_table(sin_t)

    def rope_idx(t):
        rot = jnp.where(first, pltpu.roll(t, LANES - IDX_DIM // 2, 1), pltpu.roll(t, IDX_DIM // 2, 1))
        return t * cos_i + rot * sin_i

    qn, kn = qn_ref[...], kn_ref[...]
    for h in range(DSA_HEADS):
        hs = slice(h * DSA_HEAD_DIM, (h + 1) * DSA_HEAD_DIM)
        q_o[:, hs] = (rope_attn(_rms(dq_ref[:, hs], qn)) * DSA_Q_SCALE).astype(q_o.dtype)
        k_o[:, hs] = rope_attn(_rms(dk_ref[:, hs], kn)).astype(k_o.dtype)
    tk = v_o.shape[-1]
    for j in range(v_o.shape[0]):
        v_o[j] = dv_ref[j * tk:(j + 1) * tk, :].T.astype(v_o.dtype)
    for j in range(IDX_HEADS * IDX_DIM // LANES):
        r = rope_idx(iq_ref[:, j * LANES:(j + 1) * LANES])
        iq_o[2 * j] = r[:, :IDX_DIM].astype(iq_o.dtype)
        iq_o[2 * j + 1] = r[:, IDX_DIM:].astype(iq_o.dtype)
    sm = sm_ref[...]
    ik_o[...] = rope_idx(sm)[:, SM_IK:SM_IK + IDX_DIM].astype(ik_o.dtype)
    iw_o[...] = sm.T[SM_IW:SM_IW + IDX_HEADS, :]


def _proj_ab_kernel(x_ref, nw_ref, sh_ref, sc_ref, w_ref, pos_ref, qn_ref, kn_ref, freq_ref,
                    gla_o, q_o, k_o, v_o, iq_o, ik_o, iw_o, dsa_ref):
    hb = (_rms(x_ref[...], nw_ref[...]) * (1.0 + sc_ref[...]) + sh_ref[...]).astype(BF16)
    hd = DSA_HEADS * DSA_HEAD_DIM
    gla_o[:, AB_SMALL:] = _dot(hb, w_ref[:, AB_SMALL:AB_GLA_N])
    for j in range(AB_DSA_N // hd):
        dsa_ref[:, j * hd:(j + 1) * hd] = _dot(hb, w_ref[:, AB_GLA_N + j * hd:AB_GLA_N + (j + 1) * hd])
    part = lambda j: dsa_ref.at[:, j * hd:(j + 1) * hd]
    _dsa_prep_kernel(part(0), part(1), part(2), part(3), gla_o.at[:, AB_SMALL:AB_SMALL + LANES],
                     pos_ref, qn_ref, kn_ref, freq_ref, q_o, k_o, v_o, iq_o, ik_o, iw_o)
    gla_o[:, :AB_SMALL] = _dot(hb, w_ref[:, :AB_SMALL])


def _proj_ab(x, nw, shift, scale, w_bf16, positions, q_norm, k_norm, *, tm, tk):
    bsz, s, d = x.shape
    hd = DSA_HEADS * DSA_HEAD_DIM
    half_a, half_i = DSA_HEAD_DIM // 2, IDX_DIM // 2
    inv_a = ROPE_THETA ** (-jnp.arange(half_a, dtype=F32) / half_a)
    inv_i = ROPE_THETA ** (-jnp.arange(half_i, dtype=F32) / half_i)
    freq = jnp.concatenate([inv_a, inv_i, inv_i]).reshape(1, LANES)
    rows = lambda n: pl.BlockSpec((None, tm, n), lambda b, i: (b, i, 0))
    vec = lambda n: pl.BlockSpec((1, n), lambda b, i: (0, 0))
    mod = pl.BlockSpec((None, 1, d), lambda b, i: (b, 0, 0))
    return pl.pallas_call(
        _proj_ab_kernel,
        grid=(bsz, s // tm),
        in_specs=[
            rows(d), vec(d), mod, mod,
            _resident((d, AB_GLA_N + AB_DSA_N), lambda b, i: (0, 0)),
            rows(1), vec(DSA_HEAD_DIM), vec(DSA_HEAD_DIM), vec(LANES),
        ],
        out_specs=[
            rows(AB_GLA_N), rows(hd), rows(hd),
            pl.BlockSpec((None, tm // tk, hd, tk), lambda b, i: (b, i, 0, 0)),
            pl.BlockSpec((None, IDX_HEADS, tm, IDX_DIM), lambda b, i: (b, 0, i, 0)),
            rows(IDX_DIM),
            pl.BlockSpec((None, IDX_HEADS, tm), lambda b, i: (b, 0, i)),
        ],
        out_shape=[
            jax.ShapeDtypeStruct((bsz, s, AB_GLA_N), F32),
            jax.ShapeDtypeStruct((bsz, s, hd), BF16),
            jax.ShapeDtypeStruct((bsz, s, hd), BF16),
            jax.ShapeDtypeStruct((bsz, s // tk, hd, tk), BF16),
            jax.ShapeDtypeStruct((bsz, IDX_HEADS, s, IDX_DIM), BF16),
            jax.ShapeDtypeStruct((bsz, s, IDX_DIM), BF16),
            jax.ShapeDtypeStruct((bsz, IDX_HEADS, s), F32),
        ],
        scratch_shapes=[pltpu.VMEM((tm, AB_DSA_N), F32)],
        compiler_params=_cparams(("arbitrary", "arbitrary")),
        name="proj_ab",
    )(x, nw.reshape(1, d), shift, scale, w_bf16, positions.reshape(bsz, s, 1),
      q_norm.reshape(1, -1), k_norm.reshape(1, -1), freq)


def _dsa_kernel(q_ref, iq_ref, iw_ref, k_ref, vt_ref, ik_ref, o_ref, sc_ref, top_ref, acc_ref,
                *, tq, n_sel, s_len):
    i = pl.program_id(1)
    nkb = i + 1
    kofs = lax.broadcasted_iota(I32, (tq, tq), 0)
    qidx = i * tq + lax.broadcasted_iota(I32, (tq, tq), 1)
    iw = iw_ref[...]
    idx_scale = (IDX_HEADS ** -0.5) * (IDX_DIM ** -0.5)

    def over_tiles(step, init, n=nkb, widest=2):
        carry = lax.fori_loop(0, n // widest, lambda j, c: step(widest * j, widest, c), init)
        rest = n % widest
        start = n - rest
        width = widest // 2
        while width >= 1:
            carry = lax.cond((rest & width) != 0, functools.partial(step, start, width), lambda c: c, carry)
            start = start + (rest & width)
            width //= 2
        return carry

    def key_index(kb, rows):
        return kb * tq + lax.broadcasted_iota(I32, (rows, tq), 0)

    def as_float(key):
        key = jnp.maximum(key, KEY_LOWEST)
        key = jnp.where((key > 0) & (key < KEY_TINY), KEY_TINY, key)
        return pltpu.bitcast(key ^ ((key >> 31) & 0x7FFFFFFF), F32)

    def scores(kb, width, diagonal):
        k0 = pl.multiple_of(kb * tq, tq)
        rows = width * tq
        ikb = ik_ref[pl.ds(k0, rows), :]
        sc = jnp.zeros((rows, tq), F32)
        for h in range(IDX_HEADS):
            sc = sc + jnp.maximum(_dot_nt(ikb, iq_ref[h]), 0.0) * iw[h:h + 1, :]
        sc = sc * idx_scale
        sc = jnp.where(jnp.abs(sc) < FLT_TINY, 0.0, sc)
        if diagonal:
            sc = jnp.where(kofs <= qidx - i * tq, sc, -jnp.inf)
        sc_ref[pl.ds(kb, width)] = sc.reshape(width, tq, tq)
        top_ref[pl.ds(kb, width)] = sc.astype(BF16).reshape(width, tq, tq)

    def below_diagonal(kb, width, carry):
        scores(kb, width, False)
        return carry

    over_tiles(below_diagonal, 0, n=i)
    scores(i, 1, True)

    def count(pred):
        def body(kb, width, acc):
            rows = width * tq
            m = jnp.where(pred(sc_ref[pl.ds(kb, width)].reshape(rows, tq), lambda: key_index(kb, rows)), 1, 0)
            return acc + jnp.sum(m.reshape(rows // 8, 8, tq), axis=0)
        return jnp.sum(over_tiles(body, jnp.zeros((8, tq), I32)), axis=0, keepdims=True)

    n_valid = qidx[0:1, :] + 1

    def count_top(cand):
        k16 = jnp.maximum(cand, KEY_LOWEST_BF16) >> 16
        pattern = k16 ^ ((k16 >> 15) & 0x7FFF)
        pattern = jnp.where((pattern > 0) & (pattern < 0x80), 0x80, pattern)
        cand_bf = pltpu.bitcast(pattern << 16, F32).astype(BF16)

        def body(kb, width, acc):
            rows = width * tq
            top = top_ref[pl.ds(kb, width)].reshape(rows, tq)
            m = jnp.where(top >= cand_bf, jnp.ones((), BF16), jnp.zeros((), BF16))
            part = m[0:16]
            for j in range(1, rows // 16):
                part = part + m[16 * j:16 * (j + 1)]
            return acc + part.astype(F32)
        acc = over_tiles(body, jnp.zeros((16, tq), F32))
        return jnp.sum(acc, axis=0, keepdims=True).astype(I32)

    def top_step(it, st):
        u, cnt_u = st
        cand_u = u | lax.shift_left(jnp.int32(1), jnp.int32(31) - it)
        cnt = count_top(cand_u ^ INT_MIN)
        take = cnt >= n_sel
        return jnp.where(take, cand_u, u), jnp.where(take, cnt, cnt_u)

    u, _ = lax.fori_loop(0, 16, top_step, (jnp.zeros((1, tq), I32), n_valid))

    def any_lane(flag):
        return jnp.max(jnp.where(flag, 1.0, 0.0)) > 0.0

    base = u ^ INT_MIN
    base = jnp.where(base < 0, base | 0xFFFF, base)
    base = jnp.clip(base, KEY_LOWEST + BF16_HALF_STEP + 1, INT_MAX - 2 * BF16_HALF_STEP)
    many = n_valid > n_sel
    lo = jnp.where(many, base - BF16_HALF_STEP - 1, KEY_LOWEST)
    hi = jnp.where(many, base + 2 * BF16_HALF_STEP, INT_MAX)
    def count_two(pred_a, pred_b):
        def body(kb, width, acc):
            x = sc_ref[pl.ds(kb, width)].reshape(width * tq // 8, 8, tq)
            return (acc[0] + jnp.sum(jnp.where(pred_a(x), 1, 0), axis=0),
                    acc[1] + jnp.sum(jnp.where(pred_b(x), 1, 0), axis=0))
        a, b = over_tiles(body, (jnp.zeros((8, tq), I32), jnp.zeros((8, tq), I32)))
        return jnp.sum(a, axis=0, keepdims=True), jnp.sum(b, axis=0, keepdims=True)

    lo_f, hi_f = as_float(lo), as_float(hi)
    cnt_lo, cnt_hi = count_two(lambda x: x >= lo_f, lambda x: x >= hi_f)

    def open_queries(lo, cnt_lo, hi, done):
        return (cnt_lo > n_sel) & (hi - lo > 1) & (done == 0)

    def halve(s):
        lo, cnt_lo, hi, cnt_hi, done = s
        mid = lo + ((hi - lo) >> 1)
        mid_f = as_float(mid)
        cnt = count(lambda x, kidx: x >= mid_f)
        live = open_queries(lo, cnt_lo, hi, done)
        up = live & (cnt >= n_sel)
        down = live & (cnt < n_sel)
        return (jnp.where(up, mid, lo), jnp.where(up, cnt, cnt_lo),
                jnp.where(down, mid, hi), jnp.where(down, cnt, cnt_hi), done)

    st = lax.fori_loop(0, PEEL_AFTER_STEPS, lambda _, s: halve(s),
                       (lo, cnt_lo, hi, cnt_hi, jnp.zeros((1, tq), I32)))

    def peel(st):
        lo, cnt_lo, hi, cnt_hi, done = st
        lo_f, hi_f = as_float(lo), as_float(hi)

        def extremes(kb, width, carry):
            top, bot = carry
            x = sc_ref[pl.ds(kb, width)].reshape(width * tq // 8, 8, tq)
            top = jnp.maximum(top, jnp.max(jnp.where(x < hi_f, x, -jnp.inf), axis=0))
            bot = jnp.minimum(bot, jnp.min(jnp.where(x >= lo_f, x, jnp.inf), axis=0))
            return top, bot
        top, bot = over_tiles(extremes, (jnp.full((8, tq), -jnp.inf, F32), jnp.full((8, tq), jnp.inf, F32)))
        top = jnp.max(top, axis=0, keepdims=True)
        bot = jnp.min(bot, axis=0, keepdims=True)
        from_top = (cnt_hi == n_sel - 1) | (top == bot)
        from_bot = (cnt_lo == n_sel + 1) & jnp.logical_not(from_top)
        cnt_top, cnt_bot = count_two(lambda x: x >= top, lambda x: x > bot)
        cnt = jnp.where(from_top, cnt_top, cnt_bot)
        short = from_bot & (cnt < n_sel)
        hit = open_queries(lo, cnt_lo, hi, done) & (from_top | from_bot)
        thr = jnp.where(from_top, top, bot)
        incl = jnp.where(from_top | short, 1, 0)
        return hit, thr, incl, jnp.where(short, cnt_lo, cnt)

    def with_peel(st):
        hit, thr, incl, cnt = peel(st)
        st = st[:4] + (jnp.where(hit, 1, 0),)
        lo, cnt_lo, _, _, _ = lax.while_loop(lambda s: any_lane(open_queries(s[0], s[1], s[2], s[4])), halve, st)
        return jnp.where(hit, thr, as_float(lo)), jnp.where(hit, incl, 1), jnp.where(hit, cnt, cnt_lo)

    def without_peel(st):
        lo, cnt_lo, _, _, _ = st
        return as_float(lo), jnp.ones((1, tq), I32), cnt_lo

    thr, incl, cnt = lax.cond(any_lane(open_queries(st[0], st[1], st[2], st[4])), with_peel, without_peel, st)
    admits_equal = incl != 0
    excess = cnt > n_sel

    def tie_limit():
        need = n_sel - count(lambda x, kidx: x > thr)
        nbits = s_len.bit_length()

        def step(it, jv):
            cand = jv | lax.shift_left(jnp.int32(1), jnp.int32(nbits - 1) - it)
            cnt = count(lambda x, kidx: (x == thr) & (kidx() < cand))
            return jnp.where(cnt < need, cand, jv)
        jv = lax.fori_loop(0, nbits, step, jnp.zeros((1, tq), I32))
        return jnp.where(excess, jv, jnp.where(admits_equal, s_len, -1))

    tied = jnp.max(jnp.where(excess, 1.0, 0.0)) > 0.0

    acc_ref[...] = jnp.zeros_like(acc_ref)

    def attend(select):
        def block(kb, width, carry):
            ms, ls = carry
            k0 = pl.multiple_of(kb * tq, tq)
            rows = width * tq
            sel = select(sc_ref[pl.ds(kb, width)].reshape(rows, tq), kb, rows)
            heads = [slice(h * DSA_HEAD_DIM, (h + 1) * DSA_HEAD_DIM) for h in range(DSA_HEADS)]
            st = [_dot_nt(k_ref[pl.ds(k0, rows), hs], q_ref[:, hs]) for hs in heads]
            st = [jnp.where(sel, x, MASK_BIAS) for x in st]
            new_m = [jnp.maximum(m, jnp.max(x, axis=0, keepdims=True)) for m, x in zip(ms, st)]
            alpha = [jnp.exp2(m - mn) for m, mn in zip(ms, new_m)]
            p = [jnp.exp2(x - mn) for x, mn in zip(st, new_m)]
            new_l = [a * l + jnp.sum(x, axis=0, keepdims=True) for a, l, x in zip(alpha, ls, p)]
            p = [x.astype(BF16) for x in p]
            for h, hs in enumerate(heads):
                pv = _dot(vt_ref[kb, hs, :], p[h][0:tq])
                for j in range(1, width):
                    pv = pv + _dot(vt_ref[kb + j, hs, :], p[h][j * tq:(j + 1) * tq])
                acc_ref[h] = alpha[h] * acc_ref[h] + pv
            return tuple(new_m), tuple(new_l)

        carry = (tuple(jnp.full((1, tq), -jnp.inf, F32) for _ in range(DSA_HEADS)),
                 tuple(jnp.zeros((1, tq), F32) for _ in range(DSA_HEADS)))
        return over_tiles(block, carry)[1]

    def attend_tied():
        jstar = tie_limit()
        return attend(lambda x, kb, rows: (x > thr) | ((x == thr) & (key_index(kb, rows) <= jstar)))

    at_least = jnp.where(admits_equal, thr, jnp.inf)
    above = jnp.where(admits_equal, jnp.inf, thr)
    ls = lax.cond(tied, attend_tied, lambda: attend(lambda x, kb, rows: (x >= at_least) | (x > above)))
    for h in range(DSA_HEADS):
        hs = slice(h * DSA_HEAD_DIM, (h + 1) * DSA_HEAD_DIM)
        o_ref[:, hs] = (acc_ref[h] / ls[h]).T.astype(o_ref.dtype)


def _dsa(q, k, vt, iq, ik, iw, *, tq):
    bsz, s, hd = q.shape
    n_sel = min(DSA_TOPK, s // 4)
    resident = lambda n: _resident((None, s, n), lambda b, i: (b, 0, 0))
    return pl.pallas_call(
        functools.partial(_dsa_kernel, tq=tq, n_sel=n_sel, s_len=s),
        grid=(bsz, s // tq),
        in_specs=[
            _resident((None, tq, hd), lambda b, i: (b, i, 0)),
            _resident((None, IDX_HEADS, tq, IDX_DIM), lambda b, i: (b, 0, i, 0)),
            pl.BlockSpec((None, IDX_HEADS, tq), lambda b, i: (b, 0, i)),
            resident(hd),
            _resident((None, s // tq, hd, tq), lambda b, i: (b, 0, 0, 0)),
            resident(IDX_DIM),
        ],
        out_specs=pl.BlockSpec((None, tq, hd), lambda b, i: (b, i, 0)),
        out_shape=jax.ShapeDtypeStruct((bsz, s, hd), BF16),
        scratch_shapes=[
            pltpu.VMEM((s // tq, tq, tq), F32),
            pltpu.VMEM((s // tq, tq, tq), BF16),
            pltpu.VMEM((DSA_HEADS, DSA_HEAD_DIM, tq), F32),
        ],
        compiler_params=_cparams(("arbitrary", "arbitrary"), DSA_VMEM_LIMIT),
        name="dsa",
    )(q, iq, iw, k, vt, ik)


def _mix_ffn_kernel(x_ref, a_ref, b_ref, wa_ref, wb_ref, g1_ref, nw_ref, sh_ref, sc_ref, g2_ref,
                    w1_ref, w2_ref, o_ref, *, tf):
    x = x_ref[...] + g1_ref[...] * (_dot(a_ref[...], wa_ref[...]) + _dot(b_ref[...], wb_ref[...]))
    hb = (_rms(x, nw_ref[...]) * (1.0 + sc_ref[...]) + sh_ref[...]).astype(BF16)
    f = w2_ref.shape[0]
    acc = jnp.zeros(x.shape, F32)
    for j in range(f // tf):
        gate = _dot(hb, w1_ref[:, j * tf:(j + 1) * tf])
        up = _dot(hb, w1_ref[:, f + j * tf:f + (j + 1) * tf])
        acc = acc + _dot((_silu(gate) * up).astype(BF16), w2_ref[j * tf:(j + 1) * tf, :])
    o_ref[...] = x + g2_ref[...] * acc


def _mix_ffn(x, mix_a, mix_b, w_out_bf16, gate1, nw, shift, scale, gate2, w1_bf16, w2_bf16, *, tm, tf):
    bsz, s, d = x.shape
    f = w2_bf16.shape[0]
    na = mix_a.shape[-1]
    assert mix_b.shape[-1] == na and w_out_bf16.shape[0] == 2 * na
    rows = lambda n: pl.BlockSpec((None, tm, n), lambda b, i: (b, i, 0))
    mod = pl.BlockSpec((None, 1, d), lambda b, i: (b, 0, 0))
    return pl.pallas_call(
        functools.partial(_mix_ffn_kernel, tf=tf),
        grid=(bsz, s // tm),
        in_specs=[
            rows(d), rows(na), rows(na),
            _resident((na, d), lambda b, i: (0, 0)),
            _resident((na, d), lambda b, i: (1, 0)),
            mod,
            pl.BlockSpec((1, d), lambda b, i: (0, 0)),
            mod, mod, mod,
            _resident((d, 2 * f), lambda b, i: (0, 0)),
            _resident((f, d), lambda b, i: (0, 0)),
        ],
        out_specs=rows(d),
        out_shape=jax.ShapeDtypeStruct((bsz, s, d), F32),
        compiler_params=_cparams(("arbitrary", "arbitrary")),
        name="mix_ffn",
    )(x, mix_a, mix_b, w_out_bf16, w_out_bf16, gate1, nw.reshape(1, d), shift, scale, gate2,
      w1_bf16, w2_bf16)


HALO = max(POOL_WINDOWS)


def _proj_cd_kernel(x_ref, nw_ref, sh_ref, sc_ref, w_ref, pw_ref, ps_ref, qn_ref, kn_ref,
                    pool_o, q_o, k_o, v_o, ext_ref, sb_ref):
    i = pl.program_id(1)
    tm = x_ref.shape[0]
    n = POOL_GROUPS * POOL_CH
    hb = (_rms(x_ref[...], nw_ref[...]) * (1.0 + sc_ref[...]) + sh_ref[...]).astype(BF16)

    @pl.when(i == 0)
    def _():
        ext_ref[0:HALO, :] = jnp.zeros((HALO, n), F32)

    ext_ref[HALO:HALO + tm, :] = _dot(hb, w_ref[:, :n])
    for j in range(3):
        sb_ref[:, j * n:(j + 1) * n] = _dot(hb, w_ref[:, (j + 1) * n:(j + 2) * n])

    t = i * tm + lax.broadcasted_iota(I32, (tm, 1), 0)
    for g, w in enumerate(POOL_WINDOWS):
        gs = slice(g * POOL_CH, (g + 1) * POOL_CH)
        u = ext_ref[HALO:HALO + tm, gs]
        tot = u
        for j in range(1, w):
            tot = tot + ext_ref[HALO - j:HALO - j + tm, gs]
        cnt = jnp.minimum(t + 1, w).astype(F32)
        pooled = tot / cnt - u
        y = _dot(pooled.astype(BF16), pw_ref[g].astype(BF16)) * ps_ref[:, gs]
        pool_o[:, gs] = y.astype(pool_o.dtype)
    ext_ref[0:HALO, :] = ext_ref[tm:tm + HALO, :]

    qn, kn = qn_ref[...], kn_ref[...]
    for h in range(SB_HEADS):
        hs = slice(h * SB_HEAD_DIM, (h + 1) * SB_HEAD_DIM)
        q_o[:, hs] = (_rms(sb_ref[:, hs], qn) * SB_HEAD_DIM ** -0.5).astype(q_o.dtype)
        k_o[:, hs] = _rms(sb_ref[:, n + h * SB_HEAD_DIM:n + (h + 1) * SB_HEAD_DIM], kn).astype(k_o.dtype)
    tk = v_o.shape[-1]
    for j in range(v_o.shape[0]):
        v_o[j] = sb_ref[j * tk:(j + 1) * tk, 2 * n:3 * n].T.astype(v_o.dtype)


def _proj_cd(x, nw, shift, scale, w_bf16, pool_w, pool_scale, q_norm, k_norm, *, tm, tk):
    bsz, s, d = x.shape
    n = POOL_GROUPS * POOL_CH
    assert w_bf16.shape[1] == 4 * n
    rows = lambda m: pl.BlockSpec((None, tm, m), lambda b, i: (b, i, 0))
    vec = lambda m: pl.BlockSpec((1, m), lambda b, i: (0, 0))
    mod = pl.BlockSpec((None, 1, d), lambda b, i: (b, 0, 0))
    return pl.pallas_call(
        _proj_cd_kernel,
        grid=(bsz, s // tm),
        in_specs=[
            rows(d), vec(d), mod, mod,
            _resident((d, 4 * n), lambda b, i: (0, 0)),
            pl.BlockSpec((POOL_GROUPS, POOL_CH, POOL_CH), lambda b, i: (0, 0, 0)),
            vec(n), vec(SB_HEAD_DIM), vec(SB_HEAD_DIM),
        ],
        out_specs=[rows(n)] * 3 + [pl.BlockSpec((None, tm // tk, n, tk), lambda b, i: (b, i, 0, 0))],
        out_shape=[jax.ShapeDtypeStruct((bsz, s, n), BF16)] * 3
        + [jax.ShapeDtypeStruct((bsz, s // tk, n, tk), BF16)],
        scratch_shapes=[pltpu.VMEM((HALO + tm, n), F32), pltpu.VMEM((tm, 3 * n), F32)],
        compiler_params=_cparams(("arbitrary", "arbitrary")),
        name="proj_cd",
    )(x, nw.reshape(1, d), shift, scale, w_bf16, pool_w, pool_scale.reshape(1, n),
      q_norm.reshape(1, -1), k_norm.reshape(1, -1))


def _sb_kernel(q_ref, k_ref, vt_ref, o_ref, acc_ref, *, tq):
    i = pl.program_id(1)
    kofs = lax.broadcasted_iota(I32, (tq, tq), 0)
    qofs = lax.broadcasted_iota(I32, (tq, tq), 1)
    later = jnp.where(qofs > kofs, 1.0, 0.0).astype(BF16)
    heads = [slice(h * SB_HEAD_DIM, (h + 1) * SB_HEAD_DIM) for h in range(SB_HEADS)]
    acc_ref[...] = jnp.zeros_like(acc_ref)

    def block(kb, runs, diagonal):
        k0 = pl.multiple_of(kb * tq, tq)
        z = [_dot_nt(k_ref[pl.ds(k0, tq), hs], q_ref[:, hs]) for hs in heads]
        sp = [jnp.log(1.0 + jnp.exp(-jnp.abs(x))) for x in z]
        log_beta = [jnp.minimum(x, 0.0) - s for x, s in zip(z, sp)]
        log_1m = [-jnp.maximum(x, 0.0) - s for x, s in zip(z, sp)]
        if diagonal:
            strict = kofs < qofs
            log_1m = [jnp.where(strict, x, 0.0) for x in log_1m]
        hi = [x.astype(BF16) for x in log_1m]
        lo = [(x - h.astype(F32)).astype(BF16) for x, h in zip(log_1m, hi)]
        after = [_dot(later, h) + _dot(later, l) + r for h, l, r in zip(hi, lo, runs)]
        w = [jnp.exp(lb + a) for lb, a in zip(log_beta, after)]
        if diagonal:
            w = [jnp.where(strict, x, 0.0) for x in w]
        for h, hs in enumerate(heads):
            acc_ref[h] = acc_ref[h] + _dot(vt_ref[kb, hs, :], w[h].astype(BF16))
        return tuple(r + jnp.sum(x, axis=0, keepdims=True) for r, x in zip(runs, log_1m))

    def alive(runs):
        worst = functools.reduce(jnp.maximum, runs)
        return jnp.max(worst) > SB_DEAD_LOG

    runs = block(i, tuple(jnp.zeros((1, tq), F32) for _ in heads), True)

    def earlier(state):
        kb, runs, _ = state
        runs = block(kb, runs, False)
        return kb - 1, runs, alive(runs)

    lax.while_loop(lambda st: (st[0] >= 0) & st[2], earlier, (i - 1, runs, alive(runs)))
    for h, hs in enumerate(heads):
        o_ref[:, hs] = acc_ref[h].T.astype(o_ref.dtype)


def _sb(q, k, vt, *, tq):
    bsz, s, hd = q.shape
    return pl.pallas_call(
        functools.partial(_sb_kernel, tq=tq),
        grid=(bsz, s // tq),
        in_specs=[
            pl.BlockSpec((None, tq, hd), lambda b, i: (b, i, 0)),
            _resident((None, s, hd), lambda b, i: (b, 0, 0)),
            _resident((None, s // tq, hd, tq), lambda b, i: (b, 0, 0, 0)),
        ],
        out_specs=pl.BlockSpec((None, tq, hd), lambda b, i: (b, i, 0)),
        out_shape=jax.ShapeDtypeStruct((bsz, s, hd), BF16),
        scratch_shapes=[pltpu.VMEM((SB_HEADS, SB_HEAD_DIM, tq), F32)],
        compiler_params=_cparams(("arbitrary", "arbitrary")),
        name="stick_breaking",
    )(q, k, vt)


def _pack_ab_weight(w):
    d = w.shape[0]
    gq, gk, gv, glow, gr, dq, dk, dv, iq, ik, iw = jnp.split(
        w, [256, 512, 1024, 1040, 1552, 2064, 2576, 3088, 3600, 3664], axis=1)
    pad = jnp.zeros((d, LANES - (IDX_DIM + GLA_GATE_RANK + IDX_HEADS)), w.dtype)
    return jnp.concatenate([gq, gk, gv, gr, ik, glow, iw, pad, dq, dk, dv, iq], axis=1)


def kernel(x, c, positions, ada_w, ada_b, mix_norm, ffn_norm, ffn_w1, ffn_w2, ab_w_in, gla_gate_up,
           gla_gate_b, gla_out_norm, dsa_q_norm, dsa_k_norm, ab_w_out, cd_w_in, pool_w, pool_scale,
           sb_q_norm, sb_k_norm, cd_w_out):
    bsz, s, d = x.shape
    depth = ada_w.shape[0]
    mod = _ada_mod(c, ada_w, ada_b).reshape(depth, bsz, 6, 1, d)
    tm = min(512, s)
    tq_dsa = min(512, s)
    tq_sb = min(256, s)
    for layer in range(depth):
        sh1, sc1, g1, sh2, sc2, g2 = (mod[layer, :, j] for j in range(6))
        i = layer // 2
        if layer % 2 == 0:
            proj, q, k, vt, iq, ik, iw = _proj_ab(
                x, mix_norm[layer], sh1, sc1, _pack_ab_weight(ab_w_in[i]).astype(BF16), positions,
                dsa_q_norm[i], dsa_k_norm[i], tm=tq_dsa, tk=tq_dsa)
            mix_a = _gla(proj, gla_gate_up[i], gla_gate_b[i], gla_out_norm[i], ts=tm)
            mix_b = _dsa(q, k, vt, iq, ik, iw, tq=tq_dsa)
            w_out = ab_w_out[i]
        else:
            mix_a, q, k, vt = _proj_cd(x, mix_norm[layer], sh1, sc1, cd_w_in[i].astype(BF16), pool_w[i],
                                       pool_scale[i], sb_q_norm[i], sb_k_norm[i], tm=tm, tk=tq_sb)
            mix_b = _sb(q, k, vt, tq=tq_sb)
            w_out = cd_w_out[i]
        x = _mix_ffn(x, mix_a, mix_b, w_out.astype(BF16), g1, ffn_norm[layer], sh2, sc2, g2,
                     ffn_w1[layer].astype(BF16), ffn_w2[layer].astype(BF16), tm=tm, tf=256)
    return x
```

```python
import functools

import jax
import jax.numpy as jnp
from jax import lax
from jax.experimental import pallas as pl
from jax.experimental.pallas import tpu as pltpu

F32 = jnp.float32
BF16 = jnp.bfloat16
I32 = jnp.int32

D_MODEL = 1024
GLA_HEADS, GLA_DK, GLA_DV = 4, 64, 128
GLA_GATE_RANK = 16
GLA_GATE_TAU = 16.0
GLA_CHUNK = 64
DSA_HEADS, DSA_HEAD_DIM = 4, 128
IDX_HEADS, IDX_DIM = 8, 64
DSA_TOPK = 256
POOL_WINDOWS = (2, 4, 8, 16)
POOL_GROUPS, POOL_CH = 4, 128
SB_HEADS, SB_HEAD_DIM = 4, 128
ROPE_THETA = 10000.0
NORM_EPS = 1e-6
D_FF = 2816

LANES = 128
VMEM_LIMIT = 56 * 1024 * 1024
DSA_VMEM_LIMIT = 61 * 1024 * 1024

AB_GQ, AB_GK, AB_GV, AB_GR = 0, 256, 512, 1024
AB_SMALL = 1536
SM_IK, SM_GLOW, SM_IW = 0, 64, 80
AB_GLA_N = AB_SMALL + LANES
AB_DSA_N = 4 * 512

DSA_Q_SCALE = 1.4426950408889634 * DSA_HEAD_DIM ** -0.5
INT_MIN = -2 ** 31
INT_MAX = 2 ** 31 - 1
FLT_TINY = 2.0 ** -126
KEY_TINY = 0x00800000
KEY_LOWEST = INT_MIN + 0x00800000
KEY_LOWEST_BF16 = INT_MIN + 0x00810000
BF16_HALF_STEP = 0x8000
PEEL_AFTER_STEPS = 5
MASK_BIAS = -1e30
SB_DEAD_LOG = -110.0


def _dot(a, b):
    return jnp.dot(a, b, preferred_element_type=F32)


def _dot_nt(a, b):
    return lax.dot_general(a, b, (((1,), (1,)), ((), ())), preferred_element_type=F32)


def _dot_tn(a, b):
    return lax.dot_general(a, b, (((0,), (0,)), ((), ())), preferred_element_type=F32)


def _split3(a):
    hi = a.astype(BF16)
    r1 = a - hi.astype(F32)
    mid = r1.astype(BF16)
    lo = (r1 - mid.astype(F32)).astype(BF16)
    return hi, mid, lo


def _silu(x):
    return x * jax.nn.sigmoid(x)


def _rms(x, w):
    var = jnp.mean(x * x, axis=-1, keepdims=True)
    return x * lax.rsqrt(var + NORM_EPS) * w


def _cparams(sem, vmem_limit=VMEM_LIMIT):
    return pltpu.CompilerParams(dimension_semantics=sem, vmem_limit_bytes=vmem_limit)


def _resident(block_shape, index_map):
    return pl.BlockSpec(block_shape, index_map, pipeline_mode=pl.Buffered(1))


def _ada_kernel(c_ref, w_ref, b_ref, o_ref):
    cond = _silu(c_ref[...])
    c_hi, c_mid, _ = _split3(cond)
    w_hi, w_mid, _ = _split3(w_ref[...])
    acc = _dot(c_hi, w_hi) + _dot(c_hi, w_mid) + _dot(c_mid, w_hi)
    o_ref[...] = acc + b_ref[...]


def _ada_mod(c, ada_w, ada_b):
    depth, d, n = ada_w.shape
    bsz = c.shape[0]
    rows = 16
    cp =jnp.zeros((rows, d), F32).at[:bsz].set(c)
    tn = 1536
    out = pl.pallas_call(
        _ada_kernel,
        grid=(depth, n // tn),
        in_specs=[
            pl.BlockSpec((rows, d), lambda l, j: (0, 0)),
            pl.BlockSpec((None, d, tn), lambda l, j: (l, 0, j)),
            pl.BlockSpec((None, 1, tn), lambda l, j: (l, 0, j)),
        ],
        out_specs=pl.BlockSpec((None, rows, tn), lambda l, j: (l, 0, j)),
        out_shape=jax.ShapeDtypeStruct((depth, rows, n), F32),
        compiler_params=_cparams(("arbitrary", "arbitrary")),
        name="ada_mod",
    )(cp, ada_w, ada_b.reshape(depth, 1, n))
    return out[:, :bsz]


def _gla_kernel(q_ref, k_ref, v_ref, gr_ref, sm_ref, gup_ref, gb_ref, onw_ref, o_ref, st_ref, tri_ref, *, ts):
    c = GLA_CHUNK

    @pl.when(pl.program_id(1) == 0)
    def _():
        st_ref[...] = jnp.zeros_like(st_ref)
        row = lax.broadcasted_iota(I32, (ts, ts), 0)
        col = lax.broadcasted_iota(I32, (ts, ts), 1)
        tri_ref[...] = jnp.where((row // c == col // c) & (col <= row), 1.0, 0.0).astype(BF16)

    nc = ts // c
    tri = tri_ref[...]
    causal = lax.broadcasted_iota(I32, (c, c), 1) <= lax.broadcasted_iota(I32, (c, c), 0)
    onw = onw_ref[...]

    glow = sm_ref[:, SM_GLOW:SM_GLOW + GLA_GATE_RANK]
    a = _dot(glow.astype(BF16), gup_ref[...].astype(BF16)) + gb_ref[...]
    g = (jnp.minimum(a, 0.0) - jnp.log1p(jnp.exp(-jnp.abs(a)))) / GLA_GATE_TAU
    g_hi, g_mid, g_lo = _split3(g)
    b = _dot(tri, g_hi) + _dot(tri, g_mid) + _dot(tri, g_lo)
    qs = q_ref[...] * (GLA_DK ** -0.5)
    k = k_ref[...]

    qe, ke, kd, qd, dec = [], [], [], [], []
    for ci in range(nc):
        r = slice(ci * c, (ci + 1) * c)
        bc = b[r]
        b_mid = bc[c // 2 - 1:c // 2, :]
        b_last = bc[c - 1:c, :]
        qe.append((qs[r] * jnp.exp(bc - b_mid)).astype(BF16))
        ke.append((k[r] * jnp.exp(b_mid - bc)).astype(BF16))
        kd.append((k[r] * jnp.exp(b_last - bc)).astype(BF16))
        qd.append((qs[r] * jnp.exp(bc)).astype(BF16))
        dec.append(jnp.exp(b_last))

    pairs = [(ci, h) for ci in range(nc) for h in range(GLA_HEADS)]
    ksl = lambda h: slice(h * GLA_DK, (h + 1) * GLA_DK)
    vsl = lambda h: slice(h * GLA_DV, (h + 1) * GLA_DV)
    vh = {(ci, h): v_ref[ci * c:(ci + 1) * c, vsl(h)].astype(BF16) for ci, h in pairs}
    att = {p: _dot_nt(qe[p[0]][:, ksl(p[1])], ke[p[0]][:, ksl(p[1])]) for p in pairs}
    att = {p: jnp.where(causal, att[p], 0.0).astype(BF16) for p in pairs}
    o = {p: _dot(att[p], vh[p]) for p in pairs}
    kvt = {p: _dot_tn(vh[p], kd[p[0]][:, ksl(p[1])]) for p in pairs}
    st_in = {}
    for h in range(GLA_HEADS):
        st = st_ref[h]
        for ci in range(nc):
            st_in[(ci, h)] = st.astype(BF16)
            st = st * dec[ci][:, ksl(h)] + kvt[(ci, h)]
        st_ref[h] = st
    for ci, h in pairs:
        out = o[(ci, h)] + _dot_nt(qd[ci][:, ksl(h)], st_in[(ci, h)])
        grh = gr_ref[ci * c:(ci + 1) * c, vsl(h)]
        o_ref[ci * c:(ci + 1) * c, vsl(h)] = (_rms(out, onw) * _silu(grh)).astype(o_ref.dtype)


def _gla(proj, gate_up, gate_b, out_norm, *, ts):
    bsz, s, _ = proj.shape
    hk, hv = GLA_HEADS * GLA_DK, GLA_HEADS * GLA_DV
    return pl.pallas_call(
        functools.partial(_gla_kernel, ts=ts),
        grid=(bsz, s // ts),
        in_specs=[
            pl.BlockSpec((None, ts, hk), lambda b, i: (b, i, AB_GQ // hk)),
            pl.BlockSpec((None, ts, hk), lambda b, i: (b, i, AB_GK // hk)),
            pl.BlockSpec((None, ts, hv), lambda b, i: (b, i, AB_GV // hv)),
            pl.BlockSpec((None, ts, hv), lambda b, i: (b, i, AB_GR // hv)),
            pl.BlockSpec((None, ts, LANES), lambda b, i: (b, i, AB_SMALL // LANES)),
            pl.BlockSpec((GLA_GATE_RANK, hk), lambda b, i: (0, 0)),
            pl.BlockSpec((1, hk), lambda b, i: (0, 0)),
            pl.BlockSpec((1, GLA_DV), lambda b, i: (0, 0)),
        ],
        out_specs=pl.BlockSpec((None, ts, hv), lambda b, i: (b, i, 0)),
        out_shape=jax.ShapeDtypeStruct((bsz, s, hv), BF16),
        scratch_shapes=[pltpu.VMEM((GLA_HEADS, GLA_DV, GLA_DK), F32), pltpu.VMEM((ts, ts), BF16)],
        compiler_params=_cparams(("arbitrary", "arbitrary")),
        name="gla",
    )(proj, proj, proj, proj, proj, gate_up, gate_b.reshape(1, hk), out_norm.reshape(1, GLA_DV))


def _dsa_prep_kernel(dq_ref, dk_ref, dv_ref, iq_ref, sm_ref, pos_ref, qn_ref, kn_ref, freq_ref,
                     q_o, k_o, v_o, iq_o, ik_o, iw_o):
    pos = pos_ref[...].astype(F32)
    lane = lax.broadcasted_iota(I32, (1, LANES), 1)
    half_a, half_i = DSA_HEAD_DIM // 2, IDX_DIM // 2

    ang = pos * freq_ref[...]
    cos_t, sin_t = jnp.cos(ang), jnp.sin(ang)

    def attn_table(t):
        return jnp.where(lane < half_a, t, pltpu.roll(t, half_a, 1))

    def idx_table(t):
        return jnp.where(lane < half_i, pltpu.roll(t, half_a, 1),
                         jnp.where(lane < 2 * half_i, pltpu.roll(t, half_a + half_i, 1), t))

    cos_a = attn_table(cos_t)
    sin_a = jnp.where(lane < half_a, -1.0, 1.0) * attn_table(sin_t)

    def rope_attn(t):
        return t * cos_a + pltpu.roll(t, half_a, 1) * sin_a

    first = (lane % IDX_DIM) < half_i
    cos_i = idx_table(cos_t)
    sin_i = jnp.where(first, -1.0, 1.0) * idx_table(sin_t)

    def rope_idx(t):
        rot = jnp.where(first, pltpu.roll(t, LANES - IDX_DIM // 2, 1), pltpu.roll(t, IDX_DIM // 2, 1))
        return t * cos_i + rot * sin_i

    qn, kn = qn_ref[...], kn_ref[...]
    for h in range(DSA_HEADS):
        hs = slice(h * DSA_HEAD_DIM, (h + 1) * DSA_HEAD_DIM)
        q_o[:, hs] = (rope_attn(_rms(dq_ref[:, hs], qn)) * DSA_Q_SCALE).astype(q_o.dtype)
        k_o[:, hs] = rope_attn(_rms(dk_ref[:, hs], kn)).astype(k_o.dtype)
    tk = v_o.shape[-1]
    for j in range(v_o.shape[0]):
        v_o[j] = dv_ref[j * tk:(j + 1) * tk, :].T.astype(v_o.dtype)
    for j in range(IDX_HEADS * IDX_DIM // LANES):
        r = rope_idx(iq_ref[:, j * LANES:(j + 1) * LANES])
        iq_o[2 * j] = r[:, :IDX_DIM].astype(iq_o.dtype)
        iq_o[2 * j + 1] = r[:, IDX_DIM:].astype(iq_o.dtype)
    sm = sm_ref[...]
    ik_o[...] = rope_idx(sm)[:, SM_IK:SM_IK + IDX_DIM].astype(ik_o.dtype)
    iw_o[...] = sm.T[SM_IW:SM_IW + IDX_HEADS, :]


def _proj_ab_kernel(x_ref, nw_ref, sh_ref, sc_ref, w_ref, pos_ref, qn_ref, kn_ref, freq_ref,
                    gla_o, q_o, k_o, v_o, iq_o, ik_o, iw_o, dsa_ref):
    hb = (_rms(x_ref[...], nw_ref[...]) * (1.0 + sc_ref[...]) + sh_ref[...]).astype(BF16)
    hd = DSA_HEADS * DSA_HEAD_DIM
    gla_o[:, AB_SMALL:] = _dot(hb, w_ref[:, AB_SMALL:AB_GLA_N])
    for j in range(AB_DSA_N // hd):
        dsa_ref[:, j * hd:(j + 1) * hd] = _dot(hb, w_ref[:, AB_GLA_N + j * hd:AB_GLA_N + (j + 1) * hd])
    part = lambda j: dsa_ref.at[:, j * hd:(j + 1) * hd]
    _dsa_prep_kernel(part(0), part(1), part(2), part(3), gla_o.at[:, AB_SMALL:AB_SMALL + LANES],
                     pos_ref, qn_ref, kn_ref, freq_ref, q_o, k_o, v_o, iq_o, ik_o, iw_o)
    gla_o[:, :AB_SMALL] = _dot(hb, w_ref[:, :AB_SMALL])


def _proj_ab(x, nw, shift, scale, w_bf16, positions, q_norm, k_norm, *, tm, tk):
    bsz, s, d = x.shape
    hd = DSA_HEADS * DSA_HEAD_DIM
    half_a, half_i = DSA_HEAD_DIM // 2, IDX_DIM // 2
    inv_a = ROPE_THETA ** (-jnp.arange(half_a, dtype=F32) / half_a)
    inv_i = ROPE_THETA ** (-jnp.arange(half_i, dtype=F32) / half_i)
    freq = jnp.concatenate([inv_a, inv_i, inv_i]).reshape(1, LANES)
    rows = lambda n: pl.BlockSpec((None, tm, n), lambda b, i: (b, i, 0))
    vec = lambda n: pl.BlockSpec((1, n), lambda b, i: (0, 0))
    mod = pl.BlockSpec((None, 1, d), lambda b, i: (b, 0, 0))
    return pl.pallas_call(
        _proj_ab_kernel,
        grid=(bsz, s // tm),
        in_specs=[
            rows(d), vec(d), mod, mod,
            _resident((d, AB_GLA_N + AB_DSA_N), lambda b, i: (0, 0)),
            rows(1), vec(DSA_HEAD_DIM), vec(DSA_HEAD_DIM), vec(LANES),
        ],
        out_specs=[
            rows(AB_GLA_N), rows(hd), rows(hd),
            pl.BlockSpec((None, tm // tk, hd, tk), lambda b, i: (b, i, 0, 0)),
            pl.BlockSpec((None, IDX_HEADS, tm, IDX_DIM), lambda b, i: (b, 0, i, 0)),
            rows(IDX_DIM),
            pl.BlockSpec((None, IDX_HEADS, tm), lambda b, i: (b, 0, i)),
        ],
        out_shape=[
            jax.ShapeDtypeStruct((bsz, s, AB_GLA_N), F32),
            jax.ShapeDtypeStruct((bsz, s, hd), BF16),
            jax.ShapeDtypeStruct((bsz, s, hd), BF16),
            jax.ShapeDtypeStruct((bsz, s // tk, hd, tk), BF16),
            jax.ShapeDtypeStruct((bsz, IDX_HEADS, s, IDX_DIM), BF16),
            jax.ShapeDtypeStruct((bsz, s, IDX_DIM), BF16),
            jax.ShapeDtypeStruct((bsz, IDX_HEADS, s), F32),
        ],
        scratch_shapes=[pltpu.VMEM((tm, AB_DSA_N), F32)],
        compiler_params=_cparams(("arbitrary", "arbitrary")),
        name="proj_ab",
    )(x, nw.reshape(1, d), shift, scale, w_bf16, positions.reshape(bsz, s, 1),
      q_norm.reshape(1, -1), k_norm.reshape(1, -1), freq)


def _dsa_kernel(q_ref, iq_ref, iw_ref, k_ref, vt_ref, ik_ref, o_ref, sc_ref, top_ref, acc_ref,
                *, tq, n_sel, s_len):
    i = pl.program_id(1)
    nkb = i + 1
    kofs = lax.broadcasted_iota(I32, (tq, tq), 0)
    qidx = i * tq + lax.broadcasted_iota(I32, (tq, tq), 1)
    iw = iw_ref[...]
    idx_scale = (IDX_HEADS ** -0.5) * (IDX_DIM ** -0.5)

    def over_tiles(step, init, n=nkb, widest=2):
        carry = lax.fori_loop(0, n // widest, lambda j, c: step(widest * j, widest, c), init)
        rest = n % widest
        start = n - rest
        width = widest // 2
        while width >= 1:
            carry = lax.cond((rest & width) != 0, functools.partial(step, start, width), lambda c: c, carry)
            start = start + (rest & width)
            width //= 2
        return carry

    def key_index(kb, rows):
        return kb * tq + lax.broadcasted_iota(I32, (rows, tq), 0)

    def as_float(key):
        key = jnp.maximum(key, KEY_LOWEST)
        key = jnp.where((key > 0) & (key < KEY_TINY), KEY_TINY, key)
        return pltpu.bitcast(key ^ ((key >> 31) & 0x7FFFFFFF), F32)

    def scores(kb, width, diagonal):
        k0 = pl.multiple_of(kb * tq, tq)
        rows = width * tq
        ikb = ik_ref[pl.ds(k0, rows), :]
        sc = jnp.zeros((rows, tq), F32)
        for h in range(IDX_HEADS):
            sc = sc + jnp.maximum(_dot_nt(ikb, iq_ref[h]), 0.0) * iw[h:h + 1, :]
        sc = sc * idx_scale
        sc = jnp.where(jnp.abs(sc) < FLT_TINY, 0.0, sc)
        if diagonal:
            sc = jnp.where(kofs <= qidx - i * tq, sc, -jnp.inf)
        sc_ref[pl.ds(kb, width)] = sc.reshape(width, tq, tq)
        top_ref[pl.ds(kb, width)] = sc.astype(BF16).reshape(width, tq, tq)

    def below_diagonal(kb, width, carry):
        scores(kb, width, False)
        return carry

    over_tiles(below_diagonal, 0, n=i)
    scores(i, 1, True)

    def count(pred):
        def body(kb, width, acc):
            rows = width * tq
            m = jnp.where(pred(sc_ref[pl.ds(kb, width)].reshape(rows, tq), lambda: key_index(kb, rows)), 1, 0)
            return acc + jnp.sum(m.reshape(rows // 8, 8, tq), axis=0)
        return jnp.sum(over_tiles(body, jnp.zeros((8, tq), I32)), axis=0, keepdims=True)

    n_valid = qidx[0:1, :] + 1

    def count_top(cand):
        k16 = jnp.maximum(cand, KEY_LOWEST_BF16) >> 16
        pattern = k16 ^ ((k16 >> 15) & 0x7FFF)
        pattern = jnp.where((pattern > 0) & (pattern < 0x80), 0x80, pattern)
        cand_bf = pltpu.bitcast(pattern << 16, F32).astype(BF16)

        def body(kb, width, acc):
            rows = width * tq
            top = top_ref[pl.ds(kb, width)].reshape(rows, tq)
            m = jnp.where(top >= cand_bf, jnp.ones((), BF16), jnp.zeros((), BF16))
            part = m[0:16]
            for j in range(1, rows // 16):
                part = part + m[16 * j:16 * (j + 1)]
            return acc + part.astype(F32)
        acc = over_tiles(body, jnp.zeros((16, tq), F32))
        return jnp.sum(acc, axis=0, keepdims=True).astype(I32)

    def top_step(it, st):
        u, cnt_u = st
        cand_u = u | lax.shift_left(jnp.int32(1), jnp.int32(31) - it)
        cnt = count_top(cand_u ^ INT_MIN)
        take = cnt >= n_sel
        return jnp.where(take, cand_u, u), jnp.where(take, cnt, cnt_u)

    u, _ = lax.fori_loop(0, 16, top_step, (jnp.zeros((1, tq), I32), n_valid))

    def any_lane(flag):
        return jnp.max(jnp.where(flag, 1.0, 0.0)) > 0.0

    base = u ^ INT_MIN
    base = jnp.where(base < 0, base | 0xFFFF, base)
    base = jnp.clip(base, KEY_LOWEST + BF16_HALF_STEP + 1, INT_MAX - 2 * BF16_HALF_STEP)
    many = n_valid > n_sel
    lo = jnp.where(many, base - BF16_HALF_STEP - 1, KEY_LOWEST)
    hi = jnp.where(many, base + 2 * BF16_HALF_STEP, INT_MAX)
    def count_two(pred_a, pred_b):
        def body(kb, width, acc):
            x = sc_ref[pl.ds(kb, width)].reshape(width * tq // 8, 8, tq)
            return (acc[0] + jnp.sum(jnp.where(pred_a(x), 1, 0), axis=0),
                    acc[1] + jnp.sum(jnp.where(pred_b(x), 1, 0), axis=0))
        a, b = over_tiles(body, (jnp.zeros((8, tq), I32), jnp.zeros((8, tq), I32)))
        return jnp.sum(a, axis=0, keepdims=True), jnp.sum(b, axis=0, keepdims=True)

    lo_f, hi_f = as_float(lo), as_float(hi)
    cnt_lo, cnt_hi = count_two(lambda x: x >= lo_f, lambda x: x >= hi_f)

    def open_queries(lo, cnt_lo, hi, done):
        return (cnt_lo > n_sel) & (hi - lo > 1) & (done == 0)

    def halve(s):
        lo, cnt_lo, hi, cnt_hi, done = s
        mid = lo + ((hi - lo) >> 1)
        mid_f = as_float(mid)
        cnt = count(lambda x, kidx: x >= mid_f)
        live = open_queries(lo, cnt_lo, hi, done)
        up = live & (cnt >= n_sel)
        down = live & (cnt < n_sel)
        return (jnp.where(up, mid, lo), jnp.where(up, cnt, cnt_lo),
                jnp.where(down, mid, hi), jnp.where(down, cnt, cnt_hi), done)

    st = lax.fori_loop(0, PEEL_AFTER_STEPS, lambda _, s: halve(s),
                       (lo, cnt_lo, hi, cnt_hi, jnp.zeros((1, tq), I32)))

    def peel(st):
        lo, cnt_lo, hi, cnt_hi, done = st
        lo_f, hi_f = as_float(lo), as_float(hi)

        def extremes(kb, width, carry):
            top, bot = carry
            x = sc_ref[pl.ds(kb, width)].reshape(width * tq // 8, 8, tq)
            top = jnp.maximum(top, jnp.max(jnp.where(x < hi_f, x, -jnp.inf), axis=0))
            bot = jnp.minimum(bot, jnp.min(jnp.where(x >= lo_f, x, jnp.inf), axis=0))
            return top, bot
        top, bot = over_tiles(extremes, (jnp.full((8, tq), -jnp.inf, F32), jnp.full((8, tq), jnp.inf, F32)))
        top = jnp.max(top, axis=0, keepdims=True)
        bot = jnp.min(bot, axis=0, keepdims=True)
        from_top = (cnt_hi == n_sel - 1) | (top == bot)
        from_bot = (cnt_lo == n_sel + 1) & jnp.logical_not(from_top)
        cnt_top, cnt_bot = count_two(lambda x: x >= top, lambda x: x > bot)
        cnt = jnp.where(from_top, cnt_top, cnt_bot)
        short = from_bot & (cnt < n_sel)
        hit = open_queries(lo, cnt_lo, hi, done) & (from_top | from_bot)
        thr = jnp.where(from_top, top, bot)
        incl = jnp.where(from_top | short, 1, 0)
        return hit, thr, incl, jnp.where(short, cnt_lo, cnt)

    def with_peel(st):
        hit, thr, incl, cnt = peel(st)
        st = st[:4] + (jnp.where(hit, 1, 0),)
        lo, cnt_lo, _, _, _ = lax.while_loop(lambda s: any_lane(open_queries(s[0], s[1], s[2], s[4])), halve, st)
        return jnp.where(hit, thr, as_float(lo)), jnp.where(hit, incl, 1), jnp.where(hit, cnt, cnt_lo)

    def without_peel(st):
        lo, cnt_lo, _, _, _ = st
        return as_float(lo), jnp.ones((1, tq), I32), cnt_lo

    thr, incl, cnt = lax.cond(any_lane(open_queries(st[0], st[1], st[2], st[4])), with_peel, without_peel, st)
    admits_equal = incl != 0
    excess = cnt > n_sel

    def tie_limit():
        need = n_sel - count(lambda x, kidx: x > thr)
        nbits = s_len.bit_length()

        def step(it, jv):
            cand = jv | lax.shift_left(jnp.int32(1), jnp.int32(nbits - 1) - it)
            cnt = count(lambda x, kidx: (x == thr) & (kidx() < cand))
            return jnp.where(cnt < need, cand, jv)
        jv = lax.fori_loop(0, nbits, step, jnp.zeros((1, tq), I32))
        return jnp.where(excess, jv, jnp.where(admits_equal, s_len, -1))

    tied = jnp.max(jnp.where(excess, 1.0, 0.0)) > 0.0

    acc_ref[...] = jnp.zeros_like(acc_ref)

    def attend(select):
        def block(kb, width, carry):
            ms, ls = carry
            k0 = pl.multiple_of(kb * tq, tq)
            rows = width * tq
            sel = select(sc_ref[pl.ds(kb, width)].reshape(rows, tq), kb, rows)
            heads = [slice(h * DSA_HEAD_DIM, (h + 1) * DSA_HEAD_DIM) for h in range(DSA_HEADS)]
            st = [_dot_nt(k_ref[pl.ds(k0, rows), hs], q_ref[:, hs]) for hs in heads]
            st = [jnp.where(sel, x, MASK_BIAS) for x in st]
            new_m = [jnp.maximum(m, jnp.max(x, axis=0, keepdims=True)) for m, x in zip(ms, st)]
            alpha = [jnp.exp2(m - mn) for m, mn in zip(ms, new_m)]
            p = [jnp.exp2(x - mn) for x, mn in zip(st, new_m)]
            new_l = [a * l + jnp.sum(x, axis=0, keepdims=True) for a, l, x in zip(alpha, ls, p)]
            p = [x.astype(BF16) for x in p]
            for h, hs in enumerate(heads):
                pv = _dot(vt_ref[kb, hs, :], p[h][0:tq])
                for j in range(1, width):
                    pv = pv + _dot(vt_ref[kb + j, hs, :], p[h][j * tq:(j + 1) * tq])
                acc_ref[h] = alpha[h] * acc_ref[h] + pv
            return tuple(new_m), tuple(new_l)

        carry = (tuple(jnp.full((1, tq), -jnp.inf, F32) for _ in range(DSA_HEADS)),
                 tuple(jnp.zeros((1, tq), F32) for _ in range(DSA_HEADS)))
        return over_tiles(block, carry)[1]

    def attend_tied():
        jstar = tie_limit()
        return attend(lambda x, kb, rows: (x > thr) | ((x == thr) & (key_index(kb, rows) <= jstar)))

    at_least = jnp.where(admits_equal, thr, jnp.inf)
    above = jnp.where(admits_equal, jnp.inf, thr)
    ls = lax.cond(tied, attend_tied, lambda: attend(lambda x, kb, rows: (x >= at_least) | (x > above)))
    for h in range(DSA_HEADS):
        hs = slice(h * DSA_HEAD_DIM, (h + 1) * DSA_HEAD_DIM)
        o_ref[:, hs] = (acc_ref[h] / ls[h]).T.astype(o_ref.dtype)


def _dsa(q, k, vt, iq, ik, iw, *, tq):
    bsz, s, hd = q.shape
    n_sel = min(DSA_TOPK, s // 4)
    resident = lambda n: _resident((None, s, n), lambda b, i: (b, 0, 0))
    return pl.pallas_call(
        functools.partial(_dsa_kernel, tq=tq, n_sel=n_sel, s_len=s),
        grid=(bsz, s // tq),
        in_specs=[
            pl.BlockSpec((None, tq, hd), lambda b, i: (b, i, 0)),
            pl.BlockSpec((None, IDX_HEADS, tq, IDX_DIM), lambda b, i: (b, 0, i, 0)),
            pl.BlockSpec((None, IDX_HEADS, tq), lambda b, i: (b, 0, i)),
            resident(hd),
            _resident((None, s // tq, hd, tq), lambda b, i: (b, 0, 0, 0)),
            resident(IDX_DIM),
        ],
        out_specs=pl.BlockSpec((None, tq, hd), lambda b, i: (b, i, 0)),
        out_shape=jax.ShapeDtypeStruct((bsz, s, hd), BF16),
        scratch_shapes=[
            pltpu.VMEM((s // tq, tq, tq), F32),
            pltpu.VMEM((s // tq, tq, tq), BF16),
            pltpu.VMEM((DSA_HEADS, DSA_HEAD_DIM, tq), F32),
        ],
        compiler_params=_cparams(("arbitrary", "arbitrary"), DSA_VMEM_LIMIT),
        name="dsa",
    )(q, iq, iw, k, vt, ik)


def _mix_ffn_kernel(x_ref, a_ref, b_ref, wa_ref, wb_ref, g1_ref, nw_ref, sh_ref, sc_ref, g2_ref,
                    w1_ref, w2_ref, o_ref, *, tf):
    x = x_ref[...] + g1_ref[...] * (_dot(a_ref[...], wa_ref[...]) + _dot(b_ref[...], wb_ref[...]))
    hb = (_rms(x, nw_ref[...]) * (1.0 + sc_ref[...]) + sh_ref[...]).astype(BF16)
    f = w2_ref.shape[0]
    acc = jnp.zeros(x.shape, F32)
    for j in range(f // tf):
        gate = _dot(hb, w1_ref[:, j * tf:(j + 1) * tf])
        up = _dot(hb, w1_ref[:, f + j * tf:f + (j + 1) * tf])
        acc = acc + _dot((_silu(gate) * up).astype(BF16), w2_ref[j * tf:(j + 1) * tf, :])
    o_ref[...] = x + g2_ref[...] * acc


def _mix_ffn(x, mix_a, mix_b, w_out_bf16, gate1, nw, shift, scale, gate2, w1_bf16, w2_bf16, *, tm, tf):
    bsz, s, d = x.shape
    f = w2_bf16.shape[0]
    na = mix_a.shape[-1]
    assert mix_b.shape[-1] == na and w_out_bf16.shape[0] == 2 * na
    rows = lambda n: pl.BlockSpec((None, tm, n), lambda b, i: (b, i, 0))
    mod = pl.BlockSpec((None, 1, d), lambda b, i: (b, 0, 0))
    return pl.pallas_call(
        functools.partial(_mix_ffn_kernel, tf=tf),
        grid=(bsz, s // tm),
        in_specs=[
            rows(d), rows(na), rows(na),
            _resident((na, d), lambda b, i: (0, 0)),
            _resident((na, d), lambda b, i: (1, 0)),
            mod,
            pl.BlockSpec((1, d), lambda b, i: (0, 0)),
            mod, mod, mod,
            _resident((d, 2 * f), lambda b, i: (0, 0)),
            _resident((f, d), lambda b, i: (0, 0)),
        ],
        out_specs=rows(d),
        out_shape=jax.ShapeDtypeStruct((bsz, s, d), F32),
        compiler_params=_cparams(("arbitrary", "arbitrary")),
        name="mix_ffn",
    )(x, mix_a, mix_b, w_out_bf16, w_out_bf16, gate1, nw.reshape(1, d), shift, scale, gate2,
      w1_bf16, w2_bf16)


HALO = max(POOL_WINDOWS)


def _proj_cd_kernel(x_ref, nw_ref, sh_ref, sc_ref, w_ref, pw_ref, ps_ref, qn_ref, kn_ref,
                    pool_o, q_o, k_o, v_o, ext_ref, sb_ref):
    i = pl.program_id(1)
    tm = x_ref.shape[0]
    n = POOL_GROUPS * POOL_CH
    hb = (_rms(x_ref[...], nw_ref[...]) * (1.0 + sc_ref[...]) + sh_ref[...]).astype(BF16)

    @pl.when(i == 0)
    def _():
        ext_ref[0:HALO, :] = jnp.zeros((HALO, n), F32)

    ext_ref[HALO:HALO + tm, :] = _dot(hb, w_ref[:, :n])
    for j in range(3):
        sb_ref[:, j * n:(j + 1) * n] = _dot(hb, w_ref[:, (j + 1) * n:(j + 2) * n])

    t = i * tm + lax.broadcasted_iota(I32, (tm, 1), 0)
    for g, w in enumerate(POOL_WINDOWS):
        gs = slice(g * POOL_CH, (g + 1) * POOL_CH)
        u = ext_ref[HALO:HALO + tm, gs]
        tot = u
        for j in range(1, w):
            tot = tot + ext_ref[HALO - j:HALO - j + tm, gs]
        cnt = jnp.minimum(t + 1, w).astype(F32)
        pooled = tot / cnt - u
        y = _dot(pooled.astype(BF16), pw_ref[g].astype(BF16)) * ps_ref[:, gs]
        pool_o[:, gs] = y.astype(pool_o.dtype)
    ext_ref[0:HALO, :] = ext_ref[tm:tm + HALO, :]

    qn, kn = qn_ref[...], kn_ref[...]
    for h in range(SB_HEADS):
        hs = slice(h * SB_HEAD_DIM, (h + 1) * SB_HEAD_DIM)
        q_o[:, hs] = (_rms(sb_ref[:, hs], qn) * SB_HEAD_DIM ** -0.5).astype(q_o.dtype)
        k_o[:, hs] = _rms(sb_ref[:, n + h * SB_HEAD_DIM:n + (h + 1) * SB_HEAD_DIM], kn).astype(k_o.dtype)
    tk = v_o.shape[-1]
    for j in range(v_o.shape[0]):
        v_o[j] = sb_ref[j * tk:(j + 1) * tk, 2 * n:3 * n].T.astype(v_o.dtype)


def _proj_cd(x, nw, shift, scale, w_bf16, pool_w, pool_scale, q_norm, k_norm, *, tm, tk):
    bsz, s, d = x.shape
    n = POOL_GROUPS * POOL_CH
    assert w_bf16.shape[1] == 4 * n
    rows = lambda m: pl.BlockSpec((None, tm, m), lambda b, i: (b, i, 0))
    vec = lambda m: pl.BlockSpec((1, m), lambda b, i: (0, 0))
    mod = pl.BlockSpec((None, 1, d), lambda b, i: (b, 0, 0))
    return pl.pallas_call(
        _proj_cd_kernel,
        grid=(bsz, s // tm),
        in_specs=[
            rows(d), vec(d), mod, mod,
            _resident((d, 4 * n), lambda b, i: (0, 0)),
            pl.BlockSpec((POOL_GROUPS, POOL_CH, POOL_CH), lambda b, i: (0, 0, 0)),
            vec(n), vec(SB_HEAD_DIM), vec(SB_HEAD_DIM),
        ],
        out_specs=[rows(n)] * 3 + [pl.BlockSpec((None, tm // tk, n, tk), lambda b, i: (b, i, 0, 0))],
        out_shape=[jax.ShapeDtypeStruct((bsz, s, n), BF16)] * 3
        + [jax.ShapeDtypeStruct((bsz, s // tk, n, tk), BF16)],
        scratch_shapes=[pltpu.VMEM((HALO + tm, n), F32), pltpu.VMEM((tm, 3 * n), F32)],
        compiler_params=_cparams(("arbitrary", "arbitrary")),
        name="proj_cd",
    )(x, nw.reshape(1, d), shift, scale, w_bf16, pool_w, pool_scale.reshape(1, n),
      q_norm.reshape(1, -1), k_norm.reshape(1, -1))


def _sb_kernel(q_ref, k_ref, vt_ref, o_ref, acc_ref, *, tq):
    i = pl.program_id(1)
    kofs = lax.broadcasted_iota(I32, (tq, tq), 0)
    qofs = lax.broadcasted_iota(I32, (tq, tq), 1)
    later = jnp.where(qofs > kofs, 1.0, 0.0).astype(BF16)
    heads = [slice(h * SB_HEAD_DIM, (h + 1) * SB_HEAD_DIM) for h in range(SB_HEADS)]
    acc_ref[...] = jnp.zeros_like(acc_ref)

    def block(kb, runs, diagonal):
        k0 = pl.multiple_of(kb * tq, tq)
        z = [_dot_nt(k_ref[pl.ds(k0, tq), hs], q_ref[:, hs]) for hs in heads]
        sp = [jnp.log(1.0 + jnp.exp(-jnp.abs(x))) for x in z]
        log_beta = [jnp.minimum(x, 0.0) - s for x, s in zip(z, sp)]
        log_1m = [-jnp.maximum(x, 0.0) - s for x, s in zip(z, sp)]
        if diagonal:
            strict = kofs < qofs
            log_1m = [jnp.where(strict, x, 0.0) for x in log_1m]
        hi = [x.astype(BF16) for x in log_1m]
        lo = [(x - h.astype(F32)).astype(BF16) for x, h in zip(log_1m, hi)]
        after = [_dot(later, h) + _dot(later, l) + r for h, l, r in zip(hi, lo, runs)]
        w = [jnp.exp(lb + a) for lb, a in zip(log_beta, after)]
        if diagonal:
            w = [jnp.where(strict, x, 0.0) for x in w]
        for h, hs in enumerate(heads):
            acc_ref[h] = acc_ref[h] + _dot(vt_ref[kb, hs, :], w[h].astype(BF16))
        return tuple(r + jnp.sum(x, axis=0, keepdims=True) for r, x in zip(runs, log_1m))

    def alive(runs):
        worst = functools.reduce(jnp.maximum, runs)
        return jnp.max(worst) > SB_DEAD_LOG

    runs = block(i, tuple(jnp.zeros((1, tq), F32) for _ in heads), True)

    def earlier(state):
        kb, runs, _ = state
        runs = block(kb, runs, False)
        return kb - 1, runs, alive(runs)

    lax.while_loop(lambda st: (st[0] >= 0) & st[2], earlier, (i - 1, runs, alive(runs)))
    for h, hs in enumerate(heads):
        o_ref[:, hs] = acc_ref[h].T.astype(o_ref.dtype)


def _sb(q, k, vt, *, tq):
    bsz, s, hd = q.shape
    return pl.pallas_call(
        functools.partial(_sb_kernel, tq=tq),
        grid=(bsz, s // tq),
        in_specs=[
            pl.BlockSpec((None, tq, hd), lambda b, i: (b, i, 0)),
            _resident((None, s, hd), lambda b, i: (b, 0, 0)),
            _resident((None, s // tq, hd, tq), lambda b, i: (b, 0, 0, 0)),
        ],
        out_specs=pl.BlockSpec((None, tq, hd), lambda b, i: (b, i, 0)),
        out_shape=jax.ShapeDtypeStruct((bsz, s, hd), BF16),
        scratch_shapes=[pltpu.VMEM((SB_HEADS, SB_HEAD_DIM, tq), F32)],
        compiler_params=_cparams(("arbitrary", "arbitrary")),
        name="stick_breaking",
    )(q, k, vt)


def _pack_ab_weight(w):
    d = w.shape[0]
    gq, gk, gv, glow, gr, dq, dk, dv, iq, ik, iw = jnp.split(
        w, [256, 512, 1024, 1040, 1552, 2064, 2576, 3088, 3600, 3664], axis=1)
    pad = jnp.zeros((d, LANES - (IDX_DIM + GLA_GATE_RANK + IDX_HEADS)), w.dtype)
    return jnp.concatenate([gq, gk, gv, gr, ik, glow, iw, pad, dq, dk, dv, iq], axis=1)


def kernel(x, c, positions, ada_w, ada_b, mix_norm, ffn_norm, ffn_w1, ffn_w2, ab_w_in, gla_gate_up,
           gla_gate_b, gla_out_norm, dsa_q_norm, dsa_k_norm, ab_w_out, cd_w_in, pool_w, pool_scale,
           sb_q_norm, sb_k_norm, cd_w_out):
    bsz, s, d = x.shape
    depth = ada_w.shape[0]
    mod = _ada_mod(c, ada_w, ada_b).reshape(depth, bsz, 6, 1, d)
    tm = min(512, s)
    tq_dsa = min(512, s)
    tq_sb = min(256, s)
    for layer in range(depth):
        sh1, sc1, g1, sh2, sc2, g2 = (mod[layer, :, j] for j in range(6))
        i = layer // 2
        if layer % 2 == 0:
            proj, q, k, vt, iq, ik, iw = _proj_ab(
                x, mix_norm[layer], sh1, sc1, _pack_ab_weight(ab_w_in[i]).astype(BF16), positions,
                dsa_q_norm[i], dsa_k_norm[i], tm=tq_dsa, tk=tq_dsa)
            mix_a = _gla(proj, gla_gate_up[i], gla_gate_b[i], gla_out_norm[i], ts=tm)
            mix_b = _dsa(q, k, vt, iq, ik, iw, tq=tq_dsa)
            w_out = ab_w_out[i]
        else:
            mix_a, q, k, vt = _proj_cd(x, mix_norm[layer], sh1, sc1, cd_w_in[i].astype(BF16), pool_w[i],
                                       pool_scale[i], sb_q_norm[i], sb_k_norm[i], tm=tm, tk=tq_sb)
            mix_b = _sb(q, k, vt, tq=tq_sb)
            w_out = cd_w_out[i]
        x = _mix_ffn(x, mix_a, mix_b, w_out.astype(BF16), g1, ffn_norm[layer], sh2, sc2, g2,
                     ffn_w1[layer].astype(BF16), ffn_w2[layer].astype(BF16), tm=tm, tf=256)
    return x
```

```python
import functools

import jax
import jax.numpy as jnp
from jax import lax
from jax.experimental import pallas as pl
from jax.experimental.pallas import tpu as pltpu

F32 = jnp.float32
BF16 = jnp.bfloat16
I32 = jnp.int32

D_MODEL = 1024
GLA_HEADS, GLA_DK, GLA_DV = 4, 64, 128
GLA_GATE_RANK = 16
GLA_GATE_TAU = 16.0
GLA_CHUNK = 64
DSA_HEADS, DSA_HEAD_DIM = 4, 128
IDX_HEADS, IDX_DIM = 8, 64
DSA_TOPK = 256
POOL_WINDOWS = (2, 4, 8, 16)
POOL_GROUPS, POOL_CH = 4, 128
SB_HEADS, SB_HEAD_DIM = 4, 128
ROPE_THETA = 10000.0
NORM_EPS = 1e-6
D_FF = 2816

LANES = 128
VMEM_LIMIT = 56 * 1024 * 1024
DSA_VMEM_LIMIT = 61 * 1024 * 1024

AB_GQ, AB_GK, AB_GV, AB_GR = 0, 256, 512, 1024
AB_SMALL = 1536
SM_IK, SM_GLOW, SM_IW = 0, 64, 80
AB_GLA_N = AB_SMALL + LANES
AB_DSA_N = 4 * 512

DSA_Q_SCALE = 1.4426950408889634 * DSA_HEAD_DIM ** -0.5
INT_MIN = -2 ** 31
INT_MAX = 2 ** 31 - 1
FLT_TINY = 2.0 ** -126
KEY_TINY = 0x00800000
KEY_LOWEST = INT_MIN + 0x00800000
KEY_LOWEST_BF16 = INT_MIN + 0x00810000
BF16_HALF_STEP = 0x8000
PEEL_AFTER_STEPS = 5
MASK_BIAS = -1e30
SB_DEAD_LOG = -110.0


def _dot(a, b):
    return jnp.dot(a, b, preferred_element_type=F32)


def _dot_nt(a, b):
    return lax.dot_general(a, b, (((1,), (1,)), ((), ())), preferred_element_type=F32)


def _dot_tn(a, b):
    return lax.dot_general(a, b, (((0,), (0,)), ((), ())), preferred_element_type=F32)


def _split3(a):
    hi = a.astype(BF16)
    r1 = a - hi.astype(F32)
    mid = r1.astype(BF16)
    lo = (r1 - mid.astype(F32)).astype(BF16)
    return hi, mid, lo


def _silu(x):
    return x * jax.nn.sigmoid(x)


def _rms(x, w):
    var = jnp.mean(x * x, axis=-1, keepdims=True)
    return x * lax.rsqrt(var + NORM_EPS) * w


def _cparams(sem, vmem_limit=VMEM_LIMIT):
    return pltpu.CompilerParams(dimension_semantics=sem, vmem_limit_bytes=vmem_limit)


def _resident(block_shape, index_map):
    return pl.BlockSpec(block_shape, index_map, pipeline_mode=pl.Buffered(1))


def _ada_kernel(c_ref, w_ref, b_ref, o_ref):
    cond = _silu(c_ref[...])
    c_hi, c_mid, _ = _split3(cond)
    w_hi, w_mid, _ = _split3(w_ref[...])
    acc = _dot(c_hi, w_hi) + _dot(c_hi, w_mid) + _dot(c_mid, w_hi)
    o_ref[...] = acc + b_ref[...]


def _ada_mod(c, ada_w, ada_b):
    depth, d, n = ada_w.shape
    bsz = c.shape[0]
    rows = 16
    cp =jnp.zeros((rows, d), F32).at[:bsz].set(c)
    tn = 1536
    out = pl.pallas_call(
        _ada_kernel,
        grid=(depth, n // tn),
        in_specs=[
            pl.BlockSpec((rows, d), lambda l, j: (0, 0)),
            pl.BlockSpec((None, d, tn), lambda l, j: (l, 0, j)),
            pl.BlockSpec((None, 1, tn), lambda l, j: (l, 0, j)),
        ],
        out_specs=pl.BlockSpec((None, rows, tn), lambda l, j: (l, 0, j)),
        out_shape=jax.ShapeDtypeStruct((depth, rows, n), F32),
        compiler_params=_cparams(("arbitrary", "arbitrary")),
        name="ada_mod",
    )(cp, ada_w, ada_b.reshape(depth, 1, n))
    return out[:, :bsz]


def _gla_kernel(q_ref, k_ref, v_ref, gr_ref, sm_ref, gup_ref, gb_ref, onw_ref, o_ref, st_ref, tri_ref, *, ts):
    c = GLA_CHUNK

    @pl.when(pl.program_id(1) == 0)
    def _():
        st_ref[...] = jnp.zeros_like(st_ref)
        row = lax.broadcasted_iota(I32, (ts, ts), 0)
        col = lax.broadcasted_iota(I32, (ts, ts), 1)
        tri_ref[...] = jnp.where((row // c == col // c) & (col <= row), 1.0, 0.0).astype(BF16)

    nc = ts // c
    tri = tri_ref[...]
    causal = lax.broadcasted_iota(I32, (c, c), 1) <= lax.broadcasted_iota(I32, (c, c), 0)
    onw = onw_ref[...]

    glow = sm_ref[:, SM_GLOW:SM_GLOW + GLA_GATE_RANK]
    a = _dot(glow.astype(BF16), gup_ref[...].astype(BF16)) + gb_ref[...]
    g = (jnp.minimum(a, 0.0) - jnp.log1p(jnp.exp(-jnp.abs(a)))) / GLA_GATE_TAU
    g_hi, g_mid, g_lo = _split3(g)
    b = _dot(tri, g_hi) + _dot(tri, g_mid) + _dot(tri, g_lo)
    qs = q_ref[...] * (GLA_DK ** -0.5)
    k = k_ref[...]

    qe, ke, kd, qd, dec = [], [], [], [], []
    for ci in range(nc):
        r = slice(ci * c, (ci + 1) * c)
        bc = b[r]
        b_mid = bc[c // 2 - 1:c // 2, :]
        b_last = bc[c - 1:c, :]
        qe.append((qs[r] * jnp.exp(bc - b_mid)).astype(BF16))
        ke.append((k[r] * jnp.exp(b_mid - bc)).astype(BF16))
        kd.append((k[r] * jnp.exp(b_last - bc)).astype(BF16))
        qd.append((qs[r] * jnp.exp(bc)).astype(BF16))
        dec.append(jnp.exp(b_last))

    pairs = [(ci, h) for ci in range(nc) for h in range(GLA_HEADS)]
    ksl = lambda h: slice(h * GLA_DK, (h + 1) * GLA_DK)
    vsl = lambda h: slice(h * GLA_DV, (h + 1) * GLA_DV)
    vh = {(ci, h): v_ref[ci * c:(ci + 1) * c, vsl(h)].astype(BF16) for ci, h in pairs}
    att = {p: _dot_nt(qe[p[0]][:, ksl(p[1])], ke[p[0]][:, ksl(p[1])]) for p in pairs}
    att = {p: jnp.where(causal, att[p], 0.0).astype(BF16) for p in pairs}
    o = {p: _dot(att[p], vh[p]) for p in pairs}
    kvt = {p: _dot_tn(vh[p], kd[p[0]][:, ksl(p[1])]) for p in pairs}
    st_in = {}
    for h in range(GLA_HEADS):
        st = st_ref[h]
        for ci in range(nc):
            st_in[(ci, h)] = st.astype(BF16)
            st = st * dec[ci][:, ksl(h)] + kvt[(ci, h)]
        st_ref[h] = st
    for ci, h in pairs:
        out = o[(ci, h)] + _dot_nt(qd[ci][:, ksl(h)], st_in[(ci, h)])
        grh = gr_ref[ci * c:(ci + 1) * c, vsl(h)]
        o_ref[ci * c:(ci + 1) * c, vsl(h)] = (_rms(out, onw) * _silu(grh)).astype(o_ref.dtype)


def _gla(proj, gate_up, gate_b, out_norm, *, ts):
    bsz, s, _ = proj.shape
    hk, hv = GLA_HEADS * GLA_DK, GLA_HEADS * GLA_DV
    return pl.pallas_call(
        functools.partial(_gla_kernel, ts=ts),
        grid=(bsz, s // ts),
        in_specs=[
            pl.BlockSpec((None, ts, hk), lambda b, i: (b, i, AB_GQ // hk)),
            pl.BlockSpec((None, ts, hk), lambda b, i: (b, i, AB_GK // hk)),
            pl.BlockSpec((None, ts, hv), lambda b, i: (b, i, AB_GV // hv)),
            pl.BlockSpec((None, ts, hv), lambda b, i: (b, i, AB_GR // hv)),
            pl.BlockSpec((None, ts, LANES), lambda b, i: (b, i, AB_SMALL // LANES)),
            pl.BlockSpec((GLA_GATE_RANK, hk), lambda b, i: (0, 0)),
            pl.BlockSpec((1, hk), lambda b, i: (0, 0)),
            pl.BlockSpec((1, GLA_DV), lambda b, i: (0, 0)),
        ],
        out_specs=pl.BlockSpec((None, ts, hv), lambda b, i: (b, i, 0)),
        out_shape=jax.ShapeDtypeStruct((bsz, s, hv), BF16),
        scratch_shapes=[pltpu.VMEM((GLA_HEADS, GLA_DV, GLA_DK), F32), pltpu.VMEM((ts, ts), BF16)],
        compiler_params=_cparams(("arbitrary", "arbitrary")),
        name="gla",
    )(proj, proj, proj, proj, proj, gate_up, gate_b.reshape(1, hk), out_norm.reshape(1, GLA_DV))


def _dsa_prep_kernel(dq_ref, dk_ref, dv_ref, iq_ref, sm_ref, pos_ref, qn_ref, kn_ref, freq_ref,
                     q_o, k_o, v_o, iq_o, ik_o, iw_o):
    pos = pos_ref[...].astype(F32)
    lane = lax.broadcasted_iota(I32, (1, LANES), 1)
    half_a, half_i = DSA_HEAD_DIM // 2, IDX_DIM // 2

    ang = pos * freq_ref[...]
    cos_t, sin_t = jnp.cos(ang), jnp.sin(ang)

    def attn_table(t):
        return jnp.where(lane < half_a, t, pltpu.roll(t, half_a, 1))

    def idx_table(t):
        return jnp.where(lane < half_i, pltpu.roll(t, half_a, 1),
                         jnp.where(lane < 2 * half_i, pltpu.roll(t, half_a + half_i, 1), t))

    cos_a = attn_table(cos_t)
    sin_a = jnp.where(lane < half_a, -1.0, 1.0) * attn_table(sin_t)

    def rope_attn(t):
        return t * cos_a + pltpu.roll(t, half_a, 1) * sin_a

    first = (lane % IDX_DIM) < half_i
    cos_i = idx_table(cos_t)
    sin_i = jnp.where(first, -1.0, 1.0) * idx_table(sin_t)

    def rope_idx(t):
        rot = jnp.where(first, pltpu.roll(t, LANES - IDX_DIM // 2, 1), pltpu.roll(t, IDX_DIM // 2, 1))
        return t * cos_i + rot * sin_i

    qn, kn = qn_ref[...], kn_ref[...]
    for h in range(DSA_HEADS):
        hs = slice(h * DSA_HEAD_DIM, (h + 1) * DSA_HEAD_DIM)
        q_o[:, hs] = (rope_attn(_rms(dq_ref[:, hs], qn)) * DSA_Q_SCALE).astype(q_o.dtype)
        k_o[:, hs] = rope_attn(_rms(dk_ref[:, hs], kn)).astype(k_o.dtype)
    tk = v_o.shape[-1]
    for j in range(v_o.shape[0]):
        v_o[j] = dv_ref[j * tk:(j + 1) * tk, :].T.astype(v_o.dtype)
    for j in range(IDX_HEADS * IDX_DIM // LANES):
        r = rope_idx(iq_ref[:, j * LANES:(j + 1) * LANES])
        iq_o[2 * j] = r[:, :IDX_DIM].astype(iq_o.dtype)
        iq_o[2 * j + 1] = r[:, IDX_DIM:].astype(iq_o.dtype)
    sm = sm_ref[...]
    ik_o[...] = rope_idx(sm)[:, SM_IK:SM_IK + IDX_DIM].astype(ik_o.dtype)
    iw_o[...] = sm.T[SM_IW:SM_IW + IDX_HEADS, :]


def _proj_ab_kernel(x_ref, nw_ref, sh_ref, sc_ref, w_ref, pos_ref, qn_ref, kn_ref, freq_ref,
                    gla_o, q_o, k_o, v_o, iq_o, ik_o, iw_o, dsa_ref):
    hb = (_rms(x_ref[...], nw_ref[...]) * (1.0 + sc_ref[...]) + sh_ref[...]).astype(BF16)
    hd = DSA_HEADS * DSA_HEAD_DIM
    gla_o[:, AB_SMALL:] = _dot(hb, w_ref[:, AB_SMALL:AB_GLA_N])
    for j in range(AB_DSA_N // hd):
        dsa_ref[:, j * hd:(j + 1) * hd] = _dot(hb, w_ref[:, AB_GLA_N + j * hd:AB_GLA_N + (j + 1) * hd])
    part = lambda j: dsa_ref.at[:, j * hd:(j + 1) * hd]
    _dsa_prep_kernel(part(0), part(1), part(2), part(3), gla_o.at[:, AB_SMALL:AB_SMALL + LANES],
                     pos_ref, qn_ref, kn_ref, freq_ref, q_o, k_o, v_o, iq_o, ik_o, iw_o)
    gla_o[:, :AB_SMALL] = _dot(hb, w_ref[:, :AB_SMALL])


def _proj_ab(x, nw, shift, scale, w_bf16, positions, q_norm, k_norm, *, tm, tk):
    bsz, s, d = x.shape
    hd = DSA_HEADS * DSA_HEAD_DIM
    half_a, half_i = DSA_HEAD_DIM // 2, IDX_DIM // 2
    inv_a = ROPE_THETA ** (-jnp.arange(half_a, dtype=F32) / half_a)
    inv_i = ROPE_THETA ** (-jnp.arange(half_i, dtype=F32) / half_i)
    freq = jnp.concatenate([inv_a, inv_i, inv_i]).reshape(1, LANES)
    rows = lambda n: pl.BlockSpec((None, tm, n), lambda b, i: (b, i, 0))
    vec = lambda n: pl.BlockSpec((1, n), lambda b, i: (0, 0))
    mod = pl.BlockSpec((None, 1, d), lambda b, i: (b, 0, 0))
    return pl.pallas_call(
        _proj_ab_kernel,
        grid=(bsz, s // tm),
        in_specs=[
            rows(d), vec(d), mod, mod,
            _resident((d, AB_GLA_N + AB_DSA_N), lambda b, i: (0, 0)),
            rows(1), vec(DSA_HEAD_DIM), vec(DSA_HEAD_DIM), vec(LANES),
        ],
        out_specs=[
            rows(AB_GLA_N), rows(hd), rows(hd),
            pl.BlockSpec((None, tm // tk, hd, tk), lambda b, i: (b, i, 0, 0)),
            pl.BlockSpec((None, IDX_HEADS, tm, IDX_DIM), lambda b, i: (b, 0, i, 0)),
            rows(IDX_DIM),
            pl.BlockSpec((None, IDX_HEADS, tm), lambda b, i: (b, 0, i)),
        ],
        out_shape=[
            jax.ShapeDtypeStruct((bsz, s, AB_GLA_N), F32),
            jax.ShapeDtypeStruct((bsz, s, hd), BF16),
            jax.ShapeDtypeStruct((bsz, s, hd), BF16),
            jax.ShapeDtypeStruct((bsz, s // tk, hd, tk), BF16),
            jax.ShapeDtypeStruct((bsz, IDX_HEADS, s, IDX_DIM), BF16),
            jax.ShapeDtypeStruct((bsz, s, IDX_DIM), BF16),
            jax.ShapeDtypeStruct((bsz, IDX_HEADS, s), F32),
        ],
        scratch_shapes=[pltpu.VMEM((tm, AB_DSA_N), F32)],
        compiler_params=_cparams(("arbitrary", "arbitrary")),
        name="proj_ab",
    )(x, nw.reshape(1, d), shift, scale, w_bf16, positions.reshape(bsz, s, 1),
      q_norm.reshape(1, -1), k_norm.reshape(1, -1), freq)


def _dsa_kernel(q_ref, iq_ref, iw_ref, k_ref, vt_ref, ik_ref, o_ref, sc_ref, top_ref, acc_ref,
                *, tq, n_sel, s_len):
    i = pl.program_id(1)
    nkb = i + 1
    kofs = lax.broadcasted_iota(I32, (tq, tq), 0)
    qidx = i * tq + lax.broadcasted_iota(I32, (tq, tq), 1)
    iw = iw_ref[...]
    idx_scale = (IDX_HEADS ** -0.5) * (IDX_DIM ** -0.5)

    def over_tiles(step, init, n=nkb, widest=2):
        carry = lax.fori_loop(0, n // widest, lambda j, c: step(widest * j, widest, c), init)
        rest = n % widest
        start = n - rest
        width = widest // 2
        while width >= 1:
            carry = lax.cond((rest & width) != 0, functools.partial(step, start, width), lambda c: c, carry)
            start = start + (rest & width)
            width //= 2
        return carry

    def as_float(key):
        key = jnp.maximum(key, KEY_LOWEST)
        key = jnp.where((key > 0) & (key < KEY_TINY), KEY_TINY, key)
        return pltpu.bitcast(key ^ ((key >> 31) & 0x7FFFFFFF), F32)

    def scores(kb, width, diagonal):
        k0 = pl.multiple_of(kb * tq, tq)
        rows = width * tq
        ikb = ik_ref[pl.ds(k0, rows), :]
        sc = jnp.zeros((rows, tq), F32)
        for h in range(IDX_HEADS):
            sc = sc + jnp.maximum(_dot_nt(ikb, iq_ref[h]), 0.0) * iw[h:h + 1, :]
        sc = sc * idx_scale
        sc = jnp.where(jnp.abs(sc) < FLT_TINY, 0.0, sc)
        if diagonal:
            sc = jnp.where(kofs <= qidx - i * tq, sc, -jnp.inf)
        sc_ref[pl.ds(kb, width)] = sc.reshape(width, tq, tq)
        top_ref[pl.ds(kb, width)] = sc.astype(BF16).reshape(width, tq, tq)

    def below_diagonal(kb, width, carry):
        scores(kb, width, False)
        return carry

    over_tiles(below_diagonal, 0, n=i)
    scores(i, 1, True)

    def count(pred):
        def body(kb, width, acc):
            rows = width * tq
            m = jnp.where(pred(sc_ref[pl.ds(kb, width)].reshape(rows, tq)), 1, 0)
            return acc + jnp.sum(m.reshape(rows // 8, 8, tq), axis=0)
        return jnp.sum(over_tiles(body, jnp.zeros((8, tq), I32)), axis=0, keepdims=True)

    n_valid = qidx[0:1, :] + 1

    def as_bf16(cand):
        k16 = jnp.maximum(cand, KEY_LOWEST_BF16) >> 16
        pattern = k16 ^ ((k16 >> 15) & 0x7FFF)
        pattern = jnp.where((pattern > 0) & (pattern < 0x80), 0x80, pattern)
        return pltpu.bitcast(pattern << 16, F32).astype(BF16)

    def count_top(cand):
        cand_bf = as_bf16(cand)

        def body(kb, width, acc):
            rows = width * tq
            top = top_ref[pl.ds(kb, width)].reshape(rows, tq)
            m = jnp.where(top >= cand_bf, jnp.ones((), BF16), jnp.zeros((), BF16))
            part = m[0:16]
            for j in range(1, rows // 16):
                part = part + m[16 * j:16 * (j + 1)]
            return acc + part.astype(F32)
        acc = over_tiles(body, jnp.zeros((16, tq), F32))
        return jnp.sum(acc, axis=0, keepdims=True).astype(I32)

    def top_step(it, st):
        u, cnt_u = st
        cand_u = u | lax.shift_left(jnp.int32(1), jnp.int32(31) - it)
        cnt = count_top(cand_u ^ INT_MIN)
        take = cnt >= n_sel
        return jnp.where(take, cand_u, u), jnp.where(take, cnt, cnt_u)

    u, _ = lax.fori_loop(0, 16, top_step, (jnp.zeros((1, tq), I32), n_valid))

    def any_lane(flag):
        return jnp.max(jnp.where(flag, 1.0, 0.0)) > 0.0

    base = u ^ INT_MIN
    base = jnp.where(base < 0, base | 0xFFFF, base)
    base = jnp.clip(base, KEY_LOWEST + BF16_HALF_STEP + 1, INT_MAX - 2 * BF16_HALF_STEP)
    many = n_valid > n_sel
    lo = jnp.where(many, base - BF16_HALF_STEP - 1, KEY_LOWEST)
    hi = jnp.where(many, base + 2 * BF16_HALF_STEP, INT_MAX)
    def count_two(pred_a, pred_b):
        def body(kb, width, acc):
            x = sc_ref[pl.ds(kb, width)].reshape(width * tq // 8, 8, tq)
            return (acc[0] + jnp.sum(jnp.where(pred_a(x), 1, 0), axis=0),
                    acc[1] + jnp.sum(jnp.where(pred_b(x), 1, 0), axis=0))
        a, b = over_tiles(body, (jnp.zeros((8, tq), I32), jnp.zeros((8, tq), I32)))
        return jnp.sum(a, axis=0, keepdims=True), jnp.sum(b, axis=0, keepdims=True)

    lo_f, hi_f = as_float(lo), as_float(hi)
    cnt_lo, cnt_hi = count_two(lambda x: x >= lo_f, lambda x: x >= hi_f)

    def open_queries(lo, cnt_lo, hi, done):
        return (cnt_lo > n_sel) & (hi - lo > 1) & (done == 0)

    def halve(s):
        lo, cnt_lo, hi, cnt_hi, done = s
        mid = lo + ((hi - lo) >> 1)
        mid_f = as_float(mid)
        cnt = count(lambda x: x >= mid_f)
        live = open_queries(lo, cnt_lo, hi, done)
        up = live & (cnt >= n_sel)
        down = live & (cnt < n_sel)
        return (jnp.where(up, mid, lo), jnp.where(up, cnt, cnt_lo),
                jnp.where(down, mid, hi), jnp.where(down, cnt, cnt_hi), done)

    st = lax.fori_loop(0, PEEL_AFTER_STEPS, lambda _, s: halve(s),
                       (lo, cnt_lo, hi, cnt_hi, jnp.zeros((1, tq), I32)))

    def peel(st):
        lo, cnt_lo, hi, cnt_hi, done = st
        lo_f, hi_f = as_float(lo), as_float(hi)

        def extremes(kb, width, carry):
            top, bot = carry
            x = sc_ref[pl.ds(kb, width)].reshape(width * tq // 8, 8, tq)
            top = jnp.maximum(top, jnp.max(jnp.where(x < hi_f, x, -jnp.inf), axis=0))
            bot = jnp.minimum(bot, jnp.min(jnp.where(x >= lo_f, x, jnp.inf), axis=0))
            return top, bot
        top, bot = over_tiles(extremes, (jnp.full((8, tq), -jnp.inf, F32), jnp.full((8, tq), jnp.inf, F32)))
        top = jnp.max(top, axis=0, keepdims=True)
        bot = jnp.min(bot, axis=0, keepdims=True)
        from_top = (cnt_hi == n_sel - 1) | (top == bot)
        from_bot = (cnt_lo == n_sel + 1) & jnp.logical_not(from_top)
        cnt_top, cnt_bot = count_two(lambda x: x >= top, lambda x: x > bot)
        cnt = jnp.where(from_top, cnt_top, cnt_bot)
        short = from_bot & (cnt < n_sel)
        hit = open_queries(lo, cnt_lo, hi, done) & (from_top | from_bot)
        thr = jnp.where(from_top, top, bot)
        incl = jnp.where(from_top | short, 1, 0)
        return hit, thr, incl, jnp.where(short, cnt_lo, cnt)

    def with_peel(st):
        hit, thr, incl, cnt = peel(st)
        st = st[:4] + (jnp.where(hit, 1, 0),)
        lo, cnt_lo, _, _, _ = lax.while_loop(lambda s: any_lane(open_queries(s[0], s[1], s[2], s[4])), halve, st)
        return jnp.where(hit, thr, as_float(lo)), jnp.where(hit, incl, 1), jnp.where(hit, cnt, cnt_lo)

    def without_peel(st):
        lo, cnt_lo, _, _, _ = st
        return as_float(lo), jnp.ones((1, tq), I32), cnt_lo

    thr, incl, cnt = lax.cond(any_lane(open_queries(st[0], st[1], st[2], st[4])), with_peel, without_peel, st)
    admits_equal = incl != 0
    excess = cnt > n_sel

    tied = jnp.max(jnp.where(excess, 1.0, 0.0)) > 0.0

    acc_ref[...] = jnp.zeros_like(acc_ref)

    def attend(select, seen):
        def block(kb, width, carry):
            ms, ls, seen = carry
            k0 = pl.multiple_of(kb * tq, tq)
            rows = width * tq
            sel, seen = select(sc_ref[pl.ds(kb, width)].reshape(rows, tq), seen)
            heads = [slice(h * DSA_HEAD_DIM, (h + 1) * DSA_HEAD_DIM) for h in range(DSA_HEADS)]
            st = [_dot_nt(k_ref[pl.ds(k0, rows), hs], q_ref[:, hs]) for hs in heads]
            st = [jnp.where(sel, x, MASK_BIAS) for x in st]
            new_m = [jnp.maximum(m, jnp.max(x, axis=0, keepdims=True)) for m, x in zip(ms, st)]
            alpha = [jnp.exp2(m - mn) for m, mn in zip(ms, new_m)]
            p = [jnp.exp2(x - mn) for x, mn in zip(st, new_m)]
            new_l = [a * l + jnp.sum(x, axis=0, keepdims=True) for a, l, x in zip(alpha, ls, p)]
            p = [x.astype(BF16) for x in p]
            for h, hs in enumerate(heads):
                pv = _dot(vt_ref[kb, hs, :], p[h][0:tq])
                for j in range(1, width):
                    pv = pv + _dot(vt_ref[kb + j, hs, :], p[h][j * tq:(j + 1) * tq])
                acc_ref[h] = alpha[h] * acc_ref[h] + pv
            return tuple(new_m), tuple(new_l), seen

        carry = (tuple(jnp.full((1, tq), -jnp.inf, F32) for _ in range(DSA_HEADS)),
                 tuple(jnp.zeros((1, tq), F32) for _ in range(DSA_HEADS)), seen)
        return over_tiles(block, carry)[1]

    def attend_tied():
        room = n_sel - count(lambda x: x > thr)
        room = jnp.where(excess, room, jnp.where(admits_equal, s_len, 0)).astype(F32)
        r = lax.broadcasted_iota(I32, (tq, tq), 0)
        c = lax.broadcasted_iota(I32, (tq, tq), 1)
        upto = jnp.where(c <= r, 1.0, 0.0).astype(BF16)

        def select(x, seen):
            equal = x == thr
            ones = jnp.where(equal, 1.0, 0.0).astype(BF16)
            ranks = []
            for j in range(x.shape[0] // tq):
                part = ones[j * tq:(j + 1) * tq]
                ranks.append(_dot(upto, part) + seen)
                seen = seen + jnp.sum(part.astype(F32), axis=0, keepdims=True)
            rank = ranks[0] if len(ranks) == 1 else jnp.concatenate(ranks, axis=0)
            return (x > thr) | (equal & (rank <= room)), seen
        return attend(select, jnp.zeros((1, tq), F32))

    at_least = jnp.where(admits_equal, thr, jnp.inf)
    above = jnp.where(admits_equal, jnp.inf, thr)
    ls = lax.cond(tied, attend_tied,
                  lambda: attend(lambda x, seen: ((x >= at_least) | (x > above), seen), jnp.zeros((1, 1), F32)))
    for h in range(DSA_HEADS):
        hs = slice(h * DSA_HEAD_DIM, (h + 1) * DSA_HEAD_DIM)
        o_ref[:, hs] = (acc_ref[h] / ls[h]).T.astype(o_ref.dtype)


def _dsa(q, k, vt, iq, ik, iw, *, tq):
    bsz, s, hd = q.shape
    n_sel = min(DSA_TOPK, s // 4)
    resident = lambda n: _resident((None, s, n), lambda b, i: (b, 0, 0))
    return pl.pallas_call(
        functools.partial(_dsa_kernel, tq=tq, n_sel=n_sel, s_len=s),
        grid=(bsz, s // tq),
        in_specs=[
            pl.BlockSpec((None, tq, hd), lambda b, i: (b, i, 0)),
            pl.BlockSpec((None, IDX_HEADS, tq, IDX_DIM), lambda b, i: (b, 0, i, 0)),
            pl.BlockSpec((None, IDX_HEADS, tq), lambda b, i: (b, 0, i)),
            resident(hd),
            _resident((None, s // tq, hd, tq), lambda b, i: (b, 0, 0, 0)),
            resident(IDX_DIM),
        ],
        out_specs=pl.BlockSpec((None, tq, hd), lambda b, i: (b, i, 0)),
        out_shape=jax.ShapeDtypeStruct((bsz, s, hd), BF16),
        scratch_shapes=[
            pltpu.VMEM((s // tq, tq, tq), F32),
            pltpu.VMEM((s // tq, tq, tq), BF16),
            pltpu.VMEM((DSA_HEADS, DSA_HEAD_DIM, tq), F32),
        ],
        compiler_params=_cparams(("arbitrary", "arbitrary"), DSA_VMEM_LIMIT),
        name="dsa",
    )(q, iq, iw, k, vt, ik)


def _mix_ffn_kernel(x_ref, a_ref, b_ref, wa_ref, wb_ref, g1_ref, nw_ref, sh_ref, sc_ref, g2_ref,
                    w1_ref, w2_ref, o_ref, *, tf):
    x = x_ref[...] + g1_ref[...] * (_dot(a_ref[...], wa_ref[...]) + _dot(b_ref[...], wb_ref[...]))
    hb = (_rms(x, nw_ref[...]) * (1.0 + sc_ref[...]) + sh_ref[...]).astype(BF16)
    f = w2_ref.shape[0]
    acc = jnp.zeros(x.shape, F32)
    for j in range(f // tf):
        gate = _dot(hb, w1_ref[:, j * tf:(j + 1) * tf])
        up = _dot(hb, w1_ref[:, f + j * tf:f + (j + 1) * tf])
        acc = acc + _dot((_silu(gate) * up).astype(BF16), w2_ref[j * tf:(j + 1) * tf, :])
    o_ref[...] = x + g2_ref[...] * acc


def _mix_ffn(x, mix_a, mix_b, w_out_bf16, gate1, nw, shift, scale, gate2, w1_bf16, w2_bf16, *, tm, tf):
    bsz, s, d = x.shape
    f = w2_bf16.shape[0]
    na = mix_a.shape[-1]
    assert mix_b.shape[-1] == na and w_out_bf16.shape[0] == 2 * na
    rows = lambda n: pl.BlockSpec((None, tm, n), lambda b, i: (b, i, 0))
    mod = pl.BlockSpec((None, 1, d), lambda b, i: (b, 0, 0))
    return pl.pallas_call(
        functools.partial(_mix_ffn_kernel, tf=tf),
        grid=(bsz, s // tm),
        in_specs=[
            rows(d), rows(na), rows(na),
            _resident((na, d), lambda b, i: (0, 0)),
            _resident((na, d), lambda b, i: (1, 0)),
            mod,
            pl.BlockSpec((1, d), lambda b, i: (0, 0)),
            mod, mod, mod,
            _resident((d, 2 * f), lambda b, i: (0, 0)),
            _resident((f, d), lambda b, i: (0, 0)),
        ],
        out_specs=rows(d),
        out_shape=jax.ShapeDtypeStruct((bsz, s, d), F32),
        compiler_params=_cparams(("arbitrary", "arbitrary")),
        name="mix_ffn",
    )(x, mix_a, mix_b, w_out_bf16, w_out_bf16, gate1, nw.reshape(1, d), shift, scale, gate2,
      w1_bf16, w2_bf16)


HALO = max(POOL_WINDOWS)


def _proj_cd_kernel(x_ref, nw_ref, sh_ref, sc_ref, w_ref, pw_ref, ps_ref, qn_ref, kn_ref,
                    pool_o, q_o, k_o, v_o, ext_ref, sb_ref):
    i = pl.program_id(1)
    tm = x_ref.shape[0]
    n = POOL_GROUPS * POOL_CH
    hb = (_rms(x_ref[...], nw_ref[...]) * (1.0 + sc_ref[...]) + sh_ref[...]).astype(BF16)

    @pl.when(i == 0)
    def _():
        ext_ref[0:HALO, :] = jnp.zeros((HALO, n), F32)

    ext_ref[HALO:HALO + tm, :] = _dot(hb, w_ref[:, :n])
    for j in range(3):
        sb_ref[:, j * n:(j + 1) * n] = _dot(hb, w_ref[:, (j + 1) * n:(j + 2) * n])

    t = i * tm + lax.broadcasted_iota(I32, (tm, 1), 0)
    for g, w in enumerate(POOL_WINDOWS):
        gs = slice(g * POOL_CH, (g + 1) * POOL_CH)
        u = ext_ref[HALO:HALO + tm, gs]
        tot = u
        for j in range(1, w):
            tot = tot + ext_ref[HALO - j:HALO - j + tm, gs]
        cnt = jnp.minimum(t + 1, w).astype(F32)
        pooled = tot / cnt - u
        y = _dot(pooled.astype(BF16), pw_ref[g].astype(BF16)) * ps_ref[:, gs]
        pool_o[:, gs] = y.astype(pool_o.dtype)
    ext_ref[0:HALO, :] = ext_ref[tm:tm + HALO, :]

    qn, kn = qn_ref[...], kn_ref[...]
    for h in range(SB_HEADS):
        hs = slice(h * SB_HEAD_DIM, (h + 1) * SB_HEAD_DIM)
        q_o[:, hs] = (_rms(sb_ref[:, hs], qn) * SB_HEAD_DIM ** -0.5).astype(q_o.dtype)
        k_o[:, hs] = _rms(sb_ref[:, n + h * SB_HEAD_DIM:n + (h + 1) * SB_HEAD_DIM], kn).astype(k_o.dtype)
    tk = v_o.shape[-1]
    for j in range(v_o.shape[0]):
        v_o[j] = sb_ref[j * tk:(j + 1) * tk, 2 * n:3 * n].T.astype(v_o.dtype)


def _proj_cd(x, nw, shift, scale, w_bf16, pool_w, pool_scale, q_norm, k_norm, *, tm, tk):
    bsz, s, d = x.shape
    n = POOL_GROUPS * POOL_CH
    assert w_bf16.shape[1] == 4 * n
    rows = lambda m: pl.BlockSpec((None, tm, m), lambda b, i: (b, i, 0))
    vec = lambda m: pl.BlockSpec((1, m), lambda b, i: (0, 0))
    mod = pl.BlockSpec((None, 1, d), lambda b, i: (b, 0, 0))
    return pl.pallas_call(
        _proj_cd_kernel,
        grid=(bsz, s // tm),
        in_specs=[
            rows(d), vec(d), mod, mod,
            _resident((d, 4 * n), lambda b, i: (0, 0)),
            pl.BlockSpec((POOL_GROUPS, POOL_CH, POOL_CH), lambda b, i: (0, 0, 0)),
            vec(n), vec(SB_HEAD_DIM), vec(SB_HEAD_DIM),
        ],
        out_specs=[rows(n)] * 3 + [pl.BlockSpec((None, tm // tk, n, tk), lambda b, i: (b, i, 0, 0))],
        out_shape=[jax.ShapeDtypeStruct((bsz, s, n), BF16)] * 3
        + [jax.ShapeDtypeStruct((bsz, s // tk, n, tk), BF16)],
        scratch_shapes=[pltpu.VMEM((HALO + tm, n), F32), pltpu.VMEM((tm, 3 * n), F32)],
        compiler_params=_cparams(("arbitrary", "arbitrary")),
        name="proj_cd",
    )(x, nw.reshape(1, d), shift, scale, w_bf16, pool_w, pool_scale.reshape(1, n),
      q_norm.reshape(1, -1), k_norm.reshape(1, -1))


def _sb_kernel(q_ref, k_ref, vt_ref, o_ref, acc_ref, *, tq):
    i = pl.program_id(1)
    kofs = lax.broadcasted_iota(I32, (tq, tq), 0)
    qofs = lax.broadcasted_iota(I32, (tq, tq), 1)
    later = jnp.where(qofs > kofs, 1.0, 0.0).astype(BF16)
    heads = [slice(h * SB_HEAD_DIM, (h + 1) * SB_HEAD_DIM) for h in range(SB_HEADS)]
    acc_ref[...] = jnp.zeros_like(acc_ref)

    def block(kb, runs, diagonal):
        k0 = pl.multiple_of(kb * tq, tq)
        z = [_dot_nt(k_ref[pl.ds(k0, tq), hs], q_ref[:, hs]) for hs in heads]
        sp = [jnp.log(1.0 + jnp.exp(-jnp.abs(x))) for x in z]
        log_beta = [jnp.minimum(x, 0.0) - s for x, s in zip(z, sp)]
        log_1m = [-jnp.maximum(x, 0.0) - s for x, s in zip(z, sp)]
        if diagonal:
            strict = kofs < qofs
            log_1m = [jnp.where(strict, x, 0.0) for x in log_1m]
        hi = [x.astype(BF16) for x in log_1m]
        lo = [(x - h.astype(F32)).astype(BF16) for x, h in zip(log_1m, hi)]
        after = [_dot(later, h) + _dot(later, l) + r for h, l, r in zip(hi, lo, runs)]
        w = [jnp.exp(lb + a) for lb, a in zip(log_beta, after)]
        if diagonal:
            w = [jnp.where(strict, x, 0.0) for x in w]
        for h, hs in enumerate(heads):
            acc_ref[h] = acc_ref[h] + _dot(vt_ref[kb, hs, :], w[h].astype(BF16))
        return tuple(r + jnp.sum(x, axis=0, keepdims=True) for r, x in zip(runs, log_1m))

    def alive(runs):
        worst = functools.reduce(jnp.maximum, runs)
        return jnp.max(worst) > SB_DEAD_LOG

    runs = block(i, tuple(jnp.zeros((1, tq), F32) for _ in heads), True)

    def earlier(state):
        kb, runs, _ = state
        runs = block(kb, runs, False)
        return kb - 1, runs, alive(runs)

    lax.while_loop(lambda st: (st[0] >= 0) & st[2], earlier, (i - 1, runs, alive(runs)))
    for h, hs in enumerate(heads):
        o_ref[:, hs] = acc_ref[h].T.astype(o_ref.dtype)


def _sb(q, k, vt, *, tq):
    bsz, s, hd = q.shape
    return pl.pallas_call(
        functools.partial(_sb_kernel, tq=tq),
        grid=(bsz, s // tq),
        in_specs=[
            pl.BlockSpec((None, tq, hd), lambda b, i: (b, i, 0)),
            _resident((None, s, hd), lambda b, i: (b, 0, 0)),
            _resident((None, s // tq, hd, tq), lambda b, i: (b, 0, 0, 0)),
        ],
        out_specs=pl.BlockSpec((None, tq, hd), lambda b, i: (b, i, 0)),
        out_shape=jax.ShapeDtypeStruct((bsz, s, hd), BF16),
        scratch_shapes=[pltpu.VMEM((SB_HEADS, SB_HEAD_DIM, tq), F32)],
        compiler_params=_cparams(("arbitrary", "arbitrary")),
        name="stick_breaking",
    )(q, k, vt)


def _pack_ab_weight(w):
    d = w.shape[0]
    gq, gk, gv, glow, gr, dq, dk, dv, iq, ik, iw = jnp.split(
        w, [256, 512, 1024, 1040, 1552, 2064, 2576, 3088, 3600, 3664], axis=1)
    pad = jnp.zeros((d, LANES - (IDX_DIM + GLA_GATE_RANK + IDX_HEADS)), w.dtype)
    return jnp.concatenate([gq, gk, gv, gr, ik, glow, iw, pad, dq, dk, dv, iq], axis=1)


def kernel(x, c, positions, ada_w, ada_b, mix_norm, ffn_norm, ffn_w1, ffn_w2, ab_w_in, gla_gate_up,
           gla_gate_b, gla_out_norm, dsa_q_norm, dsa_k_norm, ab_w_out, cd_w_in, pool_w, pool_scale,
           sb_q_norm, sb_k_norm, cd_w_out):
    bsz, s, d = x.shape
    depth = ada_w.shape[0]
    mod = _ada_mod(c, ada_w, ada_b).reshape(depth, bsz, 6, 1, d)
    tm = min(512, s)
    tq_dsa = min(512, s)
    tq_sb = min(256, s)
    for layer in range(depth):
        sh1, sc1, g1, sh2, sc2, g2 = (mod[layer, :, j] for j in range(6))
        i = layer // 2
        if layer % 2 == 0:
            proj, q, k, vt, iq, ik, iw = _proj_ab(
                x, mix_norm[layer], sh1, sc1, _pack_ab_weight(ab_w_in[i]).astype(BF16), positions,
                dsa_q_norm[i], dsa_k_norm[i], tm=tq_dsa, tk=tq_dsa)
            mix_a = _gla(proj, gla_gate_up[i], gla_gate_b[i], gla_out_norm[i], ts=tm)
            mix_b = _dsa(q, k, vt, iq, ik, iw, tq=tq_dsa)
            w_out = ab_w_out[i]
        else:
            mix_a, q, k, vt = _proj_cd(x, mix_norm[layer], sh1, sc1, cd_w_in[i].astype(BF16), pool_w[i],
                                       pool_scale[i], sb_q_norm[i], sb_k_norm[i], tm=tm, tk=tq_sb)
            mix_b = _sb(q, k, vt, tq=tq_sb)
            w_out = cd_w_out[i]
        x = _mix_ffn(x, mix_a, mix_b, w_out.astype(BF16), g1, ffn_norm[layer], sh2, sc2, g2,
                     ffn_w1[layer].astype(BF16), ffn_w2[layer].astype(BF16), tm=tm, tf=256)
    return x
```

```python
import functools

import jax
import jax.numpy as jnp
from jax import lax
from jax.experimental import pallas as pl
from jax.experimental.pallas import tpu as pltpu

F32 = jnp.float32
BF16 = jnp.bfloat16
I32 = jnp.int32

GLA_HEADS, GLA_DK, GLA_DV = 4, 64, 128
GLA_GATE_RANK = 16
GLA_GATE_TAU = 16.0
GLA_CHUNK = 64
DSA_HEADS, DSA_HEAD_DIM = 4, 128
IDX_HEADS, IDX_DIM = 8, 64
DSA_TOPK = 256
POOL_WINDOWS = (2, 4, 8, 16)
POOL_GROUPS, POOL_CH = 4, 128
SB_HEADS, SB_HEAD_DIM = 4, 128
ROPE_THETA = 10000.0
NORM_EPS = 1e-6

LANES = 128
VMEM_LIMIT = 56 * 1024 * 1024
DSA_VMEM_LIMIT = 61 * 1024 * 1024

AB_GQ, AB_GK, AB_GV, AB_GR = 0, 256, 512, 1024
AB_SMALL = 1536
SM_IK, SM_GLOW, SM_IW = 0, 64, 80
AB_GLA_N = AB_SMALL + LANES
AB_DSA_N = 4 * 512

DSA_Q_SCALE = 1.4426950408889634 * DSA_HEAD_DIM ** -0.5
INT_MIN = -2 ** 31
INT_MAX = 2 ** 31 - 1
FLT_TINY = 2.0 ** -126
KEY_TINY = 0x00800000
KEY_LOWEST = INT_MIN + 0x00800000
KEY_LOWEST_BF16 = INT_MIN + 0x00810000
BF16_HALF_STEP = 0x8000
PEEL_AFTER_STEPS = 5
MASK_BIAS = -1e30
SB_DEAD_LOG = -110.0


def _dot(a, b):
    return jnp.dot(a, b, preferred_element_type=F32)


def _dot_nt(a, b):
    return lax.dot_general(a, b, (((1,), (1,)), ((), ())), preferred_element_type=F32)


def _dot_tn(a, b):
    return lax.dot_general(a, b, (((0,), (0,)), ((), ())), preferred_element_type=F32)


def _split3(a):
    hi = a.astype(BF16)
    r1 = a - hi.astype(F32)
    mid = r1.astype(BF16)
    lo = (r1 - mid.astype(F32)).astype(BF16)
    return hi, mid, lo


def _silu(x):
    return x * jax.nn.sigmoid(x)


def _rms(x, w):
    var = jnp.mean(x * x, axis=-1, keepdims=True)
    return x * lax.rsqrt(var + NORM_EPS) * w


def _cparams(sem, vmem_limit=VMEM_LIMIT):
    return pltpu.CompilerParams(dimension_semantics=sem, vmem_limit_bytes=vmem_limit)


def _resident(block_shape, index_map):
    return pl.BlockSpec(block_shape, index_map, pipeline_mode=pl.Buffered(1))


def _ada_kernel(c_ref, w_ref, b_ref, o_ref):
    cond = _silu(c_ref[...])
    c_hi, c_mid, _ = _split3(cond)
    w_hi, w_mid, _ = _split3(w_ref[...])
    acc = _dot(c_hi, w_hi) + _dot(c_hi, w_mid) + _dot(c_mid, w_hi)
    o_ref[...] = acc + b_ref[...]


def _ada_mod(c, ada_w, ada_b):
    depth, d, n = ada_w.shape
    bsz = c.shape[0]
    rows = 16
    cp =jnp.zeros((rows, d), F32).at[:bsz].set(c)
    tn = 1536
    out = pl.pallas_call(
        _ada_kernel,
        grid=(depth, n // tn),
        in_specs=[
            pl.BlockSpec((rows, d), lambda l, j: (0, 0)),
            pl.BlockSpec((None, d, tn), lambda l, j: (l, 0, j)),
            pl.BlockSpec((None, 1, tn), lambda l, j: (l, 0, j)),
        ],
        out_specs=pl.BlockSpec((None, rows, tn), lambda l, j: (l, 0, j)),
        out_shape=jax.ShapeDtypeStruct((depth, rows, n), F32),
        compiler_params=_cparams(("arbitrary", "arbitrary")),
        name="ada_mod",
    )(cp, ada_w, ada_b.reshape(depth, 1, n))
    return out[:, :bsz]


def _gla_kernel(q_ref, k_ref, v_ref, gr_ref, sm_ref, gup_ref, gb_ref, onw_ref, o_ref, st_ref, tri_ref, *, ts):
    c = GLA_CHUNK

    @pl.when(pl.program_id(1) == 0)
    def _():
        st_ref[...] = jnp.zeros_like(st_ref)
        row = lax.broadcasted_iota(I32, (ts, ts), 0)
        col = lax.broadcasted_iota(I32, (ts, ts), 1)
        tri_ref[...] = jnp.where((row // c == col // c) & (col <= row), 1.0, 0.0).astype(BF16)

    nc = ts // c
    tri = tri_ref[...]
    causal = lax.broadcasted_iota(I32, (c, c), 1) <= lax.broadcasted_iota(I32, (c, c), 0)
    onw = onw_ref[...]

    glow = sm_ref[:, SM_GLOW:SM_GLOW + GLA_GATE_RANK]
    a = _dot(glow.astype(BF16), gup_ref[...].astype(BF16)) + gb_ref[...]
    g = (jnp.minimum(a, 0.0) - jnp.log1p(jnp.exp(-jnp.abs(a)))) / GLA_GATE_TAU
    g_hi, g_mid, g_lo = _split3(g)
    b = _dot(tri, g_hi) + _dot(tri, g_mid) + _dot(tri, g_lo)
    qs = q_ref[...] * (GLA_DK ** -0.5)
    k = k_ref[...]

    qe, ke, kd, qd, dec = [], [], [], [], []
    for ci in range(nc):
        r = slice(ci * c, (ci + 1) * c)
        bc = b[r]
        b_mid = bc[c // 2 - 1:c // 2, :]
        b_last = bc[c - 1:c, :]
        qe.append((qs[r] * jnp.exp(bc - b_mid)).astype(BF16))
        ke.append((k[r] * jnp.exp(b_mid - bc)).astype(BF16))
        kd.append((k[r] * jnp.exp(b_last - bc)).astype(BF16))
        qd.append((qs[r] * jnp.exp(bc)).astype(BF16))
        dec.append(jnp.exp(b_last))

    pairs = [(ci, h) for ci in range(nc) for h in range(GLA_HEADS)]
    ksl = lambda h: slice(h * GLA_DK, (h + 1) * GLA_DK)
    vsl = lambda h: slice(h * GLA_DV, (h + 1) * GLA_DV)
    vh = {(ci, h): v_ref[ci * c:(ci + 1) * c, vsl(h)].astype(BF16) for ci, h in pairs}
    att = {p: _dot_nt(qe[p[0]][:, ksl(p[1])], ke[p[0]][:, ksl(p[1])]) for p in pairs}
    att = {p: jnp.where(causal, att[p], 0.0).astype(BF16) for p in pairs}
    o = {p: _dot(att[p], vh[p]) for p in pairs}
    kvt = {p: _dot_tn(vh[p], kd[p[0]][:, ksl(p[1])]) for p in pairs}
    st_in = {}
    for h in range(GLA_HEADS):
        st = st_ref[h]
        for ci in range(nc):
            st_in[(ci, h)] = st.astype(BF16)
            st = st * dec[ci][:, ksl(h)] + kvt[(ci, h)]
        st_ref[h] = st
    for ci, h in pairs:
        out = o[(ci, h)] + _dot_nt(qd[ci][:, ksl(h)], st_in[(ci, h)])
        grh = gr_ref[ci * c:(ci + 1) * c, vsl(h)]
        o_ref[ci * c:(ci + 1) * c, vsl(h)] = (_rms(out, onw) * _silu(grh)).astype(o_ref.dtype)


def _gla(proj, gate_up, gate_b, out_norm, *, ts):
    bsz, s, _ = proj.shape
    hk, hv = GLA_HEADS * GLA_DK, GLA_HEADS * GLA_DV
    return pl.pallas_call(
        functools.partial(_gla_kernel, ts=ts),
        grid=(bsz, s // ts),
        in_specs=[
            pl.BlockSpec((None, ts, hk), lambda b, i: (b, i, AB_GQ // hk)),
            pl.BlockSpec((None, ts, hk), lambda b, i: (b, i, AB_GK // hk)),
            pl.BlockSpec((None, ts, hv), lambda b, i: (b, i, AB_GV // hv)),
            pl.BlockSpec((None, ts, hv), lambda b, i: (b, i, AB_GR // hv)),
            pl.BlockSpec((None, ts, LANES), lambda b, i: (b, i, AB_SMALL // LANES)),
            pl.BlockSpec((GLA_GATE_RANK, hk), lambda b, i: (0, 0)),
            pl.BlockSpec((1, hk), lambda b, i: (0, 0)),
            pl.BlockSpec((1, GLA_DV), lambda b, i: (0, 0)),
        ],
        out_specs=pl.BlockSpec((None, ts, hv), lambda b, i: (b, i, 0)),
        out_shape=jax.ShapeDtypeStruct((bsz, s, hv), BF16),
        scratch_shapes=[pltpu.VMEM((GLA_HEADS, GLA_DV, GLA_DK), F32), pltpu.VMEM((ts, ts), BF16)],
        compiler_params=_cparams(("arbitrary", "arbitrary")),
        name="gla",
    )(proj, proj, proj, proj, proj, gate_up, gate_b.reshape(1, hk), out_norm.reshape(1, GLA_DV))


def _dsa_prep_kernel(dq_ref, dk_ref, dv_ref, iq_ref, sm_ref, pos_ref, qn_ref, kn_ref, freq_ref,
                     q_o, k_o, v_o, iq_o, ik_o, iw_o):
    pos = pos_ref[...].astype(F32)
    lane = lax.broadcasted_iota(I32, (1, LANES), 1)
    half_a, half_i = DSA_HEAD_DIM // 2, IDX_DIM // 2

    ang = pos * freq_ref[...]
    cos_t, sin_t = jnp.cos(ang), jnp.sin(ang)

    def attn_table(t):
        return jnp.where(lane < half_a, t, pltpu.roll(t, half_a, 1))

    def idx_table(t):
        return jnp.where(lane < half_i, pltpu.roll(t, half_a, 1),
                         jnp.where(lane < 2 * half_i, pltpu.roll(t, half_a + half_i, 1), t))

    cos_a = attn_table(cos_t)
    sin_a = jnp.where(lane < half_a, -1.0, 1.0) * attn_table(sin_t)

    def rope_attn(t):
        return t * cos_a + pltpu.roll(t, half_a, 1) * sin_a

    first = (lane % IDX_DIM) < half_i
    cos_i = idx_table(cos_t)
    sin_i = jnp.where(first, -1.0, 1.0) * idx_table(sin_t)

    def rope_idx(t):
        rot = jnp.where(first, pltpu.roll(t, LANES - IDX_DIM // 2, 1), pltpu.roll(t, IDX_DIM // 2, 1))
        return t * cos_i + rot * sin_i

    qn, kn = qn_ref[...], kn_ref[...]
    for h in range(DSA_HEADS):
        hs = slice(h * DSA_HEAD_DIM, (h + 1) * DSA_HEAD_DIM)
        q_o[:, hs] = (rope_attn(_rms(dq_ref[:, hs], qn)) * DSA_Q_SCALE).astype(q_o.dtype)
        k_o[:, hs] = rope_attn(_rms(dk_ref[:, hs], kn)).astype(k_o.dtype)
    tk = v_o.shape[-1]
    for j in range(v_o.shape[0]):
        v_o[j] = dv_ref[j * tk:(j + 1) * tk, :].T.astype(v_o.dtype)
    for j in range(IDX_HEADS * IDX_DIM // LANES):
        r = rope_idx(iq_ref[:, j * LANES:(j + 1) * LANES])
        iq_o[2 * j] = r[:, :IDX_DIM].astype(iq_o.dtype)
        iq_o[2 * j + 1] = r[:, IDX_DIM:].astype(iq_o.dtype)
    sm = sm_ref[...]
    ik_o[...] = rope_idx(sm)[:, SM_IK:SM_IK + IDX_DIM].astype(ik_o.dtype)
    iw_o[...] = sm.T[SM_IW:SM_IW + IDX_HEADS, :]


def _proj_ab_kernel(x_ref, nw_ref, sh_ref, sc_ref, w_ref, pos_ref, qn_ref, kn_ref, freq_ref,
                    gla_o, q_o, k_o, v_o, iq_o, ik_o, iw_o, dsa_ref):
    hb = (_rms(x_ref[...], nw_ref[...]) * (1.0 + sc_ref[...]) + sh_ref[...]).astype(BF16)
    hd = DSA_HEADS * DSA_HEAD_DIM
    gla_o[:, AB_SMALL:] = _dot(hb, w_ref[:, AB_SMALL:AB_GLA_N])
    for j in range(AB_DSA_N // hd):
        dsa_ref[:, j * hd:(j + 1) * hd] = _dot(hb, w_ref[:, AB_GLA_N + j * hd:AB_GLA_N + (j + 1) * hd])
    part = lambda j: dsa_ref.at[:, j * hd:(j + 1) * hd]
    _dsa_prep_kernel(part(0), part(1), part(2), part(3), gla_o.at[:, AB_SMALL:AB_SMALL + LANES],
                     pos_ref, qn_ref, kn_ref, freq_ref, q_o, k_o, v_o, iq_o, ik_o, iw_o)
    gla_o[:, :AB_SMALL] = _dot(hb, w_ref[:, :AB_SMALL])


def _proj_ab(x, nw, shift, scale, w_bf16, positions, q_norm, k_norm, *, tm, tk):
    bsz, s, d = x.shape
    hd = DSA_HEADS * DSA_HEAD_DIM
    half_a, half_i = DSA_HEAD_DIM // 2, IDX_DIM // 2
    inv_a = ROPE_THETA ** (-jnp.arange(half_a, dtype=F32) / half_a)
    inv_i = ROPE_THETA ** (-jnp.arange(half_i, dtype=F32) / half_i)
    freq = jnp.concatenate([inv_a, inv_i, inv_i]).reshape(1, LANES)
    rows = lambda n: pl.BlockSpec((None, tm, n), lambda b, i: (b, i, 0))
    vec = lambda n: pl.BlockSpec((1, n), lambda b, i: (0, 0))
    mod = pl.BlockSpec((None, 1, d), lambda b, i: (b, 0, 0))
    return pl.pallas_call(
        _proj_ab_kernel,
        grid=(bsz, s // tm),
        in_specs=[
            rows(d), vec(d), mod, mod,
            _resident((d, AB_GLA_N + AB_DSA_N), lambda b, i: (0, 0)),
            rows(1), vec(DSA_HEAD_DIM), vec(DSA_HEAD_DIM), vec(LANES),
        ],
        out_specs=[
            rows(AB_GLA_N), rows(hd), rows(hd),
            pl.BlockSpec((None, tm // tk, hd, tk), lambda b, i: (b, i, 0, 0)),
            pl.BlockSpec((None, IDX_HEADS, tm, IDX_DIM), lambda b, i: (b, 0, i, 0)),
            rows(IDX_DIM),
            pl.BlockSpec((None, IDX_HEADS, tm), lambda b, i: (b, 0, i)),
        ],
        out_shape=[
            jax.ShapeDtypeStruct((bsz, s, AB_GLA_N), F32),
            jax.ShapeDtypeStruct((bsz, s, hd), BF16),
            jax.ShapeDtypeStruct((bsz, s, hd), BF16),
            jax.ShapeDtypeStruct((bsz, s // tk, hd, tk), BF16),
            jax.ShapeDtypeStruct((bsz, IDX_HEADS, s, IDX_DIM), BF16),
            jax.ShapeDtypeStruct((bsz, s, IDX_DIM), BF16),
            jax.ShapeDtypeStruct((bsz, IDX_HEADS, s), F32),
        ],
        scratch_shapes=[pltpu.VMEM((tm, AB_DSA_N), F32)],
        compiler_params=_cparams(("arbitrary", "arbitrary")),
        name="proj_ab",
    )(x, nw.reshape(1, d), shift, scale, w_bf16, positions.reshape(bsz, s, 1),
      q_norm.reshape(1, -1), k_norm.reshape(1, -1), freq)


def _dsa_kernel(q_ref, iq_ref, iw_ref, k_ref, vt_ref, ik_ref, o_ref, sc_ref, top_ref, acc_ref,
                *, tq, n_sel, s_len):
    i = pl.program_id(1)
    nkb = i + 1
    kofs = lax.broadcasted_iota(I32, (tq, tq), 0)
    qidx = i * tq + lax.broadcasted_iota(I32, (tq, tq), 1)
    iw = iw_ref[...]
    idx_scale = (IDX_HEADS ** -0.5) * (IDX_DIM ** -0.5)

    def over_tiles(step, init, n=nkb, widest=2):
        carry = lax.fori_loop(0, n // widest, lambda j, c: step(widest * j, widest, c), init)
        rest = n % widest
        start = n - rest
        width = widest // 2
        while width >= 1:
            carry = lax.cond((rest & width) != 0, functools.partial(step, start, width), lambda c: c, carry)
            start = start + (rest & width)
            width //= 2
        return carry

    def as_float(key):
        key = jnp.maximum(key, KEY_LOWEST)
        key = jnp.where((key > 0) & (key < KEY_TINY), KEY_TINY, key)
        return pltpu.bitcast(key ^ ((key >> 31) & 0x7FFFFFFF), F32)

    def scores(kb, width, diagonal):
        k0 = pl.multiple_of(kb * tq, tq)
        rows = width * tq
        ikb = ik_ref[pl.ds(k0, rows), :]
        sc = jnp.zeros((rows, tq), F32)
        for h in range(IDX_HEADS):
            sc = sc + jnp.maximum(_dot_nt(ikb, iq_ref[h]), 0.0) * iw[h:h + 1, :]
        sc = sc * idx_scale
        sc = jnp.where(jnp.abs(sc) < FLT_TINY, 0.0, sc)
        if diagonal:
            sc = jnp.where(kofs <= qidx - i * tq, sc, -jnp.inf)
        sc_ref[pl.ds(kb, width)] = sc.reshape(width, tq, tq)
        top_ref[pl.ds(kb, width)] = sc.astype(BF16).reshape(width, tq, tq)

    def below_diagonal(kb, width, carry):
        scores(kb, width, False)
        return carry

    over_tiles(below_diagonal, 0, n=i)
    scores(i, 1, True)

    def count(pred):
        def body(kb, width, acc):
            rows = width * tq
            m = jnp.where(pred(sc_ref[pl.ds(kb, width)].reshape(rows, tq)), 1, 0)
            return acc + jnp.sum(m.reshape(rows // 8, 8, tq), axis=0)
        return jnp.sum(over_tiles(body, jnp.zeros((8, tq), I32)), axis=0, keepdims=True)

    n_valid = qidx[0:1, :] + 1

    def as_bf16(cand):
        k16 = jnp.maximum(cand, KEY_LOWEST_BF16) >> 16
        pattern = k16 ^ ((k16 >> 15) & 0x7FFF)
        pattern = jnp.where((pattern > 0) & (pattern < 0x80), 0x80, pattern)
        return pltpu.bitcast(pattern << 16, F32).astype(BF16)

    def count_top(cand):
        cand_bf = as_bf16(cand)

        def body(kb, width, acc):
            rows = width * tq
            top = top_ref[pl.ds(kb, width)].reshape(rows, tq)
            m = jnp.where(top >= cand_bf, jnp.ones((), BF16), jnp.zeros((), BF16))
            part = m[0:16]
            for j in range(1, rows // 16):
                part = part + m[16 * j:16 * (j + 1)]
            return acc + part.astype(F32)
        acc = over_tiles(body, jnp.zeros((16, tq), F32))
        return jnp.sum(acc, axis=0, keepdims=True).astype(I32)

    def top_step(it, u):
        cand_u = u | lax.shift_left(jnp.int32(1), jnp.int32(31) - it)
        return jnp.where(count_top(cand_u ^ INT_MIN) >= n_sel, cand_u, u)

    u = lax.fori_loop(0, 16, top_step, jnp.zeros((1, tq), I32))

    def any_lane(flag):
        return jnp.max(jnp.where(flag, 1.0, 0.0)) > 0.0

    base = u ^ INT_MIN
    base = jnp.where(base < 0, base | 0xFFFF, base)
    base = jnp.clip(base, KEY_LOWEST + BF16_HALF_STEP + 1, INT_MAX - 2 * BF16_HALF_STEP)
    many = n_valid > n_sel
    lo = jnp.where(many, base - BF16_HALF_STEP - 1, KEY_LOWEST)
    hi = jnp.where(many, base + 2 * BF16_HALF_STEP, INT_MAX)

    def count_two(pred_a, pred_b):
        def body(kb, width, acc):
            x = sc_ref[pl.ds(kb, width)].reshape(width * tq // 8, 8, tq)
            return (acc[0] + jnp.sum(jnp.where(pred_a(x), 1, 0), axis=0),
                    acc[1] + jnp.sum(jnp.where(pred_b(x), 1, 0), axis=0))
        a, b = over_tiles(body, (jnp.zeros((8, tq), I32), jnp.zeros((8, tq), I32)))
        return jnp.sum(a, axis=0, keepdims=True), jnp.sum(b, axis=0, keepdims=True)

    lo_f, hi_f = as_float(lo), as_float(hi)
    cnt_lo, cnt_hi = count_two(lambda x: x >= lo_f, lambda x: x >= hi_f)

    def open_queries(lo, cnt_lo, hi, done):
        return (cnt_lo > n_sel) & (hi - lo > 1) & (done == 0)

    def halve(s):
        lo, cnt_lo, hi, cnt_hi, done = s
        mid = lo + ((hi - lo) >> 1)
        mid_f = as_float(mid)
        cnt = count(lambda x: x >= mid_f)
        live = open_queries(lo, cnt_lo, hi, done)
        up = live & (cnt >= n_sel)
        down = live & (cnt < n_sel)
        return (jnp.where(up, mid, lo), jnp.where(up, cnt, cnt_lo),
                jnp.where(down, mid, hi), jnp.where(down, cnt, cnt_hi), done)

    st = lax.fori_loop(0, PEEL_AFTER_STEPS, lambda _, s: halve(s),
                       (lo, cnt_lo, hi, cnt_hi, jnp.zeros((1, tq), I32)))

    def peel(st):
        lo, cnt_lo, hi, cnt_hi, done = st
        lo_f, hi_f = as_float(lo), as_float(hi)

        def extremes(kb, width, carry):
            top, bot = carry
            x = sc_ref[pl.ds(kb, width)].reshape(width * tq // 8, 8, tq)
            top = jnp.maximum(top, jnp.max(jnp.where(x < hi_f, x, -jnp.inf), axis=0))
            bot = jnp.minimum(bot, jnp.min(jnp.where(x >= lo_f, x, jnp.inf), axis=0))
            return top, bot
        top, bot = over_tiles(extremes, (jnp.full((8, tq), -jnp.inf, F32), jnp.full((8, tq), jnp.inf, F32)))
        top = jnp.max(top, axis=0, keepdims=True)
        bot = jnp.min(bot, axis=0, keepdims=True)
        from_top = (cnt_hi == n_sel - 1) | (top == bot)
        from_bot = (cnt_lo == n_sel + 1) & jnp.logical_not(from_top)
        cnt_top, cnt_bot = count_two(lambda x: x >= top, lambda x: x > bot)
        cnt = jnp.where(from_top, cnt_top, cnt_bot)
        short = from_bot & (cnt < n_sel)
        hit = open_queries(lo, cnt_lo, hi, done) & (from_top | from_bot)
        thr = jnp.where(from_top, top, bot)
        incl = jnp.where(from_top | short, 1, 0)
        return hit, thr, incl, jnp.where(short, cnt_lo, cnt)

    def with_peel(st):
        hit, thr, incl, cnt = peel(st)
        st = st[:4] + (jnp.where(hit, 1, 0),)
        lo, cnt_lo, _, _, _ = lax.while_loop(lambda s: any_lane(open_queries(s[0], s[1], s[2], s[4])), halve, st)
        return jnp.where(hit, thr, as_float(lo)), jnp.where(hit, incl, 1), jnp.where(hit, cnt, cnt_lo)

    def without_peel(st):
        lo, cnt_lo, _, _, _ = st
        return as_float(lo), jnp.ones((1, tq), I32), cnt_lo

    thr, incl, cnt = lax.cond(any_lane(open_queries(st[0], st[1], st[2], st[4])), with_peel, without_peel, st)
    admits_equal = incl != 0
    excess = cnt > n_sel

    tied = jnp.max(jnp.where(excess, 1.0, 0.0)) > 0.0

    acc_ref[...] = jnp.zeros_like(acc_ref)

    def attend(select, seen):
        def block(kb, width, carry):
            ms, ls, seen = carry
            k0 = pl.multiple_of(kb * tq, tq)
            rows = width * tq
            sel, seen = select(sc_ref[pl.ds(kb, width)].reshape(rows, tq), seen)
            heads = [slice(h * DSA_HEAD_DIM, (h + 1) * DSA_HEAD_DIM) for h in range(DSA_HEADS)]
            st = [_dot_nt(k_ref[pl.ds(k0, rows), hs], q_ref[:, hs]) for hs in heads]
            st = [jnp.where(sel, x, MASK_BIAS) for x in st]
            new_m = [jnp.maximum(m, jnp.max(x, axis=0, keepdims=True)) for m, x in zip(ms, st)]
            alpha = [jnp.exp2(m - mn) for m, mn in zip(ms, new_m)]
            p = [jnp.exp2(x - mn) for x, mn in zip(st, new_m)]
            new_l = [a * l + jnp.sum(x, axis=0, keepdims=True) for a, l, x in zip(alpha, ls, p)]
            p = [x.astype(BF16) for x in p]
            for h, hs in enumerate(heads):
                pv = _dot(vt_ref[kb, hs, :], p[h][0:tq])
                for j in range(1, width):
                    pv = pv + _dot(vt_ref[kb + j, hs, :], p[h][j * tq:(j + 1) * tq])
                acc_ref[h] = alpha[h] * acc_ref[h] + pv
            return tuple(new_m), tuple(new_l), seen

        carry = (tuple(jnp.full((1, tq), -jnp.inf, F32) for _ in range(DSA_HEADS)),
                 tuple(jnp.zeros((1, tq), F32) for _ in range(DSA_HEADS)), seen)
        return over_tiles(block, carry)[1]

    def attend_tied():
        room = n_sel - count(lambda x: x > thr)
        room = jnp.where(excess, room, jnp.where(admits_equal, s_len, 0)).astype(F32)
        r = lax.broadcasted_iota(I32, (tq, tq), 0)
        c = lax.broadcasted_iota(I32, (tq, tq), 1)
        upto = jnp.where(c <= r, 1.0, 0.0).astype(BF16)

        def select(x, seen):
            equal = x == thr
            ones = jnp.where(equal, 1.0, 0.0).astype(BF16)
            ranks = []
            for j in range(x.shape[0] // tq):
                part = ones[j * tq:(j + 1) * tq]
                ranks.append(_dot(upto, part) + seen)
                seen = seen + jnp.sum(part.astype(F32), axis=0, keepdims=True)
            rank = ranks[0] if len(ranks) == 1 else jnp.concatenate(ranks, axis=0)
            return (x > thr) | (equal & (rank <= room)), seen
        return attend(select, jnp.zeros((1, tq), F32))

    at_least = jnp.where(admits_equal, thr, jnp.inf)
    above = jnp.where(admits_equal, jnp.inf, thr)
    ls = lax.cond(tied, attend_tied,
                  lambda: attend(lambda x, seen: ((x >= at_least) | (x > above), seen), jnp.zeros((1, 1), F32)))
    for h in range(DSA_HEADS):
        hs = slice(h * DSA_HEAD_DIM, (h + 1) * DSA_HEAD_DIM)
        o_ref[:, hs] = (acc_ref[h] / ls[h]).T.astype(o_ref.dtype)


def _dsa(q, k, vt, iq, ik, iw, *, tq):
    bsz, s, hd = q.shape
    n_sel = min(DSA_TOPK, s // 4)
    resident = lambda n: _resident((None, s, n), lambda b, i: (b, 0, 0))
    return pl.pallas_call(
        functools.partial(_dsa_kernel, tq=tq, n_sel=n_sel, s_len=s),
        grid=(bsz, s // tq),
        in_specs=[
            pl.BlockSpec((None, tq, hd), lambda b, i: (b, i, 0)),
            pl.BlockSpec((None, IDX_HEADS, tq, IDX_DIM), lambda b, i: (b, 0, i, 0)),
            pl.BlockSpec((None, IDX_HEADS, tq), lambda b, i: (b, 0, i)),
            resident(hd),
            _resident((None, s // tq, hd, tq), lambda b, i: (b, 0, 0, 0)),
            resident(IDX_DIM),
        ],
        out_specs=pl.BlockSpec((None, tq, hd), lambda b, i: (b, i, 0)),
        out_shape=jax.ShapeDtypeStruct((bsz, s, hd), BF16),
        scratch_shapes=[
            pltpu.VMEM((s // tq, tq, tq), F32),
            pltpu.VMEM((s // tq, tq, tq), BF16),
            pltpu.VMEM((DSA_HEADS, DSA_HEAD_DIM, tq), F32),
        ],
        compiler_params=_cparams(("arbitrary", "arbitrary"), DSA_VMEM_LIMIT),
        name="dsa",
    )(q, iq, iw, k, vt, ik)


def _mix_ffn_kernel(x_ref, a_ref, b_ref, wa_ref, wb_ref, g1_ref, nw_ref, sh_ref, sc_ref, g2_ref,
                    w1_ref, w2_ref, o_ref, *, tf):
    x = x_ref[...] + g1_ref[...] * (_dot(a_ref[...], wa_ref[...]) + _dot(b_ref[...], wb_ref[...]))
    hb = (_rms(x, nw_ref[...]) * (1.0 + sc_ref[...]) + sh_ref[...]).astype(BF16)
    f = w2_ref.shape[0]
    acc = jnp.zeros(x.shape, F32)
    for j in range(f // tf):
        gate = _dot(hb, w1_ref[:, j * tf:(j + 1) * tf])
        up = _dot(hb, w1_ref[:, f + j * tf:f + (j + 1) * tf])
        acc = acc + _dot((_silu(gate) * up).astype(BF16), w2_ref[j * tf:(j + 1) * tf, :])
    o_ref[...] = x + g2_ref[...] * acc


def _mix_ffn(x, mix_a, mix_b, w_out_bf16, gate1, nw, shift, scale, gate2, w1_bf16, w2_bf16, *, tm, tf):
    bsz, s, d = x.shape
    f = w2_bf16.shape[0]
    na = mix_a.shape[-1]
    assert mix_b.shape[-1] == na and w_out_bf16.shape[0] == 2 * na
    rows = lambda n: pl.BlockSpec((None, tm, n), lambda b, i: (b, i, 0))
    mod = pl.BlockSpec((None, 1, d), lambda b, i: (b, 0, 0))
    return pl.pallas_call(
        functools.partial(_mix_ffn_kernel, tf=tf),
        grid=(bsz, s // tm),
        in_specs=[
            rows(d), rows(na), rows(na),
            _resident((na, d), lambda b, i: (0, 0)),
            _resident((na, d), lambda b, i: (1, 0)),
            mod,
            pl.BlockSpec((1, d), lambda b, i: (0, 0)),
            mod, mod, mod,
            _resident((d, 2 * f), lambda b, i: (0, 0)),
            _resident((f, d), lambda b, i: (0, 0)),
        ],
        out_specs=rows(d),
        out_shape=jax.ShapeDtypeStruct((bsz, s, d), F32),
        compiler_params=_cparams(("arbitrary", "arbitrary")),
        name="mix_ffn",
    )(x, mix_a, mix_b, w_out_bf16, w_out_bf16, gate1, nw.reshape(1, d), shift, scale, gate2,
      w1_bf16, w2_bf16)


HALO = max(POOL_WINDOWS)


def _proj_cd_kernel(x_ref, nw_ref, sh_ref, sc_ref, w_ref, pw_ref, ps_ref, qn_ref, kn_ref,
                    pool_o, q_o, k_o, v_o, ext_ref, sb_ref):
    i = pl.program_id(1)
    tm = x_ref.shape[0]
    n = POOL_GROUPS * POOL_CH
    hb = (_rms(x_ref[...], nw_ref[...]) * (1.0 + sc_ref[...]) + sh_ref[...]).astype(BF16)

    @pl.when(i == 0)
    def _():
        ext_ref[0:HALO, :] = jnp.zeros((HALO, n), F32)

    ext_ref[HALO:HALO + tm, :] = _dot(hb, w_ref[:, :n])
    for j in range(3):
        sb_ref[:, j * n:(j + 1) * n] = _dot(hb, w_ref[:, (j + 1) * n:(j + 2) * n])

    t = i * tm + lax.broadcasted_iota(I32, (tm, 1), 0)
    for g, w in enumerate(POOL_WINDOWS):
        gs = slice(g * POOL_CH, (g + 1) * POOL_CH)
        u = ext_ref[HALO:HALO + tm, gs]
        tot = u
        for j in range(1, w):
            tot = tot + ext_ref[HALO - j:HALO - j + tm, gs]
        cnt = jnp.minimum(t + 1, w).astype(F32)
        pooled = tot / cnt - u
        y = _dot(pooled.astype(BF16), pw_ref[g].astype(BF16)) * ps_ref[:, gs]
        pool_o[:, gs] = y.astype(pool_o.dtype)
    ext_ref[0:HALO, :] = ext_ref[tm:tm + HALO, :]

    qn, kn = qn_ref[...], kn_ref[...]
    for h in range(SB_HEADS):
        hs = slice(h * SB_HEAD_DIM, (h + 1) * SB_HEAD_DIM)
        q_o[:, hs] = (_rms(sb_ref[:, hs], qn) * SB_HEAD_DIM ** -0.5).astype(q_o.dtype)
        k_o[:, hs] = _rms(sb_ref[:, n + h * SB_HEAD_DIM:n + (h + 1) * SB_HEAD_DIM], kn).astype(k_o.dtype)
    tk = v_o.shape[-1]
    for j in range(v_o.shape[0]):
        v_o[j] = sb_ref[j * tk:(j + 1) * tk, 2 * n:3 * n].T.astype(v_o.dtype)


def _proj_cd(x, nw, shift, scale, w_bf16, pool_w, pool_scale, q_norm, k_norm, *, tm, tk):
    bsz, s, d = x.shape
    n = POOL_GROUPS * POOL_CH
    assert w_bf16.shape[1] == 4 * n
    rows = lambda m: pl.BlockSpec((None, tm, m), lambda b, i: (b, i, 0))
    vec = lambda m: pl.BlockSpec((1, m), lambda b, i: (0, 0))
    mod = pl.BlockSpec((None, 1, d), lambda b, i: (b, 0, 0))
    return pl.pallas_call(
        _proj_cd_kernel,
        grid=(bsz, s // tm),
        in_specs=[
            rows(d), vec(d), mod, mod,
            _resident((d, 4 * n), lambda b, i: (0, 0)),
            pl.BlockSpec((POOL_GROUPS, POOL_CH, POOL_CH), lambda b, i: (0, 0, 0)),
            vec(n), vec(SB_HEAD_DIM), vec(SB_HEAD_DIM),
        ],
        out_specs=[rows(n)] * 3 + [pl.BlockSpec((None, tm // tk, n, tk), lambda b, i: (b, i, 0, 0))],
        out_shape=[jax.ShapeDtypeStruct((bsz, s, n), BF16)] * 3
        + [jax.ShapeDtypeStruct((bsz, s // tk, n, tk), BF16)],
        scratch_shapes=[pltpu.VMEM((HALO + tm, n), F32), pltpu.VMEM((tm, 3 * n), F32)],
        compiler_params=_cparams(("arbitrary", "arbitrary")),
        name="proj_cd",
    )(x, nw.reshape(1, d), shift, scale, w_bf16, pool_w, pool_scale.reshape(1, n),
      q_norm.reshape(1, -1), k_norm.reshape(1, -1))


def _sb_kernel(q_ref, k_ref, vt_ref, o_ref, acc_ref, *, tq):
    i = pl.program_id(1)
    kofs = lax.broadcasted_iota(I32, (tq, tq), 0)
    qofs = lax.broadcasted_iota(I32, (tq, tq), 1)
    later = jnp.where(qofs > kofs, 1.0, 0.0).astype(BF16)
    heads = [slice(h * SB_HEAD_DIM, (h + 1) * SB_HEAD_DIM) for h in range(SB_HEADS)]
    acc_ref[...] = jnp.zeros_like(acc_ref)

    def block(kb, runs, diagonal):
        k0 = pl.multiple_of(kb * tq, tq)
        z = [_dot_nt(k_ref[pl.ds(k0, tq), hs], q_ref[:, hs]) for hs in heads]
        sp = [jnp.log(1.0 + jnp.exp(-jnp.abs(x))) for x in z]
        log_beta = [jnp.minimum(x, 0.0) - s for x, s in zip(z, sp)]
        log_1m = [-jnp.maximum(x, 0.0) - s for x, s in zip(z, sp)]
        if diagonal:
            strict = kofs < qofs
            log_1m = [jnp.where(strict, x, 0.0) for x in log_1m]
        hi = [x.astype(BF16) for x in log_1m]
        lo = [(x - h.astype(F32)).astype(BF16) for x, h in zip(log_1m, hi)]
        after = [_dot(later, h) + _dot(later, l) + r for h, l, r in zip(hi, lo, runs)]
        w = [jnp.exp(lb + a) for lb, a in zip(log_beta, after)]
        if diagonal:
            w = [jnp.where(strict, x, 0.0) for x in w]
        for h, hs in enumerate(heads):
            acc_ref[h] = acc_ref[h] + _dot(vt_ref[kb, hs, :], w[h].astype(BF16))
        return tuple(r + jnp.sum(x, axis=0, keepdims=True) for r, x in zip(runs, log_1m))

    def alive(runs):
        worst = functools.reduce(jnp.maximum, runs)
        return jnp.max(worst) > SB_DEAD_LOG

    runs = block(i, tuple(jnp.zeros((1, tq), F32) for _ in heads), True)

    def earlier(state):
        kb, runs, _ = state
        runs = block(kb, runs, False)
        return kb - 1, runs, alive(runs)

    lax.while_loop(lambda st: (st[0] >= 0) & st[2], earlier, (i - 1, runs, alive(runs)))
    for h, hs in enumerate(heads):
        o_ref[:, hs] = acc_ref[h].T.astype(o_ref.dtype)


def _sb(q, k, vt, *, tq):
    bsz, s, hd = q.shape
    return pl.pallas_call(
        functools.partial(_sb_kernel, tq=tq),
        grid=(bsz, s // tq),
        in_specs=[
            pl.BlockSpec((None, tq, hd), lambda b, i: (b, i, 0)),
            _resident((None, s, hd), lambda b, i: (b, 0, 0)),
            _resident((None, s // tq, hd, tq), lambda b, i: (b, 0, 0, 0)),
        ],
        out_specs=pl.BlockSpec((None, tq, hd), lambda b, i: (b, i, 0)),
        out_shape=jax.ShapeDtypeStruct((bsz, s, hd), BF16),
        scratch_shapes=[pltpu.VMEM((SB_HEADS, SB_HEAD_DIM, tq), F32)],
        compiler_params=_cparams(("arbitrary", "arbitrary")),
        name="stick_breaking",
    )(q, k, vt)


def _pack_ab_weight(w):
    d = w.shape[0]
    gq, gk, gv, glow, gr, dq, dk, dv, iq, ik, iw = jnp.split(
        w, [256, 512, 1024, 1040, 1552, 2064, 2576, 3088, 3600, 3664], axis=1)
    pad = jnp.zeros((d, LANES - (IDX_DIM + GLA_GATE_RANK + IDX_HEADS)), w.dtype)
    return jnp.concatenate([gq, gk, gv, gr, ik, glow, iw, pad, dq, dk, dv, iq], axis=1)


def kernel(x, c, positions, ada_w, ada_b, mix_norm, ffn_norm, ffn_w1, ffn_w2, ab_w_in, gla_gate_up,
           gla_gate_b, gla_out_norm, dsa_q_norm, dsa_k_norm, ab_w_out, cd_w_in, pool_w, pool_scale,
           sb_q_norm, sb_k_norm, cd_w_out):
    bsz, s, d = x.shape
    depth = ada_w.shape[0]
    mod = _ada_mod(c, ada_w, ada_b).reshape(depth, bsz, 6, 1, d)
    tm = min(512, s)
    tq_dsa = min(512, s)
    tq_sb = min(256, s)
    for layer in range(depth):
        sh1, sc1, g1, sh2, sc2, g2 = (mod[layer, :, j] for j in range(6))
        i = layer // 2
        if layer % 2 == 0:
            proj, q, k, vt, iq, ik, iw = _proj_ab(
                x, mix_norm[layer], sh1, sc1, _pack_ab_weight(ab_w_in[i]).astype(BF16), positions,
                dsa_q_norm[i], dsa_k_norm[i], tm=tq_dsa, tk=tq_dsa)
            mix_a = _gla(proj, gla_gate_up[i], gla_gate_b[i], gla_out_norm[i], ts=tm)
            mix_b = _dsa(q, k, vt, iq, ik, iw, tq=tq_dsa)
            w_out = ab_w_out[i]
        else:
            mix_a, q, k, vt = _proj_cd(x, mix_norm[layer], sh1, sc1, cd_w_in[i].astype(BF16), pool_w[i],
                                       pool_scale[i], sb_q_norm[i], sb_k_norm[i], tm=tm, tk=tq_sb)
            mix_b = _sb(q, k, vt, tq=tq_sb)
            w_out = cd_w_out[i]
        x = _mix_ffn(x, mix_a, mix_b, w_out.astype(BF16), g1, ffn_norm[layer], sh2, sc2, g2,
                     ffn_w1[layer].astype(BF16), ffn_w2[layer].astype(BF16), tm=tm, tf=256)
    return x
```

```python
import functools

import jax
import jax.numpy as jnp
from jax import lax
from jax.experimental import pallas as pl
from jax.experimental.pallas import tpu as pltpu

F32 = jnp.float32
BF16 = jnp.bfloat16
I32 = jnp.int32

GLA_HEADS, GLA_DK, GLA_DV = 4, 64, 128
GLA_GATE_RANK = 16
GLA_GATE_TAU = 16.0
GLA_CHUNK = 64
DSA_HEADS, DSA_HEAD_DIM = 4, 128
IDX_HEADS, IDX_DIM = 8, 64
DSA_TOPK = 256
POOL_WINDOWS = (2, 4, 8, 16)
POOL_GROUPS, POOL_CH = 4, 128
SB_HEADS, SB_HEAD_DIM = 4, 128
ROPE_THETA = 10000.0
NORM_EPS = 1e-6

LANES = 128
VMEM_LIMIT = 56 * 1024 * 1024
DSA_VMEM_LIMIT = 61 * 1024 * 1024

AB_GQ, AB_GK, AB_GV, AB_GR = 0, 256, 512, 1024
AB_SMALL = 1536
SM_IK, SM_GLOW, SM_IW = 0, 64, 80
AB_GLA_N = AB_SMALL + LANES
AB_DSA_N = 4 * 512

DSA_Q_SCALE = 1.4426950408889634 * DSA_HEAD_DIM ** -0.5
INT_MIN = -2 ** 31
INT_MAX = 2 ** 31 - 1
FLT_TINY = 2.0 ** -126
KEY_TINY = 0x00800000
KEY_LOWEST = INT_MIN + 0x00800000
KEY_LOWEST_BF16 = INT_MIN + 0x00810000
BF16_HALF_STEP = 0x8000
PEEL_AFTER_STEPS = 5
MASK_BIAS = -1e30
SB_DEAD_LOG = -110.0


def _dot(a, b):
    return jnp.dot(a, b, preferred_element_type=F32)


def _dot_nt(a, b):
    return lax.dot_general(a, b, (((1,), (1,)), ((), ())), preferred_element_type=F32)


def _dot_tn(a, b):
    return lax.dot_general(a, b, (((0,), (0,)), ((), ())), preferred_element_type=F32)


def _split3(a):
    hi = a.astype(BF16)
    r1 = a - hi.astype(F32)
    mid = r1.astype(BF16)
    lo = (r1 - mid.astype(F32)).astype(BF16)
    return hi, mid, lo


def _silu(x):
    return x * jax.nn.sigmoid(x)


def _rms(x, w):
    var = jnp.mean(x * x, axis=-1, keepdims=True)
    return x * lax.rsqrt(var + NORM_EPS) * w


def _cparams(sem, vmem_limit=VMEM_LIMIT):
    return pltpu.CompilerParams(dimension_semantics=sem, vmem_limit_bytes=vmem_limit)


def _resident(block_shape, index_map):
    return pl.BlockSpec(block_shape, index_map, pipeline_mode=pl.Buffered(1))


def _ada_kernel(c_ref, w_ref, b_ref, o_ref):
    cond = _silu(c_ref[...])
    c_hi, c_mid, _ = _split3(cond)
    w_hi, w_mid, _ = _split3(w_ref[...])
    acc = _dot(c_hi, w_hi) + _dot(c_hi, w_mid) + _dot(c_mid, w_hi)
    o_ref[...] = acc + b_ref[...]


def _ada_mod(c, ada_w, ada_b):
    depth, d, n = ada_w.shape
    bsz = c.shape[0]
    rows = 16
    cp =jnp.zeros((rows, d), F32).at[:bsz].set(c)
    tn = 1536
    out = pl.pallas_call(
        _ada_kernel,
        grid=(depth, n // tn),
        in_specs=[
            pl.BlockSpec((rows, d), lambda l, j: (0, 0)),
            pl.BlockSpec((None, d, tn), lambda l, j: (l, 0, j)),
            pl.BlockSpec((None, 1, tn), lambda l, j: (l, 0, j)),
        ],
        out_specs=pl.BlockSpec((None, rows, tn), lambda l, j: (l, 0, j)),
        out_shape=jax.ShapeDtypeStruct((depth, rows, n), F32),
        compiler_params=_cparams(("arbitrary", "arbitrary")),
        name="ada_mod",
    )(cp, ada_w, ada_b.reshape(depth, 1, n))
    return out[:, :bsz]


def _gla_kernel(q_ref, k_ref, v_ref, gr_ref, sm_ref, gup_ref, gb_ref, onw_ref, o_ref, st_ref, tri_ref, *, ts):
    c = GLA_CHUNK

    @pl.when(pl.program_id(1) == 0)
    def _():
        st_ref[...] = jnp.zeros_like(st_ref)
        row = lax.broadcasted_iota(I32, (ts, ts), 0)
        col = lax.broadcasted_iota(I32, (ts, ts), 1)
        tri_ref[...] = jnp.where((row // c == col // c) & (col <= row), 1.0, 0.0).astype(BF16)

    nc = ts // c
    tri = tri_ref[...]
    causal = lax.broadcasted_iota(I32, (c, c), 1) <= lax.broadcasted_iota(I32, (c, c), 0)
    onw = onw_ref[...]

    glow = sm_ref[:, SM_GLOW:SM_GLOW + GLA_GATE_RANK]
    a = _dot(glow.astype(BF16), gup_ref[...].astype(BF16)) + gb_ref[...]
    g = (jnp.minimum(a, 0.0) - jnp.log1p(jnp.exp(-jnp.abs(a)))) / GLA_GATE_TAU
    g_hi, g_mid, g_lo = _split3(g)
    b = _dot(tri, g_hi) + _dot(tri, g_mid) + _dot(tri, g_lo)
    qs = q_ref[...] * (GLA_DK ** -0.5)
    k = k_ref[...]

    qe, ke, kd, qd, dec = [], [], [], [], []
    for ci in range(nc):
        r = slice(ci * c, (ci + 1) * c)
        bc = b[r]
        b_mid = bc[c // 2 - 1:c // 2, :]
        b_last = bc[c - 1:c, :]
        qe.append((qs[r] * jnp.exp(bc - b_mid)).astype(BF16))
        ke.append((k[r] * jnp.exp(b_mid - bc)).astype(BF16))
        kd.append((k[r] * jnp.exp(b_last - bc)).astype(BF16))
        qd.append((qs[r] * jnp.exp(bc)).astype(BF16))
        dec.append(jnp.exp(b_last))

    pairs = [(ci, h) for ci in range(nc) for h in range(GLA_HEADS)]
    ksl = lambda h: slice(h * GLA_DK, (h + 1) * GLA_DK)
    vsl = lambda h: slice(h * GLA_DV, (h + 1) * GLA_DV)
    vh = {(ci, h): v_ref[ci * c:(ci + 1) * c, vsl(h)].astype(BF16) for ci, h in pairs}
    att = {p: _dot_nt(qe[p[0]][:, ksl(p[1])], ke[p[0]][:, ksl(p[1])]) for p in pairs}
    att = {p: jnp.where(causal, att[p], 0.0).astype(BF16) for p in pairs}
    o = {p: _dot(att[p], vh[p]) for p in pairs}
    kvt = {p: _dot_tn(vh[p], kd[p[0]][:, ksl(p[1])]) for p in pairs}
    st_in = {}
    for h in range(GLA_HEADS):
        st = st_ref[h]
        for ci in range(nc):
            st_in[(ci, h)] = st.astype(BF16)
            st = st * dec[ci][:, ksl(h)] + kvt[(ci, h)]
        st_ref[h] = st
    for ci, h in pairs:
        out = o[(ci, h)] + _dot_nt(qd[ci][:, ksl(h)], st_in[(ci, h)])
        grh = gr_ref[ci * c:(ci + 1) * c, vsl(h)]
        o_ref[ci * c:(ci + 1) * c, vsl(h)] = (_rms(out, onw) * _silu(grh)).astype(o_ref.dtype)


def _gla(proj, gate_up, gate_b, out_norm, *, ts):
    bsz, s, _ = proj.shape
    hk, hv = GLA_HEADS * GLA_DK, GLA_HEADS * GLA_DV
    return pl.pallas_call(
        functools.partial(_gla_kernel, ts=ts),
        grid=(bsz, s // ts),
        in_specs=[
            pl.BlockSpec((None, ts, hk), lambda b, i: (b, i, AB_GQ // hk)),
            pl.BlockSpec((None, ts, hk), lambda b, i: (b, i, AB_GK // hk)),
            pl.BlockSpec((None, ts, hv), lambda b, i: (b, i, AB_GV // hv)),
            pl.BlockSpec((None, ts, hv), lambda b, i: (b, i, AB_GR // hv)),
            pl.BlockSpec((None, ts, LANES), lambda b, i: (b, i, AB_SMALL // LANES)),
            pl.BlockSpec((GLA_GATE_RANK, hk), lambda b, i: (0, 0)),
            pl.BlockSpec((1, hk), lambda b, i: (0, 0)),
            pl.BlockSpec((1, GLA_DV), lambda b, i: (0, 0)),
        ],
        out_specs=pl.BlockSpec((None, ts, hv), lambda b, i: (b, i, 0)),
        out_shape=jax.ShapeDtypeStruct((bsz, s, hv), BF16),
        scratch_shapes=[pltpu.VMEM((GLA_HEADS, GLA_DV, GLA_DK), F32), pltpu.VMEM((ts, ts), BF16)],
        compiler_params=_cparams(("arbitrary", "arbitrary")),
        name="gla",
    )(proj, proj, proj, proj, proj, gate_up, gate_b.reshape(1, hk), out_norm.reshape(1, GLA_DV))


def _dsa_prep_kernel(dq_ref, dk_ref, dv_ref, iq_ref, sm_ref, pos_ref, qn_ref, kn_ref, freq_ref,
                     q_o, k_o, v_o, iq_o, ik_o, iw_o):
    pos = pos_ref[...].astype(F32)
    lane = lax.broadcasted_iota(I32, (1, LANES), 1)
    half_a, half_i = DSA_HEAD_DIM // 2, IDX_DIM // 2

    ang = pos * freq_ref[...]
    cos_t, sin_t = jnp.cos(ang), jnp.sin(ang)

    def attn_table(t):
        return jnp.where(lane < half_a, t, pltpu.roll(t, half_a, 1))

    def idx_table(t):
        return jnp.where(lane < half_i, pltpu.roll(t, half_a, 1),
                         jnp.where(lane < 2 * half_i, pltpu.roll(t, half_a + half_i, 1), t))

    cos_a = attn_table(cos_t)
    sin_a = jnp.where(lane < half_a, -1.0, 1.0) * attn_table(sin_t)

    def rope_attn(t):
        return t * cos_a + pltpu.roll(t, half_a, 1) * sin_a

    first = (lane % IDX_DIM) < half_i
    cos_i = idx_table(cos_t)
    sin_i = jnp.where(first, -1.0, 1.0) * idx_table(sin_t)

    def rope_idx(t):
        rot = jnp.where(first, pltpu.roll(t, LANES - IDX_DIM // 2, 1), pltpu.roll(t, IDX_DIM // 2, 1))
        return t * cos_i + rot * sin_i

    qn, kn = qn_ref[...], kn_ref[...]
    for h in range(DSA_HEADS):
        hs = slice(h * DSA_HEAD_DIM, (h + 1) * DSA_HEAD_DIM)
        q_o[:, hs] = (rope_attn(_rms(dq_ref[:, hs], qn)) * DSA_Q_SCALE).astype(q_o.dtype)
        k_o[:, hs] = rope_attn(_rms(dk_ref[:, hs], kn)).astype(k_o.dtype)
    tk = v_o.shape[-1]
    for j in range(v_o.shape[0]):
        v_o[j] = dv_ref[j * tk:(j + 1) * tk, :].T.astype(v_o.dtype)
    for j in range(IDX_HEADS * IDX_DIM // LANES):
        r = rope_idx(iq_ref[:, j * LANES:(j + 1) * LANES])
        iq_o[2 * j] = r[:, :IDX_DIM].astype(iq_o.dtype)
        iq_o[2 * j + 1] = r[:, IDX_DIM:].astype(iq_o.dtype)
    sm = sm_ref[...]
    ik_o[...] = rope_idx(sm)[:, SM_IK:SM_IK + IDX_DIM].astype(ik_o.dtype)
    iw_o[...] = sm.T[SM_IW:SM_IW + IDX_HEADS, :]


def _proj_ab_kernel(x_ref, nw_ref, sh_ref, sc_ref, w_ref, pos_ref, qn_ref, kn_ref, freq_ref,
                    gla_o, q_o, k_o, v_o, iq_o, ik_o, iw_o, dsa_ref):
    hb = (_rms(x_ref[...], nw_ref[...]) * (1.0 + sc_ref[...]) + sh_ref[...]).astype(BF16)
    hd = DSA_HEADS * DSA_HEAD_DIM
    gla_o[:, AB_SMALL:] = _dot(hb, w_ref[:, AB_SMALL:AB_GLA_N])
    for j in range(AB_DSA_N // hd):
        dsa_ref[:, j * hd:(j + 1) * hd] = _dot(hb, w_ref[:, AB_GLA_N + j * hd:AB_GLA_N + (j + 1) * hd])
    part = lambda j: dsa_ref.at[:, j * hd:(j + 1) * hd]
    _dsa_prep_kernel(part(0), part(1), part(2), part(3), gla_o.at[:, AB_SMALL:AB_SMALL + LANES],
                     pos_ref, qn_ref, kn_ref, freq_ref, q_o, k_o, v_o, iq_o, ik_o, iw_o)
    gla_o[:, :AB_SMALL] = _dot(hb, w_ref[:, :AB_SMALL])


def _proj_ab(x, nw, shift, scale, w_bf16, positions, q_norm, k_norm, *, tm, tk):
    bsz, s, d = x.shape
    hd = DSA_HEADS * DSA_HEAD_DIM
    half_a, half_i = DSA_HEAD_DIM // 2, IDX_DIM // 2
    inv_a = ROPE_THETA ** (-jnp.arange(half_a, dtype=F32) / half_a)
    inv_i = ROPE_THETA ** (-jnp.arange(half_i, dtype=F32) / half_i)
    freq = jnp.concatenate([inv_a, inv_i, inv_i]).reshape(1, LANES)
    rows = lambda n: pl.BlockSpec((None, tm, n), lambda b, i: (b, i, 0))
    vec = lambda n: pl.BlockSpec((1, n), lambda b, i: (0, 0))
    mod = pl.BlockSpec((None, 1, d), lambda b, i: (b, 0, 0))
    return pl.pallas_call(
        _proj_ab_kernel,
        grid=(bsz, s // tm),
        in_specs=[
            rows(d), vec(d), mod, mod,
            _resident((d, AB_GLA_N + AB_DSA_N), lambda b, i: (0, 0)),
            rows(1), vec(DSA_HEAD_DIM), vec(DSA_HEAD_DIM), vec(LANES),
        ],
        out_specs=[
            rows(AB_GLA_N), rows(hd), rows(hd),
            pl.BlockSpec((None, tm // tk, hd, tk), lambda b, i: (b, i, 0, 0)),
            pl.BlockSpec((None, IDX_HEADS, tm, IDX_DIM), lambda b, i: (b, 0, i, 0)),
            rows(IDX_DIM),
            pl.BlockSpec((None, IDX_HEADS, tm), lambda b, i: (b, 0, i)),
        ],
        out_shape=[
            jax.ShapeDtypeStruct((bsz, s, AB_GLA_N), F32),
            jax.ShapeDtypeStruct((bsz, s, hd), BF16),
            jax.ShapeDtypeStruct((bsz, s, hd), BF16),
            jax.ShapeDtypeStruct((bsz, s // tk, hd, tk), BF16),
            jax.ShapeDtypeStruct((bsz, IDX_HEADS, s, IDX_DIM), BF16),
            jax.ShapeDtypeStruct((bsz, s, IDX_DIM), BF16),
            jax.ShapeDtypeStruct((bsz, IDX_HEADS, s), F32),
        ],
        scratch_shapes=[pltpu.VMEM((tm, AB_DSA_N), F32)],
        compiler_params=_cparams(("arbitrary", "arbitrary")),
        name="proj_ab",
    )(x, nw.reshape(1, d), shift, scale, w_bf16, positions.reshape(bsz, s, 1),
      q_norm.reshape(1, -1), k_norm.reshape(1, -1), freq)


def _dsa_kernel(q_ref, iq_ref, iw_ref, k_ref, vt_ref, ik_ref, o_ref, sc_ref, top_ref, acc_ref,
                *, tq, n_sel, s_len):
    i = pl.program_id(1)
    nkb = i + 1
    kofs = lax.broadcasted_iota(I32, (tq, tq), 0)
    qidx = i * tq + lax.broadcasted_iota(I32, (tq, tq), 1)
    iw = iw_ref[...]
    idx_scale = (IDX_HEADS ** -0.5) * (IDX_DIM ** -0.5)

    def over_tiles(step, init, n=nkb, widest=2):
        carry = lax.fori_loop(0, n // widest, lambda j, c: step(widest * j, widest, c), init)
        rest = n % widest
        start = n - rest
        width = widest // 2
        while width >= 1:
            carry = lax.cond((rest & width) != 0, functools.partial(step, start, width), lambda c: c, carry)
            start = start + (rest & width)
            width //= 2
        return carry

    def as_float(key):
        key = jnp.maximum(key, KEY_LOWEST)
        key = jnp.where((key > 0) & (key < KEY_TINY), KEY_TINY, key)
        return pltpu.bitcast(key ^ ((key >> 31) & 0x7FFFFFFF), F32)

    def scores(kb, width, diagonal):
        k0 = pl.multiple_of(kb * tq, tq)
        rows = width * tq
        ikb = ik_ref[pl.ds(k0, rows), :]
        sc = jnp.zeros((rows, tq), F32)
        for h in range(IDX_HEADS):
            sc = sc + jnp.maximum(_dot_nt(ikb, iq_ref[h]), 0.0) * iw[h:h + 1, :]
        sc = sc * idx_scale
        sc = jnp.where(jnp.abs(sc) < FLT_TINY, 0.0, sc)
        if diagonal:
            sc = jnp.where(kofs <= qidx - i * tq, sc, -jnp.inf)
        sc_ref[pl.ds(kb, width)] = sc.reshape(width, tq, tq)
        top_ref[pl.ds(kb, width)] = sc.astype(BF16).reshape(width, tq, tq)

    def below_diagonal(kb, width, carry):
        scores(kb, width, False)
        return carry

    over_tiles(below_diagonal, 0, n=i)
    scores(i, 1, True)

    def count(pred):
        def body(kb, width, acc):
            rows = width * tq
            m = jnp.where(pred(sc_ref[pl.ds(kb, width)].reshape(rows, tq)), 1, 0)
            return acc + jnp.sum(m.reshape(rows // 8, 8, tq), axis=0)
        return jnp.sum(over_tiles(body, jnp.zeros((8, tq), I32)), axis=0, keepdims=True)

    n_valid = qidx[0:1, :] + 1

    def as_bf16(cand):
        k16 = jnp.maximum(cand, KEY_LOWEST_BF16) >> 16
        pattern = k16 ^ ((k16 >> 15) & 0x7FFF)
        pattern = jnp.where((pattern > 0) & (pattern < 0x80), 0x80, pattern)
        return pltpu.bitcast(pattern << 16, F32).astype(BF16)

    def count_top(cand):
        cand_bf = as_bf16(cand)

        def body(kb, width, acc):
            rows = width * tq
            top = top_ref[pl.ds(kb, width)].reshape(rows, tq)
            m = jnp.where(top >= cand_bf, jnp.ones((), BF16), jnp.zeros((), BF16))
            part = m[0:16]
            for j in range(1, rows // 16):
                part = part + m[16 * j:16 * (j + 1)]
            return acc + part.astype(F32)
        acc = over_tiles(body, jnp.zeros((16, tq), F32))
        return jnp.sum(acc, axis=0, keepdims=True).astype(I32)

    def top_step(it, u):
        cand_u = u | lax.shift_left(jnp.int32(1), jnp.int32(31) - it)
        return jnp.where(count_top(cand_u ^ INT_MIN) >= n_sel, cand_u, u)

    u = lax.fori_loop(0, 16, top_step, jnp.zeros((1, tq), I32))

    def any_lane(flag):
        return jnp.max(jnp.where(flag, 1.0, 0.0)) > 0.0

    base = u ^ INT_MIN
    base = jnp.where(base < 0, base | 0xFFFF, base)
    base = jnp.clip(base, KEY_LOWEST + BF16_HALF_STEP + 1, INT_MAX - 2 * BF16_HALF_STEP)
    many = n_valid > n_sel
    lo = jnp.where(many, base - BF16_HALF_STEP - 1, KEY_LOWEST)
    hi = jnp.where(many, base + 2 * BF16_HALF_STEP, INT_MAX)

    def count_two(pred_a, pred_b):
        def body(kb, width, acc):
            x = sc_ref[pl.ds(kb, width)].reshape(width * tq // 8, 8, tq)
            return (acc[0] + jnp.sum(jnp.where(pred_a(x), 1, 0), axis=0),
                    acc[1] + jnp.sum(jnp.where(pred_b(x), 1, 0), axis=0))
        a, b = over_tiles(body, (jnp.zeros((8, tq), I32), jnp.zeros((8, tq), I32)))
        return jnp.sum(a, axis=0, keepdims=True), jnp.sum(b, axis=0, keepdims=True)

    lo_f, hi_f = as_float(lo), as_float(hi)
    cnt_lo, cnt_hi = count_two(lambda x: x >= lo_f, lambda x: x >= hi_f)

    def open_queries(lo, cnt_lo, hi, done):
        return (cnt_lo > n_sel) & (hi - lo > 1) & (done == 0)

    def halve(s):
        lo, cnt_lo, hi, cnt_hi, done = s
        mid = lo + ((hi - lo) >> 1)
        mid_f = as_float(mid)
        cnt = count(lambda x: x >= mid_f)
        live = open_queries(lo, cnt_lo, hi, done)
        up = live & (cnt >= n_sel)
        down = live & (cnt < n_sel)
        return (jnp.where(up, mid, lo), jnp.where(up, cnt, cnt_lo),
                jnp.where(down, mid, hi), jnp.where(down, cnt, cnt_hi), done)

    st = lax.fori_loop(0, PEEL_AFTER_STEPS, lambda _, s: halve(s),
                       (lo, cnt_lo, hi, cnt_hi, jnp.zeros((1, tq), I32)))

    def peel(st):
        lo, cnt_lo, hi, cnt_hi, done = st
        lo_f, hi_f = as_float(lo), as_float(hi)

        def extremes(kb, width, carry):
            top, bot = carry
            x = sc_ref[pl.ds(kb, width)].reshape(width * tq // 8, 8, tq)
            top = jnp.maximum(top, jnp.max(jnp.where(x < hi_f, x, -jnp.inf), axis=0))
            bot = jnp.minimum(bot, jnp.min(jnp.where(x >= lo_f, x, jnp.inf), axis=0))
            return top, bot
        top, bot = over_tiles(extremes, (jnp.full((8, tq), -jnp.inf, F32), jnp.full((8, tq), jnp.inf, F32)))
        top = jnp.max(top, axis=0, keepdims=True)
        bot = jnp.min(bot, axis=0, keepdims=True)
        from_top = (cnt_hi == n_sel - 1) | (top == bot)
        from_bot = (cnt_lo == n_sel + 1) & jnp.logical_not(from_top)
        cnt_top, cnt_bot = count_two(lambda x: x >= top, lambda x: x > bot)
        cnt = jnp.where(from_top, cnt_top, cnt_bot)
        short = from_bot & (cnt < n_sel)
        hit = open_queries(lo, cnt_lo, hi, done) & (from_top | from_bot)
        thr = jnp.where(from_top, top, bot)
        incl = jnp.where(from_top | short, 1, 0)
        return hit, thr, incl, jnp.where(short, cnt_lo, cnt)

    def with_peel(st):
        hit, thr, incl, cnt = peel(st)
        st = st[:4] + (jnp.where(hit, 1, 0),)
        lo, cnt_lo, _, _, _ = lax.while_loop(lambda s: any_lane(open_queries(s[0], s[1], s[2], s[4])), halve, st)
        return jnp.where(hit, thr, as_float(lo)), jnp.where(hit, incl, 1), jnp.where(hit, cnt, cnt_lo)

    def without_peel(st):
        lo, cnt_lo, _, _, _ = st
        return as_float(lo), jnp.ones((1, tq), I32), cnt_lo

    thr, incl, cnt = lax.cond(any_lane(open_queries(st[0], st[1], st[2], st[4])), with_peel, without_peel, st)
    admits_equal = incl != 0
    excess = cnt > n_sel

    tied = jnp.max(jnp.where(excess, 1.0, 0.0)) > 0.0

    acc_ref[...] = jnp.zeros_like(acc_ref)

    def attend(select, seen):
        def block(kb, width, carry):
            ms, ls, seen = carry
            k0 = pl.multiple_of(kb * tq, tq)
            rows = width * tq
            sel, seen = select(sc_ref[pl.ds(kb, width)].reshape(rows, tq), seen)
            heads = [slice(h * DSA_HEAD_DIM, (h + 1) * DSA_HEAD_DIM) for h in range(DSA_HEADS)]
            st = [_dot_nt(k_ref[pl.ds(k0, rows), hs], q_ref[:, hs]) for hs in heads]
            bias = jnp.where(sel, 0.0, MASK_BIAS)
            st = [x + bias for x in st]
            new_m = [jnp.maximum(m, jnp.max(x, axis=0, keepdims=True)) for m, x in zip(ms, st)]
            alpha = [jnp.exp2(m - mn) for m, mn in zip(ms, new_m)]
            p = [jnp.exp2(x - mn) for x, mn in zip(st, new_m)]
            new_l = [a * l + jnp.sum(x, axis=0, keepdims=True) for a, l, x in zip(alpha, ls, p)]
            p = [x.astype(BF16) for x in p]
            for h, hs in enumerate(heads):
                pv = _dot(vt_ref[kb, hs, :], p[h][0:tq])
                for j in range(1, width):
                    pv = pv + _dot(vt_ref[kb + j, hs, :], p[h][j * tq:(j + 1) * tq])
                acc_ref[h] = alpha[h] * acc_ref[h] + pv
            return tuple(new_m), tuple(new_l), seen

        carry = (tuple(jnp.full((1, tq), -jnp.inf, F32) for _ in range(DSA_HEADS)),
                 tuple(jnp.zeros((1, tq), F32) for _ in range(DSA_HEADS)), seen)
        return over_tiles(block, carry)[1]

    def attend_tied():
        room = n_sel - count(lambda x: x > thr)
        room = jnp.where(excess, room, jnp.where(admits_equal, s_len, 0)).astype(F32)
        r = lax.broadcasted_iota(I32, (tq, tq), 0)
        c = lax.broadcasted_iota(I32, (tq, tq), 1)
        upto = jnp.where(c <= r, 1.0, 0.0).astype(BF16)

        def select(x, seen):
            equal = x == thr
            ones = jnp.where(equal, 1.0, 0.0).astype(BF16)
            fits = []
            for j in range(x.shape[0] // tq):
                rank = _dot(upto, ones[j * tq:(j + 1) * tq])
                fits.append(rank <= room - seen)
                seen = seen + rank[tq - 1:tq, :]
            fits = fits[0] if len(fits) == 1 else jnp.concatenate(fits, axis=0)
            return (x > thr) | (equal & fits), seen
        return attend(select, jnp.zeros((1, tq), F32))

    at_least = jnp.where(admits_equal, thr, jnp.inf)
    above = jnp.where(admits_equal, jnp.inf, thr)
    ls = lax.cond(tied, attend_tied,
                  lambda: attend(lambda x, seen: ((x >= at_least) | (x > above), seen), jnp.zeros((1, 1), F32)))
    for h in range(DSA_HEADS):
        hs = slice(h * DSA_HEAD_DIM, (h + 1) * DSA_HEAD_DIM)
        o_ref[:, hs] = (acc_ref[h] / ls[h]).T.astype(o_ref.dtype)


def _dsa(q, k, vt, iq, ik, iw, *, tq):
    bsz, s, hd = q.shape
    n_sel = min(DSA_TOPK, s // 4)
    resident = lambda n: _resident((None, s, n), lambda b, i: (b, 0, 0))
    return pl.pallas_call(
        functools.partial(_dsa_kernel, tq=tq, n_sel=n_sel, s_len=s),
        grid=(bsz, s // tq),
        in_specs=[
            pl.BlockSpec((None, tq, hd), lambda b, i: (b, i, 0)),
            pl.BlockSpec((None, IDX_HEADS, tq, IDX_DIM), lambda b, i: (b, 0, i, 0)),
            pl.BlockSpec((None, IDX_HEADS, tq), lambda b, i: (b, 0, i)),
            resident(hd),
            _resident((None, s // tq, hd, tq), lambda b, i: (b, 0, 0, 0)),
            resident(IDX_DIM),
        ],
        out_specs=pl.BlockSpec((None, tq, hd), lambda b, i: (b, i, 0)),
        out_shape=jax.ShapeDtypeStruct((bsz, s, hd), BF16),
        scratch_shapes=[
            pltpu.VMEM((s // tq, tq, tq), F32),
            pltpu.VMEM((s // tq, tq, tq), BF16),
            pltpu.VMEM((DSA_HEADS, DSA_HEAD_DIM, tq), F32),
        ],
        compiler_params=_cparams(("arbitrary", "arbitrary"), DSA_VMEM_LIMIT),
        name="dsa",
    )(q, iq, iw, k, vt, ik)


def _mix_ffn_kernel(x_ref, a_ref, b_ref, wa_ref, wb_ref, g1_ref, nw_ref, sh_ref, sc_ref, g2_ref,
                    w1_ref, w2_ref, o_ref, *, tf):
    x = x_ref[...] + g1_ref[...] * (_dot(a_ref[...], wa_ref[...]) + _dot(b_ref[...], wb_ref[...]))
    hb = (_rms(x, nw_ref[...]) * (1.0 + sc_ref[...]) + sh_ref[...]).astype(BF16)
    f = w2_ref.shape[0]
    acc = jnp.zeros(x.shape, F32)
    for j in range(f // tf):
        gate = _dot(hb, w1_ref[:, j * tf:(j + 1) * tf])
        up = _dot(hb, w1_ref[:, f + j * tf:f + (j + 1) * tf])
        acc = acc + _dot((_silu(gate) * up).astype(BF16), w2_ref[j * tf:(j + 1) * tf, :])
    o_ref[...] = x + g2_ref[...] * acc


def _mix_ffn(x, mix_a, mix_b, w_out_bf16, gate1, nw, shift, scale, gate2, w1_bf16, w2_bf16, *, tm, tf):
    bsz, s, d = x.shape
    f = w2_bf16.shape[0]
    na = mix_a.shape[-1]
    assert mix_b.shape[-1] == na and w_out_bf16.shape[0] == 2 * na
    rows = lambda n: pl.BlockSpec((None, tm, n), lambda b, i: (b, i, 0))
    mod = pl.BlockSpec((None, 1, d), lambda b, i: (b, 0, 0))
    return pl.pallas_call(
        functools.partial(_mix_ffn_kernel, tf=tf),
        grid=(bsz, s // tm),
        in_specs=[
            rows(d), rows(na), rows(na),
            _resident((na, d), lambda b, i: (0, 0)),
            _resident((na, d), lambda b, i: (1, 0)),
            mod,
            pl.BlockSpec((1, d), lambda b, i: (0, 0)),
            mod, mod, mod,
            _resident((d, 2 * f), lambda b, i: (0, 0)),
            _resident((f, d), lambda b, i: (0, 0)),
        ],
        out_specs=rows(d),
        out_shape=jax.ShapeDtypeStruct((bsz, s, d), F32),
        compiler_params=_cparams(("arbitrary", "arbitrary")),
        name="mix_ffn",
    )(x, mix_a, mix_b, w_out_bf16, w_out_bf16, gate1, nw.reshape(1, d), shift, scale, gate2,
      w1_bf16, w2_bf16)


HALO = max(POOL_WINDOWS)


def _proj_cd_kernel(x_ref, nw_ref, sh_ref, sc_ref, w_ref, pw_ref, ps_ref, qn_ref, kn_ref,
                    pool_o, q_o, k_o, v_o, ext_ref, sb_ref):
    i = pl.program_id(1)
    tm = x_ref.shape[0]
    n = POOL_GROUPS * POOL_CH
    hb = (_rms(x_ref[...], nw_ref[...]) * (1.0 + sc_ref[...]) + sh_ref[...]).astype(BF16)

    @pl.when(i == 0)
    def _():
        ext_ref[0:HALO, :] = jnp.zeros((HALO, n), F32)

    ext_ref[HALO:HALO + tm, :] = _dot(hb, w_ref[:, :n])
    for j in range(3):
        sb_ref[:, j * n:(j + 1) * n] = _dot(hb, w_ref[:, (j + 1) * n:(j + 2) * n])

    t = i * tm + lax.broadcasted_iota(I32, (tm, 1), 0)
    for g, w in enumerate(POOL_WINDOWS):
        gs = slice(g * POOL_CH, (g + 1) * POOL_CH)
        u = ext_ref[HALO:HALO + tm, gs]
        tot = u
        for j in range(1, w):
            tot = tot + ext_ref[HALO - j:HALO - j + tm, gs]
        cnt = jnp.minimum(t + 1, w).astype(F32)
        pooled = tot / cnt - u
        y = _dot(pooled.astype(BF16), pw_ref[g].astype(BF16)) * ps_ref[:, gs]
        pool_o[:, gs] = y.astype(pool_o.dtype)
    ext_ref[0:HALO, :] = ext_ref[tm:tm + HALO, :]

    qn, kn = qn_ref[...], kn_ref[...]
    for h in range(SB_HEADS):
        hs = slice(h * SB_HEAD_DIM, (h + 1) * SB_HEAD_DIM)
        q_o[:, hs] = (_rms(sb_ref[:, hs], qn) * SB_HEAD_DIM ** -0.5).astype(q_o.dtype)
        k_o[:, hs] = _rms(sb_ref[:, n + h * SB_HEAD_DIM:n + (h + 1) * SB_HEAD_DIM], kn).astype(k_o.dtype)
    tk = v_o.shape[-1]
    for j in range(v_o.shape[0]):
        v_o[j] = sb_ref[j * tk:(j + 1) * tk, 2 * n:3 * n].T.astype(v_o.dtype)


def _proj_cd(x, nw, shift, scale, w_bf16, pool_w, pool_scale, q_norm, k_norm, *, tm, tk):
    bsz, s, d = x.shape
    n = POOL_GROUPS * POOL_CH
    assert w_bf16.shape[1] == 4 * n
    rows = lambda m: pl.BlockSpec((None, tm, m), lambda b, i: (b, i, 0))
    vec = lambda m: pl.BlockSpec((1, m), lambda b, i: (0, 0))
    mod = pl.BlockSpec((None, 1, d), lambda b, i: (b, 0, 0))
    return pl.pallas_call(
        _proj_cd_kernel,
        grid=(bsz, s // tm),
        in_specs=[
            rows(d), vec(d), mod, mod,
            _resident((d, 4 * n), lambda b, i: (0, 0)),
            pl.BlockSpec((POOL_GROUPS, POOL_CH, POOL_CH), lambda b, i: (0, 0, 0)),
            vec(n), vec(SB_HEAD_DIM), vec(SB_HEAD_DIM),
        ],
        out_specs=[rows(n)] * 3 + [pl.BlockSpec((None, tm // tk, n, tk), lambda b, i: (b, i, 0, 0))],
        out_shape=[jax.ShapeDtypeStruct((bsz, s, n), BF16)] * 3
        + [jax.ShapeDtypeStruct((bsz, s // tk, n, tk), BF16)],
        scratch_shapes=[pltpu.VMEM((HALO + tm, n), F32), pltpu.VMEM((tm, 3 * n), F32)],
        compiler_params=_cparams(("arbitrary", "arbitrary")),
        name="proj_cd",
    )(x, nw.reshape(1, d), shift, scale, w_bf16, pool_w, pool_scale.reshape(1, n),
      q_norm.reshape(1, -1), k_norm.reshape(1, -1))


def _sb_kernel(q_ref, k_ref, vt_ref, o_ref, acc_ref, *, tq):
    i = pl.program_id(1)
    kofs = lax.broadcasted_iota(I32, (tq, tq), 0)
    qofs = lax.broadcasted_iota(I32, (tq, tq), 1)
    later = jnp.where(qofs > kofs, 1.0, 0.0).astype(BF16)
    heads = [slice(h * SB_HEAD_DIM, (h + 1) * SB_HEAD_DIM) for h in range(SB_HEADS)]
    acc_ref[...] = jnp.zeros_like(acc_ref)

    def block(kb, runs, diagonal):
        k0 = pl.multiple_of(kb * tq, tq)
        z = [_dot_nt(k_ref[pl.ds(k0, tq), hs], q_ref[:, hs]) for hs in heads]
        sp = [jnp.log(1.0 + jnp.exp(-jnp.abs(x))) for x in z]
        log_beta = [jnp.minimum(x, 0.0) - s for x, s in zip(z, sp)]
        log_1m = [-jnp.maximum(x, 0.0) - s for x, s in zip(z, sp)]
        if diagonal:
            strict = kofs < qofs
            log_1m = [jnp.where(strict, x, 0.0) for x in log_1m]
        hi = [x.astype(BF16) for x in log_1m]
        lo = [(x - h.astype(F32)).astype(BF16) for x, h in zip(log_1m, hi)]
        after = [_dot(later, h) + _dot(later, l) + r for h, l, r in zip(hi, lo, runs)]
        w = [jnp.exp(lb + a) for lb, a in zip(log_beta, after)]
        if diagonal:
            w = [jnp.where(strict, x, 0.0) for x in w]
        for h, hs in enumerate(heads):
            acc_ref[h] = acc_ref[h] + _dot(vt_ref[kb, hs, :], w[h].astype(BF16))
        return tuple(r + jnp.sum(x, axis=0, keepdims=True) for r, x in zip(runs, log_1m))

    def alive(runs):
        worst = functools.reduce(jnp.maximum, runs)
        return jnp.max(worst) > SB_DEAD_LOG

    runs = block(i, tuple(jnp.zeros((1, tq), F32) for _ in heads), True)

    def earlier(state):
        kb, runs, _ = state
        runs = block(kb, runs, False)
        return kb - 1, runs, alive(runs)

    lax.while_loop(lambda st: (st[0] >= 0) & st[2], earlier, (i - 1, runs, alive(runs)))
    for h, hs in enumerate(heads):
        o_ref[:, hs] = acc_ref[h].T.astype(o_ref.dtype)


def _sb(q, k, vt, *, tq):
    bsz, s, hd = q.shape
    return pl.pallas_call(
        functools.partial(_sb_kernel, tq=tq),
        grid=(bsz, s // tq),
        in_specs=[
            pl.BlockSpec((None, tq, hd), lambda b, i: (b, i, 0)),
            _resident((None, s, hd), lambda b, i: (b, 0, 0)),
            _resident((None, s // tq, hd, tq), lambda b, i: (b, 0, 0, 0)),
        ],
        out_specs=pl.BlockSpec((None, tq, hd), lambda b, i: (b, i, 0)),
        out_shape=jax.ShapeDtypeStruct((bsz, s, hd), BF16),
        scratch_shapes=[pltpu.VMEM((SB_HEADS, SB_HEAD_DIM, tq), F32)],
        compiler_params=_cparams(("arbitrary", "arbitrary")),
        name="stick_breaking",
    )(q, k, vt)


def _pack_ab_weight(w):
    d = w.shape[0]
    gq, gk, gv, glow, gr, dq, dk, dv, iq, ik, iw = jnp.split(
        w, [256, 512, 1024, 1040, 1552, 2064, 2576, 3088, 3600, 3664], axis=1)
    pad = jnp.zeros((d, LANES - (IDX_DIM + GLA_GATE_RANK + IDX_HEADS)), w.dtype)
    return jnp.concatenate([gq, gk, gv, gr, ik, glow, iw, pad, dq, dk, dv, iq], axis=1)


def kernel(x, c, positions, ada_w, ada_b, mix_norm, ffn_norm, ffn_w1, ffn_w2, ab_w_in, gla_gate_up,
           gla_gate_b, gla_out_norm, dsa_q_norm, dsa_k_norm, ab_w_out, cd_w_in, pool_w, pool_scale,
           sb_q_norm, sb_k_norm, cd_w_out):
    bsz, s, d = x.shape
    depth = ada_w.shape[0]
    mod = _ada_mod(c, ada_w, ada_b).reshape(depth, bsz, 6, 1, d)
    tm = min(512, s)
    tq_dsa = min(512, s)
    tq_sb = min(256, s)
    for layer in range(depth):
        sh1, sc1, g1, sh2, sc2, g2 = (mod[layer, :, j] for j in range(6))
        i = layer // 2
        if layer % 2 == 0:
            proj, q, k, vt, iq, ik, iw = _proj_ab(
                x, mix_norm[layer], sh1, sc1, _pack_ab_weight(ab_w_in[i]).astype(BF16), positions,
                dsa_q_norm[i], dsa_k_norm[i], tm=tq_dsa, tk=tq_dsa)
            mix_a = _gla(proj, gla_gate_up[i], gla_gate_b[i], gla_out_norm[i], ts=tm)
            mix_b = _dsa(q, k, vt, iq, ik, iw, tq=tq_dsa)
            w_out = ab_w_out[i]
        else:
            mix_a, q, k, vt = _proj_cd(x, mix_norm[layer], sh1, sc1, cd_w_in[i].astype(BF16), pool_w[i],
                                       pool_scale[i], sb_q_norm[i], sb_k_norm[i], tm=tm, tk=tq_sb)
            mix_b = _sb(q, k, vt, tq=tq_sb)
            w_out = cd_w_out[i]
        x = _mix_ffn(x, mix_a, mix_b, w_out.astype(BF16), g1, ffn_norm[layer], sh2, sc2, g2,
                     ffn_w1[layer].astype(BF16), ffn_w2[layer].astype(BF16), tm=tm, tf=256)
    return x
```

```python
import functools

import jax
import jax.numpy as jnp
from jax import lax
from jax.experimental import pallas as pl
from jax.experimental.pallas import tpu as pltpu

F32 = jnp.float32
BF16 = jnp.bfloat16
I32 = jnp.int32

GLA_HEADS, GLA_DK, GLA_DV = 4, 64, 128
GLA_GATE_RANK = 16
GLA_GATE_TAU = 16.0
GLA_CHUNK = 64
DSA_HEADS, DSA_HEAD_DIM = 4, 128
IDX_HEADS, IDX_DIM = 8, 64
DSA_TOPK = 256
POOL_WINDOWS = (2, 4, 8, 16)
POOL_GROUPS, POOL_CH = 4, 128
SB_HEADS, SB_HEAD_DIM = 4, 128
ROPE_THETA = 10000.0
NORM_EPS = 1e-6

LANES = 128
VMEM_LIMIT = 56 * 1024 * 1024
DSA_VMEM_LIMIT = 61 * 1024 * 1024

AB_GQ, AB_GK, AB_GV, AB_GR = 0, 256, 512, 1024
AB_SMALL = 1536
SM_IK, SM_GLOW, SM_IW = 0, 64, 80
AB_GLA_N = AB_SMALL + LANES
AB_DSA_N = 4 * 512

DSA_Q_SCALE = 1.4426950408889634 * DSA_HEAD_DIM ** -0.5
INT_MIN = -2 ** 31
INT_MAX = 2 ** 31 - 1
FLT_TINY = 2.0 ** -126
KEY_TINY = 0x00800000
KEY_LOWEST = INT_MIN + 0x00800000
KEY_LOWEST_BF16 = INT_MIN + 0x00810000
BF16_HALF_STEP = 0x8000
PEEL_AFTER_STEPS = 5
MASK_BIAS = -1e30
SB_DEAD_LOG = -110.0


def _dot(a, b):
    return jnp.dot(a, b, preferred_element_type=F32)


def _dot_nt(a, b):
    return lax.dot_general(a, b, (((1,), (1,)), ((), ())), preferred_element_type=F32)


def _dot_tn(a, b):
    return lax.dot_general(a, b, (((0,), (0,)), ((), ())), preferred_element_type=F32)


def _split3(a):
    hi = a.astype(BF16)
    r1 = a - hi.astype(F32)
    mid = r1.astype(BF16)
    lo = (r1 - mid.astype(F32)).astype(BF16)
    return hi, mid, lo


def _silu(x):
    return x * jax.nn.sigmoid(x)


def _rms(x, w):
    var = jnp.mean(x * x, axis=-1, keepdims=True)
    return x * lax.rsqrt(var + NORM_EPS) * w


def _cparams(sem, vmem_limit=VMEM_LIMIT):
    return pltpu.CompilerParams(dimension_semantics=sem, vmem_limit_bytes=vmem_limit)


def _resident(block_shape, index_map):
    return pl.BlockSpec(block_shape, index_map, pipeline_mode=pl.Buffered(1))


def _ada_kernel(c_ref, w_ref, b_ref, o_ref):
    cond = _silu(c_ref[...])
    c_hi, c_mid, _ = _split3(cond)
    w_hi, w_mid, _ = _split3(w_ref[...])
    acc = _dot(c_hi, w_hi) + _dot(c_hi, w_mid) + _dot(c_mid, w_hi)
    o_ref[...] = acc + b_ref[...]


def _ada_mod(c, ada_w, ada_b):
    depth, d, n = ada_w.shape
    bsz = c.shape[0]
    rows = 16
    cp =jnp.zeros((rows, d), F32).at[:bsz].set(c)
    tn = 1536
    out = pl.pallas_call(
        _ada_kernel,
        grid=(depth, n // tn),
        in_specs=[
            pl.BlockSpec((rows, d), lambda l, j: (0, 0)),
            pl.BlockSpec((None, d, tn), lambda l, j: (l, 0, j)),
            pl.BlockSpec((None, 1, tn), lambda l, j: (l, 0, j)),
        ],
        out_specs=pl.BlockSpec((None, rows, tn), lambda l, j: (l, 0, j)),
        out_shape=jax.ShapeDtypeStruct((depth, rows, n), F32),
        compiler_params=_cparams(("arbitrary", "arbitrary")),
        name="ada_mod",
    )(cp, ada_w, ada_b.reshape(depth, 1, n))
    return out[:, :bsz]


def _gla_kernel(q_ref, k_ref, v_ref, gr_ref, sm_ref, gup_ref, gb_ref, onw_ref, o_ref, st_ref, tri_ref, *, ts):
    c = GLA_CHUNK

    @pl.when(pl.program_id(1) == 0)
    def _():
        st_ref[...] = jnp.zeros_like(st_ref)
        row = lax.broadcasted_iota(I32, (ts, ts), 0)
        col = lax.broadcasted_iota(I32, (ts, ts), 1)
        tri_ref[...] = jnp.where((row // c == col // c) & (col <= row), 1.0, 0.0).astype(BF16)

    nc = ts // c
    tri = tri_ref[...]
    causal = lax.broadcasted_iota(I32, (c, c), 1) <= lax.broadcasted_iota(I32, (c, c), 0)
    onw = onw_ref[...]

    glow = sm_ref[:, SM_GLOW:SM_GLOW + GLA_GATE_RANK]
    a = _dot(glow.astype(BF16), gup_ref[...].astype(BF16)) + gb_ref[...]
    g = (jnp.minimum(a, 0.0) - jnp.log1p(jnp.exp(-jnp.abs(a)))) / GLA_GATE_TAU
    g_hi, g_mid, g_lo = _split3(g)
    b = _dot(tri, g_hi) + _dot(tri, g_mid) + _dot(tri, g_lo)
    qs = q_ref[...] * (GLA_DK ** -0.5)
    k = k_ref[...]

    qe, ke, kd, qd, dec = [], [], [], [], []
    for ci in range(nc):
        r = slice(ci * c, (ci + 1) * c)
        bc = b[r]
        b_mid = bc[c // 2 - 1:c // 2, :]
        b_last = bc[c - 1:c, :]
        qe.append((qs[r] * jnp.exp(bc - b_mid)).astype(BF16))
        ke.append((k[r] * jnp.exp(b_mid - bc)).astype(BF16))
        kd.append((k[r] * jnp.exp(b_last - bc)).astype(BF16))
        qd.append((qs[r] * jnp.exp(bc)).astype(BF16))
        dec.append(jnp.exp(b_last))

    pairs = [(ci, h) for ci in range(nc) for h in range(GLA_HEADS)]
    ksl = lambda h: slice(h * GLA_DK, (h + 1) * GLA_DK)
    vsl = lambda h: slice(h * GLA_DV, (h + 1) * GLA_DV)
    vh = {(ci, h): v_ref[ci * c:(ci + 1) * c, vsl(h)].astype(BF16) for ci, h in pairs}
    att = {p: _dot_nt(qe[p[0]][:, ksl(p[1])], ke[p[0]][:, ksl(p[1])]) for p in pairs}
    att = {p: jnp.where(causal, att[p], 0.0).astype(BF16) for p in pairs}
    o = {p: _dot(att[p], vh[p]) for p in pairs}
    kvt = {p: _dot_tn(vh[p], kd[p[0]][:, ksl(p[1])]) for p in pairs}
    st_in = {}
    for h in range(GLA_HEADS):
        st = st_ref[h]
        for ci in range(nc):
            st_in[(ci, h)] = st.astype(BF16)
            st = st * dec[ci][:, ksl(h)] + kvt[(ci, h)]
        st_ref[h] = st
    for ci, h in pairs:
        out = o[(ci, h)] + _dot_nt(qd[ci][:, ksl(h)], st_in[(ci, h)])
        grh = gr_ref[ci * c:(ci + 1) * c, vsl(h)]
        o_ref[ci * c:(ci + 1) * c, vsl(h)] = (_rms(out, onw) * _silu(grh)).astype(o_ref.dtype)


def _gla(proj, gate_up, gate_b, out_norm, *, ts):
    bsz, s, _ = proj.shape
    hk, hv = GLA_HEADS * GLA_DK, GLA_HEADS * GLA_DV
    return pl.pallas_call(
        functools.partial(_gla_kernel, ts=ts),
        grid=(bsz, s // ts),
        in_specs=[
            pl.BlockSpec((None, ts, hk), lambda b, i: (b, i, AB_GQ // hk)),
            pl.BlockSpec((None, ts, hk), lambda b, i: (b, i, AB_GK // hk)),
            pl.BlockSpec((None, ts, hv), lambda b, i: (b, i, AB_GV // hv)),
            pl.BlockSpec((None, ts, hv), lambda b, i: (b, i, AB_GR // hv)),
            pl.BlockSpec((None, ts, LANES), lambda b, i: (b, i, AB_SMALL // LANES)),
            pl.BlockSpec((GLA_GATE_RANK, hk), lambda b, i: (0, 0)),
            pl.BlockSpec((1, hk), lambda b, i: (0, 0)),
            pl.BlockSpec((1, GLA_DV), lambda b, i: (0, 0)),
        ],
        out_specs=pl.BlockSpec((None, ts, hv), lambda b, i: (b, i, 0)),
        out_shape=jax.ShapeDtypeStruct((bsz, s, hv), BF16),
        scratch_shapes=[pltpu.VMEM((GLA_HEADS, GLA_DV, GLA_DK), F32), pltpu.VMEM((ts, ts), BF16)],
        compiler_params=_cparams(("arbitrary", "arbitrary")),
        name="gla",
    )(proj, proj, proj, proj, proj, gate_up, gate_b.reshape(1, hk), out_norm.reshape(1, GLA_DV))


def _dsa_prep_rows(rows, dsa_ref, sm_ref, pos_ref, qn_ref, kn_ref, freq_ref, q_o, k_o, v_o, iq_o, ik_o, iw_o):
    hd = DSA_HEADS * DSA_HEAD_DIM
    pos = pos_ref[rows, :].astype(F32)
    lane = lax.broadcasted_iota(I32, (1, LANES), 1)
    half_a, half_i = DSA_HEAD_DIM // 2, IDX_DIM // 2

    ang = pos * freq_ref[...]
    cos_t, sin_t = jnp.cos(ang), jnp.sin(ang)

    def attn_table(t):
        return jnp.where(lane < half_a, t, pltpu.roll(t, half_a, 1))

    def idx_table(t):
        return jnp.where(lane < half_i, pltpu.roll(t, half_a, 1),
                         jnp.where(lane < 2 * half_i, pltpu.roll(t, half_a + half_i, 1), t))

    cos_a = attn_table(cos_t)
    sin_a = jnp.where(lane < half_a, -1.0, 1.0) * attn_table(sin_t)

    def rope_attn(t):
        return t * cos_a + pltpu.roll(t, half_a, 1) * sin_a

    first = (lane % IDX_DIM) < half_i
    cos_i = idx_table(cos_t)
    sin_i = jnp.where(first, -1.0, 1.0) * idx_table(sin_t)

    def rope_idx(t):
        rot = jnp.where(first, pltpu.roll(t, LANES - IDX_DIM // 2, 1), pltpu.roll(t, IDX_DIM // 2, 1))
        return t * cos_i + rot * sin_i

    qn, kn = qn_ref[...], kn_ref[...]
    for h in range(DSA_HEADS):
        hs = slice(h * DSA_HEAD_DIM, (h + 1) * DSA_HEAD_DIM)
        vs = slice(2 * hd + h * DSA_HEAD_DIM, 2 * hd + (h + 1) * DSA_HEAD_DIM)
        q_o[rows, hs] = (rope_attn(_rms(dsa_ref[rows, hs], qn)) * DSA_Q_SCALE).astype(q_o.dtype)
        k_o[rows, hs] = rope_attn(_rms(dsa_ref[rows, hd + h * DSA_HEAD_DIM:hd + (h + 1) * DSA_HEAD_DIM], kn)
                                  ).astype(k_o.dtype)
        tk = v_o.shape[-1]
        n = rows.stop - rows.start
        assert tk % n == 0 and rows.start % n == 0
        v_o[rows.start // tk, hs, rows.start % tk:rows.start % tk + n] = dsa_ref[rows, vs].T.astype(v_o.dtype)
    for j in range(IDX_HEADS * IDX_DIM // LANES):
        r = rope_idx(dsa_ref[rows, 3 * hd + j * LANES:3 * hd + (j + 1) * LANES])
        iq_o[2 * j, rows, :] = r[:, :IDX_DIM].astype(iq_o.dtype)
        iq_o[2 * j + 1, rows, :] = r[:, IDX_DIM:].astype(iq_o.dtype)
    sm = sm_ref[rows, :]
    ik_o[rows, :] = rope_idx(sm)[:, SM_IK:SM_IK + IDX_DIM].astype(ik_o.dtype)
    iw_o[:, rows] = sm.T[SM_IW:SM_IW + IDX_HEADS, :]


def _proj_ab_kernel(x_ref, nw_ref, sh_ref, sc_ref, w_ref, pos_ref, qn_ref, kn_ref, freq_ref,
                    gla_o, q_o, k_o, v_o, iq_o, ik_o, iw_o, dsa_ref):
    hb = (_rms(x_ref[...], nw_ref[...]) * (1.0 + sc_ref[...]) + sh_ref[...]).astype(BF16)
    hd = DSA_HEADS * DSA_HEAD_DIM
    tm = x_ref.shape[0]
    halves = [slice(0, tm // 2), slice(tm // 2, tm)]
    small = gla_o.at[:, AB_SMALL:AB_SMALL + LANES]
    for rows in halves:
        gla_o[rows, AB_SMALL:] = _dot(hb[rows], w_ref[:, AB_SMALL:AB_GLA_N])
        for j in range(AB_DSA_N // hd):
            dsa_ref[rows, j * hd:(j + 1) * hd] = _dot(hb[rows], w_ref[:, AB_GLA_N + j * hd:AB_GLA_N + (j + 1) * hd])
    for rows in halves:
        _dsa_prep_rows(rows, dsa_ref, small, pos_ref, qn_ref, kn_ref, freq_ref, q_o, k_o, v_o, iq_o, ik_o, iw_o)
        gla_o[rows, :AB_SMALL] = _dot(hb[rows], w_ref[:, :AB_SMALL])


def _proj_ab(x, nw, shift, scale, w_bf16, positions, q_norm, k_norm, *, tm, tk):
    bsz, s, d = x.shape
    hd = DSA_HEADS * DSA_HEAD_DIM
    half_a, half_i = DSA_HEAD_DIM // 2, IDX_DIM // 2
    inv_a = ROPE_THETA ** (-jnp.arange(half_a, dtype=F32) / half_a)
    inv_i = ROPE_THETA ** (-jnp.arange(half_i, dtype=F32) / half_i)
    freq = jnp.concatenate([inv_a, inv_i, inv_i]).reshape(1, LANES)
    rows = lambda n: pl.BlockSpec((None, tm, n), lambda b, i: (b, i, 0))
    vec = lambda n: pl.BlockSpec((1, n), lambda b, i: (0, 0))
    mod = pl.BlockSpec((None, 1, d), lambda b, i: (b, 0, 0))
    return pl.pallas_call(
        _proj_ab_kernel,
        grid=(bsz, s // tm),
        in_specs=[
            rows(d), vec(d), mod, mod,
            _resident((d, AB_GLA_N + AB_DSA_N), lambda b, i: (0, 0)),
            rows(1), vec(DSA_HEAD_DIM), vec(DSA_HEAD_DIM), vec(LANES),
        ],
        out_specs=[
            rows(AB_GLA_N), rows(hd), rows(hd),
            pl.BlockSpec((None, tm // tk, hd, tk), lambda b, i: (b, i, 0, 0)),
            pl.BlockSpec((None, IDX_HEADS, tm, IDX_DIM), lambda b, i: (b, 0, i, 0)),
            rows(IDX_DIM),
            pl.BlockSpec((None, IDX_HEADS, tm), lambda b, i: (b, 0, i)),
        ],
        out_shape=[
            jax.ShapeDtypeStruct((bsz, s, AB_GLA_N), F32),
            jax.ShapeDtypeStruct((bsz, s, hd), BF16),
            jax.ShapeDtypeStruct((bsz, s, hd), BF16),
            jax.ShapeDtypeStruct((bsz, s // tk, hd, tk), BF16),
            jax.ShapeDtypeStruct((bsz, IDX_HEADS, s, IDX_DIM), BF16),
            jax.ShapeDtypeStruct((bsz, s, IDX_DIM), BF16),
            jax.ShapeDtypeStruct((bsz, IDX_HEADS, s), F32),
        ],
        scratch_shapes=[pltpu.VMEM((tm, AB_DSA_N), F32)],
        compiler_params=_cparams(("arbitrary", "arbitrary")),
        name="proj_ab",
    )(x, nw.reshape(1, d), shift, scale, w_bf16, positions.reshape(bsz, s, 1),
      q_norm.reshape(1, -1), k_norm.reshape(1, -1), freq)


def _dsa_kernel(q_ref, iq_ref, iw_ref, k_ref, vt_ref, ik_ref, o_ref, sc_ref, top_ref, acc_ref,
                *, tq, n_sel, s_len):
    i = pl.program_id(1)
    nkb = i + 1
    kofs = lax.broadcasted_iota(I32, (tq, tq), 0)
    qidx = i * tq + lax.broadcasted_iota(I32, (tq, tq), 1)
    iw = iw_ref[...]
    idx_scale = (IDX_HEADS ** -0.5) * (IDX_DIM ** -0.5)

    def over_tiles(step, init, n=nkb, widest=2):
        carry = lax.fori_loop(0, n // widest, lambda j, c: step(widest * j, widest, c), init)
        rest = n % widest
        start = n - rest
        width = widest // 2
        while width >= 1:
            carry = lax.cond((rest & width) != 0, functools.partial(step, start, width), lambda c: c, carry)
            start = start + (rest & width)
            width //= 2
        return carry

    def as_float(key):
        key = jnp.maximum(key, KEY_LOWEST)
        key = jnp.where((key > 0) & (key < KEY_TINY), KEY_TINY, key)
        return pltpu.bitcast(key ^ ((key >> 31) & 0x7FFFFFFF), F32)

    def scores(kb, width, diagonal):
        k0 = pl.multiple_of(kb * tq, tq)
        rows = width * tq
        ikb = ik_ref[pl.ds(k0, rows), :]
        sc = jnp.zeros((rows, tq), F32)
        for h in range(IDX_HEADS):
            sc = sc + jnp.maximum(_dot_nt(ikb, iq_ref[h]), 0.0) * iw[h:h + 1, :]
        sc = sc * idx_scale
        sc = jnp.where(jnp.abs(sc) < FLT_TINY, 0.0, sc)
        if diagonal:
            sc = jnp.where(kofs <= qidx - i * tq, sc, -jnp.inf)
        sc_ref[pl.ds(kb, width)] = sc.reshape(width, tq, tq)
        top_ref[pl.ds(kb, width)] = sc.astype(BF16).reshape(width, tq, tq)

    def below_diagonal(kb, width, carry):
        scores(kb, width, False)
        return carry

    over_tiles(below_diagonal, 0, n=i)
    scores(i, 1, True)

    def count(pred):
        def body(kb, width, acc):
            rows = width * tq
            m = jnp.where(pred(sc_ref[pl.ds(kb, width)].reshape(rows, tq)), 1, 0)
            return acc + jnp.sum(m.reshape(rows // 8, 8, tq), axis=0)
        return jnp.sum(over_tiles(body, jnp.zeros((8, tq), I32)), axis=0, keepdims=True)

    n_valid = qidx[0:1, :] + 1

    def as_bf16(cand):
        k16 = jnp.maximum(cand, KEY_LOWEST_BF16) >> 16
        pattern = k16 ^ ((k16 >> 15) & 0x7FFF)
        pattern = jnp.where((pattern > 0) & (pattern < 0x80), 0x80, pattern)
        return pltpu.bitcast(pattern << 16, F32).astype(BF16)

    def count_top(cand):
        cand_bf = as_bf16(cand)

        def body(kb, width, acc):
            rows = width * tq
            top = top_ref[pl.ds(kb, width)].reshape(rows, tq)
            m = jnp.where(top >= cand_bf, jnp.ones((), BF16), jnp.zeros((), BF16))
            part = m[0:16]
            for j in range(1, rows // 16):
                part = part + m[16 * j:16 * (j + 1)]
            return acc + part.astype(F32)
        acc = over_tiles(body, jnp.zeros((16, tq), F32))
        return jnp.sum(acc, axis=0, keepdims=True).astype(I32)

    def top_step(it, u):
        cand_u = u | lax.shift_left(jnp.int32(1), jnp.int32(31) - it)
        return jnp.where(count_top(cand_u ^ INT_MIN) >= n_sel, cand_u, u)

    u = lax.fori_loop(0, 16, top_step, jnp.zeros((1, tq), I32))

    def any_lane(flag):
        return jnp.max(jnp.where(flag, 1.0, 0.0)) > 0.0

    base = u ^ INT_MIN
    base = jnp.where(base < 0, base | 0xFFFF, base)
    base = jnp.clip(base, KEY_LOWEST + BF16_HALF_STEP + 1, INT_MAX - 2 * BF16_HALF_STEP)
    many = n_valid > n_sel
    lo = jnp.where(many, base - BF16_HALF_STEP - 1, KEY_LOWEST)
    hi = jnp.where(many, base + 2 * BF16_HALF_STEP, INT_MAX)

    def count_two(pred_a, pred_b):
        def body(kb, width, acc):
            x = sc_ref[pl.ds(kb, width)].reshape(width * tq // 8, 8, tq)
            return (acc[0] + jnp.sum(jnp.where(pred_a(x), 1, 0), axis=0),
                    acc[1] + jnp.sum(jnp.where(pred_b(x), 1, 0), axis=0))
        a, b = over_tiles(body, (jnp.zeros((8, tq), I32), jnp.zeros((8, tq), I32)))
        return jnp.sum(a, axis=0, keepdims=True), jnp.sum(b, axis=0, keepdims=True)

    lo_f, hi_f = as_float(lo), as_float(hi)
    cnt_lo, cnt_hi = count_two(lambda x: x >= lo_f, lambda x: x >= hi_f)

    def open_queries(lo, cnt_lo, hi, done):
        return (cnt_lo > n_sel) & (hi - lo > 1) & (done == 0)

    def halve(s):
        lo, cnt_lo, hi, cnt_hi, done = s
        mid = lo + ((hi - lo) >> 1)
        mid_f = as_float(mid)
        cnt = count(lambda x: x >= mid_f)
        live = open_queries(lo, cnt_lo, hi, done)
        up = live & (cnt >= n_sel)
        down = live & (cnt < n_sel)
        return (jnp.where(up, mid, lo), jnp.where(up, cnt, cnt_lo),
                jnp.where(down, mid, hi), jnp.where(down, cnt, cnt_hi), done)

    st = lax.fori_loop(0, PEEL_AFTER_STEPS, lambda _, s: halve(s),
                       (lo, cnt_lo, hi, cnt_hi, jnp.zeros((1, tq), I32)))

    def peel(st):
        lo, cnt_lo, hi, cnt_hi, done = st
        lo_f, hi_f = as_float(lo), as_float(hi)

        def extremes(kb, width, carry):
            top, bot = carry
            x = sc_ref[pl.ds(kb, width)].reshape(width * tq // 8, 8, tq)
            top = jnp.maximum(top, jnp.max(jnp.where(x < hi_f, x, -jnp.inf), axis=0))
            bot = jnp.minimum(bot, jnp.min(jnp.where(x >= lo_f, x, jnp.inf), axis=0))
            return top, bot
        top, bot = over_tiles(extremes, (jnp.full((8, tq), -jnp.inf, F32), jnp.full((8, tq), jnp.inf, F32)))
        top = jnp.max(top, axis=0, keepdims=True)
        bot = jnp.min(bot, axis=0, keepdims=True)
        from_top = (cnt_hi == n_sel - 1) | (top == bot)
        from_bot = (cnt_lo == n_sel + 1) & jnp.logical_not(from_top)
        cnt_top, cnt_bot = count_two(lambda x: x >= top, lambda x: x > bot)
        cnt = jnp.where(from_top, cnt_top, cnt_bot)
        short = from_bot & (cnt < n_sel)
        hit = open_queries(lo, cnt_lo, hi, done) & (from_top | from_bot)
        thr = jnp.where(from_top, top, bot)
        incl = jnp.where(from_top | short, 1, 0)
        return hit, thr, incl, jnp.where(short, cnt_lo, cnt)

    def with_peel(st):
        hit, thr, incl, cnt = peel(st)
        st = st[:4] + (jnp.where(hit, 1, 0),)
        lo, cnt_lo, _, _, _ = lax.while_loop(lambda s: any_lane(open_queries(s[0], s[1], s[2], s[4])), halve, st)
        return jnp.where(hit, thr, as_float(lo)), jnp.where(hit, incl, 1), jnp.where(hit, cnt, cnt_lo)

    def without_peel(st):
        lo, cnt_lo, _, _, _ = st
        return as_float(lo), jnp.ones((1, tq), I32), cnt_lo

    thr, incl, cnt = lax.cond(any_lane(open_queries(st[0], st[1], st[2], st[4])), with_peel, without_peel, st)
    admits_equal = incl != 0
    excess = cnt > n_sel

    tied = jnp.max(jnp.where(excess, 1.0, 0.0)) > 0.0

    acc_ref[...] = jnp.zeros_like(acc_ref)

    def attend(select, seen):
        def block(kb, width, carry):
            ms, ls, seen = carry
            k0 = pl.multiple_of(kb * tq, tq)
            rows = width * tq
            sel, seen = select(sc_ref[pl.ds(kb, width)].reshape(rows, tq), seen)
            heads = [slice(h * DSA_HEAD_DIM, (h + 1) * DSA_HEAD_DIM) for h in range(DSA_HEADS)]
            st = [_dot_nt(k_ref[pl.ds(k0, rows), hs], q_ref[:, hs]) for hs in heads]
            bias = jnp.where(sel, 0.0, MASK_BIAS)
            st = [x + bias for x in st]
            new_m = [jnp.maximum(m, jnp.max(x, axis=0, keepdims=True)) for m, x in zip(ms, st)]
            alpha = [jnp.exp2(m - mn) for m, mn in zip(ms, new_m)]
            p = [jnp.exp2(x - mn) for x, mn in zip(st, new_m)]
            new_l = [a * l + jnp.sum(x, axis=0, keepdims=True) for a, l, x in zip(alpha, ls, p)]
            p = [x.astype(BF16) for x in p]
            for h, hs in enumerate(heads):
                pv = _dot(vt_ref[kb, hs, :], p[h][0:tq])
                for j in range(1, width):
                    pv = pv + _dot(vt_ref[kb + j, hs, :], p[h][j * tq:(j + 1) * tq])
                acc_ref[h] = alpha[h] * acc_ref[h] + pv
            return tuple(new_m), tuple(new_l), seen

        carry = (tuple(jnp.full((1, tq), -jnp.inf, F32) for _ in range(DSA_HEADS)),
                 tuple(jnp.zeros((1, tq), F32) for _ in range(DSA_HEADS)), seen)
        return over_tiles(block, carry)[1]

    def attend_tied():
        room = n_sel - count(lambda x: x > thr)
        room = jnp.where(excess, room, jnp.where(admits_equal, s_len, 0)).astype(F32)
        r = lax.broadcasted_iota(I32, (tq, tq), 0)
        c = lax.broadcasted_iota(I32, (tq, tq), 1)
        upto = jnp.where(c <= r, 1.0, 0.0).astype(BF16)

        def select(x, seen):
            equal = x == thr
            ones = jnp.where(equal, 1.0, 0.0).astype(BF16)
            fits = []
            for j in range(x.shape[0] // tq):
                rank = _dot(upto, ones[j * tq:(j + 1) * tq])
                fits.append(rank <= room - seen)
                seen = seen + rank[tq - 1:tq, :]
            fits = fits[0] if len(fits) == 1 else jnp.concatenate(fits, axis=0)
            return (x > thr) | (equal & fits), seen
        return attend(select, jnp.zeros((1, tq), F32))

    at_least = jnp.where(admits_equal, thr, jnp.inf)
    above = jnp.where(admits_equal, jnp.inf, thr)
    ls = lax.cond(tied, attend_tied,
                  lambda: attend(lambda x, seen: ((x >= at_least) | (x > above), seen), jnp.zeros((1, 1), F32)))
    for h in range(DSA_HEADS):
        hs = slice(h * DSA_HEAD_DIM, (h + 1) * DSA_HEAD_DIM)
        o_ref[:, hs] = (acc_ref[h] / ls[h]).T.astype(o_ref.dtype)


def _dsa(q, k, vt, iq, ik, iw, *, tq):
    bsz, s, hd = q.shape
    n_sel = min(DSA_TOPK, s // 4)
    resident = lambda n: _resident((None, s, n), lambda b, i: (b, 0, 0))
    return pl.pallas_call(
        functools.partial(_dsa_kernel, tq=tq, n_sel=n_sel, s_len=s),
        grid=(bsz, s // tq),
        in_specs=[
            pl.BlockSpec((None, tq, hd), lambda b, i: (b, i, 0)),
            pl.BlockSpec((None, IDX_HEADS, tq, IDX_DIM), lambda b, i: (b, 0, i, 0)),
            pl.BlockSpec((None, IDX_HEADS, tq), lambda b, i: (b, 0, i)),
            resident(hd),
            _resident((None, s // tq, hd, tq), lambda b, i: (b, 0, 0, 0)),
            resident(IDX_DIM),
        ],
        out_specs=pl.BlockSpec((None, tq, hd), lambda b, i: (b, i, 0)),
        out_shape=jax.ShapeDtypeStruct((bsz, s, hd), BF16),
        scratch_shapes=[
            pltpu.VMEM((s // tq, tq, tq), F32),
            pltpu.VMEM((s // tq, tq, tq), BF16),
            pltpu.VMEM((DSA_HEADS, DSA_HEAD_DIM, tq), F32),
        ],
        compiler_params=_cparams(("arbitrary", "arbitrary"), DSA_VMEM_LIMIT),
        name="dsa",
    )(q, iq, iw, k, vt, ik)


def _mix_ffn_kernel(x_ref, a_ref, b_ref, wa_ref, wb_ref, g1_ref, nw_ref, sh_ref, sc_ref, g2_ref,
                    w1_ref, w2_ref, o_ref, *, tf):
    x = x_ref[...] + g1_ref[...] * (_dot(a_ref[...], wa_ref[...]) + _dot(b_ref[...], wb_ref[...]))
    hb = (_rms(x, nw_ref[...]) * (1.0 + sc_ref[...]) + sh_ref[...]).astype(BF16)
    f = w2_ref.shape[0]
    acc = jnp.zeros(x.shape, F32)
    for j in range(f // tf):
        gate = _dot(hb, w1_ref[:, j * tf:(j + 1) * tf])
        up = _dot(hb, w1_ref[:, f + j * tf:f + (j + 1) * tf])
        acc = acc + _dot((_silu(gate) * up).astype(BF16), w2_ref[j * tf:(j + 1) * tf, :])
    o_ref[...] = x + g2_ref[...] * acc


def _mix_ffn(x, mix_a, mix_b, w_out_bf16, gate1, nw, shift, scale, gate2, w1_bf16, w2_bf16, *, tm, tf):
    bsz, s, d = x.shape
    f = w2_bf16.shape[0]
    na = mix_a.shape[-1]
    assert mix_b.shape[-1] == na and w_out_bf16.shape[0] == 2 * na
    rows = lambda n: pl.BlockSpec((None, tm, n), lambda b, i: (b, i, 0))
    mod = pl.BlockSpec((None, 1, d), lambda b, i: (b, 0, 0))
    return pl.pallas_call(
        functools.partial(_mix_ffn_kernel, tf=tf),
        grid=(bsz, s // tm),
        in_specs=[
            rows(d), rows(na), rows(na),
            _resident((na, d), lambda b, i: (0, 0)),
            _resident((na, d), lambda b, i: (1, 0)),
            mod,
            pl.BlockSpec((1, d), lambda b, i: (0, 0)),
            mod, mod, mod,
            _resident((d, 2 * f), lambda b, i: (0, 0)),
            _resident((f, d), lambda b, i: (0, 0)),
        ],
        out_specs=rows(d),
        out_shape=jax.ShapeDtypeStruct((bsz, s, d), F32),
        compiler_params=_cparams(("arbitrary", "arbitrary")),
        name="mix_ffn",
    )(x, mix_a, mix_b, w_out_bf16, w_out_bf16, gate1, nw.reshape(1, d), shift, scale, gate2,
      w1_bf16, w2_bf16)


HALO = max(POOL_WINDOWS)


def _proj_cd_kernel(x_ref, nw_ref, sh_ref, sc_ref, w_ref, pw_ref, ps_ref, qn_ref, kn_ref,
                    pool_o, q_o, k_o, v_o, ext_ref, sb_ref):
    i = pl.program_id(1)
    tm = x_ref.shape[0]
    n = POOL_GROUPS * POOL_CH
    hb = (_rms(x_ref[...], nw_ref[...]) * (1.0 + sc_ref[...]) + sh_ref[...]).astype(BF16)

    @pl.when(i == 0)
    def _():
        ext_ref[0:HALO, :] = jnp.zeros((HALO, n), F32)

    ext_ref[HALO:HALO + tm, :] = _dot(hb, w_ref[:, :n])
    for j in range(3):
        sb_ref[:, j * n:(j + 1) * n] = _dot(hb, w_ref[:, (j + 1) * n:(j + 2) * n])

    t = i * tm + lax.broadcasted_iota(I32, (tm, 1), 0)
    for g, w in enumerate(POOL_WINDOWS):
        gs = slice(g * POOL_CH, (g + 1) * POOL_CH)
        u = ext_ref[HALO:HALO + tm, gs]
        tot = u
        for j in range(1, w):
            tot = tot + ext_ref[HALO - j:HALO - j + tm, gs]
        cnt = jnp.minimum(t + 1, w).astype(F32)
        pooled = tot / cnt - u
        y = _dot(pooled.astype(BF16), pw_ref[g].astype(BF16)) * ps_ref[:, gs]
        pool_o[:, gs] = y.astype(pool_o.dtype)
    ext_ref[0:HALO, :] = ext_ref[tm:tm + HALO, :]

    qn, kn = qn_ref[...], kn_ref[...]
    for h in range(SB_HEADS):
        hs = slice(h * SB_HEAD_DIM, (h + 1) * SB_HEAD_DIM)
        q_o[:, hs] = (_rms(sb_ref[:, hs], qn) * SB_HEAD_DIM ** -0.5).astype(q_o.dtype)
        k_o[:, hs] = _rms(sb_ref[:, n + h * SB_HEAD_DIM:n + (h + 1) * SB_HEAD_DIM], kn).astype(k_o.dtype)
    tk = v_o.shape[-1]
    for j in range(v_o.shape[0]):
        v_o[j] = sb_ref[j * tk:(j + 1) * tk, 2 * n:3 * n].T.astype(v_o.dtype)


def _proj_cd(x, nw, shift, scale, w_bf16, pool_w, pool_scale, q_norm, k_norm, *, tm, tk):
    bsz, s, d = x.shape
    n = POOL_GROUPS * POOL_CH
    assert w_bf16.shape[1] == 4 * n
    rows = lambda m: pl.BlockSpec((None, tm, m), lambda b, i: (b, i, 0))
    vec = lambda m: pl.BlockSpec((1, m), lambda b, i: (0, 0))
    mod = pl.BlockSpec((None, 1, d), lambda b, i: (b, 0, 0))
    return pl.pallas_call(
        _proj_cd_kernel,
        grid=(bsz, s // tm),
        in_specs=[
            rows(d), vec(d), mod, mod,
            _resident((d, 4 * n), lambda b, i: (0, 0)),
            pl.BlockSpec((POOL_GROUPS, POOL_CH, POOL_CH), lambda b, i: (0, 0, 0)),
            vec(n), vec(SB_HEAD_DIM), vec(SB_HEAD_DIM),
        ],
        out_specs=[rows(n)] * 3 + [pl.BlockSpec((None, tm // tk, n, tk), lambda b, i: (b, i, 0, 0))],
        out_shape=[jax.ShapeDtypeStruct((bsz, s, n), BF16)] * 3
        + [jax.ShapeDtypeStruct((bsz, s // tk, n, tk), BF16)],
        scratch_shapes=[pltpu.VMEM((HALO + tm, n), F32), pltpu.VMEM((tm, 3 * n), F32)],
        compiler_params=_cparams(("arbitrary", "arbitrary")),
        name="proj_cd",
    )(x, nw.reshape(1, d), shift, scale, w_bf16, pool_w, pool_scale.reshape(1, n),
      q_norm.reshape(1, -1), k_norm.reshape(1, -1))


def _sb_kernel(q_ref, k_ref, vt_ref, o_ref, acc_ref, *, tq):
    i = pl.program_id(1)
    kofs = lax.broadcasted_iota(I32, (tq, tq), 0)
    qofs = lax.broadcasted_iota(I32, (tq, tq), 1)
    later = jnp.where(qofs > kofs, 1.0, 0.0).astype(BF16)
    heads = [slice(h * SB_HEAD_DIM, (h + 1) * SB_HEAD_DIM) for h in range(SB_HEADS)]
    acc_ref[...] = jnp.zeros_like(acc_ref)

    def block(kb, runs, diagonal):
        k0 = pl.multiple_of(kb * tq, tq)
        z = [_dot_nt(k_ref[pl.ds(k0, tq), hs], q_ref[:, hs]) for hs in heads]
        sp = [jnp.log(1.0 + jnp.exp(-jnp.abs(x))) for x in z]
        log_beta = [jnp.minimum(x, 0.0) - s for x, s in zip(z, sp)]
        log_1m = [lb - x for lb, x in zip(log_beta, z)]
        if diagonal:
            strict = kofs < qofs
            log_1m = [jnp.where(strict, x, 0.0) for x in log_1m]
        hi = [x.astype(BF16) for x in log_1m]
        lo = [(x - h.astype(F32)).astype(BF16) for x, h in zip(log_1m, hi)]
        after = [_dot(later, h) + _dot(later, l) + r for h, l, r in zip(hi, lo, runs)]
        w = [jnp.exp(lb + a) for lb, a in zip(log_beta, after)]
        if diagonal:
            w = [jnp.where(strict, x, 0.0) for x in w]
        for h, hs in enumerate(heads):
            acc_ref[h] = acc_ref[h] + _dot(vt_ref[kb, hs, :], w[h].astype(BF16))
        return tuple(r + jnp.sum(x, axis=0, keepdims=True) for r, x in zip(runs, log_1m))

    def alive(runs):
        worst = functools.reduce(jnp.maximum, runs)
        return jnp.max(worst) > SB_DEAD_LOG

    runs = block(i, tuple(jnp.zeros((1, tq), F32) for _ in heads), True)

    def earlier(state):
        kb, runs, _ = state
        runs = block(kb, runs, False)
        return kb - 1, runs, alive(runs)

    lax.while_loop(lambda st: (st[0] >= 0) & st[2], earlier, (i - 1, runs, alive(runs)))
    for h, hs in enumerate(heads):
        o_ref[:, hs] = acc_ref[h].T.astype(o_ref.dtype)


def _sb(q, k, vt, *, tq):
    bsz, s, hd = q.shape
    return pl.pallas_call(
        functools.partial(_sb_kernel, tq=tq),
        grid=(bsz, s // tq),
        in_specs=[
            pl.BlockSpec((None, tq, hd), lambda b, i: (b, i, 0)),
            _resident((None, s, hd), lambda b, i: (b, 0, 0)),
            _resident((None, s // tq, hd, tq), lambda b, i: (b, 0, 0, 0)),
        ],
        out_specs=pl.BlockSpec((None, tq, hd), lambda b, i: (b, i, 0)),
        out_shape=jax.ShapeDtypeStruct((bsz, s, hd), BF16),
        scratch_shapes=[pltpu.VMEM((SB_HEADS, SB_HEAD_DIM, tq), F32)],
        compiler_params=_cparams(("arbitrary", "arbitrary")),
        name="stick_breaking",
    )(q, k, vt)


def _pack_ab_weight(w):
    d = w.shape[0]
    gq, gk, gv, glow, gr, dq, dk, dv, iq, ik, iw = jnp.split(
        w, [256, 512, 1024, 1040, 1552, 2064, 2576, 3088, 3600, 3664], axis=1)
    pad = jnp.zeros((d, LANES - (IDX_DIM + GLA_GATE_RANK + IDX_HEADS)), w.dtype)
    return jnp.concatenate([gq, gk, gv, gr, ik, glow, iw, pad, dq, dk, dv, iq], axis=1)


def kernel(x, c, positions, ada_w, ada_b, mix_norm, ffn_norm, ffn_w1, ffn_w2, ab_w_in, gla_gate_up,
           gla_gate_b, gla_out_norm, dsa_q_norm, dsa_k_norm, ab_w_out, cd_w_in, pool_w, pool_scale,
           sb_q_norm, sb_k_norm, cd_w_out):
    bsz, s, d = x.shape
    depth = ada_w.shape[0]
    mod = _ada_mod(c, ada_w, ada_b).reshape(depth, bsz, 6, 1, d)
    tm = min(512, s)
    tq_dsa = min(512, s)
    tq_sb = min(256, s)
    for layer in range(depth):
        sh1, sc1, g1, sh2, sc2, g2 = (mod[layer, :, j] for j in range(6))
        i = layer // 2
        if layer % 2 == 0:
            proj, q, k, vt, iq, ik, iw = _proj_ab(
                x, mix_norm[layer], sh1, sc1, _pack_ab_weight(ab_w_in[i]).astype(BF16), positions,
                dsa_q_norm[i], dsa_k_norm[i], tm=tq_dsa, tk=tq_dsa)
            mix_a = _gla(proj, gla_gate_up[i], gla_gate_b[i], gla_out_norm[i], ts=tm)
            mix_b = _dsa(q, k, vt, iq, ik, iw, tq=tq_dsa)
            w_out = ab_w_out[i]
        else:
            mix_a, q, k, vt = _proj_cd(x, mix_norm[layer], sh1, sc1, cd_w_in[i].astype(BF16), pool_w[i],
                                       pool_scale[i], sb_q_norm[i], sb_k_norm[i], tm=tm, tk=tq_sb)
            mix_b = _sb(q, k, vt, tq=tq_sb)
            w_out = cd_w_out[i]
        x = _mix_ffn(x, mix_a, mix_b, w_out.astype(BF16), g1, ffn_norm[layer], sh2, sc2, g2,
                     ffn_w1[layer].astype(BF16), ffn_w2[layer].astype(BF16), tm=tm, tf=256)
    return x
```

```python
import functools

import jax
import jax.numpy as jnp
from jax import lax
from jax.experimental import pallas as pl
from jax.experimental.pallas import tpu as pltpu

F32 = jnp.float32
BF16 = jnp.bfloat16
I32 = jnp.int32

GLA_HEADS, GLA_DK, GLA_DV = 4, 64, 128
GLA_GATE_RANK = 16
GLA_GATE_TAU = 16.0
GLA_CHUNK = 64
DSA_HEADS, DSA_HEAD_DIM = 4, 128
IDX_HEADS, IDX_DIM = 8, 64
DSA_TOPK = 256
POOL_WINDOWS = (2, 4, 8, 16)
POOL_GROUPS, POOL_CH = 4, 128
SB_HEADS, SB_HEAD_DIM = 4, 128
ROPE_THETA = 10000.0
NORM_EPS = 1e-6

LANES = 128
VMEM_LIMIT = 56 * 1024 * 1024
DSA_VMEM_LIMIT = 61 * 1024 * 1024

AB_GQ, AB_GK, AB_GV, AB_GR = 0, 256, 512, 1024
AB_SMALL = 1536
SM_IK, SM_GLOW, SM_IW = 0, 64, 80
AB_GLA_N = AB_SMALL + LANES
AB_DSA_N = 4 * 512

DSA_Q_SCALE = 1.4426950408889634 * DSA_HEAD_DIM ** -0.5
INT_MIN = -2 ** 31
INT_MAX = 2 ** 31 - 1
FLT_TINY = 2.0 ** -126
KEY_TINY = 0x00800000
KEY_LOWEST = INT_MIN + 0x00800000
KEY_LOWEST_BF16 = INT_MIN + 0x00810000
BF16_HALF_STEP = 0x8000
PEEL_AFTER_STEPS = 5
MASK_BIAS = -1e30
SB_DEAD_LOG = -110.0


def _dot(a, b):
    return jnp.dot(a, b, preferred_element_type=F32)


def _dot_nt(a, b):
    return lax.dot_general(a, b, (((1,), (1,)), ((), ())), preferred_element_type=F32)


def _dot_tn(a, b):
    return lax.dot_general(a, b, (((0,), (0,)), ((), ())), preferred_element_type=F32)


def _split3(a):
    hi = a.astype(BF16)
    r1 = a - hi.astype(F32)
    mid = r1.astype(BF16)
    lo = (r1 - mid.astype(F32)).astype(BF16)
    return hi, mid, lo


def _silu(x):
    return x * jax.nn.sigmoid(x)


def _rms(x, w):
    var = jnp.mean(x * x, axis=-1, keepdims=True)
    return x * lax.rsqrt(var + NORM_EPS) * w


def _cparams(sem, vmem_limit=VMEM_LIMIT):
    return pltpu.CompilerParams(dimension_semantics=sem, vmem_limit_bytes=vmem_limit)


def _resident(block_shape, index_map):
    return pl.BlockSpec(block_shape, index_map, pipeline_mode=pl.Buffered(1))


def _ada_kernel(c_ref, w_ref, b_ref, o_ref):
    cond = _silu(c_ref[...])
    c_hi, c_mid, _ = _split3(cond)
    w_hi, w_mid, _ = _split3(w_ref[...])
    acc = _dot(c_hi, w_hi) + _dot(c_hi, w_mid) + _dot(c_mid, w_hi)
    o_ref[...] = acc + b_ref[...]


def _ada_mod(c, ada_w, ada_b):
    depth, d, n = ada_w.shape
    bsz = c.shape[0]
    rows = 16
    cp =jnp.zeros((rows, d), F32).at[:bsz].set(c)
    tn = 1536
    out = pl.pallas_call(
        _ada_kernel,
        grid=(depth, n // tn),
        in_specs=[
            pl.BlockSpec((rows, d), lambda l, j: (0, 0)),
            pl.BlockSpec((None, d, tn), lambda l, j: (l, 0, j)),
            pl.BlockSpec((None, 1, tn), lambda l, j: (l, 0, j)),
        ],
        out_specs=pl.BlockSpec((None, rows, tn), lambda l, j: (l, 0, j)),
        out_shape=jax.ShapeDtypeStruct((depth, rows, n), F32),
        compiler_params=_cparams(("arbitrary", "arbitrary")),
        name="ada_mod",
    )(cp, ada_w, ada_b.reshape(depth, 1, n))
    return out[:, :bsz]


def _gla_kernel(q_ref, k_ref, v_ref, gr_ref, sm_ref, gup_ref, gb_ref, onw_ref, o_ref, st_ref, tri_ref, *, ts):
    c = GLA_CHUNK

    @pl.when(pl.program_id(1) == 0)
    def _():
        st_ref[...] = jnp.zeros_like(st_ref)
        row = lax.broadcasted_iota(I32, (ts, ts), 0)
        col = lax.broadcasted_iota(I32, (ts, ts), 1)
        tri_ref[...] = jnp.where((row // c == col // c) & (col <= row), 1.0, 0.0).astype(BF16)

    nc = ts // c
    tri = tri_ref[...]
    causal = lax.broadcasted_iota(I32, (c, c), 1) <= lax.broadcasted_iota(I32, (c, c), 0)
    onw = onw_ref[...]

    glow = sm_ref[:, SM_GLOW:SM_GLOW + GLA_GATE_RANK]
    a = _dot(glow.astype(BF16), gup_ref[...].astype(BF16)) + gb_ref[...]
    g = (jnp.minimum(a, 0.0) - jnp.log1p(jnp.exp(-jnp.abs(a)))) / GLA_GATE_TAU
    g_hi, g_mid, g_lo = _split3(g)
    b = _dot(tri, g_hi) + _dot(tri, g_mid) + _dot(tri, g_lo)
    qs = q_ref[...] * (GLA_DK ** -0.5)
    k = k_ref[...]

    qe, ke, kd, qd, dec = [], [], [], [], []
    for ci in range(nc):
        r = slice(ci * c, (ci + 1) * c)
        bc = b[r]
        b_mid = bc[c // 2 - 1:c // 2, :]
        b_last = bc[c - 1:c, :]
        qe.append((qs[r] * jnp.exp(bc - b_mid)).astype(BF16))
        ke.append((k[r] * jnp.exp(b_mid - bc)).astype(BF16))
        kd.append((k[r] * jnp.exp(b_last - bc)).astype(BF16))
        qd.append((qs[r] * jnp.exp(bc)).astype(BF16))
        dec.append(jnp.exp(b_last))

    pairs = [(ci, h) for ci in range(nc) for h in range(GLA_HEADS)]
    ksl = lambda h: slice(h * GLA_DK, (h + 1) * GLA_DK)
    vsl = lambda h: slice(h * GLA_DV, (h + 1) * GLA_DV)
    vh = {(ci, h): v_ref[ci * c:(ci + 1) * c, vsl(h)].astype(BF16) for ci, h in pairs}
    att = {p: _dot_nt(qe[p[0]][:, ksl(p[1])], ke[p[0]][:, ksl(p[1])]) for p in pairs}
    att = {p: jnp.where(causal, att[p], 0.0).astype(BF16) for p in pairs}
    o = {p: _dot(att[p], vh[p]) for p in pairs}
    kvt = {p: _dot_tn(vh[p], kd[p[0]][:, ksl(p[1])]) for p in pairs}
    st_in = {}
    for h in range(GLA_HEADS):
        st = st_ref[h]
        for ci in range(nc):
            st_in[(ci, h)] = st.astype(BF16)
            st = st * dec[ci][:, ksl(h)] + kvt[(ci, h)]
        st_ref[h] = st
    for ci, h in pairs:
        out = o[(ci, h)] + _dot_nt(qd[ci][:, ksl(h)], st_in[(ci, h)])
        grh = gr_ref[ci * c:(ci + 1) * c, vsl(h)]
        o_ref[ci * c:(ci + 1) * c, vsl(h)] = (_rms(out, onw) * _silu(grh)).astype(o_ref.dtype)


def _gla(proj, gate_up, gate_b, out_norm, *, ts):
    bsz, s, _ = proj.shape
    hk, hv = GLA_HEADS * GLA_DK, GLA_HEADS * GLA_DV
    return pl.pallas_call(
        functools.partial(_gla_kernel, ts=ts),
        grid=(bsz, s // ts),
        in_specs=[
            pl.BlockSpec((None, ts, hk), lambda b, i: (b, i, AB_GQ // hk)),
            pl.BlockSpec((None, ts, hk), lambda b, i: (b, i, AB_GK // hk)),
            pl.BlockSpec((None, ts, hv), lambda b, i: (b, i, AB_GV // hv)),
            pl.BlockSpec((None, ts, hv), lambda b, i: (b, i, AB_GR // hv)),
            pl.BlockSpec((None, ts, LANES), lambda b, i: (b, i, AB_SMALL // LANES)),
            pl.BlockSpec((GLA_GATE_RANK, hk), lambda b, i: (0, 0)),
            pl.BlockSpec((1, hk), lambda b, i: (0, 0)),
            pl.BlockSpec((1, GLA_DV), lambda b, i: (0, 0)),
        ],
        out_specs=pl.BlockSpec((None, ts, hv), lambda b, i: (b, i, 0)),
        out_shape=jax.ShapeDtypeStruct((bsz, s, hv), BF16),
        scratch_shapes=[pltpu.VMEM((GLA_HEADS, GLA_DV, GLA_DK), F32), pltpu.VMEM((ts, ts), BF16)],
        compiler_params=_cparams(("arbitrary", "arbitrary")),
        name="gla",
    )(proj, proj, proj, proj, proj, gate_up, gate_b.reshape(1, hk), out_norm.reshape(1, GLA_DV))


def _dsa_prep_rows(rows, dsa_ref, sm_ref, pos_ref, qn_ref, kn_ref, freq_ref, q_o, k_o, v_o, iq_o, ik_o, iw_o):
    hd = DSA_HEADS * DSA_HEAD_DIM
    pos = pos_ref[rows, :].astype(F32)
    lane = lax.broadcasted_iota(I32, (1, LANES), 1)
    half_a, half_i = DSA_HEAD_DIM // 2, IDX_DIM // 2

    ang = pos * freq_ref[...]
    cos_t, sin_t = jnp.cos(ang), jnp.sin(ang)

    def attn_table(t):
        return jnp.where(lane < half_a, t, pltpu.roll(t, half_a, 1))

    def idx_table(t):
        return jnp.where(lane < half_i, pltpu.roll(t, half_a, 1),
                         jnp.where(lane < 2 * half_i, pltpu.roll(t, half_a + half_i, 1), t))

    cos_a = attn_table(cos_t)
    sin_a = jnp.where(lane < half_a, -1.0, 1.0) * attn_table(sin_t)

    def rope_attn(t):
        return t * cos_a + pltpu.roll(t, half_a, 1) * sin_a

    first = (lane % IDX_DIM) < half_i
    cos_i = idx_table(cos_t)
    sin_i = jnp.where(first, -1.0, 1.0) * idx_table(sin_t)

    def rope_idx(t):
        rot = jnp.where(first, pltpu.roll(t, LANES - IDX_DIM // 2, 1), pltpu.roll(t, IDX_DIM // 2, 1))
        return t * cos_i + rot * sin_i

    qn, kn = qn_ref[...], kn_ref[...]
    for h in range(DSA_HEADS):
        hs = slice(h * DSA_HEAD_DIM, (h + 1) * DSA_HEAD_DIM)
        vs = slice(2 * hd + h * DSA_HEAD_DIM, 2 * hd + (h + 1) * DSA_HEAD_DIM)
        q_o[rows, hs] = (rope_attn(_rms(dsa_ref[rows, hs], qn)) * DSA_Q_SCALE).astype(q_o.dtype)
        k_o[rows, hs] = rope_attn(_rms(dsa_ref[rows, hd + h * DSA_HEAD_DIM:hd + (h + 1) * DSA_HEAD_DIM], kn)
                                  ).astype(k_o.dtype)
        tk = v_o.shape[-1]
        n = rows.stop - rows.start
        assert tk % n == 0 and rows.start % n == 0
        v_o[rows.start // tk, hs, rows.start % tk:rows.start % tk + n] = dsa_ref[rows, vs].T.astype(v_o.dtype)
    for j in range(IDX_HEADS * IDX_DIM // LANES):
        r = rope_idx(dsa_ref[rows, 3 * hd + j * LANES:3 * hd + (j + 1) * LANES])
        iq_o[2 * j, rows, :] = r[:, :IDX_DIM].astype(iq_o.dtype)
        iq_o[2 * j + 1, rows, :] = r[:, IDX_DIM:].astype(iq_o.dtype)
    sm = sm_ref[rows, :]
    ik_o[rows, :] = rope_idx(sm)[:, SM_IK:SM_IK + IDX_DIM].astype(ik_o.dtype)
    iw_o[:, rows] = sm.T[SM_IW:SM_IW + IDX_HEADS, :]


def _proj_ab_kernel(x_ref, nw_ref, sh_ref, sc_ref, w_ref, pos_ref, qn_ref, kn_ref, freq_ref,
                    gla_o, q_o, k_o, v_o, iq_o, ik_o, iw_o, dsa_ref):
    hb = (_rms(x_ref[...], nw_ref[...]) * (1.0 + sc_ref[...]) + sh_ref[...]).astype(BF16)
    hd = DSA_HEADS * DSA_HEAD_DIM
    tm = x_ref.shape[0]
    halves = [slice(0, tm // 2), slice(tm // 2, tm)]
    small = gla_o.at[:, AB_SMALL:AB_SMALL + LANES]
    for rows in halves:
        gla_o[rows, AB_SMALL:] = _dot(hb[rows], w_ref[:, AB_SMALL:AB_GLA_N])
        for j in range(AB_DSA_N // hd):
            dsa_ref[rows, j * hd:(j + 1) * hd] = _dot(hb[rows], w_ref[:, AB_GLA_N + j * hd:AB_GLA_N + (j + 1) * hd])
    for rows in halves:
        _dsa_prep_rows(rows, dsa_ref, small, pos_ref, qn_ref, kn_ref, freq_ref, q_o, k_o, v_o, iq_o, ik_o, iw_o)
        gla_o[rows, :AB_SMALL] = _dot(hb[rows], w_ref[:, :AB_SMALL])


def _proj_ab(x, nw, shift, scale, w_bf16, positions, q_norm, k_norm, *, tm, tk):
    bsz, s, d = x.shape
    hd = DSA_HEADS * DSA_HEAD_DIM
    half_a, half_i = DSA_HEAD_DIM // 2, IDX_DIM // 2
    inv_a = ROPE_THETA ** (-jnp.arange(half_a, dtype=F32) / half_a)
    inv_i = ROPE_THETA ** (-jnp.arange(half_i, dtype=F32) / half_i)
    freq = jnp.concatenate([inv_a, inv_i, inv_i]).reshape(1, LANES)
    rows = lambda n: pl.BlockSpec((None, tm, n), lambda b, i: (b, i, 0))
    vec = lambda n: pl.BlockSpec((1, n), lambda b, i: (0, 0))
    mod = pl.BlockSpec((None, 1, d), lambda b, i: (b, 0, 0))
    return pl.pallas_call(
        _proj_ab_kernel,
        grid=(bsz, s // tm),
        in_specs=[
            rows(d), vec(d), mod, mod,
            _resident((d, AB_GLA_N + AB_DSA_N), lambda b, i: (0, 0)),
            rows(1), vec(DSA_HEAD_DIM), vec(DSA_HEAD_DIM), vec(LANES),
        ],
        out_specs=[
            rows(AB_GLA_N), rows(hd), rows(hd),
            pl.BlockSpec((None, tm // tk, hd, tk), lambda b, i: (b, i, 0, 0)),
            pl.BlockSpec((None, IDX_HEADS, tm, IDX_DIM), lambda b, i: (b, 0, i, 0)),
            rows(IDX_DIM),
            pl.BlockSpec((None, IDX_HEADS, tm), lambda b, i: (b, 0, i)),
        ],
        out_shape=[
            jax.ShapeDtypeStruct((bsz, s, AB_GLA_N), F32),
            jax.ShapeDtypeStruct((bsz, s, hd), BF16),
            jax.ShapeDtypeStruct((bsz, s, hd), BF16),
            jax.ShapeDtypeStruct((bsz, s // tk, hd, tk), BF16),
            jax.ShapeDtypeStruct((bsz, IDX_HEADS, s, IDX_DIM), BF16),
            jax.ShapeDtypeStruct((bsz, s, IDX_DIM), BF16),
            jax.ShapeDtypeStruct((bsz, IDX_HEADS, s), F32),
        ],
        scratch_shapes=[pltpu.VMEM((tm, AB_DSA_N), F32)],
        compiler_params=_cparams(("arbitrary", "arbitrary")),
        name="proj_ab",
    )(x, nw.reshape(1, d), shift, scale, w_bf16, positions.reshape(bsz, s, 1),
      q_norm.reshape(1, -1), k_norm.reshape(1, -1), freq)


def _dsa_kernel(q_ref, iq_ref, iw_ref, k_ref, vt_ref, ik_ref, o_ref, sc_ref, top_ref, acc_ref,
                *, tq, n_sel, s_len):
    i = pl.program_id(1)
    nkb = i + 1
    kofs = lax.broadcasted_iota(I32, (tq, tq), 0)
    qidx = i * tq + lax.broadcasted_iota(I32, (tq, tq), 1)
    iw = iw_ref[...]
    idx_scale = (IDX_HEADS ** -0.5) * (IDX_DIM ** -0.5)

    def over_tiles(step, init, n=nkb, widest=2):
        carry = lax.fori_loop(0, n // widest, lambda j, c: step(widest * j, widest, c), init)
        rest = n % widest
        start = n - rest
        width = widest // 2
        while width >= 1:
            carry = lax.cond((rest & width) != 0, functools.partial(step, start, width), lambda c: c, carry)
            start = start + (rest & width)
            width //= 2
        return carry

    def as_float(key):
        key = jnp.maximum(key, KEY_LOWEST)
        key = jnp.where((key > 0) & (key < KEY_TINY), KEY_TINY, key)
        return pltpu.bitcast(key ^ ((key >> 31) & 0x7FFFFFFF), F32)

    def scores(kb, width, diagonal):
        k0 = pl.multiple_of(kb * tq, tq)
        rows = width * tq
        ikb = ik_ref[pl.ds(k0, rows), :]
        sc = jnp.zeros((rows, tq), F32)
        for h in range(IDX_HEADS):
            sc = sc + jnp.maximum(_dot_nt(ikb, iq_ref[h]), 0.0) * iw[h:h + 1, :]
        sc = sc * idx_scale
        sc = jnp.where(jnp.abs(sc) < FLT_TINY, 0.0, sc)
        if diagonal:
            sc = jnp.where(kofs <= qidx - i * tq, sc, -jnp.inf)
        sc_ref[pl.ds(kb, width)] = sc.reshape(width, tq, tq)
        top_ref[pl.ds(kb, width)] = sc.astype(BF16).reshape(width, tq, tq)

    def below_diagonal(kb, width, carry):
        scores(kb, width, False)
        return carry

    over_tiles(below_diagonal, 0, n=i)
    scores(i, 1, True)

    def count(pred):
        def body(kb, width, acc):
            rows = width * tq
            m = jnp.where(pred(sc_ref[pl.ds(kb, width)].reshape(rows, tq)), 1, 0)
            return acc + jnp.sum(m.reshape(rows // 8, 8, tq), axis=0)
        return jnp.sum(over_tiles(body, jnp.zeros((8, tq), I32)), axis=0, keepdims=True)

    n_valid = qidx[0:1, :] + 1

    def as_bf16(cand):
        k16 = jnp.maximum(cand, KEY_LOWEST_BF16) >> 16
        pattern = k16 ^ ((k16 >> 15) & 0x7FFF)
        pattern = jnp.where((pattern > 0) & (pattern < 0x80), 0x80, pattern)
        return pltpu.bitcast(pattern << 16, F32).astype(BF16)

    def count_top(cand):
        cand_bf = as_bf16(cand)

        def body(kb, width, acc):
            rows = width * tq
            top = top_ref[pl.ds(kb, width)].reshape(rows, tq)
            m = jnp.where(top >= cand_bf, jnp.ones((), BF16), jnp.zeros((), BF16))
            part = m[0:16]
            for j in range(1, rows // 16):
                part = part + m[16 * j:16 * (j + 1)]
            return acc + part.astype(F32)
        acc = over_tiles(body, jnp.zeros((16, tq), F32))
        return jnp.sum(acc, axis=0, keepdims=True).astype(I32)

    def top_step(it, u):
        cand_u = u | lax.shift_left(jnp.int32(1), jnp.int32(31) - it)
        return jnp.where(count_top(cand_u ^ INT_MIN) >= n_sel, cand_u, u)

    u = lax.fori_loop(0, 16, top_step, jnp.zeros((1, tq), I32))

    def any_lane(flag):
        return jnp.max(jnp.where(flag, 1.0, 0.0)) > 0.0

    base = u ^ INT_MIN
    base = jnp.where(base < 0, base | 0xFFFF, base)
    base = jnp.clip(base, KEY_LOWEST + BF16_HALF_STEP + 1, INT_MAX - 2 * BF16_HALF_STEP)
    many = n_valid > n_sel
    lo = jnp.where(many, base - BF16_HALF_STEP - 1, KEY_LOWEST)
    hi = jnp.where(many, base + 2 * BF16_HALF_STEP, INT_MAX)

    def count_two(pred_a, pred_b):
        def body(kb, width, acc):
            x = sc_ref[pl.ds(kb, width)].reshape(width * tq // 8, 8, tq)
            return (acc[0] + jnp.sum(jnp.where(pred_a(x), 1, 0), axis=0),
                    acc[1] + jnp.sum(jnp.where(pred_b(x), 1, 0), axis=0))
        a, b = over_tiles(body, (jnp.zeros((8, tq), I32), jnp.zeros((8, tq), I32)))
        return jnp.sum(a, axis=0, keepdims=True), jnp.sum(b, axis=0, keepdims=True)

    lo_f, hi_f = as_float(lo), as_float(hi)
    cnt_lo, cnt_hi = count_two(lambda x: x >= lo_f, lambda x: x >= hi_f)

    def open_queries(lo, cnt_lo, hi, done):
        return (cnt_lo > n_sel) & (hi - lo > 1) & (done == 0)

    def halve(s):
        lo, cnt_lo, hi, cnt_hi, done = s
        mid = lo + ((hi - lo) >> 1)
        mid_f = as_float(mid)
        cnt = count(lambda x: x >= mid_f)
        live = open_queries(lo, cnt_lo, hi, done)
        up = live & (cnt >= n_sel)
        down = live & (cnt < n_sel)
        return (jnp.where(up, mid, lo), jnp.where(up, cnt, cnt_lo),
                jnp.where(down, mid, hi), jnp.where(down, cnt, cnt_hi), done)

    st = lax.fori_loop(0, PEEL_AFTER_STEPS, lambda _, s: halve(s),
                       (lo, cnt_lo, hi, cnt_hi, jnp.zeros((1, tq), I32)))

    def peel(st):
        lo, cnt_lo, hi, cnt_hi, done = st
        lo_f, hi_f = as_float(lo), as_float(hi)

        def extremes(kb, width, carry):
            top, bot = carry
            x = sc_ref[pl.ds(kb, width)].reshape(width * tq // 8, 8, tq)
            top = jnp.maximum(top, jnp.max(jnp.where(x < hi_f, x, -jnp.inf), axis=0))
            bot = jnp.minimum(bot, jnp.min(jnp.where(x >= lo_f, x, jnp.inf), axis=0))
            return top, bot
        top, bot = over_tiles(extremes, (jnp.full((8, tq), -jnp.inf, F32), jnp.full((8, tq), jnp.inf, F32)))
        top = jnp.max(top, axis=0, keepdims=True)
        bot = jnp.min(bot, axis=0, keepdims=True)
        from_top = (cnt_hi == n_sel - 1) | (top == bot)
        from_bot = (cnt_lo == n_sel + 1) & jnp.logical_not(from_top)
        cnt_top, cnt_bot = count_two(lambda x: x >= top, lambda x: x > bot)
        cnt = jnp.where(from_top, cnt_top, cnt_bot)
        short = from_bot & (cnt < n_sel)
        hit = open_queries(lo, cnt_lo, hi, done) & (from_top | from_bot)
        thr = jnp.where(from_top, top, bot)
        incl = jnp.where(from_top | short, 1, 0)
        return hit, thr, incl, jnp.where(short, cnt_lo, cnt)

    def with_peel(st):
        hit, thr, incl, cnt = peel(st)
        st = st[:4] + (jnp.where(hit, 1, 0),)
        lo, cnt_lo, _, _, _ = lax.while_loop(lambda s: any_lane(open_queries(s[0], s[1], s[2], s[4])), halve, st)
        return jnp.where(hit, thr, as_float(lo)), jnp.where(hit, incl, 1), jnp.where(hit, cnt, cnt_lo)

    def without_peel(st):
        lo, cnt_lo, _, _, _ = st
        return as_float(lo), jnp.ones((1, tq), I32), cnt_lo

    thr, incl, cnt = lax.cond(any_lane(open_queries(st[0], st[1], st[2], st[4])), with_peel, without_peel, st)
    admits_equal = incl != 0
    excess = cnt > n_sel

    tied = jnp.max(jnp.where(excess, 1.0, 0.0)) > 0.0

    acc_ref[...] = jnp.zeros_like(acc_ref)

    def attend(select, seen):
        def block(kb, width, carry):
            ms, ls, seen = carry
            k0 = pl.multiple_of(kb * tq, tq)
            rows = width * tq
            sel, seen = select(sc_ref[pl.ds(kb, width)].reshape(rows, tq), seen)
            heads = [slice(h * DSA_HEAD_DIM, (h + 1) * DSA_HEAD_DIM) for h in range(DSA_HEADS)]
            st = [_dot_nt(k_ref[pl.ds(k0, rows), hs], q_ref[:, hs]) for hs in heads]
            bias = jnp.where(sel, 0.0, MASK_BIAS)
            st = [x + bias for x in st]
            new_m = [jnp.maximum(m, jnp.max(x, axis=0, keepdims=True)) for m, x in zip(ms, st)]
            alpha = [jnp.exp2(m - mn) for m, mn in zip(ms, new_m)]
            p = [jnp.exp2(x - mn) for x, mn in zip(st, new_m)]
            new_l = [a * l + jnp.sum(x, axis=0, keepdims=True) for a, l, x in zip(alpha, ls, p)]
            p = [x.astype(BF16) for x in p]
            for h, hs in enumerate(heads):
                pv = _dot(vt_ref[kb, hs, :], p[h][0:tq])
                for j in range(1, width):
                    pv = pv + _dot(vt_ref[kb + j, hs, :], p[h][j * tq:(j + 1) * tq])
                acc_ref[h] = alpha[h] * acc_ref[h] + pv
            return tuple(new_m), tuple(new_l), seen

        carry = (tuple(jnp.full((1, tq), -jnp.inf, F32) for _ in range(DSA_HEADS)),
                 tuple(jnp.zeros((1, tq), F32) for _ in range(DSA_HEADS)), seen)
        return over_tiles(block, carry)[1]

    def attend_tied():
        room = n_sel - count(lambda x: x > thr)
        room = jnp.where(excess, room, jnp.where(admits_equal, s_len, 0)).astype(F32)
        r = lax.broadcasted_iota(I32, (tq, tq), 0)
        c = lax.broadcasted_iota(I32, (tq, tq), 1)
        upto = jnp.where(c <= r, 1.0, 0.0).astype(BF16)

        def select(x, seen):
            equal = x == thr
            ones = jnp.where(equal, 1.0, 0.0).astype(BF16)
            fits = []
            for j in range(x.shape[0] // tq):
                rank = _dot(upto, ones[j * tq:(j + 1) * tq])
                fits.append(rank <= room - seen)
                seen = seen + rank[tq - 1:tq, :]
            fits = fits[0] if len(fits) == 1 else jnp.concatenate(fits, axis=0)
            return (x > thr) | (equal & fits), seen
        return attend(select, jnp.zeros((1, tq), F32))

    at_least = jnp.where(admits_equal, thr, jnp.inf)
    above = jnp.where(admits_equal, jnp.inf, thr)
    ls = lax.cond(tied, attend_tied,
                  lambda: attend(lambda x, seen: ((x >= at_least) | (x > above), seen), jnp.zeros((1, 1), F32)))
    for h in range(DSA_HEADS):
        hs = slice(h * DSA_HEAD_DIM, (h + 1) * DSA_HEAD_DIM)
        o_ref[:, hs] = (acc_ref[h] / ls[h]).T.astype(o_ref.dtype)


def _dsa(q, k, vt, iq, ik, iw, *, tq):
    bsz, s, hd = q.shape
    n_sel = min(DSA_TOPK, s // 4)
    resident = lambda n: _resident((None, s, n), lambda b, i: (b, 0, 0))
    return pl.pallas_call(
        functools.partial(_dsa_kernel, tq=tq, n_sel=n_sel, s_len=s),
        grid=(bsz, s // tq),
        in_specs=[
            pl.BlockSpec((None, tq, hd), lambda b, i: (b, i, 0)),
            pl.BlockSpec((None, IDX_HEADS, tq, IDX_DIM), lambda b, i: (b, 0, i, 0)),
            pl.BlockSpec((None, IDX_HEADS, tq), lambda b, i: (b, 0, i)),
            resident(hd),
            _resident((None, s // tq, hd, tq), lambda b, i: (b, 0, 0, 0)),
            resident(IDX_DIM),
        ],
        out_specs=pl.BlockSpec((None, tq, hd), lambda b, i: (b, i, 0)),
        out_shape=jax.ShapeDtypeStruct((bsz, s, hd), BF16),
        scratch_shapes=[
            pltpu.VMEM((s // tq, tq, tq), F32),
            pltpu.VMEM((s // tq, tq, tq), BF16),
            pltpu.VMEM((DSA_HEADS, DSA_HEAD_DIM, tq), F32),
        ],
        compiler_params=_cparams(("arbitrary", "arbitrary"), DSA_VMEM_LIMIT),
        name="dsa",
    )(q, iq, iw, k, vt, ik)


def _mix_ffn_kernel(x_ref, a_ref, b_ref, wa_ref, wb_ref, g1_ref, nw_ref, sh_ref, sc_ref, g2_ref,
                    w1_ref, w2_ref, o_ref, *, tf):
    x = x_ref[...] + g1_ref[...] * (_dot(a_ref[...], wa_ref[...]) + _dot(b_ref[...], wb_ref[...]))
    hb = (_rms(x, nw_ref[...]) * (1.0 + sc_ref[...]) + sh_ref[...]).astype(BF16)
    f = w2_ref.shape[0]
    acc = jnp.zeros(x.shape, F32)
    for j in range(f // tf):
        gate = _dot(hb, w1_ref[:, j * tf:(j + 1) * tf])
        up = _dot(hb, w1_ref[:, f + j * tf:f + (j + 1) * tf])
        acc = acc + _dot((_silu(gate) * up).astype(BF16), w2_ref[j * tf:(j + 1) * tf, :])
    o_ref[...] = x + g2_ref[...] * acc


def _mix_ffn(x, mix_a, mix_b, w_out_bf16, gate1, nw, shift, scale, gate2, w1_bf16, w2_bf16, *, tm, tf):
    bsz, s, d = x.shape
    f = w2_bf16.shape[0]
    na = mix_a.shape[-1]
    assert mix_b.shape[-1] == na and w_out_bf16.shape[0] == 2 * na
    rows = lambda n: pl.BlockSpec((None, tm, n), lambda b, i: (b, i, 0))
    mod = pl.BlockSpec((None, 1, d), lambda b, i: (b, 0, 0))
    return pl.pallas_call(
        functools.partial(_mix_ffn_kernel, tf=tf),
        grid=(bsz, s // tm),
        in_specs=[
            rows(d), rows(na), rows(na),
            _resident((na, d), lambda b, i: (0, 0)),
            _resident((na, d), lambda b, i: (1, 0)),
            mod,
            pl.BlockSpec((1, d), lambda b, i: (0, 0)),
            mod, mod, mod,
            _resident((d, 2 * f), lambda b, i: (0, 0)),
            _resident((f, d), lambda b, i: (0, 0)),
        ],
        out_specs=rows(d),
        out_shape=jax.ShapeDtypeStruct((bsz, s, d), F32),
        compiler_params=_cparams(("arbitrary", "arbitrary")),
        name="mix_ffn",
    )(x, mix_a, mix_b, w_out_bf16, w_out_bf16, gate1, nw.reshape(1, d), shift, scale, gate2,
      w1_bf16, w2_bf16)


HALO = max(POOL_WINDOWS)


def _proj_cd_kernel(x_ref, nw_ref, sh_ref, sc_ref, w_ref, pw_ref, ps_ref, qn_ref, kn_ref,
                    pool_o, q_o, k_o, v_o, ext_ref, sb_ref):
    i = pl.program_id(1)
    tm = x_ref.shape[0]
    n = POOL_GROUPS * POOL_CH
    hb = (_rms(x_ref[...], nw_ref[...]) * (1.0 + sc_ref[...]) + sh_ref[...]).astype(BF16)

    @pl.when(i == 0)
    def _():
        ext_ref[0:HALO, :] = jnp.zeros((HALO, n), F32)

    ext_ref[HALO:HALO + tm, :] = _dot(hb, w_ref[:, :n])
    for j in range(3):
        sb_ref[:, j * n:(j + 1) * n] = _dot(hb, w_ref[:, (j + 1) * n:(j + 2) * n])

    t = i * tm + lax.broadcasted_iota(I32, (tm, 1), 0)
    for g, w in enumerate(POOL_WINDOWS):
        gs = slice(g * POOL_CH, (g + 1) * POOL_CH)
        u = ext_ref[HALO:HALO + tm, gs]
        tot = u
        for j in range(1, w):
            tot = tot + ext_ref[HALO - j:HALO - j + tm, gs]
        cnt = jnp.minimum(t + 1, w).astype(F32)
        pooled = tot / cnt - u
        y = _dot(pooled.astype(BF16), pw_ref[g].astype(BF16)) * ps_ref[:, gs]
        pool_o[:, gs] = y.astype(pool_o.dtype)
    ext_ref[0:HALO, :] = ext_ref[tm:tm + HALO, :]

    qn, kn = qn_ref[...], kn_ref[...]
    for h in range(SB_HEADS):
        hs = slice(h * SB_HEAD_DIM, (h + 1) * SB_HEAD_DIM)
        q_o[:, hs] = (_rms(sb_ref[:, hs], qn) * SB_HEAD_DIM ** -0.5).astype(q_o.dtype)
        k_o[:, hs] = _rms(sb_ref[:, n + h * SB_HEAD_DIM:n + (h + 1) * SB_HEAD_DIM], kn).astype(k_o.dtype)
    tk = v_o.shape[-1]
    for j in range(v_o.shape[0]):
        v_o[j] = sb_ref[j * tk:(j + 1) * tk, 2 * n:3 * n].T.astype(v_o.dtype)


def _proj_cd(x, nw, shift, scale, w_bf16, pool_w, pool_scale, q_norm, k_norm, *, tm, tk):
    bsz, s, d = x.shape
    n = POOL_GROUPS * POOL_CH
    assert w_bf16.shape[1] == 4 * n
    rows = lambda m: pl.BlockSpec((None, tm, m), lambda b, i: (b, i, 0))
    vec = lambda m: pl.BlockSpec((1, m), lambda b, i: (0, 0))
    mod = pl.BlockSpec((None, 1, d), lambda b, i: (b, 0, 0))
    return pl.pallas_call(
        _proj_cd_kernel,
        grid=(bsz, s // tm),
        in_specs=[
            rows(d), vec(d), mod, mod,
            _resident((d, 4 * n), lambda b, i: (0, 0)),
            pl.BlockSpec((POOL_GROUPS, POOL_CH, POOL_CH), lambda b, i: (0, 0, 0)),
            vec(n), vec(SB_HEAD_DIM), vec(SB_HEAD_DIM),
        ],
        out_specs=[rows(n)] * 3 + [pl.BlockSpec((None, tm // tk, n, tk), lambda b, i: (b, i, 0, 0))],
        out_shape=[jax.ShapeDtypeStruct((bsz, s, n), BF16)] * 3
        + [jax.ShapeDtypeStruct((bsz, s // tk, n, tk), BF16)],
        scratch_shapes=[pltpu.VMEM((HALO + tm, n), F32), pltpu.VMEM((tm, 3 * n), F32)],
        compiler_params=_cparams(("arbitrary", "arbitrary")),
        name="proj_cd",
    )(x, nw.reshape(1, d), shift, scale, w_bf16, pool_w, pool_scale.reshape(1, n),
      q_norm.reshape(1, -1), k_norm.reshape(1, -1))


def _sb_kernel(q_ref, k_ref, vt_ref, o_ref, acc_ref, *, tq):
    i = pl.program_id(1)
    kofs = lax.broadcasted_iota(I32, (tq, tq), 0)
    qofs = lax.broadcasted_iota(I32, (tq, tq), 1)
    later = jnp.where(qofs > kofs, 1.0, 0.0).astype(BF16)
    heads = [slice(h * SB_HEAD_DIM, (h + 1) * SB_HEAD_DIM) for h in range(SB_HEADS)]
    acc_ref[...] = jnp.zeros_like(acc_ref)

    strict = kofs < qofs

    def walk(tiles, runs):
        pairs = [(t, h) for t in range(len(tiles)) for h in range(SB_HEADS)]
        rows = [pl.ds(pl.multiple_of(kb * tq, tq), tq) for kb, _ in tiles]
        z = {(t, h): _dot_nt(k_ref[rows[t], heads[h]], q_ref[:, heads[h]]) for t, h in pairs}
        sp = {p: jnp.log(1.0 + jnp.exp(-jnp.abs(z[p]))) for p in pairs}
        log_beta = {p: jnp.minimum(z[p], 0.0) - sp[p] for p in pairs}
        log_1m = {p: log_beta[p] - z[p] for p in pairs}
        log_1m = {p: jnp.where(strict, log_1m[p], 0.0) if tiles[p[0]][1] else log_1m[p] for p in pairs}
        hi = {p: log_1m[p].astype(BF16) for p in pairs}
        lo = {p: (log_1m[p] - hi[p].astype(F32)).astype(BF16) for p in pairs}
        within = {p: _dot(later, hi[p]) + _dot(later, lo[p]) for p in pairs}
        runs = list(runs)
        for t, (kb, diagonal) in enumerate(tiles):
            for h in range(SB_HEADS):
                w = jnp.exp(log_beta[(t, h)] + within[(t, h)] + runs[h])
                if diagonal:
                    w = jnp.where(strict, w, 0.0)
                acc_ref[h] = acc_ref[h] + _dot(vt_ref[kb, heads[h], :], w.astype(BF16))
                runs[h] = runs[h] + jnp.sum(log_1m[(t, h)], axis=0, keepdims=True)
        return tuple(runs)

    def alive(runs):
        worst = functools.reduce(jnp.maximum, runs)
        return jnp.max(worst) > SB_DEAD_LOG

    zero = tuple(jnp.zeros((1, tq), F32) for _ in heads)
    runs = lax.cond(i >= 1, lambda: walk([(i, True), (i - 1, False)], zero), lambda: walk([(i, True)], zero))

    def earlier(state):
        kb, runs, _ = state
        runs = walk([(kb, False)], runs)
        return kb - 1, runs, alive(runs)

    lax.while_loop(lambda st: (st[0] >= 0) & st[2], earlier, (i - 2, runs, alive(runs)))
    for h, hs in enumerate(heads):
        o_ref[:, hs] = acc_ref[h].T.astype(o_ref.dtype)


def _sb(q, k, vt, *, tq):
    bsz, s, hd = q.shape
    return pl.pallas_call(
        functools.partial(_sb_kernel, tq=tq),
        grid=(bsz, s // tq),
        in_specs=[
            pl.BlockSpec((None, tq, hd), lambda b, i: (b, i, 0)),
            _resident((None, s, hd), lambda b, i: (b, 0, 0)),
            _resident((None, s // tq, hd, tq), lambda b, i: (b, 0, 0, 0)),
        ],
        out_specs=pl.BlockSpec((None, tq, hd), lambda b, i: (b, i, 0)),
        out_shape=jax.ShapeDtypeStruct((bsz, s, hd), BF16),
        scratch_shapes=[pltpu.VMEM((SB_HEADS, SB_HEAD_DIM, tq), F32)],
        compiler_params=_cparams(("arbitrary", "arbitrary")),
        name="stick_breaking",
    )(q, k, vt)


def _pack_ab_weight(w):
    d = w.shape[0]
    gq, gk, gv, glow, gr, dq, dk, dv, iq, ik, iw = jnp.split(
        w, [256, 512, 1024, 1040, 1552, 2064, 2576, 3088, 3600, 3664], axis=1)
    pad = jnp.zeros((d, LANES - (IDX_DIM + GLA_GATE_RANK + IDX_HEADS)), w.dtype)
    return jnp.concatenate([gq, gk, gv, gr, ik, glow, iw, pad, dq, dk, dv, iq], axis=1)


def kernel(x, c, positions, ada_w, ada_b, mix_norm, ffn_norm, ffn_w1, ffn_w2, ab_w_in, gla_gate_up,
           gla_gate_b, gla_out_norm, dsa_q_norm, dsa_k_norm, ab_w_out, cd_w_in, pool_w, pool_scale,
           sb_q_norm, sb_k_norm, cd_w_out):
    bsz, s, d = x.shape
    depth = ada_w.shape[0]
    mod = _ada_mod(c, ada_w, ada_b).reshape(depth, bsz, 6, 1, d)
    tm = min(512, s)
    tq_dsa = min(512, s)
    tq_sb = min(256, s)
    for layer in range(depth):
        sh1, sc1, g1, sh2, sc2, g2 = (mod[layer, :, j] for j in range(6))
        i = layer // 2
        if layer % 2 == 0:
            proj, q, k, vt, iq, ik, iw = _proj_ab(
                x, mix_norm[layer], sh1, sc1, _pack_ab_weight(ab_w_in[i]).astype(BF16), positions,
                dsa_q_norm[i], dsa_k_norm[i], tm=tq_dsa, tk=tq_dsa)
            mix_a = _gla(proj, gla_gate_up[i], gla_gate_b[i], gla_out_norm[i], ts=tm)
            mix_b = _dsa(q, k, vt, iq, ik, iw, tq=tq_dsa)
            w_out = ab_w_out[i]
        else:
            mix_a, q, k, vt = _proj_cd(x, mix_norm[layer], sh1, sc1, cd_w_in[i].astype(BF16), pool_w[i],
                                       pool_scale[i], sb_q_norm[i], sb_k_norm[i], tm=tm, tk=tq_sb)
            mix_b = _sb(q, k, vt, tq=tq_sb)
            w_out = cd_w_out[i]
        x = _mix_ffn(x, mix_a, mix_b, w_out.astype(BF16), g1, ffn_norm[layer], sh2, sc2, g2,
                     ffn_w1[layer].astype(BF16), ffn_w2[layer].astype(BF16), tm=tm, tf=256)
    return x
```

```python
import functools

import jax
import jax.numpy as jnp
from jax import lax
from jax.experimental import pallas as pl
from jax.experimental.pallas import tpu as pltpu

F32 = jnp.float32
BF16 = jnp.bfloat16
I32 = jnp.int32

GLA_HEADS, GLA_DK, GLA_DV = 4, 64, 128
GLA_GATE_RANK = 16
GLA_GATE_TAU = 16.0
GLA_CHUNK = 64
DSA_HEADS, DSA_HEAD_DIM = 4, 128
IDX_HEADS, IDX_DIM = 8, 64
DSA_TOPK = 256
POOL_WINDOWS = (2, 4, 8, 16)
POOL_GROUPS, POOL_CH = 4, 128
SB_HEADS, SB_HEAD_DIM = 4, 128
ROPE_THETA = 10000.0
NORM_EPS = 1e-6

LANES = 128
VMEM_LIMIT = 56 * 1024 * 1024
DSA_VMEM_LIMIT = 61 * 1024 * 1024

AB_GQ, AB_GK, AB_GV, AB_GR = 0, 256, 512, 1024
AB_SMALL = 1536
SM_IK, SM_GLOW, SM_IW = 0, 64, 80
AB_GLA_N = AB_SMALL + LANES
AB_DSA_N = 4 * 512

DSA_Q_SCALE = 1.4426950408889634 * DSA_HEAD_DIM ** -0.5
INT_MIN = -2 ** 31
INT_MAX = 2 ** 31 - 1
FLT_TINY = 2.0 ** -126
KEY_TINY = 0x00800000
KEY_LOWEST = INT_MIN + 0x00800000
KEY_LOWEST_BF16 = INT_MIN + 0x00810000
BF16_HALF_STEP = 0x8000
PEEL_AFTER_STEPS = 5
MASK_BIAS = -1e30
SB_DEAD_LOG = -110.0


def _dot(a, b):
    return jnp.dot(a, b, preferred_element_type=F32)


def _dot_nt(a, b):
    return lax.dot_general(a, b, (((1,), (1,)), ((), ())), preferred_element_type=F32)


def _dot_tn(a, b):
    return lax.dot_general(a, b, (((0,), (0,)), ((), ())), preferred_element_type=F32)


def _split3(a):
    hi = a.astype(BF16)
    r1 = a - hi.astype(F32)
    mid = r1.astype(BF16)
    lo = (r1 - mid.astype(F32)).astype(BF16)
    return hi, mid, lo


def _silu(x):
    return x * jax.nn.sigmoid(x)


def _rms(x, w):
    var = jnp.mean(x * x, axis=-1, keepdims=True)
    return x * lax.rsqrt(var + NORM_EPS) * w


def _cparams(sem, vmem_limit=VMEM_LIMIT):
    return pltpu.CompilerParams(dimension_semantics=sem, vmem_limit_bytes=vmem_limit)


def _resident(block_shape, index_map):
    return pl.BlockSpec(block_shape, index_map, pipeline_mode=pl.Buffered(1))


def _ada_kernel(c_ref, w_ref, b_ref, o_ref):
    cond = _silu(c_ref[...])
    c_hi, c_mid, _ = _split3(cond)
    w_hi, w_mid, _ = _split3(w_ref[...])
    acc = _dot(c_hi, w_hi) + _dot(c_hi, w_mid) + _dot(c_mid, w_hi)
    o_ref[...] = acc + b_ref[...]


def _ada_mod(c, ada_w, ada_b):
    depth, d, n = ada_w.shape
    bsz = c.shape[0]
    rows = 16
    cp =jnp.zeros((rows, d), F32).at[:bsz].set(c)
    tn = 1536
    out = pl.pallas_call(
        _ada_kernel,
        grid=(depth, n // tn),
        in_specs=[
            pl.BlockSpec((rows, d), lambda l, j: (0, 0)),
            pl.BlockSpec((None, d, tn), lambda l, j: (l, 0, j)),
            pl.BlockSpec((None, 1, tn), lambda l, j: (l, 0, j)),
        ],
        out_specs=pl.BlockSpec((None, rows, tn), lambda l, j: (l, 0, j)),
        out_shape=jax.ShapeDtypeStruct((depth, rows, n), F32),
        compiler_params=_cparams(("arbitrary", "arbitrary")),
        name="ada_mod",
    )(cp, ada_w, ada_b.reshape(depth, 1, n))
    return out[:, :bsz]


def _gla_kernel(q_ref, k_ref, v_ref, gr_ref, sm_ref, gup_ref, gb_ref, onw_ref, o_ref, st_ref, tri_ref, *, ts):
    c = GLA_CHUNK

    @pl.when(pl.program_id(1) == 0)
    def _():
        st_ref[...] = jnp.zeros_like(st_ref)
        row = lax.broadcasted_iota(I32, (ts, ts), 0)
        col = lax.broadcasted_iota(I32, (ts, ts), 1)
        tri_ref[...] = jnp.where((row // c == col // c) & (col <= row), 1.0, 0.0).astype(BF16)

    nc = ts // c
    tri = tri_ref[...]
    causal = lax.broadcasted_iota(I32, (c, c), 1) <= lax.broadcasted_iota(I32, (c, c), 0)
    onw = onw_ref[...]

    glow = sm_ref[:, SM_GLOW:SM_GLOW + GLA_GATE_RANK]
    a = _dot(glow.astype(BF16), gup_ref[...].astype(BF16)) + gb_ref[...]
    g = (jnp.minimum(a, 0.0) - jnp.log1p(jnp.exp(-jnp.abs(a)))) / GLA_GATE_TAU
    g_hi, g_mid, g_lo = _split3(g)
    b = _dot(tri, g_hi) + _dot(tri, g_mid) + _dot(tri, g_lo)
    qs = q_ref[...] * (GLA_DK ** -0.5)
    k = k_ref[...]

    qe, ke, kd, qd, dec = [], [], [], [], []
    for ci in range(nc):
        r = slice(ci * c, (ci + 1) * c)
        bc = b[r]
        b_mid = bc[c // 2 - 1:c // 2, :]
        b_last = bc[c - 1:c, :]
        qe.append((qs[r] * jnp.exp(bc - b_mid)).astype(BF16))
        ke.append((k[r] * jnp.exp(b_mid - bc)).astype(BF16))
        kd.append((k[r] * jnp.exp(b_last - bc)).astype(BF16))
        qd.append((qs[r] * jnp.exp(bc)).astype(BF16))
        dec.append(jnp.exp(b_last))

    pairs = [(ci, h) for ci in range(nc) for h in range(GLA_HEADS)]
    ksl = lambda h: slice(h * GLA_DK, (h + 1) * GLA_DK)
    vsl = lambda h: slice(h * GLA_DV, (h + 1) * GLA_DV)
    vh = {(ci, h): v_ref[ci * c:(ci + 1) * c, vsl(h)].astype(BF16) for ci, h in pairs}
    att = {p: _dot_nt(qe[p[0]][:, ksl(p[1])], ke[p[0]][:, ksl(p[1])]) for p in pairs}
    att = {p: jnp.where(causal, att[p], 0.0).astype(BF16) for p in pairs}
    o = {p: _dot(att[p], vh[p]) for p in pairs}
    kvt = {p: _dot_tn(vh[p], kd[p[0]][:, ksl(p[1])]) for p in pairs}
    st_in = {}
    for h in range(GLA_HEADS):
        st = st_ref[h]
        for ci in range(nc):
            st_in[(ci, h)] = st.astype(BF16)
            st = st * dec[ci][:, ksl(h)] + kvt[(ci, h)]
        st_ref[h] = st
    for ci, h in pairs:
        out = o[(ci, h)] + _dot_nt(qd[ci][:, ksl(h)], st_in[(ci, h)])
        grh = gr_ref[ci * c:(ci + 1) * c, vsl(h)]
        o_ref[ci * c:(ci + 1) * c, vsl(h)] = (_rms(out, onw) * _silu(grh)).astype(o_ref.dtype)


def _gla(proj, gate_up, gate_b, out_norm, *, ts):
    bsz, s, _ = proj.shape
    hk, hv = GLA_HEADS * GLA_DK, GLA_HEADS * GLA_DV
    return pl.pallas_call(
        functools.partial(_gla_kernel, ts=ts),
        grid=(bsz, s // ts),
        in_specs=[
            pl.BlockSpec((None, ts, hk), lambda b, i: (b, i, AB_GQ // hk)),
            pl.BlockSpec((None, ts, hk), lambda b, i: (b, i, AB_GK // hk)),
            pl.BlockSpec((None, ts, hv), lambda b, i: (b, i, AB_GV // hv)),
            pl.BlockSpec((None, ts, hv), lambda b, i: (b, i, AB_GR // hv)),
            pl.BlockSpec((None, ts, LANES), lambda b, i: (b, i, AB_SMALL // LANES)),
            pl.BlockSpec((GLA_GATE_RANK, hk), lambda b, i: (0, 0)),
            pl.BlockSpec((1, hk), lambda b, i: (0, 0)),
            pl.BlockSpec((1, GLA_DV), lambda b, i: (0, 0)),
        ],
        out_specs=pl.BlockSpec((None, ts, hv), lambda b, i: (b, i, 0)),
        out_shape=jax.ShapeDtypeStruct((bsz, s, hv), BF16),
        scratch_shapes=[pltpu.VMEM((GLA_HEADS, GLA_DV, GLA_DK), F32), pltpu.VMEM((ts, ts), BF16)],
        compiler_params=_cparams(("arbitrary", "arbitrary")),
        name="gla",
    )(proj, proj, proj, proj, proj, gate_up, gate_b.reshape(1, hk), out_norm.reshape(1, GLA_DV))


def _dsa_prep_rows(rows, dsa_ref, sm_ref, pos_ref, qn_ref, kn_ref, freq_ref, q_o, k_o, v_o, iq_o, ik_o, iw_o):
    hd = DSA_HEADS * DSA_HEAD_DIM
    pos = pos_ref[rows, :].astype(F32)
    lane = lax.broadcasted_iota(I32, (1, LANES), 1)
    half_a, half_i = DSA_HEAD_DIM // 2, IDX_DIM // 2

    ang = pos * freq_ref[...]
    cos_t, sin_t = jnp.cos(ang), jnp.sin(ang)

    def attn_table(t):
        return jnp.where(lane < half_a, t, pltpu.roll(t, half_a, 1))

    def idx_table(t):
        return jnp.where(lane < half_i, pltpu.roll(t, half_a, 1),
                         jnp.where(lane < 2 * half_i, pltpu.roll(t, half_a + half_i, 1), t))

    cos_a = attn_table(cos_t)
    sin_a = jnp.where(lane < half_a, -1.0, 1.0) * attn_table(sin_t)

    def rope_attn(t):
        return t * cos_a + pltpu.roll(t, half_a, 1) * sin_a

    first = (lane % IDX_DIM) < half_i
    cos_i = idx_table(cos_t)
    sin_i = jnp.where(first, -1.0, 1.0) * idx_table(sin_t)

    def rope_idx(t):
        rot = jnp.where(first, pltpu.roll(t, LANES - IDX_DIM // 2, 1), pltpu.roll(t, IDX_DIM // 2, 1))
        return t * cos_i + rot * sin_i

    qn, kn = qn_ref[...], kn_ref[...]
    for h in range(DSA_HEADS):
        hs = slice(h * DSA_HEAD_DIM, (h + 1) * DSA_HEAD_DIM)
        vs = slice(2 * hd + h * DSA_HEAD_DIM, 2 * hd + (h + 1) * DSA_HEAD_DIM)
        q_o[rows, hs] = (rope_attn(_rms(dsa_ref[rows, hs], qn)) * DSA_Q_SCALE).astype(q_o.dtype)
        k_o[rows, hs] = rope_attn(_rms(dsa_ref[rows, hd + h * DSA_HEAD_DIM:hd + (h + 1) * DSA_HEAD_DIM], kn)
                                  ).astype(k_o.dtype)
        tk = v_o.shape[-1]
        n = rows.stop - rows.start
        assert tk % n == 0 and rows.start % n == 0
        v_o[rows.start // tk, hs, rows.start % tk:rows.start % tk + n] = dsa_ref[rows, vs].T.astype(v_o.dtype)
    for j in range(IDX_HEADS * IDX_DIM // LANES):
        r = rope_idx(dsa_ref[rows, 3 * hd + j * LANES:3 * hd + (j + 1) * LANES])
        iq_o[2 * j, rows, :] = r[:, :IDX_DIM].astype(iq_o.dtype)
        iq_o[2 * j + 1, rows, :] = r[:, IDX_DIM:].astype(iq_o.dtype)
    sm = sm_ref[rows, :]
    ik_o[rows, :] = rope_idx(sm)[:, SM_IK:SM_IK + IDX_DIM].astype(ik_o.dtype)
    iw_o[:, rows] = sm.T[SM_IW:SM_IW + IDX_HEADS, :]


def _proj_ab_kernel(x_ref, nw_ref, sh_ref, sc_ref, w_ref, pos_ref, qn_ref, kn_ref, freq_ref,
                    gla_o, q_o, k_o, v_o, iq_o, ik_o, iw_o, dsa_ref):
    hb = (_rms(x_ref[...], nw_ref[...]) * (1.0 + sc_ref[...]) + sh_ref[...]).astype(BF16)
    hd = DSA_HEADS * DSA_HEAD_DIM
    tm = x_ref.shape[0]
    halves = [slice(0, tm // 2), slice(tm // 2, tm)]
    small = gla_o.at[:, AB_SMALL:AB_SMALL + LANES]
    for rows in halves:
        gla_o[rows, AB_SMALL:] = _dot(hb[rows], w_ref[:, AB_SMALL:AB_GLA_N])
        for j in range(AB_DSA_N // hd):
            dsa_ref[rows, j * hd:(j + 1) * hd] = _dot(hb[rows], w_ref[:, AB_GLA_N + j * hd:AB_GLA_N + (j + 1) * hd])
    for rows in halves:
        _dsa_prep_rows(rows, dsa_ref, small, pos_ref, qn_ref, kn_ref, freq_ref, q_o, k_o, v_o, iq_o, ik_o, iw_o)
        gla_o[rows, :AB_SMALL] = _dot(hb[rows], w_ref[:, :AB_SMALL])


def _proj_ab(x, nw, shift, scale, w_bf16, positions, q_norm, k_norm, *, tm, tk):
    bsz, s, d = x.shape
    hd = DSA_HEADS * DSA_HEAD_DIM
    half_a, half_i = DSA_HEAD_DIM // 2, IDX_DIM // 2
    inv_a = ROPE_THETA ** (-jnp.arange(half_a, dtype=F32) / half_a)
    inv_i = ROPE_THETA ** (-jnp.arange(half_i, dtype=F32) / half_i)
    freq = jnp.concatenate([inv_a, inv_i, inv_i]).reshape(1, LANES)
    rows = lambda n: pl.BlockSpec((None, tm, n), lambda b, i: (b, i, 0))
    vec = lambda n: pl.BlockSpec((1, n), lambda b, i: (0, 0))
    mod = pl.BlockSpec((None, 1, d), lambda b, i: (b, 0, 0))
    return pl.pallas_call(
        _proj_ab_kernel,
        grid=(bsz, s // tm),
        in_specs=[
            rows(d), vec(d), mod, mod,
            _resident((d, AB_GLA_N + AB_DSA_N), lambda b, i: (0, 0)),
            rows(1), vec(DSA_HEAD_DIM), vec(DSA_HEAD_DIM), vec(LANES),
        ],
        out_specs=[
            rows(AB_GLA_N), rows(hd), rows(hd),
            pl.BlockSpec((None, tm // tk, hd, tk), lambda b, i: (b, i, 0, 0)),
            pl.BlockSpec((None, IDX_HEADS, tm, IDX_DIM), lambda b, i: (b, 0, i, 0)),
            rows(IDX_DIM),
            pl.BlockSpec((None, IDX_HEADS, tm), lambda b, i: (b, 0, i)),
        ],
        out_shape=[
            jax.ShapeDtypeStruct((bsz, s, AB_GLA_N), F32),
            jax.ShapeDtypeStruct((bsz, s, hd), BF16),
            jax.ShapeDtypeStruct((bsz, s, hd), BF16),
            jax.ShapeDtypeStruct((bsz, s // tk, hd, tk), BF16),
            jax.ShapeDtypeStruct((bsz, IDX_HEADS, s, IDX_DIM), BF16),
            jax.ShapeDtypeStruct((bsz, s, IDX_DIM), BF16),
            jax.ShapeDtypeStruct((bsz, IDX_HEADS, s), F32),
        ],
        scratch_shapes=[pltpu.VMEM((tm, AB_DSA_N), F32)],
        compiler_params=_cparams(("arbitrary", "arbitrary")),
        name="proj_ab",
    )(x, nw.reshape(1, d), shift, scale, w_bf16, positions.reshape(bsz, s, 1),
      q_norm.reshape(1, -1), k_norm.reshape(1, -1), freq)


def _dsa_kernel(q_ref, iq_ref, iw_ref, k_ref, vt_ref, ik_ref, o_ref, sc_ref, top_ref, acc_ref,
                *, tq, n_sel, s_len):
    i = pl.program_id(1)
    nkb = i + 1
    kofs = lax.broadcasted_iota(I32, (tq, tq), 0)
    qidx = i * tq + lax.broadcasted_iota(I32, (tq, tq), 1)
    iw = iw_ref[...]
    idx_scale = (IDX_HEADS ** -0.5) * (IDX_DIM ** -0.5)

    def over_tiles(step, init, n=nkb, widest=2):
        carry = lax.fori_loop(0, n // widest, lambda j, c: step(widest * j, widest, c), init)
        rest = n % widest
        start = n - rest
        width = widest // 2
        while width >= 1:
            carry = lax.cond((rest & width) != 0, functools.partial(step, start, width), lambda c: c, carry)
            start = start + (rest & width)
            width //= 2
        return carry

    def as_float(key):
        key = jnp.maximum(key, KEY_LOWEST)
        key = jnp.where((key > 0) & (key < KEY_TINY), KEY_TINY, key)
        return pltpu.bitcast(key ^ ((key >> 31) & 0x7FFFFFFF), F32)

    def scores(kb, width, diagonal):
        k0 = pl.multiple_of(kb * tq, tq)
        rows = width * tq
        ikb = ik_ref[pl.ds(k0, rows), :]
        sc = jnp.zeros((rows, tq), F32)
        for h in range(IDX_HEADS):
            sc = sc + jnp.maximum(_dot_nt(ikb, iq_ref[h]), 0.0) * iw[h:h + 1, :]
        sc = sc * idx_scale
        sc = jnp.where(jnp.abs(sc) < FLT_TINY, 0.0, sc)
        if diagonal:
            sc = jnp.where(kofs <= qidx - i * tq, sc, -jnp.inf)
        sc_ref[pl.ds(kb, width)] = sc.reshape(width, tq, tq)
        top_ref[pl.ds(kb, width)] = sc.astype(BF16).reshape(width, tq, tq)

    def below_diagonal(kb, width, carry):
        scores(kb, width, False)
        return carry

    over_tiles(below_diagonal, 0, n=i)
    scores(i, 1, True)

    def count(pred):
        def body(kb, width, acc):
            rows = width * tq
            m = jnp.where(pred(sc_ref[pl.ds(kb, width)].reshape(rows, tq)), 1, 0)
            return acc + jnp.sum(m.reshape(rows // 8, 8, tq), axis=0)
        return jnp.sum(over_tiles(body, jnp.zeros((8, tq), I32)), axis=0, keepdims=True)

    n_valid = qidx[0:1, :] + 1

    def as_bf16(cand):
        k16 = jnp.maximum(cand, KEY_LOWEST_BF16) >> 16
        pattern = k16 ^ ((k16 >> 15) & 0x7FFF)
        pattern = jnp.where((pattern > 0) & (pattern < 0x80), 0x80, pattern)
        return pltpu.bitcast(pattern << 16, F32).astype(BF16)

    def count_top(cand):
        cand_bf = as_bf16(cand)

        def body(kb, width, acc):
            rows = width * tq
            top = top_ref[pl.ds(kb, width)].reshape(rows, tq)
            m = jnp.where(top >= cand_bf, jnp.ones((), BF16), jnp.zeros((), BF16))
            part = m[0:16]
            for j in range(1, rows // 16):
                part = part + m[16 * j:16 * (j + 1)]
            return acc + part.astype(F32)
        acc = over_tiles(body, jnp.zeros((16, tq), F32))
        return jnp.sum(acc, axis=0, keepdims=True).astype(I32)

    def top_step(it, u):
        cand_u = u | lax.shift_left(jnp.int32(1), jnp.int32(31) - it)
        return jnp.where(count_top(cand_u ^ INT_MIN) >= n_sel, cand_u, u)

    u = lax.fori_loop(0, 16, top_step, jnp.zeros((1, tq), I32))

    def any_lane(flag):
        return jnp.max(jnp.where(flag, 1.0, 0.0)) > 0.0

    base = u ^ INT_MIN
    base = jnp.where(base < 0, base | 0xFFFF, base)
    base = jnp.clip(base, KEY_LOWEST + BF16_HALF_STEP + 1, INT_MAX - 2 * BF16_HALF_STEP)
    many = n_valid > n_sel
    lo = jnp.where(many, base - BF16_HALF_STEP - 1, KEY_LOWEST)
    hi = jnp.where(many, base + 2 * BF16_HALF_STEP, INT_MAX)

    def count_two(pred_a, pred_b):
        def body(kb, width, acc):
            x = sc_ref[pl.ds(kb, width)].reshape(width * tq // 8, 8, tq)
            return (acc[0] + jnp.sum(jnp.where(pred_a(x), 1, 0), axis=0),
                    acc[1] + jnp.sum(jnp.where(pred_b(x), 1, 0), axis=0))
        a, b = over_tiles(body, (jnp.zeros((8, tq), I32), jnp.zeros((8, tq), I32)))
        return jnp.sum(a, axis=0, keepdims=True), jnp.sum(b, axis=0, keepdims=True)

    lo_f, hi_f = as_float(lo), as_float(hi)
    cnt_lo, cnt_hi = count_two(lambda x: x >= lo_f, lambda x: x >= hi_f)

    def open_queries(lo, cnt_lo, hi, done):
        return (cnt_lo > n_sel) & (hi - lo > 1) & (done == 0)

    def halve(s):
        lo, cnt_lo, hi, cnt_hi, done = s
        mid = lo + ((hi - lo) >> 1)
        mid_f = as_float(mid)
        cnt = count(lambda x: x >= mid_f)
        live = open_queries(lo, cnt_lo, hi, done)
        up = live & (cnt >= n_sel)
        down = live & (cnt < n_sel)
        return (jnp.where(up, mid, lo), jnp.where(up, cnt, cnt_lo),
                jnp.where(down, mid, hi), jnp.where(down, cnt, cnt_hi), done)

    st = lax.fori_loop(0, PEEL_AFTER_STEPS, lambda _, s: halve(s),
                       (lo, cnt_lo, hi, cnt_hi, jnp.zeros((1, tq), I32)))

    def peel(st):
        lo, cnt_lo, hi, cnt_hi, done = st
        lo_f, hi_f = as_float(lo), as_float(hi)

        def extremes(kb, width, carry):
            top, bot = carry
            x = sc_ref[pl.ds(kb, width)].reshape(width * tq // 8, 8, tq)
            top = jnp.maximum(top, jnp.max(jnp.where(x < hi_f, x, -jnp.inf), axis=0))
            bot = jnp.minimum(bot, jnp.min(jnp.where(x >= lo_f, x, jnp.inf), axis=0))
            return top, bot
        top, bot = over_tiles(extremes, (jnp.full((8, tq), -jnp.inf, F32), jnp.full((8, tq), jnp.inf, F32)))
        top = jnp.max(top, axis=0, keepdims=True)
        bot = jnp.min(bot, axis=0, keepdims=True)
        from_top = (cnt_hi == n_sel - 1) | (top == bot)
        from_bot = (cnt_lo == n_sel + 1) & jnp.logical_not(from_top)
        cnt_top, cnt_bot = count_two(lambda x: x >= top, lambda x: x > bot)
        cnt = jnp.where(from_top, cnt_top, cnt_bot)
        short = from_bot & (cnt < n_sel)
        hit = open_queries(lo, cnt_lo, hi, done) & (from_top | from_bot)
        thr = jnp.where(from_top, top, bot)
        incl = jnp.where(from_top | short, 1, 0)
        return hit, thr, incl, jnp.where(short, cnt_lo, cnt)

    def with_peel(st):
        hit, thr, incl, cnt = peel(st)
        st = st[:4] + (jnp.where(hit, 1, 0),)
        lo, cnt_lo, _, _, _ = lax.while_loop(lambda s: any_lane(open_queries(s[0], s[1], s[2], s[4])), halve, st)
        return jnp.where(hit, thr, as_float(lo)), jnp.where(hit, incl, 1), jnp.where(hit, cnt, cnt_lo)

    def without_peel(st):
        lo, cnt_lo, _, _, _ = st
        return as_float(lo), jnp.ones((1, tq), I32), cnt_lo

    thr, incl, cnt = lax.cond(any_lane(open_queries(st[0], st[1], st[2], st[4])), with_peel, without_peel, st)
    admits_equal = incl != 0
    excess = cnt > n_sel

    tied = jnp.max(jnp.where(excess, 1.0, 0.0)) > 0.0

    acc_ref[...] = jnp.zeros_like(acc_ref)

    def attend(select, seen):
        def block(kb, width, carry):
            ms, ls, seen = carry
            k0 = pl.multiple_of(kb * tq, tq)
            rows = width * tq
            sel, seen = select(sc_ref[pl.ds(kb, width)].reshape(rows, tq), seen)
            heads = [slice(h * DSA_HEAD_DIM, (h + 1) * DSA_HEAD_DIM) for h in range(DSA_HEADS)]
            st = [_dot_nt(k_ref[pl.ds(k0, rows), hs], q_ref[:, hs]) for hs in heads]
            bias = jnp.where(sel, 0.0, MASK_BIAS)
            st = [x + bias for x in st]
            new_m = [jnp.maximum(m, jnp.max(x, axis=0, keepdims=True)) for m, x in zip(ms, st)]
            alpha = [jnp.exp2(m - mn) for m, mn in zip(ms, new_m)]
            p = [jnp.exp2(x - mn) for x, mn in zip(st, new_m)]
            new_l = [a * l + jnp.sum(x, axis=0, keepdims=True) for a, l, x in zip(alpha, ls, p)]
            p = [x.astype(BF16) for x in p]
            for h, hs in enumerate(heads):
                pv = _dot(vt_ref[kb, hs, :], p[h][0:tq])
                for j in range(1, width):
                    pv = pv + _dot(vt_ref[kb + j, hs, :], p[h][j * tq:(j + 1) * tq])
                acc_ref[h] = alpha[h] * acc_ref[h] + pv
            return tuple(new_m), tuple(new_l), seen

        carry = (tuple(jnp.full((1, tq), -jnp.inf, F32) for _ in range(DSA_HEADS)),
                 tuple(jnp.zeros((1, tq), F32) for _ in range(DSA_HEADS)), seen)
        return over_tiles(block, carry)[1]

    def attend_tied():
        room = n_sel - count(lambda x: x > thr)
        room = jnp.where(excess, room, jnp.where(admits_equal, s_len, 0)).astype(F32)
        r = lax.broadcasted_iota(I32, (tq, tq), 0)
        c = lax.broadcasted_iota(I32, (tq, tq), 1)
        upto = jnp.where(c <= r, 1.0, 0.0).astype(BF16)

        def select(x, seen):
            equal = x == thr
            ones = jnp.where(equal, 1.0, 0.0).astype(BF16)
            fits = []
            for j in range(x.shape[0] // tq):
                rank = _dot(upto, ones[j * tq:(j + 1) * tq])
                fits.append(rank <= room - seen)
                seen = seen + rank[tq - 1:tq, :]
            fits = fits[0] if len(fits) == 1 else jnp.concatenate(fits, axis=0)
            return (x > thr) | (equal & fits), seen
        return attend(select, jnp.zeros((1, tq), F32))

    at_least = jnp.where(admits_equal, thr, jnp.inf)
    above = jnp.where(admits_equal, jnp.inf, thr)
    ls = lax.cond(tied, attend_tied,
                  lambda: attend(lambda x, seen: ((x >= at_least) | (x > above), seen), jnp.zeros((1, 1), F32)))
    for h in range(DSA_HEADS):
        hs = slice(h * DSA_HEAD_DIM, (h + 1) * DSA_HEAD_DIM)
        o_ref[:, hs] = (acc_ref[h] / ls[h]).T.astype(o_ref.dtype)


def _dsa(q, k, vt, iq, ik, iw, *, tq):
    bsz, s, hd = q.shape
    n_sel = min(DSA_TOPK, s // 4)
    resident = lambda n: _resident((None, s, n), lambda b, i: (b, 0, 0))
    return pl.pallas_call(
        functools.partial(_dsa_kernel, tq=tq, n_sel=n_sel, s_len=s),
        grid=(bsz, s // tq),
        in_specs=[
            pl.BlockSpec((None, tq, hd), lambda b, i: (b, i, 0)),
            pl.BlockSpec((None, IDX_HEADS, tq, IDX_DIM), lambda b, i: (b, 0, i, 0)),
            pl.BlockSpec((None, IDX_HEADS, tq), lambda b, i: (b, 0, i)),
            resident(hd),
            _resident((None, s // tq, hd, tq), lambda b, i: (b, 0, 0, 0)),
            resident(IDX_DIM),
        ],
        out_specs=pl.BlockSpec((None, tq, hd), lambda b, i: (b, i, 0)),
        out_shape=jax.ShapeDtypeStruct((bsz, s, hd), BF16),
        scratch_shapes=[
            pltpu.VMEM((s // tq, tq, tq), F32),
            pltpu.VMEM((s // tq, tq, tq), BF16),
            pltpu.VMEM((DSA_HEADS, DSA_HEAD_DIM, tq), F32),
        ],
        compiler_params=_cparams(("arbitrary", "arbitrary"), DSA_VMEM_LIMIT),
        name="dsa",
    )(q, iq, iw, k, vt, ik)


def _mix_ffn_kernel(x_ref, a_ref, b_ref, wa_ref, wb_ref, g1_ref, nw_ref, sh_ref, sc_ref, g2_ref,
                    w1_ref, w2_ref, o_ref, *, tf):
    x = x_ref[...] + g1_ref[...] * (_dot(a_ref[...], wa_ref[...]) + _dot(b_ref[...], wb_ref[...]))
    hb = (_rms(x, nw_ref[...]) * (1.0 + sc_ref[...]) + sh_ref[...]).astype(BF16)
    f = w2_ref.shape[0]
    acc = jnp.zeros(x.shape, F32)
    for j in range(f // tf):
        gate = _dot(hb, w1_ref[:, j * tf:(j + 1) * tf])
        up = _dot(hb, w1_ref[:, f + j * tf:f + (j + 1) * tf])
        acc = acc + _dot((_silu(gate) * up).astype(BF16), w2_ref[j * tf:(j + 1) * tf, :])
    o_ref[...] = x + g2_ref[...] * acc


def _mix_ffn(x, mix_a, mix_b, w_out_bf16, gate1, nw, shift, scale, gate2, w1_bf16, w2_bf16, layer, *, tm, tf):
    bsz, s, d = x.shape
    f = w2_bf16.shape[1]
    na = mix_a.shape[-1]
    assert mix_b.shape[-1] == na and w_out_bf16.shape[0] == 2 * na
    rows = lambda n: pl.BlockSpec((None, tm, n), lambda b, i: (b, i, 0))
    mod = pl.BlockSpec((None, 1, d), lambda b, i: (b, 0, 0))
    return pl.pallas_call(
        functools.partial(_mix_ffn_kernel, tf=tf),
        grid=(bsz, s // tm),
        in_specs=[
            rows(d), rows(na), rows(na),
            _resident((na, d), lambda b, i: (0, 0)),
            _resident((na, d), lambda b, i: (1, 0)),
            mod,
            pl.BlockSpec((1, d), lambda b, i: (0, 0)),
            mod, mod, mod,
            _resident((None, d, 2 * f), lambda b, i: (layer, 0, 0)),
            _resident((None, f, d), lambda b, i: (layer, 0, 0)),
        ],
        out_specs=rows(d),
        out_shape=jax.ShapeDtypeStruct((bsz, s, d), F32),
        compiler_params=_cparams(("arbitrary", "arbitrary")),
        name="mix_ffn",
    )(x, mix_a, mix_b, w_out_bf16, w_out_bf16, gate1, nw.reshape(1, d), shift, scale, gate2,
      w1_bf16, w2_bf16)


HALO = max(POOL_WINDOWS)


def _proj_cd_kernel(x_ref, nw_ref, sh_ref, sc_ref, w_ref, pw_ref, ps_ref, qn_ref, kn_ref,
                    pool_o, q_o, k_o, v_o, ext_ref, sb_ref):
    i = pl.program_id(1)
    tm = x_ref.shape[0]
    n = POOL_GROUPS * POOL_CH
    hb = (_rms(x_ref[...], nw_ref[...]) * (1.0 + sc_ref[...]) + sh_ref[...]).astype(BF16)

    @pl.when(i == 0)
    def _():
        ext_ref[0:HALO, :] = jnp.zeros((HALO, n), F32)

    ext_ref[HALO:HALO + tm, :] = _dot(hb, w_ref[:, :n])
    for j in range(3):
        sb_ref[:, j * n:(j + 1) * n] = _dot(hb, w_ref[:, (j + 1) * n:(j + 2) * n])

    t = i * tm + lax.broadcasted_iota(I32, (tm, 1), 0)
    for g, w in enumerate(POOL_WINDOWS):
        gs = slice(g * POOL_CH, (g + 1) * POOL_CH)
        u = ext_ref[HALO:HALO + tm, gs]
        tot = u
        for j in range(1, w):
            tot = tot + ext_ref[HALO - j:HALO - j + tm, gs]
        cnt = jnp.minimum(t + 1, w).astype(F32)
        pooled = tot / cnt - u
        y = _dot(pooled.astype(BF16), pw_ref[g].astype(BF16)) * ps_ref[:, gs]
        pool_o[:, gs] = y.astype(pool_o.dtype)
    ext_ref[0:HALO, :] = ext_ref[tm:tm + HALO, :]

    qn, kn = qn_ref[...], kn_ref[...]
    for h in range(SB_HEADS):
        hs = slice(h * SB_HEAD_DIM, (h + 1) * SB_HEAD_DIM)
        q_o[:, hs] = (_rms(sb_ref[:, hs], qn) * SB_HEAD_DIM ** -0.5).astype(q_o.dtype)
        k_o[:, hs] = _rms(sb_ref[:, n + h * SB_HEAD_DIM:n + (h + 1) * SB_HEAD_DIM], kn).astype(k_o.dtype)
    tk = v_o.shape[-1]
    for j in range(v_o.shape[0]):
        v_o[j] = sb_ref[j * tk:(j + 1) * tk, 2 * n:3 * n].T.astype(v_o.dtype)


def _proj_cd(x, nw, shift, scale, w_bf16, pool_w, pool_scale, q_norm, k_norm, *, tm, tk):
    bsz, s, d = x.shape
    n = POOL_GROUPS * POOL_CH
    assert w_bf16.shape[1] == 4 * n
    rows = lambda m: pl.BlockSpec((None, tm, m), lambda b, i: (b, i, 0))
    vec = lambda m: pl.BlockSpec((1, m), lambda b, i: (0, 0))
    mod = pl.BlockSpec((None, 1, d), lambda b, i: (b, 0, 0))
    return pl.pallas_call(
        _proj_cd_kernel,
        grid=(bsz, s // tm),
        in_specs=[
            rows(d), vec(d), mod, mod,
            _resident((d, 4 * n), lambda b, i: (0, 0)),
            pl.BlockSpec((POOL_GROUPS, POOL_CH, POOL_CH), lambda b, i: (0, 0, 0)),
            vec(n), vec(SB_HEAD_DIM), vec(SB_HEAD_DIM),
        ],
        out_specs=[rows(n)] * 3 + [pl.BlockSpec((None, tm // tk, n, tk), lambda b, i: (b, i, 0, 0))],
        out_shape=[jax.ShapeDtypeStruct((bsz, s, n), BF16)] * 3
        + [jax.ShapeDtypeStruct((bsz, s // tk, n, tk), BF16)],
        scratch_shapes=[pltpu.VMEM((HALO + tm, n), F32), pltpu.VMEM((tm, 3 * n), F32)],
        compiler_params=_cparams(("arbitrary", "arbitrary")),
        name="proj_cd",
    )(x, nw.reshape(1, d), shift, scale, w_bf16, pool_w, pool_scale.reshape(1, n),
      q_norm.reshape(1, -1), k_norm.reshape(1, -1))


def _sb_kernel(q_ref, k_ref, vt_ref, o_ref, acc_ref, *, tq):
    i = pl.program_id(1)
    kofs = lax.broadcasted_iota(I32, (tq, tq), 0)
    qofs = lax.broadcasted_iota(I32, (tq, tq), 1)
    later = jnp.where(qofs > kofs, 1.0, 0.0).astype(BF16)
    heads = [slice(h * SB_HEAD_DIM, (h + 1) * SB_HEAD_DIM) for h in range(SB_HEADS)]
    acc_ref[...] = jnp.zeros_like(acc_ref)

    strict = kofs < qofs

    def walk(tiles, runs):
        pairs = [(t, h) for t in range(len(tiles)) for h in range(SB_HEADS)]
        rows = [pl.ds(pl.multiple_of(kb * tq, tq), tq) for kb, _ in tiles]
        z = {(t, h): _dot_nt(k_ref[rows[t], heads[h]], q_ref[:, heads[h]]) for t, h in pairs}
        sp = {p: jnp.log(1.0 + jnp.exp(-jnp.abs(z[p]))) for p in pairs}
        log_beta = {p: jnp.minimum(z[p], 0.0) - sp[p] for p in pairs}
        log_1m = {p: log_beta[p] - z[p] for p in pairs}
        log_1m = {p: jnp.where(strict, log_1m[p], 0.0) if tiles[p[0]][1] else log_1m[p] for p in pairs}
        hi = {p: log_1m[p].astype(BF16) for p in pairs}
        lo = {p: (log_1m[p] - hi[p].astype(F32)).astype(BF16) for p in pairs}
        within = {p: _dot(later, hi[p]) + _dot(later, lo[p]) for p in pairs}
        runs = list(runs)
        for t, (kb, diagonal) in enumerate(tiles):
            for h in range(SB_HEADS):
                w = jnp.exp(log_beta[(t, h)] + within[(t, h)] + runs[h])
                if diagonal:
                    w = jnp.where(strict, w, 0.0)
                acc_ref[h] = acc_ref[h] + _dot(vt_ref[kb, heads[h], :], w.astype(BF16))
                runs[h] = runs[h] + jnp.sum(log_1m[(t, h)], axis=0, keepdims=True)
        return tuple(runs)

    def alive(runs):
        worst = functools.reduce(jnp.maximum, runs)
        return jnp.max(worst) > SB_DEAD_LOG

    zero = tuple(jnp.zeros((1, tq), F32) for _ in heads)
    runs = lax.cond(i >= 1, lambda: walk([(i, True), (i - 1, False)], zero), lambda: walk([(i, True)], zero))

    def earlier(state):
        kb, runs, _ = state
        runs = walk([(kb, False)], runs)
        return kb - 1, runs, alive(runs)

    lax.while_loop(lambda st: (st[0] >= 0) & st[2], earlier, (i - 2, runs, alive(runs)))
    for h, hs in enumerate(heads):
        o_ref[:, hs] = acc_ref[h].T.astype(o_ref.dtype)


def _sb(q, k, vt, *, tq):
    bsz, s, hd = q.shape
    return pl.pallas_call(
        functools.partial(_sb_kernel, tq=tq),
        grid=(bsz, s // tq),
        in_specs=[
            pl.BlockSpec((None, tq, hd), lambda b, i: (b, i, 0)),
            _resident((None, s, hd), lambda b, i: (b, 0, 0)),
            _resident((None, s // tq, hd, tq), lambda b, i: (b, 0, 0, 0)),
        ],
        out_specs=pl.BlockSpec((None, tq, hd), lambda b, i: (b, i, 0)),
        out_shape=jax.ShapeDtypeStruct((bsz, s, hd), BF16),
        scratch_shapes=[pltpu.VMEM((SB_HEADS, SB_HEAD_DIM, tq), F32)],
        compiler_params=_cparams(("arbitrary", "arbitrary")),
        name="stick_breaking",
    )(q, k, vt)


def _pack_ab_weight(w):
    d = w.shape[0]
    gq, gk, gv, glow, gr, dq, dk, dv, iq, ik, iw = jnp.split(
        w, [256, 512, 1024, 1040, 1552, 2064, 2576, 3088, 3600, 3664], axis=1)
    pad = jnp.zeros((d, LANES - (IDX_DIM + GLA_GATE_RANK + IDX_HEADS)), w.dtype)
    return jnp.concatenate([gq, gk, gv, gr, ik, glow, iw, pad, dq, dk, dv, iq], axis=1)


def kernel(x, c, positions, ada_w, ada_b, mix_norm, ffn_norm, ffn_w1, ffn_w2, ab_w_in, gla_gate_up,
           gla_gate_b, gla_out_norm, dsa_q_norm, dsa_k_norm, ab_w_out, cd_w_in, pool_w, pool_scale,
           sb_q_norm, sb_k_norm, cd_w_out):
    bsz, s, d = x.shape
    depth = ada_w.shape[0]
    mod = _ada_mod(c, ada_w, ada_b).reshape(depth, bsz, 6, 1, d)
    ffn_w1_bf16, ffn_w2_bf16 = ffn_w1.astype(BF16), ffn_w2.astype(BF16)
    tm = min(512, s)
    tq_dsa = min(512, s)
    tq_sb = min(256, s)
    for layer in range(depth):
        sh1, sc1, g1, sh2, sc2, g2 = (mod[layer, :, j] for j in range(6))
        i = layer // 2
        if layer % 2 == 0:
            proj, q, k, vt, iq, ik, iw = _proj_ab(
                x, mix_norm[layer], sh1, sc1, _pack_ab_weight(ab_w_in[i]).astype(BF16), positions,
                dsa_q_norm[i], dsa_k_norm[i], tm=tq_dsa, tk=tq_dsa)
            mix_a = _gla(proj, gla_gate_up[i], gla_gate_b[i], gla_out_norm[i], ts=tm)
            mix_b = _dsa(q, k, vt, iq, ik, iw, tq=tq_dsa)
            w_out = ab_w_out[i]
        else:
            mix_a, q, k, vt = _proj_cd(x, mix_norm[layer], sh1, sc1, cd_w_in[i].astype(BF16), pool_w[i],
                                       pool_scale[i], sb_q_norm[i], sb_k_norm[i], tm=tm, tk=tq_sb)
            mix_b = _sb(q, k, vt, tq=tq_sb)
            w_out = cd_w_out[i]
        x = _mix_ffn(x, mix_a, mix_b, w_out.astype(BF16), g1, ffn_norm[layer], sh2, sc2, g2,
                     ffn_w1_bf16, ffn_w2_bf16, layer, tm=tm, tf=256)
    return x
```

```python
import functools

import jax
import jax.numpy as jnp
from jax import lax
from jax.experimental import pallas as pl
from jax.experimental.pallas import tpu as pltpu

F32 = jnp.float32
BF16 = jnp.bfloat16
I32 = jnp.int32

GLA_HEADS, GLA_DK, GLA_DV = 4, 64, 128
GLA_GATE_RANK = 16
GLA_GATE_TAU = 16.0
GLA_CHUNK = 64
DSA_HEADS, DSA_HEAD_DIM = 4, 128
IDX_HEADS, IDX_DIM = 8, 64
DSA_TOPK = 256
POOL_WINDOWS = (2, 4, 8, 16)
POOL_GROUPS, POOL_CH = 4, 128
SB_HEADS, SB_HEAD_DIM = 4, 128
ROPE_THETA = 10000.0
NORM_EPS = 1e-6

LANES = 128
VMEM_LIMIT = 56 * 1024 * 1024
DSA_VMEM_LIMIT = 61 * 1024 * 1024

AB_GQ, AB_GK, AB_GV, AB_GR = 0, 256, 512, 1024
AB_SMALL = 1536
SM_IK, SM_GLOW, SM_IW = 0, 64, 80
AB_GLA_N = AB_SMALL + LANES
AB_DSA_N = 4 * 512

DSA_Q_SCALE = 1.4426950408889634 * DSA_HEAD_DIM ** -0.5
INT_MIN = -2 ** 31
INT_MAX = 2 ** 31 - 1
FLT_TINY = 2.0 ** -126
KEY_TINY = 0x00800000
KEY_LOWEST = INT_MIN + 0x00800000
KEY_LOWEST_BF16 = INT_MIN + 0x00810000
BF16_HALF_STEP = 0x8000
PEEL_AFTER_STEPS = 5
MASK_BIAS = -1e30
SB_DEAD_LOG = -110.0


def _dot(a, b):
    return jnp.dot(a, b, preferred_element_type=F32)


def _dot_nt(a, b):
    return lax.dot_general(a, b, (((1,), (1,)), ((), ())), preferred_element_type=F32)


def _dot_tn(a, b):
    return lax.dot_general(a, b, (((0,), (0,)), ((), ())), preferred_element_type=F32)


def _split3(a):
    hi = a.astype(BF16)
    r1 = a - hi.astype(F32)
    mid = r1.astype(BF16)
    lo = (r1 - mid.astype(F32)).astype(BF16)
    return hi, mid, lo


def _silu(x):
    return x * jax.nn.sigmoid(x)


def _rms(x, w):
    var = jnp.mean(x * x, axis=-1, keepdims=True)
    return x * lax.rsqrt(var + NORM_EPS) * w


def _cparams(sem, vmem_limit=VMEM_LIMIT):
    return pltpu.CompilerParams(dimension_semantics=sem, vmem_limit_bytes=vmem_limit)


def _resident(block_shape, index_map):
    return pl.BlockSpec(block_shape, index_map, pipeline_mode=pl.Buffered(1))


def _ada_kernel(c_ref, w_ref, b_ref, o_ref):
    cond = _silu(c_ref[...])
    c_hi, c_mid, _ = _split3(cond)
    w_hi, w_mid, _ = _split3(w_ref[...])
    acc = _dot(c_hi, w_hi) + _dot(c_hi, w_mid) + _dot(c_mid, w_hi)
    o_ref[...] = acc + b_ref[...]


def _ada_mod(c, ada_w, ada_b):
    depth, d, n = ada_w.shape
    bsz = c.shape[0]
    rows = 16
    cp =jnp.zeros((rows, d), F32).at[:bsz].set(c)
    tn = 1536
    out = pl.pallas_call(
        _ada_kernel,
        grid=(depth, n // tn),
        in_specs=[
            pl.BlockSpec((rows, d), lambda l, j: (0, 0)),
            pl.BlockSpec((None, d, tn), lambda l, j: (l, 0, j)),
            pl.BlockSpec((None, 1, tn), lambda l, j: (l, 0, j)),
        ],
        out_specs=pl.BlockSpec((None, rows, tn), lambda l, j: (l, 0, j)),
        out_shape=jax.ShapeDtypeStruct((depth, rows, n), F32),
        compiler_params=_cparams(("arbitrary", "arbitrary")),
        name="ada_mod",
    )(cp, ada_w, ada_b.reshape(depth, 1, n))
    return out[:, :bsz]


def _gla_kernel(q_ref, k_ref, v_ref, gr_ref, sm_ref, gup_ref, gb_ref, onw_ref, o_ref, st_ref, tri_ref, *, ts):
    c = GLA_CHUNK

    @pl.when(pl.program_id(1) == 0)
    def _():
        st_ref[...] = jnp.zeros_like(st_ref)
        row = lax.broadcasted_iota(I32, (ts, ts), 0)
        col = lax.broadcasted_iota(I32, (ts, ts), 1)
        tri_ref[...] = jnp.where((row // c == col // c) & (col <= row), 1.0, 0.0).astype(BF16)

    nc = ts // c
    tri = tri_ref[...]
    causal = lax.broadcasted_iota(I32, (c, c), 1) <= lax.broadcasted_iota(I32, (c, c), 0)
    onw = onw_ref[...]

    glow = sm_ref[:, SM_GLOW:SM_GLOW + GLA_GATE_RANK]
    a = _dot(glow.astype(BF16), gup_ref[...].astype(BF16)) + gb_ref[...]
    g = (jnp.minimum(a, 0.0) - jnp.log1p(jnp.exp(-jnp.abs(a)))) / GLA_GATE_TAU
    g_hi, g_mid, g_lo = _split3(g)
    b = _dot(tri, g_hi) + _dot(tri, g_mid) + _dot(tri, g_lo)
    qs = q_ref[...] * (GLA_DK ** -0.5)
    k = k_ref[...]

    qe, ke, kd, qd, dec = [], [], [], [], []
    for ci in range(nc):
        r = slice(ci * c, (ci + 1) * c)
        bc = b[r]
        b_mid = bc[c // 2 - 1:c // 2, :]
        b_last = bc[c - 1:c, :]
        qe.append((qs[r] * jnp.exp(bc - b_mid)).astype(BF16))
        ke.append((k[r] * jnp.exp(b_mid - bc)).astype(BF16))
        kd.append((k[r] * jnp.exp(b_last - bc)).astype(BF16))
        qd.append((qs[r] * jnp.exp(bc)).astype(BF16))
        dec.append(jnp.exp(b_last))

    pairs = [(ci, h) for ci in range(nc) for h in range(GLA_HEADS)]
    ksl = lambda h: slice(h * GLA_DK, (h + 1) * GLA_DK)
    vsl = lambda h: slice(h * GLA_DV, (h + 1) * GLA_DV)
    vh = {(ci, h): v_ref[ci * c:(ci + 1) * c, vsl(h)].astype(BF16) for ci, h in pairs}
    att = {p: _dot_nt(qe[p[0]][:, ksl(p[1])], ke[p[0]][:, ksl(p[1])]) for p in pairs}
    att = {p: jnp.where(causal, att[p], 0.0).astype(BF16) for p in pairs}
    o = {p: _dot(att[p], vh[p]) for p in pairs}
    kvt = {p: _dot_tn(vh[p], kd[p[0]][:, ksl(p[1])]) for p in pairs}
    st_in = {}
    for h in range(GLA_HEADS):
        st = st_ref[h]
        for ci in range(nc):
            st_in[(ci, h)] = st.astype(BF16)
            st = st * dec[ci][:, ksl(h)] + kvt[(ci, h)]
        st_ref[h] = st
    for ci, h in pairs:
        out = o[(ci, h)] + _dot_nt(qd[ci][:, ksl(h)], st_in[(ci, h)])
        grh = gr_ref[ci * c:(ci + 1) * c, vsl(h)]
        o_ref[ci * c:(ci + 1) * c, vsl(h)] = (_rms(out, onw) * _silu(grh)).astype(o_ref.dtype)


def _gla(proj, gate_up, gate_b, out_norm, *, ts):
    bsz, s, _ = proj.shape
    hk, hv = GLA_HEADS * GLA_DK, GLA_HEADS * GLA_DV
    return pl.pallas_call(
        functools.partial(_gla_kernel, ts=ts),
        grid=(bsz, s // ts),
        in_specs=[
            pl.BlockSpec((None, ts, hk), lambda b, i: (b, i, AB_GQ // hk)),
            pl.BlockSpec((None, ts, hk), lambda b, i: (b, i, AB_GK // hk)),
            pl.BlockSpec((None, ts, hv), lambda b, i: (b, i, AB_GV // hv)),
            pl.BlockSpec((None, ts, hv), lambda b, i: (b, i, AB_GR // hv)),
            pl.BlockSpec((None, ts, LANES), lambda b, i: (b, i, AB_SMALL // LANES)),
            pl.BlockSpec((GLA_GATE_RANK, hk), lambda b, i: (0, 0)),
            pl.BlockSpec((1, hk), lambda b, i: (0, 0)),
            pl.BlockSpec((1, GLA_DV), lambda b, i: (0, 0)),
        ],
        out_specs=pl.BlockSpec((None, ts, hv), lambda b, i: (b, i, 0)),
        out_shape=jax.ShapeDtypeStruct((bsz, s, hv), BF16),
        scratch_shapes=[pltpu.VMEM((GLA_HEADS, GLA_DV, GLA_DK), F32), pltpu.VMEM((ts, ts), BF16)],
        compiler_params=_cparams(("arbitrary", "arbitrary")),
        name="gla",
    )(proj, proj, proj, proj, proj, gate_up, gate_b.reshape(1, hk), out_norm.reshape(1, GLA_DV))


def _dsa_prep_rows(rows, dsa_ref, sm_ref, pos_ref, qn_ref, kn_ref, freq_ref, q_o, k_o, v_o, iq_o, ik_o, iw_o):
    hd = DSA_HEADS * DSA_HEAD_DIM
    pos = pos_ref[rows, :].astype(F32)
    lane = lax.broadcasted_iota(I32, (1, LANES), 1)
    half_a, half_i = DSA_HEAD_DIM // 2, IDX_DIM // 2

    ang = pos * freq_ref[...]
    cos_t, sin_t = jnp.cos(ang), jnp.sin(ang)

    def attn_table(t):
        return jnp.where(lane < half_a, t, pltpu.roll(t, half_a, 1))

    def idx_table(t):
        return jnp.where(lane < half_i, pltpu.roll(t, half_a, 1),
                         jnp.where(lane < 2 * half_i, pltpu.roll(t, half_a + half_i, 1), t))

    cos_a = attn_table(cos_t)
    sin_a = jnp.where(lane < half_a, -1.0, 1.0) * attn_table(sin_t)

    def rope_attn(t):
        return t * cos_a + pltpu.roll(t, half_a, 1) * sin_a

    first = (lane % IDX_DIM) < half_i
    cos_i = idx_table(cos_t)
    sin_i = jnp.where(first, -1.0, 1.0) * idx_table(sin_t)

    def rope_idx(t):
        rot = jnp.where(first, pltpu.roll(t, LANES - IDX_DIM // 2, 1), pltpu.roll(t, IDX_DIM // 2, 1))
        return t * cos_i + rot * sin_i

    qn, kn = qn_ref[...], kn_ref[...]
    for h in range(DSA_HEADS):
        hs = slice(h * DSA_HEAD_DIM, (h + 1) * DSA_HEAD_DIM)
        vs = slice(2 * hd + h * DSA_HEAD_DIM, 2 * hd + (h + 1) * DSA_HEAD_DIM)
        q_o[rows, hs] = (rope_attn(_rms(dsa_ref[rows, hs], qn)) * DSA_Q_SCALE).astype(q_o.dtype)
        k_o[rows, hs] = rope_attn(_rms(dsa_ref[rows, hd + h * DSA_HEAD_DIM:hd + (h + 1) * DSA_HEAD_DIM], kn)
                                  ).astype(k_o.dtype)
        tk = v_o.shape[-1]
        n = rows.stop - rows.start
        assert tk % n == 0 and rows.start % n == 0
        v_o[rows.start // tk, hs, rows.start % tk:rows.start % tk + n] = dsa_ref[rows, vs].T.astype(v_o.dtype)
    for j in range(IDX_HEADS * IDX_DIM // LANES):
        r = rope_idx(dsa_ref[rows, 3 * hd + j * LANES:3 * hd + (j + 1) * LANES])
        iq_o[2 * j, rows, :] = r[:, :IDX_DIM].astype(iq_o.dtype)
        iq_o[2 * j + 1, rows, :] = r[:, IDX_DIM:].astype(iq_o.dtype)
    sm = sm_ref[rows, :]
    ik_o[rows, :] = rope_idx(sm)[:, SM_IK:SM_IK + IDX_DIM].astype(ik_o.dtype)
    iw_o[:, rows] = sm.T[SM_IW:SM_IW + IDX_HEADS, :]


def _proj_ab_kernel(x_ref, nw_ref, sh_ref, sc_ref, w_ref, pos_ref, qn_ref, kn_ref, freq_ref,
                    gla_o, q_o, k_o, v_o, iq_o, ik_o, iw_o, dsa_ref):
    hb = (_rms(x_ref[...], nw_ref[...]) * (1.0 + sc_ref[...]) + sh_ref[...]).astype(BF16)
    hd = DSA_HEADS * DSA_HEAD_DIM
    tm = x_ref.shape[0]
    halves = [slice(0, tm // 2), slice(tm // 2, tm)]
    small = gla_o.at[:, AB_SMALL:AB_SMALL + LANES]
    for rows in halves:
        gla_o[rows, AB_SMALL:] = _dot(hb[rows], w_ref[:, AB_SMALL:AB_GLA_N])
        for j in range(AB_DSA_N // hd):
            dsa_ref[rows, j * hd:(j + 1) * hd] = _dot(hb[rows], w_ref[:, AB_GLA_N + j * hd:AB_GLA_N + (j + 1) * hd])
    for rows in halves:
        _dsa_prep_rows(rows, dsa_ref, small, pos_ref, qn_ref, kn_ref, freq_ref, q_o, k_o, v_o, iq_o, ik_o, iw_o)
        gla_o[rows, :AB_SMALL] = _dot(hb[rows], w_ref[:, :AB_SMALL])


def _proj_ab(x, nw, shift, scale, w_bf16, positions, q_norm, k_norm, *, tm, tk):
    bsz, s, d = x.shape
    hd = DSA_HEADS * DSA_HEAD_DIM
    half_a, half_i = DSA_HEAD_DIM // 2, IDX_DIM // 2
    inv_a = ROPE_THETA ** (-jnp.arange(half_a, dtype=F32) / half_a)
    inv_i = ROPE_THETA ** (-jnp.arange(half_i, dtype=F32) / half_i)
    freq = jnp.concatenate([inv_a, inv_i, inv_i]).reshape(1, LANES)
    rows = lambda n: pl.BlockSpec((None, tm, n), lambda b, i: (b, i, 0))
    vec = lambda n: pl.BlockSpec((1, n), lambda b, i: (0, 0))
    mod = pl.BlockSpec((None, 1, d), lambda b, i: (b, 0, 0))
    return pl.pallas_call(
        _proj_ab_kernel,
        grid=(bsz, s // tm),
        in_specs=[
            rows(d), vec(d), mod, mod,
            _resident((d, AB_GLA_N + AB_DSA_N), lambda b, i: (0, 0)),
            rows(1), vec(DSA_HEAD_DIM), vec(DSA_HEAD_DIM), vec(LANES),
        ],
        out_specs=[
            rows(AB_GLA_N), rows(hd), rows(hd),
            pl.BlockSpec((None, tm // tk, hd, tk), lambda b, i: (b, i, 0, 0)),
            pl.BlockSpec((None, IDX_HEADS, tm, IDX_DIM), lambda b, i: (b, 0, i, 0)),
            rows(IDX_DIM),
            pl.BlockSpec((None, IDX_HEADS, tm), lambda b, i: (b, 0, i)),
        ],
        out_shape=[
            jax.ShapeDtypeStruct((bsz, s, AB_GLA_N), F32),
            jax.ShapeDtypeStruct((bsz, s, hd), BF16),
            jax.ShapeDtypeStruct((bsz, s, hd), BF16),
            jax.ShapeDtypeStruct((bsz, s // tk, hd, tk), BF16),
            jax.ShapeDtypeStruct((bsz, IDX_HEADS, s, IDX_DIM), BF16),
            jax.ShapeDtypeStruct((bsz, s, IDX_DIM), BF16),
            jax.ShapeDtypeStruct((bsz, IDX_HEADS, s), F32),
        ],
        scratch_shapes=[pltpu.VMEM((tm, AB_DSA_N), F32)],
        compiler_params=_cparams(("arbitrary", "arbitrary")),
        name="proj_ab",
    )(x, nw.reshape(1, d), shift, scale, w_bf16, positions.reshape(bsz, s, 1),
      q_norm.reshape(1, -1), k_norm.reshape(1, -1), freq)


def _dsa_kernel(q_ref, iq_ref, iw_ref, k_ref, vt_ref, ik_ref, o_ref, sc_ref, top_ref, acc_ref,
                *, tq, n_sel, s_len):
    i = pl.program_id(1)
    nkb = i + 1
    kofs = lax.broadcasted_iota(I32, (tq, tq), 0)
    qidx = i * tq + lax.broadcasted_iota(I32, (tq, tq), 1)
    iw = iw_ref[...]
    idx_scale = (IDX_HEADS ** -0.5) * (IDX_DIM ** -0.5)

    def over_tiles(step, init, n=nkb, widest=2):
        carry = lax.fori_loop(0, n // widest, lambda j, c: step(widest * j, widest, c), init)
        rest = n % widest
        start = n - rest
        width = widest // 2
        while width >= 1:
            carry = lax.cond((rest & width) != 0, functools.partial(step, start, width), lambda c: c, carry)
            start = start + (rest & width)
            width //= 2
        return carry

    def as_float(key):
        key = jnp.maximum(key, KEY_LOWEST)
        key = jnp.where((key > 0) & (key < KEY_TINY), KEY_TINY, key)
        return pltpu.bitcast(key ^ ((key >> 31) & 0x7FFFFFFF), F32)

    def scores(kb, width, diagonal):
        k0 = pl.multiple_of(kb * tq, tq)
        rows = width * tq
        ikb = ik_ref[pl.ds(k0, rows), :]
        sc = jnp.zeros((rows, tq), F32)
        for h in range(IDX_HEADS):
            sc = sc + jnp.maximum(_dot_nt(ikb, iq_ref[h]), 0.0) * iw[h:h + 1, :]
        sc = sc * idx_scale
        sc = jnp.where(jnp.abs(sc) < FLT_TINY, 0.0, sc)
        if diagonal:
            sc = jnp.where(kofs <= qidx - i * tq, sc, -jnp.inf)
        sc_ref[pl.ds(kb, width)] = sc.reshape(width, tq, tq)
        top_ref[pl.ds(kb, width)] = sc.astype(BF16).reshape(width, tq, tq)

    def below_diagonal(kb, width, carry):
        scores(kb, width, False)
        return carry

    over_tiles(below_diagonal, 0, n=i)
    scores(i, 1, True)

    def count(pred):
        def body(kb, width, acc):
            rows = width * tq
            m = jnp.where(pred(sc_ref[pl.ds(kb, width)].reshape(rows, tq)), 1, 0)
            return acc + jnp.sum(m.reshape(rows // 8, 8, tq), axis=0)
        return jnp.sum(over_tiles(body, jnp.zeros((8, tq), I32)), axis=0, keepdims=True)

    n_valid = qidx[0:1, :] + 1

    def as_bf16(cand):
        k16 = jnp.maximum(cand, KEY_LOWEST_BF16) >> 16
        pattern = k16 ^ ((k16 >> 15) & 0x7FFF)
        pattern = jnp.where((pattern > 0) & (pattern < 0x80), 0x80, pattern)
        return pltpu.bitcast(pattern << 16, F32).astype(BF16)

    def count_top(cand):
        cand_bf = as_bf16(cand)

        def body(kb, width, acc):
            rows = width * tq
            top = top_ref[pl.ds(kb, width)].reshape(rows, tq)
            m = jnp.where(top >= cand_bf, jnp.ones((), BF16), jnp.zeros((), BF16))
            part = m[0:16]
            for j in range(1, rows // 16):
                part = part + m[16 * j:16 * (j + 1)]
            return acc + part.astype(F32)
        acc = over_tiles(body, jnp.zeros((16, tq), F32))
        return jnp.sum(acc, axis=0, keepdims=True).astype(I32)

    def top_step(it, u):
        cand_u = u | lax.shift_left(jnp.int32(1), jnp.int32(31) - it)
        return jnp.where(count_top(cand_u ^ INT_MIN) >= n_sel, cand_u, u)

    u = lax.fori_loop(0, 16, top_step, jnp.zeros((1, tq), I32))

    def any_lane(flag):
        return jnp.max(jnp.where(flag, 1.0, 0.0)) > 0.0

    base = u ^ INT_MIN
    base = jnp.where(base < 0, base | 0xFFFF, base)
    base = jnp.clip(base, KEY_LOWEST + BF16_HALF_STEP + 1, INT_MAX - 2 * BF16_HALF_STEP)
    many = n_valid > n_sel
    lo = jnp.where(many, base - BF16_HALF_STEP - 1, KEY_LOWEST)
    hi = jnp.where(many, base + 2 * BF16_HALF_STEP, INT_MAX)

    def count_two(pred_a, pred_b):
        def body(kb, width, acc):
            x = sc_ref[pl.ds(kb, width)].reshape(width * tq // 8, 8, tq)
            return (acc[0] + jnp.sum(jnp.where(pred_a(x), 1, 0), axis=0),
                    acc[1] + jnp.sum(jnp.where(pred_b(x), 1, 0), axis=0))
        a, b = over_tiles(body, (jnp.zeros((8, tq), I32), jnp.zeros((8, tq), I32)))
        return jnp.sum(a, axis=0, keepdims=True), jnp.sum(b, axis=0, keepdims=True)

    lo_f, hi_f = as_float(lo), as_float(hi)
    cnt_lo, cnt_hi = count_two(lambda x: x >= lo_f, lambda x: x >= hi_f)

    def open_queries(lo, cnt_lo, hi, done):
        return (cnt_lo > n_sel) & (hi - lo > 1) & (done == 0)

    def halve(s):
        lo, cnt_lo, hi, cnt_hi, done = s
        mid = lo + ((hi - lo) >> 1)
        mid_f = as_float(mid)
        cnt = count(lambda x: x >= mid_f)
        live = open_queries(lo, cnt_lo, hi, done)
        up = live & (cnt >= n_sel)
        down = live & (cnt < n_sel)
        return (jnp.where(up, mid, lo), jnp.where(up, cnt, cnt_lo),
                jnp.where(down, mid, hi), jnp.where(down, cnt, cnt_hi), done)

    st = lax.fori_loop(0, PEEL_AFTER_STEPS, lambda _, s: halve(s),
                       (lo, cnt_lo, hi, cnt_hi, jnp.zeros((1, tq), I32)))

    def peel(st):
        lo, cnt_lo, hi, cnt_hi, done = st
        lo_f, hi_f = as_float(lo), as_float(hi)

        def extremes(kb, width, carry):
            top, bot = carry
            x = sc_ref[pl.ds(kb, width)].reshape(width * tq // 8, 8, tq)
            top = jnp.maximum(top, jnp.max(jnp.where(x < hi_f, x, -jnp.inf), axis=0))
            bot = jnp.minimum(bot, jnp.min(jnp.where(x >= lo_f, x, jnp.inf), axis=0))
            return top, bot
        top, bot = over_tiles(extremes, (jnp.full((8, tq), -jnp.inf, F32), jnp.full((8, tq), jnp.inf, F32)))
        top = jnp.max(top, axis=0, keepdims=True)
        bot = jnp.min(bot, axis=0, keepdims=True)
        from_top = (cnt_hi == n_sel - 1) | (top == bot)
        from_bot = (cnt_lo == n_sel + 1) & jnp.logical_not(from_top)
        cnt_top, cnt_bot = count_two(lambda x: x >= top, lambda x: x > bot)
        cnt = jnp.where(from_top, cnt_top, cnt_bot)
        short = from_bot & (cnt < n_sel)
        hit = open_queries(lo, cnt_lo, hi, done) & (from_top | from_bot)
        thr = jnp.where(from_top, top, bot)
        incl = jnp.where(from_top | short, 1, 0)
        return hit, thr, incl, jnp.where(short, cnt_lo, cnt)

    def with_peel(st):
        hit, thr, incl, cnt = peel(st)
        st = st[:4] + (jnp.where(hit, 1, 0),)
        lo, cnt_lo, _, _, _ = lax.while_loop(lambda s: any_lane(open_queries(s[0], s[1], s[2], s[4])), halve, st)
        return jnp.where(hit, thr, as_float(lo)), jnp.where(hit, incl, 1), jnp.where(hit, cnt, cnt_lo)

    def without_peel(st):
        lo, cnt_lo, _, _, _ = st
        return as_float(lo), jnp.ones((1, tq), I32), cnt_lo

    thr, incl, cnt = lax.cond(any_lane(open_queries(st[0], st[1], st[2], st[4])), with_peel, without_peel, st)
    admits_equal = incl != 0
    excess = cnt > n_sel

    tied = jnp.max(jnp.where(excess, 1.0, 0.0)) > 0.0

    acc_ref[...] = jnp.zeros_like(acc_ref)

    def attend(select, seen):
        def block(kb, width, carry):
            ms, ls, seen = carry
            k0 = pl.multiple_of(kb * tq, tq)
            rows = width * tq
            sel, seen = select(sc_ref[pl.ds(kb, width)].reshape(rows, tq), seen)
            heads = [slice(h * DSA_HEAD_DIM, (h + 1) * DSA_HEAD_DIM) for h in range(DSA_HEADS)]
            st = [_dot_nt(k_ref[pl.ds(k0, rows), hs], q_ref[:, hs]) for hs in heads]
            bias = jnp.where(sel, 0.0, MASK_BIAS)
            st = [x + bias for x in st]
            new_m = [jnp.maximum(m, jnp.max(x, axis=0, keepdims=True)) for m, x in zip(ms, st)]
            alpha = [jnp.exp2(m - mn) for m, mn in zip(ms, new_m)]
            p = [jnp.exp2(x - mn) for x, mn in zip(st, new_m)]
            new_l = [a * l + jnp.sum(x, axis=0, keepdims=True) for a, l, x in zip(alpha, ls, p)]
            p = [x.astype(BF16) for x in p]
            for h, hs in enumerate(heads):
                pv = _dot(vt_ref[kb, hs, :], p[h][0:tq])
                for j in range(1, width):
                    pv = pv + _dot(vt_ref[kb + j, hs, :], p[h][j * tq:(j + 1) * tq])
                acc_ref[h] = alpha[h] * acc_ref[h] + pv
            return tuple(new_m), tuple(new_l), seen

        carry = (tuple(jnp.full((1, tq), -jnp.inf, F32) for _ in range(DSA_HEADS)),
                 tuple(jnp.zeros((1, tq), F32) for _ in range(DSA_HEADS)), seen)
        return over_tiles(block, carry)[1]

    def attend_tied():
        room = n_sel - count(lambda x: x > thr)
        room = jnp.where(excess, room, jnp.where(admits_equal, s_len, 0)).astype(F32)
        r = lax.broadcasted_iota(I32, (tq, tq), 0)
        c = lax.broadcasted_iota(I32, (tq, tq), 1)
        upto = jnp.where(c <= r, 1.0, 0.0).astype(BF16)

        def select(x, seen):
            equal = x == thr
            ones = jnp.where(equal, 1.0, 0.0).astype(BF16)
            fits = []
            for j in range(x.shape[0] // tq):
                rank = _dot(upto, ones[j * tq:(j + 1) * tq])
                fits.append(rank <= room - seen)
                seen = seen + rank[tq - 1:tq, :]
            fits = fits[0] if len(fits) == 1 else jnp.concatenate(fits, axis=0)
            return (x > thr) | (equal & fits), seen
        return attend(select, jnp.zeros((1, tq), F32))

    at_least = jnp.where(admits_equal, thr, jnp.inf)
    above = jnp.where(admits_equal, jnp.inf, thr)
    ls = lax.cond(tied, attend_tied,
                  lambda: attend(lambda x, seen: ((x >= at_least) | (x > above), seen), jnp.zeros((1, 1), F32)))
    for h in range(DSA_HEADS):
        hs = slice(h * DSA_HEAD_DIM, (h + 1) * DSA_HEAD_DIM)
        o_ref[:, hs] = (acc_ref[h] / ls[h]).T.astype(o_ref.dtype)


def _dsa(q, k, vt, iq, ik, iw, *, tq):
    bsz, s, hd = q.shape
    n_sel = min(DSA_TOPK, s // 4)
    resident = lambda n: _resident((None, s, n), lambda b, i: (b, 0, 0))
    return pl.pallas_call(
        functools.partial(_dsa_kernel, tq=tq, n_sel=n_sel, s_len=s),
        grid=(bsz, s // tq),
        in_specs=[
            pl.BlockSpec((None, tq, hd), lambda b, i: (b, i, 0)),
            pl.BlockSpec((None, IDX_HEADS, tq, IDX_DIM), lambda b, i: (b, 0, i, 0)),
            pl.BlockSpec((None, IDX_HEADS, tq), lambda b, i: (b, 0, i)),
            resident(hd),
            _resident((None, s // tq, hd, tq), lambda b, i: (b, 0, 0, 0)),
            resident(IDX_DIM),
        ],
        out_specs=pl.BlockSpec((None, tq, hd), lambda b, i: (b, i, 0)),
        out_shape=jax.ShapeDtypeStruct((bsz, s, hd), BF16),
        scratch_shapes=[
            pltpu.VMEM((s // tq, tq, tq), F32),
            pltpu.VMEM((s // tq, tq, tq), BF16),
            pltpu.VMEM((DSA_HEADS, DSA_HEAD_DIM, tq), F32),
        ],
        compiler_params=_cparams(("arbitrary", "arbitrary"), DSA_VMEM_LIMIT),
        name="dsa",
    )(q, iq, iw, k, vt, ik)


def _mix_ffn_kernel(x_ref, a_ref, b_ref, wa_ref, wb_ref, g1_ref, nw_ref, sh_ref, sc_ref, g2_ref,
                    w1_ref, w2_ref, o_ref, *, tf):
    x = x_ref[...] + g1_ref[...] * (_dot(a_ref[...], wa_ref[...]) + _dot(b_ref[...], wb_ref[...]))
    hb = (_rms(x, nw_ref[...]) * (1.0 + sc_ref[...]) + sh_ref[...]).astype(BF16)
    f = w2_ref.shape[0]
    acc = jnp.zeros(x.shape, F32)
    for j in range(f // tf):
        gate = _dot(hb, w1_ref[:, j * tf:(j + 1) * tf])
        up = _dot(hb, w1_ref[:, f + j * tf:f + (j + 1) * tf])
        acc = acc + _dot((_silu(gate) * up).astype(BF16), w2_ref[j * tf:(j + 1) * tf, :])
    o_ref[...] = x + g2_ref[...] * acc


def _mix_ffn(x, mix_a, mix_b, w_out_bf16, gate1, nw, shift, scale, gate2, w1_bf16, w2_bf16, layer, *, tm, tf):
    bsz, s, d = x.shape
    f = w2_bf16.shape[1]
    na = mix_a.shape[-1]
    assert mix_b.shape[-1] == na and w_out_bf16.shape[0] == 2 * na
    rows = lambda n: pl.BlockSpec((None, tm, n), lambda b, i: (b, i, 0))
    mod = pl.BlockSpec((None, 1, d), lambda b, i: (b, 0, 0))
    return pl.pallas_call(
        functools.partial(_mix_ffn_kernel, tf=tf),
        grid=(bsz, s // tm),
        in_specs=[
            rows(d), rows(na), rows(na),
            _resident((na, d), lambda b, i: (0, 0)),
            _resident((na, d), lambda b, i: (1, 0)),
            mod,
            pl.BlockSpec((1, d), lambda b, i: (0, 0)),
            mod, mod, mod,
            _resident((None, d, 2 * f), lambda b, i: (layer, 0, 0)),
            _resident((None, f, d), lambda b, i: (layer, 0, 0)),
        ],
        out_specs=rows(d),
        out_shape=jax.ShapeDtypeStruct((bsz, s, d), F32),
        compiler_params=_cparams(("arbitrary", "arbitrary")),
        name="mix_ffn",
    )(x, mix_a, mix_b, w_out_bf16, w_out_bf16, gate1, nw.reshape(1, d), shift, scale, gate2,
      w1_bf16, w2_bf16)


HALO = max(POOL_WINDOWS)


def _proj_cd_kernel(x_ref, nw_ref, sh_ref, sc_ref, w_ref, pw_ref, ps_ref, qn_ref, kn_ref,
                    pool_o, q_o, k_o, v_o, ext_ref, sb_ref):
    i = pl.program_id(1)
    tm = x_ref.shape[0]
    n = POOL_GROUPS * POOL_CH
    hb = (_rms(x_ref[...], nw_ref[...]) * (1.0 + sc_ref[...]) + sh_ref[...]).astype(BF16)

    @pl.when(i == 0)
    def _():
        ext_ref[0:HALO, :] = jnp.zeros((HALO, n), F32)

    ext_ref[HALO:HALO + tm, :] = _dot(hb, w_ref[:, :n])
    for j in range(3):
        sb_ref[:, j * n:(j + 1) * n] = _dot(hb, w_ref[:, (j + 1) * n:(j + 2) * n])

    t = i * tm + lax.broadcasted_iota(I32, (tm, 1), 0)
    for g, w in enumerate(POOL_WINDOWS):
        gs = slice(g * POOL_CH, (g + 1) * POOL_CH)
        u = ext_ref[HALO:HALO + tm, gs]
        tot = u
        for j in range(1, w):
            tot = tot + ext_ref[HALO - j:HALO - j + tm, gs]
        cnt = jnp.minimum(t + 1, w).astype(F32)
        pooled = tot / cnt - u
        y = _dot(pooled.astype(BF16), pw_ref[g].astype(BF16)) * ps_ref[:, gs]
        pool_o[:, gs] = y.astype(pool_o.dtype)
    ext_ref[0:HALO, :] = ext_ref[tm:tm + HALO, :]

    qn, kn = qn_ref[...], kn_ref[...]
    for h in range(SB_HEADS):
        hs = slice(h * SB_HEAD_DIM, (h + 1) * SB_HEAD_DIM)
        q_o[:, hs] = (_rms(sb_ref[:, hs], qn) * SB_HEAD_DIM ** -0.5).astype(q_o.dtype)
        k_o[:, hs] = _rms(sb_ref[:, n + h * SB_HEAD_DIM:n + (h + 1) * SB_HEAD_DIM], kn).astype(k_o.dtype)
    tk = v_o.shape[-1]
    for j in range(v_o.shape[0]):
        v_o[j] = sb_ref[j * tk:(j + 1) * tk, 2 * n:3 * n].T.astype(v_o.dtype)


def _proj_cd(x, nw, shift, scale, w_bf16, pool_w, pool_scale, q_norm, k_norm, *, tm, tk):
    bsz, s, d = x.shape
    n = POOL_GROUPS * POOL_CH
    assert w_bf16.shape[1] == 4 * n
    rows = lambda m: pl.BlockSpec((None, tm, m), lambda b, i: (b, i, 0))
    vec = lambda m: pl.BlockSpec((1, m), lambda b, i: (0, 0))
    mod = pl.BlockSpec((None, 1, d), lambda b, i: (b, 0, 0))
    return pl.pallas_call(
        _proj_cd_kernel,
        grid=(bsz, s // tm),
        in_specs=[
            rows(d), vec(d), mod, mod,
            _resident((d, 4 * n), lambda b, i: (0, 0)),
            pl.BlockSpec((POOL_GROUPS, POOL_CH, POOL_CH), lambda b, i: (0, 0, 0)),
            vec(n), vec(SB_HEAD_DIM), vec(SB_HEAD_DIM),
        ],
        out_specs=[rows(n)] * 3 + [pl.BlockSpec((None, tm // tk, n, tk), lambda b, i: (b, i, 0, 0))],
        out_shape=[jax.ShapeDtypeStruct((bsz, s, n), BF16)] * 3
        + [jax.ShapeDtypeStruct((bsz, s // tk, n, tk), BF16)],
        scratch_shapes=[pltpu.VMEM((HALO + tm, n), F32), pltpu.VMEM((tm, 3 * n), F32)],
        compiler_params=_cparams(("arbitrary", "arbitrary")),
        name="proj_cd",
    )(x, nw.reshape(1, d), shift, scale, w_bf16, pool_w, pool_scale.reshape(1, n),
      q_norm.reshape(1, -1), k_norm.reshape(1, -1))


def _sb_kernel(q_ref, k_ref, vt_ref, o_ref, acc_ref, *, tq):
    i = pl.program_id(1)
    kofs = lax.broadcasted_iota(I32, (tq, tq), 0)
    qofs = lax.broadcasted_iota(I32, (tq, tq), 1)
    later = jnp.where(qofs > kofs, 1.0, 0.0).astype(BF16)
    heads = [slice(h * SB_HEAD_DIM, (h + 1) * SB_HEAD_DIM) for h in range(SB_HEADS)]
    acc_ref[...] = jnp.zeros_like(acc_ref)

    strict = kofs < qofs

    def walk(tiles, runs):
        pairs = [(t, h) for t in range(len(tiles)) for h in range(SB_HEADS)]
        rows = [pl.ds(pl.multiple_of(kb * tq, tq), tq) for kb, _ in tiles]
        z = {(t, h): _dot_nt(k_ref[rows[t], heads[h]], q_ref[:, heads[h]]) for t, h in pairs}
        sp = {p: jnp.log(1.0 + jnp.exp(-jnp.abs(z[p]))) for p in pairs}
        log_beta = {p: jnp.minimum(z[p], 0.0) - sp[p] for p in pairs}
        log_1m = {p: log_beta[p] - z[p] for p in pairs}
        log_1m = {p: jnp.where(strict, log_1m[p], 0.0) if tiles[p[0]][1] else log_1m[p] for p in pairs}
        hi = {p: log_1m[p].astype(BF16) for p in pairs}
        lo = {p: (log_1m[p] - hi[p].astype(F32)).astype(BF16) for p in pairs}
        within = {p: _dot(later, hi[p]) + _dot(later, lo[p]) for p in pairs}
        runs = list(runs)
        for t, (kb, diagonal) in enumerate(tiles):
            for h in range(SB_HEADS):
                w = jnp.exp(log_beta[(t, h)] + within[(t, h)] + runs[h])
                if diagonal:
                    w = jnp.where(strict, w, 0.0)
                acc_ref[h] = acc_ref[h] + _dot(vt_ref[kb, heads[h], :], w.astype(BF16))
                runs[h] = runs[h] + jnp.sum(log_1m[(t, h)], axis=0, keepdims=True)
        return tuple(runs)

    def alive(runs):
        worst = functools.reduce(jnp.maximum, runs)
        return jnp.max(worst) > SB_DEAD_LOG

    zero = tuple(jnp.zeros((1, tq), F32) for _ in heads)
    runs = lax.cond(i >= 1, lambda: walk([(i, True), (i - 1, False)], zero), lambda: walk([(i, True)], zero))

    def earlier(state):
        kb, runs, _ = state
        runs = walk([(kb, False)], runs)
        return kb - 1, runs, alive(runs)

    lax.while_loop(lambda st: (st[0] >= 0) & st[2], earlier, (i - 2, runs, alive(runs)))
    for h, hs in enumerate(heads):
        o_ref[:, hs] = acc_ref[h].T.astype(o_ref.dtype)


def _sb(q, k, vt, *, tq):
    bsz, s, hd = q.shape
    return pl.pallas_call(
        functools.partial(_sb_kernel, tq=tq),
        grid=(bsz, s // tq),
        in_specs=[
            pl.BlockSpec((None, tq, hd), lambda b, i: (b, i, 0)),
            _resident((None, s, hd), lambda b, i: (b, 0, 0)),
            _resident((None, s // tq, hd, tq), lambda b, i: (b, 0, 0, 0)),
        ],
        out_specs=pl.BlockSpec((None, tq, hd), lambda b, i: (b, i, 0)),
        out_shape=jax.ShapeDtypeStruct((bsz, s, hd), BF16),
        scratch_shapes=[pltpu.VMEM((SB_HEADS, SB_HEAD_DIM, tq), F32)],
        compiler_params=_cparams(("arbitrary", "arbitrary")),
        name="stick_breaking",
    )(q, k, vt)


def _pack_ab_weight(w):
    d = w.shape[0]
    qkv, glow, gr, dsa, ik, iw = jnp.split(w, [1024, 1040, 1552, 3600, 3664], axis=1)
    pad = jnp.zeros((d, LANES - (IDX_DIM + GLA_GATE_RANK + IDX_HEADS)), w.dtype)
    return jnp.concatenate([qkv, gr, ik, glow, iw, pad, dsa], axis=1)


def kernel(x, c, positions, ada_w, ada_b, mix_norm, ffn_norm, ffn_w1, ffn_w2, ab_w_in, gla_gate_up,
           gla_gate_b, gla_out_norm, dsa_q_norm, dsa_k_norm, ab_w_out, cd_w_in, pool_w, pool_scale,
           sb_q_norm, sb_k_norm, cd_w_out):
    bsz, s, d = x.shape
    depth = ada_w.shape[0]
    mod = _ada_mod(c, ada_w, ada_b).reshape(depth, bsz, 6, 1, d)
    ffn_w1_bf16, ffn_w2_bf16 = ffn_w1.astype(BF16), ffn_w2.astype(BF16)
    tm = min(512, s)
    tq_dsa = min(512, s)
    tq_sb = min(256, s)
    for layer in range(depth):
        sh1, sc1, g1, sh2, sc2, g2 = (mod[layer, :, j] for j in range(6))
        i = layer // 2
        if layer % 2 == 0:
            proj, q, k, vt, iq, ik, iw = _proj_ab(
                x, mix_norm[layer], sh1, sc1, _pack_ab_weight(ab_w_in[i]).astype(BF16), positions,
                dsa_q_norm[i], dsa_k_norm[i], tm=tq_dsa, tk=tq_dsa)
            mix_a = _gla(proj, gla_gate_up[i], gla_gate_b[i], gla_out_norm[i], ts=tm)
            mix_b = _dsa(q, k, vt, iq, ik, iw, tq=tq_dsa)
            w_out = ab_w_out[i]
        else:
            mix_a, q, k, vt = _proj_cd(x, mix_norm[layer], sh1, sc1, cd_w_in[i].astype(BF16), pool_w[i],
                                       pool_scale[i], sb_q_norm[i], sb_k_norm[i], tm=tm, tk=tq_sb)
            mix_b = _sb(q, k, vt, tq=tq_sb)
            w_out = cd_w_out[i]
        x = _mix_ffn(x, mix_a, mix_b, w_out.astype(BF16), g1, ffn_norm[layer], sh2, sc2, g2,
                     ffn_w1_bf16, ffn_w2_bf16, layer, tm=tm, tf=256)
    return x
```
